```python
import math
import jax
import jax.numpy as jnp
from jax import lax
import numpy as np

D_MODEL = 1024
BATCH = 16
SEQ = 2048
DEPTH = 2
DEC_BATCH = 8
DEC_SEQ = 32
PAST_LEN = 2048

CHUNK = 64
N_META = 16
ALPHA = (2 * DEPTH) ** 0.25
BETA = (8 * DEPTH) ** -0.25
LN_EPS = 1e-5
RMS_EPS = 1e-6
NEG_INF = -1e30
SEES_ALL = 2 ** 30

GLA_HEADS = 4
GLA_DK = 64
GLA_DV = 128
GLA_GATE_RANK = 16
GLA_TAU = 16.0
GLA_CHUNK = 64

MLA_HEADS = 8
MLA_Q_LORA = 256
MLA_KV_LORA = 128
MLA_NOPE = 64
MLA_ROPE = 32
MLA_V = 64
MLA_SCALE = (MLA_NOPE + MLA_ROPE) ** -0.5
ROPE_THETA = 10000.0
Q_BLOCK = 128

IN0_WIDTHS = (GLA_HEADS * GLA_DK, GLA_HEADS * GLA_DK, GLA_HEADS * GLA_DV, GLA_HEADS * GLA_DV,
              GLA_GATE_RANK, MLA_Q_LORA, MLA_KV_LORA, MLA_ROPE)
D_IN0 = sum(IN0_WIDTHS)
D_MIX0 = GLA_HEADS * GLA_DV + MLA_HEADS * MLA_V

POOL_WINDOWS = (2, 4, 8, 16)
POOL_GROUP = D_MODEL // len(POOL_WINDOWS)
POOL_MAX = 16

N_EXPERTS = 64
TOP_K = 8
N_GROUPS = 8
TOPK_GROUPS = 4
D_EXPERT = 256
ROUTED_SCALE = 2.5
EXPERT_BLOCK = 128

kernel_name = 'hybrid_gla_mla_pool_moe_stream'


def layer_norm(x, g, b):
    xf = x.astype(jnp.float32)
    mu = jnp.mean(xf, axis=-1, keepdims=True)
    var = jnp.mean(jnp.square(xf - mu), axis=-1, keepdims=True)
    return ((xf - mu) * lax.rsqrt(var + LN_EPS) * g + b).astype(x.dtype)


def rms_norm(x, g):
    xf = x.astype(jnp.float32)
    return (xf * lax.rsqrt(jnp.mean(jnp.square(xf), axis=-1, keepdims=True) + RMS_EPS) * g).astype(x.dtype)


def rope(x, pos):
    half = x.shape[-1] // 2
    inv = ROPE_THETA ** (-jnp.arange(half, dtype=jnp.float32) / half)
    ang = pos.astype(jnp.float32)[:, None] * inv
    cos, sin = jnp.cos(ang), jnp.sin(ang)
    xf = x.astype(jnp.float32)
    x1, x2 = xf[..., :half], xf[..., half:]
    return jnp.concatenate([x1 * cos - x2 * sin, x1 * sin + x2 * cos], axis=-1).astype(x.dtype)


def gla_recurrence(q, k, v, lg, s0):
    B, H, S, _ = q.shape
    n = -(-S // GLA_CHUNK)
    pad = n * GLA_CHUNK - S

    def to_chunks(t):
        t = jnp.pad(t, ((0, 0), (0, 0), (0, pad), (0, 0)))
        return jnp.moveaxis(t.reshape(B, H, n, GLA_CHUNK, t.shape[-1]), 2, 0)

    causal = jnp.tril(jnp.ones((GLA_CHUNK, GLA_CHUNK), dtype=bool))

    def step(s, inp):
        qc, kc, vc, gc = inp
        b = jnp.cumsum(gc, axis=2)
        o_inter = jnp.einsum('bhtd,bhdv->bhtv', qc * jnp.exp(b), s)
        diff = b[:, :, :, None, :] - b[:, :, None, :, :]
        decay = jnp.exp(jnp.where(causal[:, :, None], diff, -jnp.inf))
        att = jnp.einsum('bhtd,bhsd,bhtsd->bhts', qc, kc, decay)
        o = o_inter + jnp.einsum('bhts,bhsv->bhtv', att, vc)
        b_last = b[:, :, -1:, :]
        s = jnp.exp(b_last[:, :, 0, :])[..., None] * s + jnp.einsum('bhsd,bhsv->bhdv', kc * jnp.exp(b_last - b), vc)
        return s, o

    s, o = lax.scan(step, s0, (to_chunks(q), to_chunks(k), to_chunks(v), to_chunks(lg)))
    o = jnp.moveaxis(o, 0, 2).reshape(B, H, n * GLA_CHUNK, v.shape[-1])[:, :, :S]
    return o, s


def mla_attention(q_lat, q_rope, q_chunk, k_lat, k_rope, k_chunk):
    B, H, Sq, C = q_lat.shape
    qb = min(Q_BLOCK, Sq)
    nb = -(-Sq // qb)
    pad = nb * qb - Sq

    def blocks(t):
        t = jnp.pad(t, ((0, 0), (0, 0), (0, pad), (0, 0)))
        return jnp.moveaxis(t.reshape(B, H, nb, qb, t.shape[-1]), 2, 0)

    qch = jnp.pad(q_chunk, (0, pad), constant_values=SEES_ALL).reshape(nb, qb)

    def one_block(args):
        ql, qr, qc = args
        s = (jnp.einsum('bhqc,bkc->bhqk', ql, k_lat) + jnp.einsum('bhqr,bkr->bhqk', qr, k_rope)).astype(jnp.float32) * MLA_SCALE
        s = jnp.where(k_chunk[None, :] <= qc[:, None], s, NEG_INF)
        p = jax.nn.softmax(s, axis=-1).astype(k_lat.dtype)
        return jnp.einsum('bhqk,bkc->bhqc', p, k_lat)

    o = lax.map(one_block, (blocks(q_lat), blocks(q_rope), qch))
    return jnp.moveaxis(o, 0, 2).reshape(B, H, nb * qb, C)[:, :, :Sq]


def hybrid_mixer(x, q_pos, q_chunk, gla_s0, past_ckv, past_krope, past_chunk,
                 w_in0, gla_w_g2, gla_b_g, gla_norm_g, mla_q_norm_g, mla_kv_norm_g,
                 mla_w_uq, mla_w_uk, mla_w_uv, w_out0):
    B, S, _ = x.shape
    h = x @ w_in0
    offs, acc = [], 0
    for w in IN0_WIDTHS[:-1]:
        acc += w
        offs.append(acc)
    q, k, v, r, a, cq, ckv, kr = jnp.split(h, offs, axis=-1)

    def heads(t, d):
        return t.reshape(B, S, GLA_HEADS, d).transpose(0, 2, 1, 3)
    lg = jax.nn.log_sigmoid((a @ gla_w_g2 + gla_b_g).astype(jnp.float32)) / GLA_TAU
    o_gla, s_gla = gla_recurrence(heads(q, GLA_DK).astype(jnp.float32) * GLA_DK ** -0.5,
                                  heads(k, GLA_DK).astype(jnp.float32),
                                  heads(v, GLA_DV).astype(jnp.float32),
                                  heads(lg, GLA_DK), gla_s0.astype(jnp.float32))
    o_gla = rms_norm(o_gla, gla_norm_g).transpose(0, 2, 1, 3).reshape(B, S, GLA_HEADS * GLA_DV)
    o_gla = o_gla.astype(x.dtype) * jax.nn.silu(r)

    qh = (rms_norm(cq, mla_q_norm_g) @ mla_w_uq).reshape(B, S, MLA_HEADS, MLA_NOPE + MLA_ROPE).transpose(0, 2, 1, 3)
    q_nope, q_rope = qh[..., :MLA_NOPE], rope(qh[..., MLA_NOPE:], q_pos)
    ckv = rms_norm(ckv, mla_kv_norm_g)
    kr = rope(kr, q_pos)
    k_lat = jnp.concatenate([past_ckv.astype(ckv.dtype), ckv], axis=1)
    k_rope = jnp.concatenate([past_krope.astype(kr.dtype), kr], axis=1)
    k_chunk = jnp.concatenate([past_chunk, q_chunk])
    q_lat = jnp.einsum('bhqn,chn->bhqc', q_nope, mla_w_uk)
    o_lat = mla_attention(q_lat, q_rope, q_chunk, k_lat, k_rope, k_chunk)
    o_mla = jnp.einsum('bhqc,chv->bqhv', o_lat, mla_w_uv).reshape(B, S, MLA_HEADS * MLA_V)

    out = jnp.concatenate([o_gla, o_mla], axis=-1) @ w_out0
    return out, s_gla.astype(gla_s0.dtype), ckv, kr


def pool_mixer(x, past, pool_w, pool_scale):
    P, S = past.shape[1], x.shape[1]
    xp = jnp.concatenate([past.astype(x.dtype), x], axis=1)
    cs = jnp.pad(jnp.cumsum(xp.astype(jnp.float32), axis=1), ((0, 0), (1, 0), (0, 0)))
    idx = P + jnp.arange(S)
    outs = []
    for g, w in enumerate(POOL_WINDOWS):
        sl = slice(g * POOL_GROUP, (g + 1) * POOL_GROUP)
        lo = jnp.maximum(idx + 1 - w, 0)
        cnt = (idx + 1 - lo).astype(jnp.float32)
        mean = (cs[:, idx + 1, sl] - cs[:, lo, sl]) / cnt[None, :, None]
        mix = (mean - x[..., sl].astype(jnp.float32)).astype(x.dtype)
        outs.append(mix @ pool_w[g])
    y = jnp.concatenate(outs, axis=-1) * pool_scale
    return y, xp[:, xp.shape[1] - (POOL_MAX - 1):]


def routed_experts(x2, idx, wts, w_gate, w_up, w_down):
    T, D = x2.shape
    A = T * TOP_K
    flat_e = idx.reshape(A)
    flat_tok = jnp.arange(A, dtype=jnp.int32) // TOP_K
    order = jnp.argsort(flat_e)
    e_sorted = flat_e[order]
    counts = jnp.zeros((N_EXPERTS,), jnp.int32).at[flat_e].add(1)
    padded = (counts + EXPERT_BLOCK - 1) // EXPERT_BLOCK * EXPERT_BLOCK
    start = jnp.cumsum(counts) - counts
    pad_end = jnp.cumsum(padded)
    pad_start = pad_end - padded
    dest = pad_start[e_sorted] + (jnp.arange(A, dtype=jnp.int32) - start[e_sorted])
    n_blocks = -(-A // EXPERT_BLOCK) + N_EXPERTS
    P = n_blocks * EXPERT_BLOCK
    tok_buf = jnp.full((P,), T, jnp.int32).at[dest].set(flat_tok[order])
    gate_buf = jnp.zeros((P,), jnp.float32).at[dest].set(wts.reshape(A)[order])
    blk_e = jnp.minimum(jnp.searchsorted(pad_end, jnp.arange(n_blocks, dtype=jnp.int32) * EXPERT_BLOCK, side='right'), N_EXPERTS - 1)
    x_pad = jnp.concatenate([x2, jnp.zeros((1, D), x2.dtype)], axis=0)

    def body(acc, blk):
        toks, gates, e = blk
        xb = x_pad[toks]
        hb = jax.nn.silu(xb @ w_gate[e]) * (xb @ w_up[e])
        return acc.at[toks].add((hb @ w_down[e]).astype(jnp.float32) * gates[:, None]), None

    acc, _ = lax.scan(body, jnp.zeros((T + 1, D), jnp.float32),
                      (tok_buf.reshape(n_blocks, EXPERT_BLOCK), gate_buf.reshape(n_blocks, EXPERT_BLOCK), blk_e))
    return acc[:T]


def moe_layer(x, router_w, router_b, w_gate, w_up, w_down, sh_gate, sh_up, sh_down):
    B, S, D = x.shape
    T = B * S
    x2 = x.reshape(T, D)
    scores = jax.nn.sigmoid((x2 @ router_w).astype(jnp.float32))
    biased = scores + router_b.astype(jnp.float32)
    grouped = biased.reshape(T, N_GROUPS, N_EXPERTS // N_GROUPS)
    group_score = jnp.sum(lax.top_k(grouped, 2)[0], axis=-1)
    _, top_groups = lax.top_k(group_score, TOPK_GROUPS)
    group_ok = jnp.sum(jax.nn.one_hot(top_groups, N_GROUPS, dtype=jnp.float32), axis=1) > 0
    masked = jnp.where(group_ok[:, :, None], grouped, -jnp.inf).reshape(T, N_EXPERTS)
    _, idx = lax.top_k(masked, TOP_K)
    wts = jnp.take_along_axis(scores, idx, axis=1)
    wts = wts / jnp.sum(wts, axis=-1, keepdims=True) * ROUTED_SCALE
    routed = routed_experts(x2, idx, wts, w_gate, w_up, w_down)
    shared = (jax.nn.silu(x2 @ sh_gate) * (x2 @ sh_up)) @ sh_down
    return (routed.astype(x.dtype) + shared).reshape(B, S, D)


def setup_inputs(seed: int = 0) -> dict:
    key = jax.random.key(seed)
    ks = iter(jax.random.split(key, 40))
    D, E, F = D_MODEL, N_EXPERTS, D_EXPERT

    def nrm(shape, scale):
        return jax.random.normal(next(ks), shape, jnp.float32) * scale

    def gain(shape):
        return 1.0 + nrm(shape, 0.01)

    return {
        'x_prompt': nrm((BATCH, SEQ, D), 1.0),
        'x_sample': nrm((DEC_BATCH, DEC_SEQ, D), 1.0),
        'cache_mla_ckv': nrm((DEC_BATCH, N_META + PAST_LEN, MLA_KV_LORA), 1.0),
        'cache_mla_krope': nrm((DEC_BATCH, N_META + PAST_LEN, MLA_ROPE), 1.0),
        'state_gla': nrm((DEC_BATCH, GLA_HEADS, GLA_DK, GLA_DV), 0.5),
        'cache_pool': nrm((DEC_BATCH, POOL_MAX - 1, D), 1.0),
        'meta_tokens': nrm((N_META, D), 1.0),
        'w_in0': nrm((D, D_IN0), D ** -0.5),
        'gla_w_g2': nrm((GLA_GATE_RANK, GLA_HEADS * GLA_DK), GLA_GATE_RANK ** -0.5),
        'gla_b_g': nrm((GLA_HEADS * GLA_DK,), 0.1),
        'gla_norm_g': gain((GLA_DV,)),
        'mla_q_norm_g': gain((MLA_Q_LORA,)),
        'mla_kv_norm_g': gain((MLA_KV_LORA,)),
        'mla_w_uq': nrm((MLA_Q_LORA, MLA_HEADS * (MLA_NOPE + MLA_ROPE)), MLA_Q_LORA ** -0.5),
        'mla_w_uk': nrm((MLA_KV_LORA, MLA_HEADS, MLA_NOPE), MLA_KV_LORA ** -0.5),
        'mla_w_uv': nrm((MLA_KV_LORA, MLA_HEADS, MLA_V), MLA_KV_LORA ** -0.5),
        'w_out0': nrm((D_MIX0, D), BETA * D_MIX0 ** -0.5),
        'pool_w': nrm((len(POOL_WINDOWS), POOL_GROUP, POOL_GROUP), BETA * POOL_GROUP ** -0.5),
        'pool_scale': gain((D,)),
        'ln_g': gain((DEPTH, 2, D)),
        'ln_b': nrm((DEPTH, 2, D), 0.01),
        'moe_router_w': nrm((DEPTH, D, E), D ** -0.5),
        'moe_router_b': nrm((DEPTH, E), 0.01),
        'moe_w_gate': nrm((DEPTH, E, D, F), D ** -0.5),
        'moe_w_up': nrm((DEPTH, E, D, F), D ** -0.5),
        'moe_w_down': nrm((DEPTH, E, F, D), BETA * F ** -0.5),
        'moe_sh_gate': nrm((DEPTH, D, F), D ** -0.5),
        'moe_sh_up': nrm((DEPTH, D, F), D ** -0.5),
        'moe_sh_down': nrm((DEPTH, F, D), BETA * F ** -0.5),
    }


def reference(x_prompt, x_sample, cache_mla_ckv, cache_mla_krope, state_gla, cache_pool,
              meta_tokens, w_in0, gla_w_g2, gla_b_g, gla_norm_g, mla_q_norm_g, mla_kv_norm_g,
              mla_w_uq, mla_w_uk, mla_w_uv, w_out0, pool_w, pool_scale, ln_g, ln_b,
              moe_router_w, moe_router_b, moe_w_gate, moe_w_up, moe_w_down,
              moe_sh_gate, moe_sh_up, moe_sh_down):

    def trunk(x, q_pos, q_chunk, gla_s0, past_ckv, past_krope, past_chunk, pool_past):
        for layer in range(DEPTH):
            if layer % 2 == 0:
                out, s_gla, ckv, kr = hybrid_mixer(
                    x, q_pos, q_chunk, gla_s0, past_ckv, past_krope, past_chunk,
                    w_in0, gla_w_g2, gla_b_g, gla_norm_g, mla_q_norm_g, mla_kv_norm_g,
                    mla_w_uq, mla_w_uk, mla_w_uv, w_out0)
            else:
                out, pool_rows = pool_mixer(x, pool_past, pool_w, pool_scale)
            x = layer_norm(ALPHA * x + out, ln_g[layer, 0], ln_b[layer, 0])
            ffn = moe_layer(x, moe_router_w[layer], moe_router_b[layer], moe_w_gate[layer], moe_w_up[layer],
                            moe_w_down[layer], moe_sh_gate[layer], moe_sh_up[layer], moe_sh_down[layer])
            x = layer_norm(ALPHA * x + ffn, ln_g[layer, 1], ln_b[layer, 1])
        return x, ckv, kr, s_gla, pool_rows

    B, S, D = x_prompt.shape
    dt = x_prompt.dtype
    meta = jnp.broadcast_to(meta_tokens.astype(dt)[None], (B, N_META, D))
    xp = jnp.concatenate([meta, x_prompt], axis=1)
    p_pos = jnp.arange(N_META + S, dtype=jnp.int32)
    p_chunk = jnp.concatenate([jnp.full((N_META,), -1, jnp.int32), jnp.arange(S, dtype=jnp.int32) // CHUNK])
    y_full, p_ckv, p_kr, p_gla, p_pool = trunk(
        xp, p_pos, p_chunk,
        jnp.zeros((B, GLA_HEADS, GLA_DK, GLA_DV), dt),
        jnp.zeros((B, 0, MLA_KV_LORA), dt), jnp.zeros((B, 0, MLA_ROPE), dt),
        jnp.zeros((0,), jnp.int32), jnp.zeros((B, 0, D), dt))
    y_prompt = y_full[:, N_META:]

    past_len = cache_mla_ckv.shape[1] - N_META
    Sd = x_sample.shape[1]
    s_pos = N_META + past_len + jnp.arange(Sd, dtype=jnp.int32)
    s_chunk = (past_len + jnp.arange(Sd, dtype=jnp.int32)) // CHUNK
    past_chunk = jnp.concatenate([jnp.full((N_META,), -1, jnp.int32), jnp.arange(past_len, dtype=jnp.int32) // CHUNK])
    y_sample, s_ckv, s_kr, s_gla, s_pool = trunk(
        x_sample, s_pos, s_chunk, state_gla, cache_mla_ckv, cache_mla_krope, past_chunk, cache_pool)

    return (y_prompt, y_sample, p_ckv, p_kr, p_gla, p_pool, s_ckv, s_kr, s_gla, s_pool)
```

```python
import functools
import math

import jax
import jax.numpy as jnp
import numpy as np
from jax import lax
from jax.experimental import pallas as pl
from jax.experimental.pallas import tpu as pltpu

F32 = jnp.float32
BF16 = jnp.bfloat16

CHUNK = 64
DEPTH = 2
ALPHA = (2 * DEPTH) ** 0.25
LN_EPS = 1e-5
RMS_EPS = 1e-6
NEG_INF = -1e30
GLA_HEADS = 4
GLA_DK = 64
GLA_DV = 128
GLA_RANK = 16
GLA_TAU = 16.0
MLA_HEADS = 8
MLA_Q_LORA = 256
MLA_KV_LORA = 128
MLA_NOPE = 64
MLA_ROPE = 32
MLA_V = 64
MLA_SCALE = (MLA_NOPE + MLA_ROPE) ** -0.5
ROPE_THETA = 10000.0
POOL_WINDOWS = (2, 4, 8, 16)
POOL_MAX = 16
N_EXPERTS = 64
TOP_K = 8
N_GROUPS = 8
TOPK_GROUPS = 4
ROUTED_SCALE = 2.5

LANES = 128
MXU_DIM = 256

TOK_TILE = 512
GLA_BLOCK = 128
MLA_Q_TILE = 256
MLA_KV_TILE = 256
MLA_QK = 256
EXPERT_TILE = 256
POOL_TILE = 512
VMEM_LIMIT = 56 * 1024 * 1024


def _params(sem, vmem=VMEM_LIMIT):
    return pltpu.CompilerParams(dimension_semantics=sem, vmem_limit_bytes=vmem)


def _const_spec(shape):
    nd = len(shape)
    return pl.BlockSpec(shape, lambda *_: (0,) * nd)


def _split3(x):
    hi = x.astype(BF16)
    r = x - hi.astype(F32)
    mid = r.astype(BF16)
    lo = (r - mid.astype(F32)).astype(BF16)
    return hi, mid, lo


def _dot(a, b):
    return jnp.dot(a, b, preferred_element_type=F32)


def _dot_f32(a, b):
    a_hi = a.astype(BF16)
    a_lo = (a - a_hi.astype(F32)).astype(BF16)
    b_hi = b.astype(BF16)
    b_lo = (b - b_hi.astype(F32)).astype(BF16)
    return _dot(a_hi, b_hi) + (_dot(a_lo, b_hi) + _dot(a_hi, b_lo))


def _silu(x):
    return x * (1.0 / (1.0 + jnp.exp(-x)))


def _layer_norm(x, g, b):
    mu = jnp.mean(x, axis=-1, keepdims=True)
    xc = x - mu
    var = jnp.mean(xc * xc, axis=-1, keepdims=True)
    return xc * lax.rsqrt(var + LN_EPS) * g + b


def _rms(x, g):
    return x * lax.rsqrt(jnp.mean(x * x, axis=-1, keepdims=True) + RMS_EPS) * g


_C_Q, _C_K, _C_V, _C_R, _C_CQ, _C_CKV, _C_KR, _C_KRS, _C_A, _C_END = (
    0, 256, 512, 1024, 1536, 1792, 1920, 2048, 2176, 2304)


def _inproj_kernel(x_ref, cos_ref, sin_ref, w_ref, wg2_ref, bg_ref, qn_ref, kvn_ref, wuq_ref, wabs_ref,
                   place_ref, q_ref, k_ref, v_ref, sr_ref, lg_ref, qall_ref, ckv_ref, kr_ref, kall_ref):
    h = _dot(x_ref[...].astype(BF16), w_ref[...])
    q_ref[...] = (h[:, _C_Q:_C_K] * GLA_DK ** -0.5).astype(BF16)
    k_ref[...] = h[:, _C_K:_C_V].astype(BF16)
    v_ref[...] = h[:, _C_V:_C_R].astype(BF16)
    sr_ref[...] = _silu(h[:, _C_R:_C_CQ]).astype(BF16)
    z = _dot_f32(h[:, _C_A:_C_END], wg2_ref[...]) + bg_ref[...]
    lg_ref[...] = (jnp.minimum(z, 0.0) - jnp.log(1.0 + jnp.exp(-jnp.abs(z)))) * (1.0 / GLA_TAU)
    cqn = _rms(h[:, _C_CQ:_C_CKV], qn_ref[...]).astype(BF16)
    qh = _dot(cqn, wuq_ref[...])
    n_nope = MLA_HEADS * MLA_NOPE
    cos, sin = cos_ref[...], sin_ref[...]
    x1, x2 = qh[:, n_nope:n_nope + LANES], qh[:, n_nope + LANES:]
    rot = jnp.concatenate([x1 * cos - x2 * sin, x1 * sin + x2 * cos], axis=1).astype(BF16)
    qlat = _dot(qh[:, :n_nope].astype(BF16), wabs_ref[...]).astype(BF16)
    qrope = _dot(rot, place_ref[...]).astype(BF16)
    for hd in range(MLA_HEADS):
        qall_ref[:, MLA_QK * hd:MLA_QK * hd + LANES] = qlat[:, LANES * hd:LANES * (hd + 1)]
        qall_ref[:, MLA_QK * hd + LANES:MLA_QK * (hd + 1)] = qrope[:, LANES * hd:LANES * (hd + 1)]
    ckv = _rms(h[:, _C_CKV:_C_KR], kvn_ref[...])
    lane = lax.broadcasted_iota(jnp.int32, (1, LANES), 1)
    sgn = jnp.where(lane < MLA_ROPE // 2, -1.0, 1.0)
    kr = h[:, _C_KR:_C_KRS] * cos + h[:, _C_KRS:_C_A] * (sin * sgn)
    ckv_ref[...] = ckv
    kr_ref[...] = kr
    kall_ref[:, :LANES] = ckv.astype(BF16)
    kall_ref[:, LANES:] = kr.astype(BF16)


def _inproj(x_all, cos_t, sin_t, w):
    t = x_all.shape[0]
    d = x_all.shape[1]
    tm = TOK_TILE
    row = lambda n: pl.BlockSpec((tm, n), lambda i: (i, 0))
    consts = [w['w_in'], w['w_g2'], w['b_g'], w['q_norm'], w['kv_norm'], w['w_uq'], w['w_abs'], w['place']]
    outs = [(GLA_HEADS * GLA_DK, BF16), (GLA_HEADS * GLA_DK, BF16), (GLA_HEADS * GLA_DV, BF16),
            (GLA_HEADS * GLA_DV, BF16), (GLA_HEADS * GLA_DK, F32), (MLA_HEADS * MLA_QK, BF16),
            (LANES, F32), (LANES, F32), (MLA_QK, BF16)]
    return pl.pallas_call(
        _inproj_kernel,
        grid=(t // tm,),
        in_specs=[row(d), row(LANES), row(LANES)] + [_const_spec(c.shape) for c in consts],
        out_specs=[row(n) for n, _ in outs],
        out_shape=[jax.ShapeDtypeStruct((t, n), dt) for n, dt in outs],
        compiler_params=_params(("parallel",)),
        name="inproj",
    )(x_all, cos_t, sin_t, *consts)


def _gla_tables(c):
    levels = int(math.log2(c))
    assert 1 << levels == c
    t = np.arange(c)[:, None]
    u = np.arange(c)[None, :]
    mats, masks = [], []
    for l in range(levels):
        half = 1 << l
        base = (t // half) * half
        upper = ((t >> l) & 1) == 1
        a_q = (u >= base) & (u <= t)
        a_k = (u > t) & (u <= base + half - 1)
        mats.append(np.where(upper, a_q, a_k))
        same = (t >> (l + 1)) == (u >> (l + 1))
        masks.append(same & upper & (((u >> l) & 1) == 0))
    mats.append(u <= t)
    mats.append(u > t)
    masks.append(t == u)
    amat = np.concatenate(mats, axis=0).astype(np.float32)
    mask = np.stack([np.tile(m, (GLA_HEADS, 1)) for m in masks]).astype(np.float32)
    return jnp.asarray(amat, BF16), jnp.asarray(mask, F32), levels


def _gla_kernel(q_ref, k_ref, v_ref, lg_ref, sr_ref, s0_ref, amat_ref, mask_ref, g_ref,
                o_ref, sout_ref, state, *, levels):
    c = q_ref.shape[0]
    hk = GLA_HEADS * GLA_DK
    step = pl.program_id(1)

    @pl.when(step == 0)
    def _():
        state[...] = s0_ref[0]

    lg = lg_ref[...]
    parts = _split3(lg)
    e3 = _dot(amat_ref[...], jnp.concatenate(parts, axis=1))
    ex = jnp.exp(e3[:, :hk] + e3[:, hk:2 * hk] + e3[:, 2 * hk:])
    q = q_ref[...].astype(F32)
    k = k_ref[...].astype(F32)
    v = v_ref[...]
    head = lax.broadcasted_iota(jnp.int32, (1, hk), 1) // GLA_DK

    def per_head(xf):
        return jnp.concatenate([jnp.where(head == h, xf, 0.0) for h in range(GLA_HEADS)], axis=0).astype(BF16)

    att = jnp.zeros((GLA_HEADS * c, c), F32)
    for l in range(levels + 1):
        if l < levels:
            el = ex[l * c:(l + 1) * c]
            ql, kl = q * el, k * el
        else:
            ql, kl = q, k
        a = lax.dot_general(per_head(ql), kl.astype(BF16), (((1,), (1,)), ((), ())), preferred_element_type=F32)
        att = att + a * mask_ref[l]
    s_prev = state[...]
    o_inter = _dot(per_head(q * ex[levels * c:(levels + 1) * c]), s_prev.astype(BF16))
    att = att.astype(BF16)
    g = g_ref[...]
    for h in range(GLA_HEADS):
        o = o_inter[h * c:(h + 1) * c] + _dot(att[h * c:(h + 1) * c], v[:, GLA_DV * h:GLA_DV * (h + 1)])
        o = _rms(o, g)
        o_ref[:, GLA_DV * h:GLA_DV * (h + 1)] = (o * sr_ref[:, GLA_DV * h:GLA_DV * (h + 1)].astype(F32)).astype(BF16)
    kr = (k * ex[(levels + 1) * c:]).astype(BF16)
    upd = lax.dot_general(kr, v, (((0,), (0,)), ((), ())), preferred_element_type=F32)
    ones = jnp.ones((c, GLA_DV), BF16)
    b_last = sum(lax.dot_general(p, ones, (((0,), (0,)), ((), ())), preferred_element_type=F32) for p in parts)
    new = jnp.exp(b_last) * s_prev + jnp.concatenate(
        [upd[GLA_DK * h:GLA_DK * (h + 1), GLA_DV * h:GLA_DV * (h + 1)] for h in range(GLA_HEADS)], axis=0)
    state[...] = new

    @pl.when(step == pl.num_programs(1) - 1)
    def _():
        sout_ref[0] = new


def _gla(q, k, v, lg, sr, s0, g, tables, batch, blocks_per_seq):
    amat, mask, levels = tables
    c = GLA_BLOCK
    hk, hv = GLA_HEADS * GLA_DK, GLA_HEADS * GLA_DV
    row = lambda n: pl.BlockSpec((c, n), lambda b, s: (b * blocks_per_seq + s, 0))
    st = pl.BlockSpec((1, hk, GLA_DV), lambda b, s: (b, 0, 0))
    return pl.pallas_call(
        functools.partial(_gla_kernel, levels=levels),
        grid=(batch, blocks_per_seq),
        in_specs=[row(hk), row(hk), row(hv), row(hk), row(hv), st,
                  _const_spec(amat.shape), _const_spec(mask.shape), _const_spec(g.shape)],
        out_specs=[row(hv), st],
        out_shape=[jax.ShapeDtypeStruct((batch * blocks_per_seq * c, hv), BF16),
                   jax.ShapeDtypeStruct((batch, hk, GLA_DV), F32)],
        scratch_shapes=[pltpu.VMEM((hk, GLA_DV), F32)],
        compiler_params=_params(("parallel", "arbitrary")),
        name="gla",
    )(q, k, v, lg, sr, s0, amat, mask, g)


def _mla_kernel(q_ref, kt_ref, kv_ref, o_ref, m_sc, l_sc, acc_sc, *, tq, n_past, past_valid, own_valid, diag_chunks):
    kvb = MLA_KV_TILE
    qi = pl.program_id(1)
    qs = jnp.concatenate([q_ref[:, MLA_QK * h:MLA_QK * (h + 1)] for h in range(MLA_HEADS)], axis=0)
    m_rows = MLA_HEADS * tq
    m_sc[...] = jnp.full(m_sc.shape, -jnp.inf, F32)
    l_sc[...] = jnp.zeros(l_sc.shape, F32)
    acc_sc[...] = jnp.zeros(acc_sc.shape, F32)

    def block(j, mask):
        s = _dot(qs, kt_ref[0, j]) * MLA_SCALE
        if mask is not None:
            s = jnp.where(mask, s, NEG_INF)
        m_old = m_sc[...]
        m_new = jnp.maximum(m_old, jnp.max(s, axis=1, keepdims=True))
        alpha = jnp.exp(m_old - m_new)
        p = jnp.exp(s - m_new)
        l_sc[...] = alpha * l_sc[...] + jnp.sum(p, axis=1, keepdims=True)
        vblk = kv_ref[0, pl.ds(pl.multiple_of(j * kvb, kvb), kvb), :MLA_KV_LORA]
        acc_sc[...] = alpha * acc_sc[...] + _dot(p.astype(BF16), vblk)
        m_sc[...] = m_new

    def full_blocks(lo, hi):
        def body(j, carry):
            block(j, None)
            return carry
        lax.fori_loop(lo, hi, body, 0)

    col = lax.broadcasted_iota(jnp.int32, (m_rows, kvb), 1)
    n_past_full = past_valid // kvb
    if n_past_full:
        full_blocks(0, n_past_full)
    if n_past_full < n_past:
        block(n_past_full, col < past_valid - n_past_full * kvb)
    full_blocks(n_past, n_past + qi)
    mask = col < own_valid - qi * kvb
    if diag_chunks:
        qrow = lax.broadcasted_iota(jnp.int32, (m_rows, kvb), 0) % tq
        mask = mask & (col // CHUNK <= qrow // CHUNK)
    block(n_past + qi, mask)
    o = acc_sc[...] / l_sc[...]
    for h in range(MLA_HEADS):
        o_ref[:, MLA_KV_LORA * h:MLA_KV_LORA * (h + 1)] = o[h * tq:(h + 1) * tq].astype(BF16)


def _mla(q_all, kt, kv, batch, nq, tq, q_block0, past_valid, own_valid):
    n_kv = kt.shape[1]
    n_past = -(-past_valid // MLA_KV_TILE)
    assert nq == 1 or tq == MLA_KV_TILE
    assert n_kv == n_past + nq
    return pl.pallas_call(
        functools.partial(_mla_kernel, tq=tq, n_past=n_past, past_valid=past_valid, own_valid=own_valid,
                          diag_chunks=nq > 1),
        grid=(batch, nq),
        in_specs=[pl.BlockSpec((tq, MLA_HEADS * MLA_QK), lambda b, i: (q_block0 + b * nq + i, 0)),
                  pl.BlockSpec((1,) + kt.shape[1:], lambda b, i: (b, 0, 0, 0)),
                  pl.BlockSpec((1,) + kv.shape[1:], lambda b, i: (b, 0, 0))],
        out_specs=pl.BlockSpec((tq, MLA_HEADS * MLA_KV_LORA), lambda b, i: (b * nq + i, 0)),
        out_shape=jax.ShapeDtypeStruct((batch * nq * tq, MLA_HEADS * MLA_KV_LORA), BF16),
        scratch_shapes=[pltpu.VMEM((MLA_HEADS * tq, 1), F32), pltpu.VMEM((MLA_HEADS * tq, 1), F32),
                        pltpu.VMEM((MLA_HEADS * tq, MLA_KV_LORA), F32)],
        compiler_params=_params(("parallel", "arbitrary")),
        name="mla",
    )(q_all, kt, kv)


def _post_mixer(x, out, g, b, rw_ref, x1_ref, x1b_ref, lgt_ref):
    y = _layer_norm(ALPHA * x + out, g, b)
    x1_ref[...] = y
    x1b_ref[...] = y.astype(BF16)
    lgt_ref[...] = _dot_f32(y, rw_ref[...])


def _outproj_kernel(og_ref, ol_ref, x_ref, wuv_ref, wo_ref, g_ref, b_ref, rw_ref, x1_ref, x1b_ref, lgt_ref):
    o_mla = _dot(ol_ref[...], wuv_ref[...]).astype(BF16)
    n_gla = og_ref.shape[1]
    out = _dot(og_ref[...], wo_ref[:n_gla, :]) + _dot(o_mla, wo_ref[n_gla:, :])
    _post_mixer(x_ref[...], out, g_ref[...], b_ref[...], rw_ref, x1_ref, x1b_ref, lgt_ref)


def _outproj(o_gla, o_lat, x_all, w_uvbd, w_out, g, b, rw):
    t, d = x_all.shape
    tm = TOK_TILE
    row = lambda n: pl.BlockSpec((tm, n), lambda i: (i, 0))
    consts = [w_uvbd, w_out, g, b, rw]
    return pl.pallas_call(
        _outproj_kernel,
        grid=(t // tm,),
        in_specs=[row(o_gla.shape[1]), row(o_lat.shape[1]), row(d)] + [_const_spec(c.shape) for c in consts],
        out_specs=[row(d), row(d), row(LANES)],
        out_shape=[jax.ShapeDtypeStruct((t, d), F32), jax.ShapeDtypeStruct((t, d), BF16),
                   jax.ShapeDtypeStruct((t, LANES), F32)],
        compiler_params=_params(("parallel",)),
        name="outproj",
    )(o_gla, o_lat, x_all, *consts)


def _pool_kernel(x_ref, prev_ref, hist_ref, pw_ref, ps_ref, g_ref, b_ref, rw_ref, x1_ref, x1b_ref, lgt_ref):
    ts = x_ref.shape[0]
    pm = POOL_MAX
    x = x_ref[...]
    prev = jnp.where(pl.program_id(1) == 0, hist_ref[0], prev_ref[...])
    xc = jnp.concatenate([prev, x], axis=0)
    grp = x.shape[1] // len(POOL_WINDOWS)
    outs = []
    for gi, w in enumerate(POOL_WINDOWS):
        s = xc[:, gi * grp:(gi + 1) * grp]
        span = 1
        while span < w:
            s = s + pltpu.roll(s, span, 0)
            span *= 2
        win = s[pm:]
        mix = (win * (1.0 / w) - x[:, gi * grp:(gi + 1) * grp]).astype(BF16)
        outs.append(_dot(mix, pw_ref[gi]))
    out = jnp.concatenate(outs, axis=1) * ps_ref[...]
    _post_mixer(x, out, g_ref[...], b_ref[...], rw_ref, x1_ref, x1b_ref, lgt_ref)


def _pool(x_all, hist, pool_w, pool_scale, g, b, rw, batch, tiles_per_seq, ts, tile0):
    d = x_all.shape[1]
    pm = POOL_MAX
    per = ts // pm
    consts = [pool_w, pool_scale, g, b, rw]
    row = lambda n: pl.BlockSpec((ts, n), lambda bb, i: (bb * tiles_per_seq + i, 0))
    return pl.pallas_call(
        _pool_kernel,
        grid=(batch, tiles_per_seq),
        in_specs=[pl.BlockSpec((ts, d), lambda bb, i: (tile0 + bb * tiles_per_seq + i, 0)),
                  pl.BlockSpec((pm, d), lambda bb, i: (jnp.maximum((tile0 + bb * tiles_per_seq + i) * per - 1, 0), 0)),
                  pl.BlockSpec((1, pm, d), lambda bb, i: (bb, 0, 0))] + [_const_spec(c.shape) for c in consts],
        out_specs=[row(d), row(d), row(LANES)],
        out_shape=[jax.ShapeDtypeStruct((batch * tiles_per_seq * ts, d), F32),
                   jax.ShapeDtypeStruct((batch * tiles_per_seq * ts, d), BF16),
                   jax.ShapeDtypeStruct((batch * tiles_per_seq * ts, LANES), F32)],
        compiler_params=_params(("parallel", "arbitrary")),
        name="pool",
    )(x_all, x_all, hist, *consts)


def _experts_kernel(be_ref, nb_ref, x_ref, wg_ref, wu_ref, wd_ref, y_ref):
    @pl.when(pl.program_id(0) < nb_ref[0])
    def _():
        x = x_ref[...]
        hmid = (_silu(_dot(x, wg_ref[0])) * _dot(x, wu_ref[0])).astype(BF16)
        y_ref[...] = _dot(hmid, wd_ref[0]).astype(BF16)

    @pl.when(pl.program_id(0) >= nb_ref[0])
    def _():
        y_ref[...] = jnp.zeros_like(y_ref)


def _experts(blk_e, n_used, x_sorted, wg, wu, wd):
    p, d = x_sorted.shape
    f = wg.shape[2]
    bm = EXPERT_TILE
    grid_spec = pltpu.PrefetchScalarGridSpec(
        num_scalar_prefetch=2,
        grid=(p // bm,),
        in_specs=[pl.BlockSpec((bm, d), lambda i, be, nb: (i, 0)),
                  pl.BlockSpec((1, d, f), lambda i, be, nb: (be[i], 0, 0)),
                  pl.BlockSpec((1, d, f), lambda i, be, nb: (be[i], 0, 0)),
                  pl.BlockSpec((1, f, d), lambda i, be, nb: (be[i], 0, 0))],
        out_specs=pl.BlockSpec((bm, d), lambda i, be, nb: (i, 0)),
    )
    return pl.pallas_call(
        _experts_kernel,
        grid_spec=grid_spec,
        out_shape=jax.ShapeDtypeStruct((p, d), BF16),
        compiler_params=_params(("arbitrary",)),
        name="experts",
    )(blk_e, n_used, x_sorted, wg, wu, wd)


def _ffn_out_kernel(x_ref, xb_ref, routed_ref, sg_ref, su_ref, sd_ref, g_ref, b_ref, y_ref, yb_ref):
    xb = xb_ref[...]
    hmid = (_silu(_dot(xb, sg_ref[...])) * _dot(xb, su_ref[...])).astype(BF16)
    ffn = routed_ref[...] + _dot(hmid, sd_ref[...])
    y = _layer_norm(ALPHA * x_ref[...] + ffn, g_ref[...], b_ref[...])
    y_ref[...] = y
    yb_ref[...] = y.astype(BF16)


def _ffn_out(x1, x1b, routed, sg, su, sd, g, b):
    t, d = x1.shape
    tm = TOK_TILE
    row = lambda n: pl.BlockSpec((tm, n), lambda i: (i, 0))
    consts = [sg, su, sd, g, b]
    return pl.pallas_call(
        _ffn_out_kernel,
        grid=(t // tm,),
        in_specs=[row(d), row(d), row(d)] + [_const_spec(c.shape) for c in consts],
        out_specs=[row(d), row(d)],
        out_shape=[jax.ShapeDtypeStruct((t, d), F32), jax.ShapeDtypeStruct((t, d), BF16)],
        compiler_params=_params(("parallel",)),
        name="ffn_out",
    )(x1, x1b, routed, *consts)


def _route(logits, router_b):
    t = logits.shape[0]
    scores = jax.nn.sigmoid(logits)
    biased = scores + router_b.astype(F32)
    grouped = biased.reshape(t, N_GROUPS, N_EXPERTS // N_GROUPS)
    group_score = jnp.sum(lax.top_k(grouped, 2)[0], axis=-1)
    _, top_groups = lax.top_k(group_score, TOPK_GROUPS)
    group_ok = jnp.sum(jax.nn.one_hot(top_groups, N_GROUPS, dtype=F32), axis=1) > 0
    masked = jnp.where(group_ok[:, :, None], grouped, -jnp.inf).reshape(t, N_EXPERTS)
    _, idx = lax.top_k(masked, TOP_K)
    wts = jnp.take_along_axis(scores, idx, axis=1)
    wts = wts / jnp.sum(wts, axis=-1, keepdims=True) * ROUTED_SCALE
    return idx, wts


def _moe(x1, x1b, logits, router_b, wg, wu, wd, sg, su, sd, g, b):
    t, d = x1.shape
    idx, wts = _route(logits[:, :N_EXPERTS], router_b)
    a = t * TOP_K
    bm = EXPERT_TILE
    flat_e = idx.reshape(a)
    order = jnp.argsort(flat_e)
    e_sorted = flat_e[order]
    counts = jnp.sum((flat_e[:, None] == jnp.arange(N_EXPERTS, dtype=flat_e.dtype)[None, :]).astype(jnp.int32), axis=0)
    padded = (counts + bm - 1) // bm * bm
    start = jnp.cumsum(counts) - counts
    pad_end = jnp.cumsum(padded)
    pad_start = pad_end - padded
    dest_sorted = pad_start[e_sorted] + (jnp.arange(a, dtype=jnp.int32) - start[e_sorted])
    n_blocks = -(-a // bm) + N_EXPERTS
    tok_buf = jnp.zeros((n_blocks * bm,), jnp.int32).at[dest_sorted].set((order // TOP_K).astype(jnp.int32))
    dest = jnp.zeros((a,), jnp.int32).at[order].set(dest_sorted)
    blk_e = jnp.minimum(jnp.searchsorted(pad_end, jnp.arange(n_blocks, dtype=jnp.int32) * bm, side='right'),
                        N_EXPERTS - 1).astype(jnp.int32)
    n_used = (pad_end[-1] // bm).astype(jnp.int32).reshape(1)
    x_sorted = jnp.take(x1b, tok_buf, axis=0)
    y_sorted = _experts(blk_e, n_used, x_sorted, wg, wu, wd)
    y_tok = jnp.take(y_sorted, dest, axis=0).reshape(t, TOP_K, d).astype(F32)
    routed = jnp.sum(y_tok * wts[:, :, None], axis=1)
    return _ffn_out(x1, x1b, routed, sg, su, sd, g, b)


def _rope_tables(pos):
    half = MLA_ROPE // 2
    inv = ROPE_THETA ** (-jnp.arange(half, dtype=F32) / half)
    ang = pos.astype(F32)[:, None] * inv
    reps = LANES // half
    return jnp.tile(jnp.cos(ang), (1, reps)), jnp.tile(jnp.sin(ang), (1, reps))


def _pack_layer0_weights(w_in0, gla_w_g2, gla_b_g, mla_q_norm_g, mla_kv_norm_g, mla_w_uq, mla_w_uk, mla_w_uv):
    d = w_in0.shape[0]
    hk, hv = GLA_HEADS * GLA_DK, GLA_HEADS * GLA_DV
    o_q, o_k, o_v, o_r = 0, hk, 2 * hk, 2 * hk + hv
    o_a = o_r + hv
    o_cq = o_a + GLA_RANK
    o_ckv = o_cq + MLA_Q_LORA
    o_kr = o_ckv + MLA_KV_LORA
    half = MLA_ROPE // 2
    zeros = lambda n: jnp.zeros((d, n), w_in0.dtype)
    kr1, kr2 = w_in0[:, o_kr:o_kr + half], w_in0[:, o_kr + half:o_kr + MLA_ROPE]
    w_in = jnp.concatenate([
        w_in0[:, o_q:o_a], w_in0[:, o_cq:o_kr],
        kr1, kr2, zeros(LANES - MLA_ROPE),
        kr2, kr1, zeros(LANES - MLA_ROPE),
        w_in0[:, o_a:o_cq], zeros(LANES - GLA_RANK)], axis=1).astype(BF16)
    assert w_in.shape[1] == _C_END
    w_g2 = jnp.concatenate([gla_w_g2, jnp.zeros((LANES - GLA_RANK, hk), gla_w_g2.dtype)], axis=0)
    uq = mla_w_uq.reshape(MLA_Q_LORA, MLA_HEADS, MLA_NOPE + MLA_ROPE)
    w_uq = jnp.concatenate([uq[:, :, :MLA_NOPE].reshape(MLA_Q_LORA, -1),
                            uq[:, :, MLA_NOPE:MLA_NOPE + half].reshape(MLA_Q_LORA, -1),
                            uq[:, :, MLA_NOPE + half:].reshape(MLA_Q_LORA, -1)], axis=1).astype(BF16)
    eye = jnp.eye(MLA_HEADS, dtype=mla_w_uk.dtype)
    w_abs = jnp.einsum('chn,hg->hngc', mla_w_uk, eye).reshape(MLA_HEADS * MLA_NOPE, MLA_HEADS * MLA_KV_LORA).astype(BF16)
    w_uvbd = jnp.einsum('chv,hg->hcgv', mla_w_uv, eye).reshape(MLA_HEADS * MLA_KV_LORA, MLA_HEADS * MLA_V).astype(BF16)
    place = np.zeros((2 * LANES, MLA_HEADS * LANES), np.float32)
    for h in range(MLA_HEADS):
        for j in range(half):
            place[h * half + j, h * LANES + j] = 1.0
            place[LANES + h * half + j, h * LANES + half + j] = 1.0
    return dict(w_in=w_in, w_g2=w_g2, b_g=gla_b_g.reshape(1, hk), q_norm=mla_q_norm_g.reshape(1, -1),
                kv_norm=mla_kv_norm_g.reshape(1, -1), w_uq=w_uq, w_abs=w_abs,
                place=jnp.asarray(place, BF16)), w_uvbd


def _pad_rows(x, n):
    return jnp.pad(x, ((0, n - x.shape[0]),) + ((0, 0),) * (x.ndim - 1))


def _kv_tiles(past, own):
    kvb = MLA_KV_TILE
    pad = lambda a: jnp.pad(a, ((0, 0), (0, -a.shape[1] % kvb), (0, 0)))
    kv = jnp.concatenate([pad(past), pad(own)], axis=1) if past.shape[1] else pad(own)
    b, n, w = kv.shape
    kt = kv.reshape(b, n // kvb, kvb, w).transpose(0, 1, 3, 2)
    return kt, kv


def kernel(x_prompt, x_sample, cache_mla_ckv, cache_mla_krope, state_gla, cache_pool, meta_tokens, w_in0, gla_w_g2, gla_b_g, gla_norm_g, mla_q_norm_g, mla_kv_norm_g, mla_w_uq, mla_w_uk, mla_w_uv, w_out0, pool_w, pool_scale, ln_g, ln_b, moe_router_w, moe_router_b, moe_w_gate, moe_w_up, moe_w_down, moe_sh_gate, moe_sh_up, moe_sh_down):
    bp, sp, d = x_prompt.shape
    bs, ss, _ = x_sample.shape
    n_meta = meta_tokens.shape[0]
    past_len = cache_mla_ckv.shape[1] - n_meta
    tp, tsm = bp * sp, bs * ss
    assert sp % MLA_Q_TILE == 0 and sp % POOL_TILE == 0 and sp % GLA_BLOCK == 0 and sp % CHUNK == 0
    assert ss <= CHUNK and past_len % CHUNK == 0 and n_meta <= CHUNK and ss <= GLA_BLOCK and n_meta <= GLA_BLOCK
    assert n_meta == POOL_MAX and ss >= POOL_MAX and tp % TOK_TILE == 0
    t_all = -(-(tp + tsm + n_meta) // TOK_TILE) * TOK_TILE
    o_s, o_m = tp, tp + tsm

    x_all = _pad_rows(jnp.concatenate([x_prompt.reshape(tp, d), x_sample.reshape(tsm, d), meta_tokens], axis=0), t_all)
    pos = jnp.concatenate([jnp.tile(n_meta + jnp.arange(sp), bp), jnp.tile(n_meta + past_len + jnp.arange(ss), bs),
                           jnp.arange(n_meta), jnp.zeros((t_all - o_m - n_meta,), jnp.int32)])
    cos_t, sin_t = _rope_tables(pos)
    w0, w_uvbd = _pack_layer0_weights(w_in0, gla_w_g2, gla_b_g, mla_q_norm_g, mla_kv_norm_g, mla_w_uq, mla_w_uk, mla_w_uv)

    q, k, v, sr, lg, q_all, ckv, kr, k_all = _inproj(x_all, cos_t, sin_t, w0)

    tables = _gla_tables(GLA_BLOCK)
    g_gla = gla_norm_g.reshape(1, GLA_DV)
    c = GLA_BLOCK
    hk = GLA_HEADS * GLA_DK

    def small(a):
        sm = jnp.pad(a[o_s:o_m].reshape(bs, ss, -1), ((0, 0), (0, c - ss), (0, 0)))
        me = jnp.pad(a[o_m:o_m + n_meta], ((0, c - n_meta), (0, 0)))[None]
        return jnp.concatenate([sm, me], axis=0).reshape((bs + 1) * c, -1)

    s0_small = jnp.concatenate([state_gla.reshape(bs, hk, GLA_DV), jnp.zeros((1, hk, GLA_DV), F32)], axis=0)
    og_small, st_small = _gla(small(q), small(k), small(v), small(lg), small(sr), s0_small, g_gla, tables, bs + 1, 1)
    s0_prompt = jnp.broadcast_to(st_small[bs:], (bp, hk, GLA_DV))
    og_prompt, st_prompt = _gla(q, k, v, lg, sr, s0_prompt, g_gla, tables, bp, sp // c)
    og_small = og_small.reshape(bs + 1, c, -1)
    o_gla = _pad_rows(jnp.concatenate([og_prompt, og_small[:bs, :ss].reshape(tsm, -1), og_small[bs, :n_meta]], axis=0), t_all)

    k_meta = k_all[o_m:o_m + n_meta]
    kt_m, kv_m = _kv_tiles(jnp.zeros((1, 0, MLA_QK), BF16), k_meta[None])
    ol_meta = _mla(q_all[o_m:o_m + n_meta], kt_m, kv_m, 1, 1, n_meta, 0, 0, n_meta)
    cache = jnp.concatenate([cache_mla_ckv, cache_mla_krope,
                             jnp.zeros(cache_mla_ckv.shape[:2] + (MLA_QK - MLA_KV_LORA - MLA_ROPE,), F32)], axis=-1).astype(BF16)
    kt_s, kv_s = _kv_tiles(cache, k_all[o_s:o_m].reshape(bs, ss, MLA_QK))
    ol_sample = _mla(q_all[o_s:o_m], kt_s, kv_s, bs, 1, ss, 0, n_meta + past_len, ss)
    kt_p, kv_p = _kv_tiles(jnp.broadcast_to(k_meta[None], (bp, n_meta, MLA_QK)), k_all[:tp].reshape(bp, sp, MLA_QK))
    ol_prompt = _mla(q_all, kt_p, kv_p, bp, sp // MLA_Q_TILE, MLA_Q_TILE, 0, n_meta, sp)
    o_lat = _pad_rows(jnp.concatenate([ol_prompt, ol_sample, ol_meta], axis=0), t_all)

    ln = lambda l, j: (ln_g[l, j].reshape(1, d), ln_b[l, j].reshape(1, d))
    rw = lambda l: jnp.pad(moe_router_w[l], ((0, 0), (0, LANES - N_EXPERTS)))
    moe_w = lambda l: (moe_w_gate[l].astype(BF16), moe_w_up[l].astype(BF16), moe_w_down[l].astype(BF16),
                       moe_sh_gate[l].astype(BF16), moe_sh_up[l].astype(BF16), moe_sh_down[l].astype(BF16))
    x1, x1b, logits = _outproj(o_gla, o_lat, x_all, w_uvbd, w_out0.astype(BF16), *ln(0, 0), rw(0))
    x2, _ = _moe(x1, x1b, logits, moe_router_b[0], *moe_w(0), *ln(0, 1))

    pm = POOL_MAX
    zrow = jnp.zeros((1, d), F32)
    hist_p = jnp.broadcast_to(jnp.concatenate([zrow, x2[o_m + n_meta - (pm - 1):o_m + n_meta]], axis=0)[None], (bp, pm, d))
    hist_s = jnp.concatenate([jnp.zeros((bs, 1, d), F32), cache_pool], axis=1)
    pool_wb = pool_w.astype(BF16)
    ps = pool_scale.reshape(1, d)
    y1p = _pool(x2, hist_p, pool_wb, ps, *ln(1, 0), rw(1), bp, sp // POOL_TILE, POOL_TILE, 0)
    y1s = _pool(x2[o_s:o_m], hist_s, pool_wb, ps, *ln(1, 0), rw(1), bs, 1, ss, 0)
    t1 = -(-(tp + tsm) // TOK_TILE) * TOK_TILE
    x3, x3b, logits1 = [_pad_rows(jnp.concatenate([a, b_], axis=0), t1) for a, b_ in zip(y1p, y1s)]
    x4, _ = _moe(x3, x3b, logits1, moe_router_b[1], *moe_w(1), *ln(1, 1))

    y_prompt = x4[:tp].reshape(bp, sp, d)
    y_sample = x4[o_s:o_m].reshape(bs, ss, d)
    ckv_meta, kr_meta = ckv[o_m:o_m + n_meta], kr[o_m:o_m + n_meta, :MLA_ROPE]
    p_ckv = jnp.concatenate([jnp.broadcast_to(ckv_meta[None], (bp, n_meta, MLA_KV_LORA)),
                             ckv[:tp].reshape(bp, sp, MLA_KV_LORA)], axis=1)
    p_kr = jnp.concatenate([jnp.broadcast_to(kr_meta[None], (bp, n_meta, MLA_ROPE)),
                            kr[:tp, :MLA_ROPE].reshape(bp, sp, MLA_ROPE)], axis=1)
    p_gla = st_prompt.reshape(bp, GLA_HEADS, GLA_DK, GLA_DV)
    x2p = x2[:tp].reshape(bp, sp, d)
    p_pool = x2p[:, sp - (pm - 1):]
    s_ckv = ckv[o_s:o_m].reshape(bs, ss, MLA_KV_LORA)
    s_kr = kr[o_s:o_m, :MLA_ROPE].reshape(bs, ss, MLA_ROPE)
    s_gla = st_small[:bs].reshape(bs, GLA_HEADS, GLA_DK, GLA_DV)
    s_pool = x2[o_s:o_m].reshape(bs, ss, d)[:, ss - (pm - 1):]
    return (y_prompt, y_sample, p_ckv, p_kr, p_gla, p_pool, s_ckv, s_kr, s_gla, s_pool)
```

```python
import functools
import math

import jax
import jax.numpy as jnp
import numpy as np
from jax import lax
from jax.experimental import pallas as pl
from jax.experimental.pallas import tpu as pltpu

F32 = jnp.float32
BF16 = jnp.bfloat16

CHUNK = 64
DEPTH = 2
ALPHA = (2 * DEPTH) ** 0.25
LN_EPS = 1e-5
RMS_EPS = 1e-6
NEG_INF = -1e30
GLA_HEADS = 4
GLA_DK = 64
GLA_DV = 128
GLA_RANK = 16
GLA_TAU = 16.0
MLA_HEADS = 8
MLA_Q_LORA = 256
MLA_KV_LORA = 128
MLA_NOPE = 64
MLA_ROPE = 32
MLA_V = 64
MLA_SCALE = (MLA_NOPE + MLA_ROPE) ** -0.5
ROPE_THETA = 10000.0
POOL_WINDOWS = (2, 4, 8, 16)
POOL_MAX = 16
N_EXPERTS = 64
TOP_K = 8
N_GROUPS = 8
TOPK_GROUPS = 4
ROUTED_SCALE = 2.5

LANES = 128
MXU_DIM = 256

TOK_TILE = 512
GLA_BLOCK = 128
MLA_Q_TILE = 256
MLA_KV_TILE = 256
MLA_QK = 256
EXPERT_TILE = 256
POOL_TILE = 512
VMEM_LIMIT = 56 * 1024 * 1024


def _params(sem, vmem=VMEM_LIMIT):
    return pltpu.CompilerParams(dimension_semantics=sem, vmem_limit_bytes=vmem)


def _const_spec(shape):
    nd = len(shape)
    return pl.BlockSpec(shape, lambda *_: (0,) * nd)


def _split3(x):
    hi = x.astype(BF16)
    r = x - hi.astype(F32)
    mid = r.astype(BF16)
    lo = (r - mid.astype(F32)).astype(BF16)
    return hi, mid, lo


def _dot(a, b):
    return jnp.dot(a, b, preferred_element_type=F32)


def _dot_f32(a, b):
    a_hi = a.astype(BF16)
    a_lo = (a - a_hi.astype(F32)).astype(BF16)
    b_hi = b.astype(BF16)
    b_lo = (b - b_hi.astype(F32)).astype(BF16)
    return _dot(a_hi, b_hi) + (_dot(a_lo, b_hi) + _dot(a_hi, b_lo))


def _dot_f32_nt(a, b):
    nt = lambda u, v: lax.dot_general(u, v, (((1,), (1,)), ((), ())), preferred_element_type=F32)
    a_hi = a.astype(BF16)
    a_lo = (a - a_hi.astype(F32)).astype(BF16)
    b_hi = b.astype(BF16)
    b_lo = (b - b_hi.astype(F32)).astype(BF16)
    return nt(a_hi, b_hi) + (nt(a_lo, b_hi) + nt(a_hi, b_lo))


def _silu(x):
    return x * (1.0 / (1.0 + jnp.exp(-x)))


def _layer_norm(x, g, b):
    mu = jnp.mean(x, axis=-1, keepdims=True)
    xc = x - mu
    var = jnp.mean(xc * xc, axis=-1, keepdims=True)
    return xc * lax.rsqrt(var + LN_EPS) * g + b


def _rms(x, g):
    return x * lax.rsqrt(jnp.mean(x * x, axis=-1, keepdims=True) + RMS_EPS) * g


_C_Q, _C_K, _C_V, _C_R, _C_CQ, _C_CKV, _C_KR, _C_KRS, _C_A, _C_END = (
    0, 256, 512, 1024, 1536, 1792, 1920, 2048, 2176, 2304)


def _inproj_kernel(x_ref, cos_ref, sin_ref, w_ref, wg2_ref, bg_ref, qn_ref, kvn_ref, wuq_ref, wabs_ref,
                   place_ref, q_ref, k_ref, v_ref, sr_ref, lg_ref, qall_ref, ckv_ref, kr_ref, kall_ref):
    h = _dot(x_ref[...].astype(BF16), w_ref[...])
    q_ref[...] = (h[:, _C_Q:_C_K] * GLA_DK ** -0.5).astype(BF16)
    k_ref[...] = h[:, _C_K:_C_V].astype(BF16)
    v_ref[...] = h[:, _C_V:_C_R].astype(BF16)
    sr_ref[...] = _silu(h[:, _C_R:_C_CQ]).astype(BF16)
    z = _dot_f32(h[:, _C_A:_C_END], wg2_ref[...]) + bg_ref[...]
    lg_ref[...] = (jnp.minimum(z, 0.0) - jnp.log(1.0 + jnp.exp(-jnp.abs(z)))) * (1.0 / GLA_TAU)
    cqn = _rms(h[:, _C_CQ:_C_CKV], qn_ref[...]).astype(BF16)
    qh = _dot(cqn, wuq_ref[...])
    n_nope = MLA_HEADS * MLA_NOPE
    cos, sin = cos_ref[...], sin_ref[...]
    x1, x2 = qh[:, n_nope:n_nope + LANES], qh[:, n_nope + LANES:]
    rot = jnp.concatenate([x1 * cos - x2 * sin, x1 * sin + x2 * cos], axis=1).astype(BF16)
    qlat = _dot(qh[:, :n_nope].astype(BF16), wabs_ref[...]).astype(BF16)
    qrope = _dot(rot, place_ref[...]).astype(BF16)
    for hd in range(MLA_HEADS):
        qall_ref[:, MLA_QK * hd:MLA_QK * hd + LANES] = qlat[:, LANES * hd:LANES * (hd + 1)]
        qall_ref[:, MLA_QK * hd + LANES:MLA_QK * (hd + 1)] = qrope[:, LANES * hd:LANES * (hd + 1)]
    ckv = _rms(h[:, _C_CKV:_C_KR], kvn_ref[...])
    lane = lax.broadcasted_iota(jnp.int32, (1, LANES), 1)
    sgn = jnp.where(lane < MLA_ROPE // 2, -1.0, 1.0)
    kr = h[:, _C_KR:_C_KRS] * cos + h[:, _C_KRS:_C_A] * (sin * sgn)
    ckv_ref[...] = ckv
    kr_ref[...] = kr
    kall_ref[:, :LANES] = ckv.astype(BF16)
    kall_ref[:, LANES:] = kr.astype(BF16)


def _inproj(x_all, cos_t, sin_t, w):
    t = x_all.shape[0]
    d = x_all.shape[1]
    tm = TOK_TILE
    row = lambda n: pl.BlockSpec((tm, n), lambda i: (i, 0))
    consts = [w['w_in'], w['w_g2'], w['b_g'], w['q_norm'], w['kv_norm'], w['w_uq'], w['w_abs'], w['place']]
    outs = [(GLA_HEADS * GLA_DK, BF16), (GLA_HEADS * GLA_DK, BF16), (GLA_HEADS * GLA_DV, BF16),
            (GLA_HEADS * GLA_DV, BF16), (GLA_HEADS * GLA_DK, F32), (MLA_HEADS * MLA_QK, BF16),
            (LANES, F32), (LANES, F32), (MLA_QK, BF16)]
    return pl.pallas_call(
        _inproj_kernel,
        grid=(t // tm,),
        in_specs=[row(d), row(LANES), row(LANES)] + [_const_spec(c.shape) for c in consts],
        out_specs=[row(n) for n, _ in outs],
        out_shape=[jax.ShapeDtypeStruct((t, n), dt) for n, dt in outs],
        compiler_params=_params(("parallel",)),
        name="inproj",
    )(x_all, cos_t, sin_t, *consts)


def _gla_tables(c):
    levels = int(math.log2(c))
    assert 1 << levels == c
    t = np.arange(c)[:, None]
    u = np.arange(c)[None, :]
    mats, masks = [], []
    for l in range(levels):
        half = 1 << l
        base = (t // half) * half
        upper = ((t >> l) & 1) == 1
        a_q = (u >= base) & (u <= t)
        a_k = (u > t) & (u <= base + half - 1)
        mats.append(np.where(upper, a_q, a_k))
        same = (t >> (l + 1)) == (u >> (l + 1))
        masks.append(same & upper & (((u >> l) & 1) == 0))
    mats.append(u <= t)
    mats.append(u > t)
    masks.append(t == u)
    amat = np.concatenate(mats, axis=0).astype(np.float32)
    mask = np.stack([np.tile(m, (GLA_HEADS, 1)) for m in masks]).astype(np.float32)
    return jnp.asarray(amat, BF16), jnp.asarray(mask, F32), levels


def _gla_kernel(q_ref, k_ref, v_ref, lg_ref, sr_ref, s0_ref, amat_ref, mask_ref, g_ref,
                o_ref, sout_ref, state, *, levels):
    c = q_ref.shape[0]
    hk = GLA_HEADS * GLA_DK
    step = pl.program_id(1)

    @pl.when(step == 0)
    def _():
        state[...] = s0_ref[0]

    lg = lg_ref[...]
    parts = _split3(lg)
    e3 = _dot(amat_ref[...], jnp.concatenate(parts, axis=1))
    ex = jnp.exp(e3[:, :hk] + e3[:, hk:2 * hk] + e3[:, 2 * hk:])
    q = q_ref[...].astype(F32)
    k = k_ref[...].astype(F32)
    v = v_ref[...]
    head = lax.broadcasted_iota(jnp.int32, (1, hk), 1) // GLA_DK

    def per_head(xf):
        return jnp.concatenate([jnp.where(head == h, xf, 0.0) for h in range(GLA_HEADS)], axis=0).astype(BF16)

    att = jnp.zeros((GLA_HEADS * c, c), F32)
    for l in range(levels + 1):
        if l < levels:
            el = ex[l * c:(l + 1) * c]
            ql, kl = q * el, k * el
        else:
            ql, kl = q, k
        a = lax.dot_general(per_head(ql), kl.astype(BF16), (((1,), (1,)), ((), ())), preferred_element_type=F32)
        att = att + a * mask_ref[l]
    s_prev = state[...]
    o_inter = _dot(per_head(q * ex[levels * c:(levels + 1) * c]), s_prev.astype(BF16))
    att = att.astype(BF16)
    g = g_ref[...]
    for h in range(GLA_HEADS):
        o = o_inter[h * c:(h + 1) * c] + _dot(att[h * c:(h + 1) * c], v[:, GLA_DV * h:GLA_DV * (h + 1)])
        o = _rms(o, g)
        o_ref[:, GLA_DV * h:GLA_DV * (h + 1)] = (o * sr_ref[:, GLA_DV * h:GLA_DV * (h + 1)].astype(F32)).astype(BF16)
    kr = (k * ex[(levels + 1) * c:]).astype(BF16)
    upd = lax.dot_general(kr, v, (((0,), (0,)), ((), ())), preferred_element_type=F32)
    ones = jnp.ones((c, GLA_DV), BF16)
    b_last = sum(lax.dot_general(p, ones, (((0,), (0,)), ((), ())), preferred_element_type=F32) for p in parts)
    new = jnp.exp(b_last) * s_prev + jnp.concatenate(
        [upd[GLA_DK * h:GLA_DK * (h + 1), GLA_DV * h:GLA_DV * (h + 1)] for h in range(GLA_HEADS)], axis=0)
    state[...] = new

    @pl.when(step == pl.num_programs(1) - 1)
    def _():
        sout_ref[0] = new


def _gla(q, k, v, lg, sr, s0, g, tables, batch, blocks_per_seq):
    amat, mask, levels = tables
    c = GLA_BLOCK
    hk, hv = GLA_HEADS * GLA_DK, GLA_HEADS * GLA_DV
    row = lambda n: pl.BlockSpec((c, n), lambda b, s: (b * blocks_per_seq + s, 0))
    st = pl.BlockSpec((1, hk, GLA_DV), lambda b, s: (b, 0, 0))
    return pl.pallas_call(
        functools.partial(_gla_kernel, levels=levels),
        grid=(batch, blocks_per_seq),
        in_specs=[row(hk), row(hk), row(hv), row(hk), row(hv), st,
                  _const_spec(amat.shape), _const_spec(mask.shape), _const_spec(g.shape)],
        out_specs=[row(hv), st],
        out_shape=[jax.ShapeDtypeStruct((batch * blocks_per_seq * c, hv), BF16),
                   jax.ShapeDtypeStruct((batch, hk, GLA_DV), F32)],
        scratch_shapes=[pltpu.VMEM((hk, GLA_DV), F32)],
        compiler_params=_params(("parallel", "arbitrary")),
        name="gla",
    )(q, k, v, lg, sr, s0, amat, mask, g)


def _mla_kernel(q_ref, kt_ref, kv_ref, o_ref, m_sc, l_sc, acc_sc, *, tq, n_past, past_valid, own_valid, diag_chunks):
    kvb = MLA_KV_TILE
    qi = pl.program_id(1)
    qs = jnp.concatenate([q_ref[:, MLA_QK * h:MLA_QK * (h + 1)] for h in range(MLA_HEADS)], axis=0)
    m_rows = MLA_HEADS * tq
    m_sc[...] = jnp.full(m_sc.shape, -jnp.inf, F32)
    l_sc[...] = jnp.zeros(l_sc.shape, F32)
    acc_sc[...] = jnp.zeros(acc_sc.shape, F32)

    def block(j, mask):
        s = _dot(qs, kt_ref[0, j]) * MLA_SCALE
        if mask is not None:
            s = jnp.where(mask, s, NEG_INF)
        m_old = m_sc[...]
        m_new = jnp.maximum(m_old, jnp.max(s, axis=1, keepdims=True))
        alpha = jnp.exp(m_old - m_new)
        p = jnp.exp(s - m_new)
        l_sc[...] = alpha * l_sc[...] + jnp.sum(p, axis=1, keepdims=True)
        vblk = kv_ref[0, pl.ds(pl.multiple_of(j * kvb, kvb), kvb), :MLA_KV_LORA]
        acc_sc[...] = alpha * acc_sc[...] + _dot(p.astype(BF16), vblk)
        m_sc[...] = m_new

    def full_blocks(lo, hi):
        def body(j, carry):
            block(j, None)
            return carry
        lax.fori_loop(lo, hi, body, 0)

    col = lax.broadcasted_iota(jnp.int32, (m_rows, kvb), 1)
    n_past_full = past_valid // kvb
    if n_past_full:
        full_blocks(0, n_past_full)
    if n_past_full < n_past:
        block(n_past_full, col < past_valid - n_past_full * kvb)
    full_blocks(n_past, n_past + qi)
    mask = col < own_valid - qi * kvb
    if diag_chunks:
        qrow = lax.broadcasted_iota(jnp.int32, (m_rows, kvb), 0) % tq
        mask = mask & (col // CHUNK <= qrow // CHUNK)
    block(n_past + qi, mask)
    o = acc_sc[...] / l_sc[...]
    for h in range(MLA_HEADS):
        o_ref[:, MLA_KV_LORA * h:MLA_KV_LORA * (h + 1)] = o[h * tq:(h + 1) * tq].astype(BF16)


def _mla(q_all, kt, kv, batch, nq, tq, q_block0, past_valid, own_valid):
    n_kv = kt.shape[1]
    n_past = -(-past_valid // MLA_KV_TILE)
    assert nq == 1 or tq == MLA_KV_TILE
    assert n_kv == n_past + nq
    return pl.pallas_call(
        functools.partial(_mla_kernel, tq=tq, n_past=n_past, past_valid=past_valid, own_valid=own_valid,
                          diag_chunks=nq > 1),
        grid=(batch, nq),
        in_specs=[pl.BlockSpec((tq, MLA_HEADS * MLA_QK), lambda b, i: (q_block0 + b * nq + i, 0)),
                  pl.BlockSpec((1,) + kt.shape[1:], lambda b, i: (b, 0, 0, 0)),
                  pl.BlockSpec((1,) + kv.shape[1:], lambda b, i: (b, 0, 0))],
        out_specs=pl.BlockSpec((tq, MLA_HEADS * MLA_KV_LORA), lambda b, i: (b * nq + i, 0)),
        out_shape=jax.ShapeDtypeStruct((batch * nq * tq, MLA_HEADS * MLA_KV_LORA), BF16),
        scratch_shapes=[pltpu.VMEM((MLA_HEADS * tq, 1), F32), pltpu.VMEM((MLA_HEADS * tq, 1), F32),
                        pltpu.VMEM((MLA_HEADS * tq, MLA_KV_LORA), F32)],
        compiler_params=_params(("parallel", "arbitrary")),
        name="mla",
    )(q_all, kt, kv)


def _post_mixer(x, out, g, b, rw_ref, x1_ref, x1b_ref, lgt_ref):
    y = _layer_norm(ALPHA * x + out, g, b)
    x1_ref[...] = y
    x1b_ref[...] = y.astype(BF16)
    lgt_ref[0] = _dot_f32_nt(rw_ref[...], y)


def _outproj_kernel(og_ref, ol_ref, x_ref, wuv_ref, wo_ref, g_ref, b_ref, rw_ref, x1_ref, x1b_ref, lgt_ref):
    o_mla = _dot(ol_ref[...], wuv_ref[...]).astype(BF16)
    n_gla = og_ref.shape[1]
    out = _dot(og_ref[...], wo_ref[:n_gla, :]) + _dot(o_mla, wo_ref[n_gla:, :])
    _post_mixer(x_ref[...], out, g_ref[...], b_ref[...], rw_ref, x1_ref, x1b_ref, lgt_ref)


def _outproj(o_gla, o_lat, x_all, w_uvbd, w_out, g, b, rw):
    t, d = x_all.shape
    tm = TOK_TILE
    row = lambda n: pl.BlockSpec((tm, n), lambda i: (i, 0))
    consts = [w_uvbd, w_out, g, b, rw]
    return pl.pallas_call(
        _outproj_kernel,
        grid=(t // tm,),
        in_specs=[row(o_gla.shape[1]), row(o_lat.shape[1]), row(d)] + [_const_spec(c.shape) for c in consts],
        out_specs=[row(d), row(d), pl.BlockSpec((1, N_EXPERTS, tm), lambda i: (i, 0, 0))],
        out_shape=[jax.ShapeDtypeStruct((t, d), F32), jax.ShapeDtypeStruct((t, d), BF16),
                   jax.ShapeDtypeStruct((t // tm, N_EXPERTS, tm), F32)],
        compiler_params=_params(("parallel",)),
        name="outproj",
    )(o_gla, o_lat, x_all, *consts)


def _pool_kernel(x_ref, prev_ref, hist_ref, pw_ref, ps_ref, g_ref, b_ref, rw_ref, x1_ref, x1b_ref, lgt_ref):
    ts = x_ref.shape[0]
    pm = POOL_MAX
    x = x_ref[...]
    prev = jnp.where(pl.program_id(1) == 0, hist_ref[0], prev_ref[...])
    xc = jnp.concatenate([prev, x], axis=0)
    grp = x.shape[1] // len(POOL_WINDOWS)
    outs = []
    for gi, w in enumerate(POOL_WINDOWS):
        s = xc[:, gi * grp:(gi + 1) * grp]
        span = 1
        while span < w:
            s = s + pltpu.roll(s, span, 0)
            span *= 2
        win = s[pm:]
        mix = (win * (1.0 / w) - x[:, gi * grp:(gi + 1) * grp]).astype(BF16)
        outs.append(_dot(mix, pw_ref[gi]))
    out = jnp.concatenate(outs, axis=1) * ps_ref[...]
    _post_mixer(x, out, g_ref[...], b_ref[...], rw_ref, x1_ref, x1b_ref, lgt_ref)


def _pool(x_all, hist, pool_w, pool_scale, g, b, rw, batch, tiles_per_seq, ts, tile0):
    d = x_all.shape[1]
    pm = POOL_MAX
    per = ts // pm
    consts = [pool_w, pool_scale, g, b, rw]
    row = lambda n: pl.BlockSpec((ts, n), lambda bb, i: (bb * tiles_per_seq + i, 0))
    return pl.pallas_call(
        _pool_kernel,
        grid=(batch, tiles_per_seq),
        in_specs=[pl.BlockSpec((ts, d), lambda bb, i: (tile0 + bb * tiles_per_seq + i, 0)),
                  pl.BlockSpec((pm, d), lambda bb, i: (jnp.maximum((tile0 + bb * tiles_per_seq + i) * per - 1, 0), 0)),
                  pl.BlockSpec((1, pm, d), lambda bb, i: (bb, 0, 0))] + [_const_spec(c.shape) for c in consts],
        out_specs=[row(d), row(d), pl.BlockSpec((1, N_EXPERTS, ts), lambda bb, i: (bb * tiles_per_seq + i, 0, 0))],
        out_shape=[jax.ShapeDtypeStruct((batch * tiles_per_seq * ts, d), F32),
                   jax.ShapeDtypeStruct((batch * tiles_per_seq * ts, d), BF16),
                   jax.ShapeDtypeStruct((batch * tiles_per_seq, N_EXPERTS, ts), F32)],
        compiler_params=_params(("parallel", "arbitrary")),
        name="pool",
    )(x_all, x_all, hist, *consts)


def _experts_kernel(be_ref, nb_ref, x_ref, wg_ref, wu_ref, wd_ref, y_ref):
    @pl.when(pl.program_id(0) < nb_ref[0])
    def _():
        x = x_ref[...]
        hmid = (_silu(_dot(x, wg_ref[0])) * _dot(x, wu_ref[0])).astype(BF16)
        y_ref[...] = _dot(hmid, wd_ref[0]).astype(BF16)

    @pl.when(pl.program_id(0) >= nb_ref[0])
    def _():
        y_ref[...] = jnp.zeros_like(y_ref)


def _experts(blk_e, n_used, x_sorted, wg, wu, wd):
    p, d = x_sorted.shape
    f = wg.shape[2]
    bm = EXPERT_TILE
    grid_spec = pltpu.PrefetchScalarGridSpec(
        num_scalar_prefetch=2,
        grid=(p // bm,),
        in_specs=[pl.BlockSpec((bm, d), lambda i, be, nb: (i, 0)),
                  pl.BlockSpec((1, d, f), lambda i, be, nb: (be[i], 0, 0)),
                  pl.BlockSpec((1, d, f), lambda i, be, nb: (be[i], 0, 0)),
                  pl.BlockSpec((1, f, d), lambda i, be, nb: (be[i], 0, 0))],
        out_specs=pl.BlockSpec((bm, d), lambda i, be, nb: (i, 0)),
    )
    return pl.pallas_call(
        _experts_kernel,
        grid_spec=grid_spec,
        out_shape=jax.ShapeDtypeStruct((p, d), BF16),
        compiler_params=_params(("arbitrary",)),
        name="experts",
    )(blk_e, n_used, x_sorted, wg, wu, wd)


def _ffn_out_kernel(x_ref, xb_ref, routed_ref, sg_ref, su_ref, sd_ref, g_ref, b_ref, y_ref, yb_ref):
    xb = xb_ref[...]
    hmid = (_silu(_dot(xb, sg_ref[...])) * _dot(xb, su_ref[...])).astype(BF16)
    ffn = routed_ref[...] + _dot(hmid, sd_ref[...])
    y = _layer_norm(ALPHA * x_ref[...] + ffn, g_ref[...], b_ref[...])
    y_ref[...] = y
    yb_ref[...] = y.astype(BF16)


def _ffn_out(x1, x1b, routed, sg, su, sd, g, b):
    t, d = x1.shape
    tm = TOK_TILE
    row = lambda n: pl.BlockSpec((tm, n), lambda i: (i, 0))
    consts = [sg, su, sd, g, b]
    return pl.pallas_call(
        _ffn_out_kernel,
        grid=(t // tm,),
        in_specs=[row(d), row(d), row(d)] + [_const_spec(c.shape) for c in consts],
        out_specs=[row(d), row(d)],
        out_shape=[jax.ShapeDtypeStruct((t, d), F32), jax.ShapeDtypeStruct((t, d), BF16)],
        compiler_params=_params(("parallel",)),
        name="ffn_out",
    )(x1, x1b, routed, *consts)


def _outranked(vals, n, ids):
    rank = jnp.zeros(vals.shape, jnp.int32)
    for j in range(n):
        row = vals[j:j + 1]
        rank = rank + ((row > vals) | ((row == vals) & (j < ids))).astype(jnp.int32)
    return rank


def _router_kernel(lg_ref, b_ref, tri_ref, low_ref, idx_ref, pos_ref, w_ref, cnt_ref, carry):
    @pl.when(pl.program_id(0) == 0)
    def _():
        carry[...] = jnp.zeros_like(carry)

    tm = lg_ref.shape[2]
    gs = N_EXPERTS // N_GROUPS
    scores = 1.0 / (1.0 + jnp.exp(-lg_ref[0]))
    biased = scores + b_ref[...]
    member = lax.broadcasted_iota(jnp.int32, (gs, tm), 0)
    group_score = []
    for g in range(N_GROUPS):
        blk = biased[g * gs:(g + 1) * gs]
        m1 = jnp.max(blk, axis=0, keepdims=True)
        first = jnp.min(jnp.where(blk == m1, member, gs), axis=0, keepdims=True)
        m2 = jnp.max(jnp.where(member == first, -jnp.inf, blk), axis=0, keepdims=True)
        group_score.append(m1 + m2)
    group_score = jnp.concatenate(group_score, axis=0)
    gid = lax.broadcasted_iota(jnp.int32, (N_GROUPS, tm), 0)
    group_ok = _outranked(group_score, N_GROUPS, gid) < TOPK_GROUPS
    masked = jnp.concatenate([jnp.where(group_ok[g:g + 1], biased[g * gs:(g + 1) * gs], -jnp.inf)
                              for g in range(N_GROUPS)], axis=0)
    eid = lax.broadcasted_iota(jnp.int32, (N_EXPERTS, tm), 0)
    sel = _outranked(masked, N_EXPERTS, eid) < TOP_K
    self = jnp.where(sel, 1.0, 0.0)
    wsel = self * scores
    wts = wsel / jnp.sum(wsel, axis=0, keepdims=True) * ROUTED_SCALE
    selb = self.astype(BF16)
    pos = carry[...] + _dot(selb, tri_ref[...])
    carry[...] = carry[...] + jnp.sum(self, axis=1, keepdims=True)
    cnt_ref[...] = carry[...]
    ordinal = _dot(low_ref[...], selb)
    eidf = eid.astype(F32)
    idx, posk, wk = [], [], []
    for kk in range(TOP_K):
        hit = sel & (ordinal == kk)
        idx.append(jnp.sum(jnp.where(hit, eidf, 0.0), axis=0, keepdims=True))
        posk.append(jnp.sum(jnp.where(hit, pos, 0.0), axis=0, keepdims=True))
        wk.append(jnp.sum(jnp.where(hit, wts, 0.0), axis=0, keepdims=True))
    idx_ref[...] = jnp.concatenate(idx, axis=0).astype(jnp.int32)
    pos_ref[...] = jnp.concatenate(posk, axis=0).astype(jnp.int32)
    w_ref[...] = jnp.concatenate(wk, axis=0)


def _router(logits, router_b):
    nt, ne, tm = logits.shape
    t = nt * tm
    tri = jnp.asarray(np.triu(np.ones((tm, tm), np.float32), 1), BF16)
    low = jnp.asarray(np.tril(np.ones((ne, ne), np.float32), -1), BF16)
    col = pl.BlockSpec((TOP_K, tm), lambda i: (0, i))
    return pl.pallas_call(
        _router_kernel,
        grid=(nt,),
        in_specs=[pl.BlockSpec((1, ne, tm), lambda i: (i, 0, 0)), _const_spec((ne, 1)),
                  _const_spec(tri.shape), _const_spec(low.shape)],
        out_specs=[col, col, col, _const_spec((ne, 1))],
        out_shape=[jax.ShapeDtypeStruct((TOP_K, t), jnp.int32), jax.ShapeDtypeStruct((TOP_K, t), jnp.int32),
                   jax.ShapeDtypeStruct((TOP_K, t), F32), jax.ShapeDtypeStruct((ne, 1), F32)],
        scratch_shapes=[pltpu.VMEM((ne, 1), F32)],
        compiler_params=_params(("arbitrary",)),
        name="router",
    )(logits, router_b.reshape(ne, 1).astype(F32), tri, low)


def _moe(x1, x1b, logits, router_b, wg, wu, wd, sg, su, sd, g, b):
    t, d = x1.shape
    idx, pos, wts, cnt = _router(logits, router_b)
    a = t * TOP_K
    bm = EXPERT_TILE
    counts = cnt[:, 0].astype(jnp.int32)
    padded = (counts + bm - 1) // bm * bm
    pad_end = jnp.cumsum(padded)
    pad_start = pad_end - padded
    experts = jnp.arange(N_EXPERTS, dtype=jnp.int32)
    dest = pos + jnp.sum(jnp.where(idx[:, :, None] == experts, pad_start, 0), axis=-1)
    n_blocks = -(-a // bm) + N_EXPERTS
    tok = jnp.broadcast_to(jnp.arange(t, dtype=jnp.int32), (TOP_K, t))
    tok_buf = jnp.zeros((n_blocks * bm,), jnp.int32).at[dest.reshape(a)].set(tok.reshape(a), unique_indices=True)
    blk_start = jnp.arange(n_blocks, dtype=jnp.int32) * bm
    blk_e = jnp.minimum(jnp.sum((pad_end[None, :] <= blk_start[:, None]).astype(jnp.int32), axis=1), N_EXPERTS - 1)
    n_used = (pad_end[-1] // bm).astype(jnp.int32).reshape(1)
    x_sorted = jnp.take(x1b, tok_buf, axis=0)
    y_sorted = _experts(blk_e, n_used, x_sorted, wg, wu, wd)
    y_tok = jnp.take(y_sorted, dest.reshape(a), axis=0).reshape(TOP_K, t, d).astype(F32)
    routed = jnp.sum(y_tok * wts[:, :, None], axis=0)
    return _ffn_out(x1, x1b, routed, sg, su, sd, g, b)


def _rope_tables(pos):
    half = MLA_ROPE // 2
    inv = ROPE_THETA ** (-jnp.arange(half, dtype=F32) / half)
    ang = pos.astype(F32)[:, None] * inv
    reps = LANES // half
    return jnp.tile(jnp.cos(ang), (1, reps)), jnp.tile(jnp.sin(ang), (1, reps))


def _pack_layer0_weights(w_in0, gla_w_g2, gla_b_g, mla_q_norm_g, mla_kv_norm_g, mla_w_uq, mla_w_uk, mla_w_uv):
    d = w_in0.shape[0]
    hk, hv = GLA_HEADS * GLA_DK, GLA_HEADS * GLA_DV
    o_q, o_k, o_v, o_r = 0, hk, 2 * hk, 2 * hk + hv
    o_a = o_r + hv
    o_cq = o_a + GLA_RANK
    o_ckv = o_cq + MLA_Q_LORA
    o_kr = o_ckv + MLA_KV_LORA
    half = MLA_ROPE // 2
    zeros = lambda n: jnp.zeros((d, n), w_in0.dtype)
    kr1, kr2 = w_in0[:, o_kr:o_kr + half], w_in0[:, o_kr + half:o_kr + MLA_ROPE]
    w_in = jnp.concatenate([
        w_in0[:, o_q:o_a], w_in0[:, o_cq:o_kr],
        kr1, kr2, zeros(LANES - MLA_ROPE),
        kr2, kr1, zeros(LANES - MLA_ROPE),
        w_in0[:, o_a:o_cq], zeros(LANES - GLA_RANK)], axis=1).astype(BF16)
    assert w_in.shape[1] == _C_END
    w_g2 = jnp.concatenate([gla_w_g2, jnp.zeros((LANES - GLA_RANK, hk), gla_w_g2.dtype)], axis=0)
    uq = mla_w_uq.reshape(MLA_Q_LORA, MLA_HEADS, MLA_NOPE + MLA_ROPE)
    w_uq = jnp.concatenate([uq[:, :, :MLA_NOPE].reshape(MLA_Q_LORA, -1),
                            uq[:, :, MLA_NOPE:MLA_NOPE + half].reshape(MLA_Q_LORA, -1),
                            uq[:, :, MLA_NOPE + half:].reshape(MLA_Q_LORA, -1)], axis=1).astype(BF16)
    eye = jnp.eye(MLA_HEADS, dtype=mla_w_uk.dtype)
    w_abs = jnp.einsum('chn,hg->hngc', mla_w_uk, eye).reshape(MLA_HEADS * MLA_NOPE, MLA_HEADS * MLA_KV_LORA).astype(BF16)
    w_uvbd = jnp.einsum('chv,hg->hcgv', mla_w_uv, eye).reshape(MLA_HEADS * MLA_KV_LORA, MLA_HEADS * MLA_V).astype(BF16)
    place = np.zeros((2 * LANES, MLA_HEADS * LANES), np.float32)
    for h in range(MLA_HEADS):
        for j in range(half):
            place[h * half + j, h * LANES + j] = 1.0
            place[LANES + h * half + j, h * LANES + half + j] = 1.0
    return dict(w_in=w_in, w_g2=w_g2, b_g=gla_b_g.reshape(1, hk), q_norm=mla_q_norm_g.reshape(1, -1),
                kv_norm=mla_kv_norm_g.reshape(1, -1), w_uq=w_uq, w_abs=w_abs,
                place=jnp.asarray(place, BF16)), w_uvbd


def _pad_rows(x, n):
    return jnp.pad(x, ((0, n - x.shape[0]),) + ((0, 0),) * (x.ndim - 1))


def _kv_tiles(past, own):
    kvb = MLA_KV_TILE
    pad = lambda a: jnp.pad(a, ((0, 0), (0, -a.shape[1] % kvb), (0, 0)))
    kv = jnp.concatenate([pad(past), pad(own)], axis=1) if past.shape[1] else pad(own)
    b, n, w = kv.shape
    kt = kv.reshape(b, n // kvb, kvb, w).transpose(0, 1, 3, 2)
    return kt, kv


def kernel(x_prompt, x_sample, cache_mla_ckv, cache_mla_krope, state_gla, cache_pool, meta_tokens, w_in0, gla_w_g2, gla_b_g, gla_norm_g, mla_q_norm_g, mla_kv_norm_g, mla_w_uq, mla_w_uk, mla_w_uv, w_out0, pool_w, pool_scale, ln_g, ln_b, moe_router_w, moe_router_b, moe_w_gate, moe_w_up, moe_w_down, moe_sh_gate, moe_sh_up, moe_sh_down):
    bp, sp, d = x_prompt.shape
    bs, ss, _ = x_sample.shape
    n_meta = meta_tokens.shape[0]
    past_len = cache_mla_ckv.shape[1] - n_meta
    tp, tsm = bp * sp, bs * ss
    assert sp % MLA_Q_TILE == 0 and sp % POOL_TILE == 0 and sp % GLA_BLOCK == 0 and sp % CHUNK == 0
    assert ss <= CHUNK and past_len % CHUNK == 0 and n_meta <= CHUNK and ss <= GLA_BLOCK and n_meta <= GLA_BLOCK
    assert n_meta == POOL_MAX and ss >= POOL_MAX and tp % TOK_TILE == 0
    t_all = -(-(tp + tsm + n_meta) // TOK_TILE) * TOK_TILE
    o_s, o_m = tp, tp + tsm

    x_all = _pad_rows(jnp.concatenate([x_prompt.reshape(tp, d), x_sample.reshape(tsm, d), meta_tokens], axis=0), t_all)
    pos = jnp.concatenate([jnp.tile(n_meta + jnp.arange(sp), bp), jnp.tile(n_meta + past_len + jnp.arange(ss), bs),
                           jnp.arange(n_meta), jnp.zeros((t_all - o_m - n_meta,), jnp.int32)])
    cos_t, sin_t = _rope_tables(pos)
    w0, w_uvbd = _pack_layer0_weights(w_in0, gla_w_g2, gla_b_g, mla_q_norm_g, mla_kv_norm_g, mla_w_uq, mla_w_uk, mla_w_uv)

    q, k, v, sr, lg, q_all, ckv, kr, k_all = _inproj(x_all, cos_t, sin_t, w0)

    tables = _gla_tables(GLA_BLOCK)
    g_gla = gla_norm_g.reshape(1, GLA_DV)
    c = GLA_BLOCK
    hk = GLA_HEADS * GLA_DK

    def small(a):
        sm = jnp.pad(a[o_s:o_m].reshape(bs, ss, -1), ((0, 0), (0, c - ss), (0, 0)))
        me = jnp.pad(a[o_m:o_m + n_meta], ((0, c - n_meta), (0, 0)))[None]
        return jnp.concatenate([sm, me], axis=0).reshape((bs + 1) * c, -1)

    s0_small = jnp.concatenate([state_gla.reshape(bs, hk, GLA_DV), jnp.zeros((1, hk, GLA_DV), F32)], axis=0)
    og_small, st_small = _gla(small(q), small(k), small(v), small(lg), small(sr), s0_small, g_gla, tables, bs + 1, 1)
    s0_prompt = jnp.broadcast_to(st_small[bs:], (bp, hk, GLA_DV))
    og_prompt, st_prompt = _gla(q, k, v, lg, sr, s0_prompt, g_gla, tables, bp, sp // c)
    og_small = og_small.reshape(bs + 1, c, -1)
    o_gla = _pad_rows(jnp.concatenate([og_prompt, og_small[:bs, :ss].reshape(tsm, -1), og_small[bs, :n_meta]], axis=0), t_all)

    k_meta = k_all[o_m:o_m + n_meta]
    kt_m, kv_m = _kv_tiles(jnp.zeros((1, 0, MLA_QK), BF16), k_meta[None])
    ol_meta = _mla(q_all[o_m:o_m + n_meta], kt_m, kv_m, 1, 1, n_meta, 0, 0, n_meta)
    cache = jnp.concatenate([cache_mla_ckv, cache_mla_krope,
                             jnp.zeros(cache_mla_ckv.shape[:2] + (MLA_QK - MLA_KV_LORA - MLA_ROPE,), F32)], axis=-1).astype(BF16)
    kt_s, kv_s = _kv_tiles(cache, k_all[o_s:o_m].reshape(bs, ss, MLA_QK))
    ol_sample = _mla(q_all[o_s:o_m], kt_s, kv_s, bs, 1, ss, 0, n_meta + past_len, ss)
    kt_p, kv_p = _kv_tiles(jnp.broadcast_to(k_meta[None], (bp, n_meta, MLA_QK)), k_all[:tp].reshape(bp, sp, MLA_QK))
    ol_prompt = _mla(q_all, kt_p, kv_p, bp, sp // MLA_Q_TILE, MLA_Q_TILE, 0, n_meta, sp)
    o_lat = _pad_rows(jnp.concatenate([ol_prompt, ol_sample, ol_meta], axis=0), t_all)

    ln = lambda l, j: (ln_g[l, j].reshape(1, d), ln_b[l, j].reshape(1, d))
    rw = lambda l: moe_router_w[l].T
    moe_w = lambda l: (moe_w_gate[l].astype(BF16), moe_w_up[l].astype(BF16), moe_w_down[l].astype(BF16),
                       moe_sh_gate[l].astype(BF16), moe_sh_up[l].astype(BF16), moe_sh_down[l].astype(BF16))
    x1, x1b, logits = _outproj(o_gla, o_lat, x_all, w_uvbd, w_out0.astype(BF16), *ln(0, 0), rw(0))
    x2, _ = _moe(x1, x1b, logits, moe_router_b[0], *moe_w(0), *ln(0, 1))

    pm = POOL_MAX
    zrow = jnp.zeros((1, d), F32)
    hist_p = jnp.broadcast_to(jnp.concatenate([zrow, x2[o_m + n_meta - (pm - 1):o_m + n_meta]], axis=0)[None], (bp, pm, d))
    hist_s = jnp.concatenate([jnp.zeros((bs, 1, d), F32), cache_pool], axis=1)
    pool_wb = pool_w.astype(BF16)
    ps = pool_scale.reshape(1, d)
    y1p = _pool(x2, hist_p, pool_wb, ps, *ln(1, 0), rw(1), bp, sp // POOL_TILE, POOL_TILE, 0)
    y1s = _pool(x2[o_s:o_m], hist_s, pool_wb, ps, *ln(1, 0), rw(1), bs, 1, ss, 0)
    t1 = -(-(tp + tsm) // TOK_TILE) * TOK_TILE
    x3, x3b = [_pad_rows(jnp.concatenate([a, b_], axis=0), t1) for a, b_ in zip(y1p[:2], y1s[:2])]
    by_token = lambda lt: lt.transpose(1, 0, 2).reshape(N_EXPERTS, -1)
    logits1 = jnp.pad(jnp.concatenate([by_token(y1p[2]), by_token(y1s[2])], axis=1), ((0, 0), (0, t1 - tp - tsm)))
    logits1 = logits1.reshape(N_EXPERTS, t1 // TOK_TILE, TOK_TILE).transpose(1, 0, 2)
    x4, _ = _moe(x3, x3b, logits1, moe_router_b[1], *moe_w(1), *ln(1, 1))

    y_prompt = x4[:tp].reshape(bp, sp, d)
    y_sample = x4[o_s:o_m].reshape(bs, ss, d)
    ckv_meta, kr_meta = ckv[o_m:o_m + n_meta], kr[o_m:o_m + n_meta, :MLA_ROPE]
    p_ckv = jnp.concatenate([jnp.broadcast_to(ckv_meta[None], (bp, n_meta, MLA_KV_LORA)),
                             ckv[:tp].reshape(bp, sp, MLA_KV_LORA)], axis=1)
    p_kr = jnp.concatenate([jnp.broadcast_to(kr_meta[None], (bp, n_meta, MLA_ROPE)),
                            kr[:tp, :MLA_ROPE].reshape(bp, sp, MLA_ROPE)], axis=1)
    p_gla = st_prompt.reshape(bp, GLA_HEADS, GLA_DK, GLA_DV)
    x2p = x2[:tp].reshape(bp, sp, d)
    p_pool = x2p[:, sp - (pm - 1):]
    s_ckv = ckv[o_s:o_m].reshape(bs, ss, MLA_KV_LORA)
    s_kr = kr[o_s:o_m, :MLA_ROPE].reshape(bs, ss, MLA_ROPE)
    s_gla = st_small[:bs].reshape(bs, GLA_HEADS, GLA_DK, GLA_DV)
    s_pool = x2[o_s:o_m].reshape(bs, ss, d)[:, ss - (pm - 1):]
    return (y_prompt, y_sample, p_ckv, p_kr, p_gla, p_pool, s_ckv, s_kr, s_gla, s_pool)
```

```python
import functools
import math

import jax
import jax.numpy as jnp
import numpy as np
from jax import lax
from jax.experimental import pallas as pl
from jax.experimental.pallas import tpu as pltpu

F32 = jnp.float32
BF16 = jnp.bfloat16

CHUNK = 64
DEPTH = 2
ALPHA = (2 * DEPTH) ** 0.25
LN_EPS = 1e-5
RMS_EPS = 1e-6
NEG_INF = -1e30
GLA_HEADS = 4
GLA_DK = 64
GLA_DV = 128
GLA_RANK = 16
GLA_TAU = 16.0
MLA_HEADS = 8
MLA_Q_LORA = 256
MLA_KV_LORA = 128
MLA_NOPE = 64
MLA_ROPE = 32
MLA_V = 64
MLA_SCALE = (MLA_NOPE + MLA_ROPE) ** -0.5
ROPE_THETA = 10000.0
POOL_WINDOWS = (2, 4, 8, 16)
POOL_MAX = 16
N_EXPERTS = 64
TOP_K = 8
N_GROUPS = 8
TOPK_GROUPS = 4
ROUTED_SCALE = 2.5

LANES = 128
MXU_DIM = 256

TOK_TILE = 512
GLA_BLOCK = 128
MLA_Q_TILE = 256
MLA_KV_TILE = 256
MLA_QK = 256
MLA_ROW_STRIP = 128
LOG2E = math.log2(math.e)
EXPERT_TILE = 512
EXPERT_SUBTILE = 256
POOL_TILE = 512
FFN_TILE = 256
VMEM_LIMIT = 56 * 1024 * 1024


def _params(sem, vmem=VMEM_LIMIT):
    return pltpu.CompilerParams(dimension_semantics=sem, vmem_limit_bytes=vmem)


def _const_spec(shape):
    nd = len(shape)
    return pl.BlockSpec(shape, lambda *_: (0,) * nd)


def _split3(x):
    hi = x.astype(BF16)
    r = x - hi.astype(F32)
    mid = r.astype(BF16)
    lo = (r - mid.astype(F32)).astype(BF16)
    return hi, mid, lo


def _dot(a, b):
    return jnp.dot(a, b, preferred_element_type=F32)


def _dot_f32(a, b):
    a_hi = a.astype(BF16)
    a_lo = (a - a_hi.astype(F32)).astype(BF16)
    b_hi = b.astype(BF16)
    b_lo = (b - b_hi.astype(F32)).astype(BF16)
    return _dot(a_hi, b_hi) + (_dot(a_lo, b_hi) + _dot(a_hi, b_lo))


def _dot_f32_nt(a, b):
    nt = lambda u, v: lax.dot_general(u, v, (((1,), (1,)), ((), ())), preferred_element_type=F32)
    a_hi = a.astype(BF16)
    a_lo = (a - a_hi.astype(F32)).astype(BF16)
    b_hi = b.astype(BF16)
    b_lo = (b - b_hi.astype(F32)).astype(BF16)
    return nt(a_hi, b_hi) + (nt(a_lo, b_hi) + nt(a_hi, b_lo))


def _silu(x):
    return x * (1.0 / (1.0 + jnp.exp(-x)))


def _layer_norm(x, g, b):
    mu = jnp.mean(x, axis=-1, keepdims=True)
    xc = x - mu
    var = jnp.mean(xc * xc, axis=-1, keepdims=True)
    return xc * lax.rsqrt(var + LN_EPS) * g + b


def _rms(x, g):
    return x * lax.rsqrt(jnp.mean(x * x, axis=-1, keepdims=True) + RMS_EPS) * g


_C_Q, _C_K, _C_V, _C_R, _C_CQ, _C_CKV, _C_KR, _C_KRS, _C_A, _C_END = (
    0, 256, 512, 1024, 1536, 1792, 1920, 2048, 2176, 2304)


def _inproj_kernel(x_ref, cos_ref, sin_ref, w_ref, wg2_ref, bg_ref, qn_ref, kvn_ref, wuq_ref, wabs_ref,
                   place_ref, q_ref, k_ref, v_ref, sr_ref, lg_ref, qall_ref, ckv_ref, kr_ref, kall_ref):
    h = _dot(x_ref[...].astype(BF16), w_ref[...])
    q_ref[...] = (h[:, _C_Q:_C_K] * GLA_DK ** -0.5).astype(BF16)
    k_ref[...] = h[:, _C_K:_C_V].astype(BF16)
    v_ref[...] = h[:, _C_V:_C_R].astype(BF16)
    sr_ref[...] = _silu(h[:, _C_R:_C_CQ]).astype(BF16)
    z = _dot_f32(h[:, _C_A:_C_END], wg2_ref[...]) + bg_ref[...]
    lg_ref[...] = (jnp.minimum(z, 0.0) - jnp.log(1.0 + jnp.exp(-jnp.abs(z)))) * (1.0 / GLA_TAU)
    cqn = _rms(h[:, _C_CQ:_C_CKV], qn_ref[...]).astype(BF16)
    qh = _dot(cqn, wuq_ref[...])
    n_nope = MLA_HEADS * MLA_NOPE
    cos, sin = cos_ref[...], sin_ref[...]
    x1, x2 = qh[:, n_nope:n_nope + LANES], qh[:, n_nope + LANES:]
    qscale = MLA_SCALE * LOG2E
    rot = (jnp.concatenate([x1 * cos - x2 * sin, x1 * sin + x2 * cos], axis=1) * qscale).astype(BF16)
    qlat = (_dot(qh[:, :n_nope].astype(BF16), wabs_ref[...]) * qscale).astype(BF16)
    qrope = _dot(rot, place_ref[...]).astype(BF16)
    for hd in range(MLA_HEADS):
        qall_ref[:, MLA_QK * hd:MLA_QK * hd + LANES] = qlat[:, LANES * hd:LANES * (hd + 1)]
        qall_ref[:, MLA_QK * hd + LANES:MLA_QK * (hd + 1)] = qrope[:, LANES * hd:LANES * (hd + 1)]
    ckv = _rms(h[:, _C_CKV:_C_KR], kvn_ref[...])
    lane = lax.broadcasted_iota(jnp.int32, (1, LANES), 1)
    sgn = jnp.where(lane < MLA_ROPE // 2, -1.0, 1.0)
    kr = h[:, _C_KR:_C_KRS] * cos + h[:, _C_KRS:_C_A] * (sin * sgn)
    ckv_ref[...] = ckv
    kr_ref[...] = kr
    kall_ref[:, :LANES] = ckv.astype(BF16)
    kall_ref[:, LANES:] = kr.astype(BF16)


def _inproj(x_all, cos_t, sin_t, w):
    t = x_all.shape[0]
    d = x_all.shape[1]
    tm = TOK_TILE
    row = lambda n: pl.BlockSpec((tm, n), lambda i: (i, 0))
    consts = [w['w_in'], w['w_g2'], w['b_g'], w['q_norm'], w['kv_norm'], w['w_uq'], w['w_abs'], w['place']]
    outs = [(GLA_HEADS * GLA_DK, BF16), (GLA_HEADS * GLA_DK, BF16), (GLA_HEADS * GLA_DV, BF16),
            (GLA_HEADS * GLA_DV, BF16), (GLA_HEADS * GLA_DK, F32), (MLA_HEADS * MLA_QK, BF16),
            (LANES, F32), (LANES, F32), (MLA_QK, BF16)]
    return pl.pallas_call(
        _inproj_kernel,
        grid=(t // tm,),
        in_specs=[row(d), row(LANES), row(LANES)] + [_const_spec(c.shape) for c in consts],
        out_specs=[row(n) for n, _ in outs],
        out_shape=[jax.ShapeDtypeStruct((t, n), dt) for n, dt in outs],
        compiler_params=_params(("parallel",)),
        name="inproj",
    )(x_all, cos_t, sin_t, *consts)


def _gla_tables(c):
    levels = int(math.log2(c))
    assert 1 << levels == c
    t = np.arange(c)[:, None]
    u = np.arange(c)[None, :]
    mats, masks = [], []
    for l in range(levels):
        half = 1 << l
        base = (t // half) * half
        upper = ((t >> l) & 1) == 1
        a_q = (u >= base) & (u <= t)
        a_k = (u > t) & (u <= base + half - 1)
        mats.append(np.where(upper, a_q, a_k))
        same = (t >> (l + 1)) == (u >> (l + 1))
        masks.append(same & upper & (((u >> l) & 1) == 0))
    mats.append(u <= t)
    mats.append(u > t)
    masks.append(t == u)
    amat = np.concatenate(mats, axis=0).astype(np.float32)
    mask = np.stack([np.tile(m, (GLA_HEADS, 1)) for m in masks]).astype(np.float32)
    return jnp.asarray(amat, BF16), jnp.asarray(mask, F32), levels


def _gla_kernel(q_ref, k_ref, v_ref, lg_ref, sr_ref, s0_ref, amat_ref, mask_ref, g_ref,
                o_ref, sout_ref, state, *, levels):
    c = q_ref.shape[0]
    hk = GLA_HEADS * GLA_DK
    step = pl.program_id(1)

    @pl.when(step == 0)
    def _():
        state[...] = s0_ref[0]

    lg = lg_ref[...]
    parts = _split3(lg)
    e3 = _dot(amat_ref[...], jnp.concatenate(parts, axis=1))
    ex = jnp.exp(e3[:, :hk] + e3[:, hk:2 * hk] + e3[:, 2 * hk:])
    q = q_ref[...].astype(F32)
    k = k_ref[...].astype(F32)
    v = v_ref[...]
    head = lax.broadcasted_iota(jnp.int32, (1, hk), 1) // GLA_DK

    def per_head(xf):
        return jnp.concatenate([jnp.where(head == h, xf, 0.0) for h in range(GLA_HEADS)], axis=0).astype(BF16)

    att = jnp.zeros((GLA_HEADS * c, c), F32)
    for l in range(levels + 1):
        if l < levels:
            el = ex[l * c:(l + 1) * c]
            ql, kl = q * el, k * el
        else:
            ql, kl = q, k
        a = lax.dot_general(per_head(ql), kl.astype(BF16), (((1,), (1,)), ((), ())), preferred_element_type=F32)
        att = att + a * mask_ref[l]
    s_prev = state[...]
    o_inter = _dot(per_head(q * ex[levels * c:(levels + 1) * c]), s_prev.astype(BF16))
    att = att.astype(BF16)
    g = g_ref[...]
    for h in range(GLA_HEADS):
        o = o_inter[h * c:(h + 1) * c] + _dot(att[h * c:(h + 1) * c], v[:, GLA_DV * h:GLA_DV * (h + 1)])
        o = _rms(o, g)
        o_ref[:, GLA_DV * h:GLA_DV * (h + 1)] = (o * sr_ref[:, GLA_DV * h:GLA_DV * (h + 1)].astype(F32)).astype(BF16)
    kr = (k * ex[(levels + 1) * c:]).astype(BF16)
    upd = lax.dot_general(kr, v, (((0,), (0,)), ((), ())), preferred_element_type=F32)
    ones = jnp.ones((c, GLA_DV), BF16)
    b_last = sum(lax.dot_general(p, ones, (((0,), (0,)), ((), ())), preferred_element_type=F32) for p in parts)
    new = jnp.exp(b_last) * s_prev + jnp.concatenate(
        [upd[GLA_DK * h:GLA_DK * (h + 1), GLA_DV * h:GLA_DV * (h + 1)] for h in range(GLA_HEADS)], axis=0)
    state[...] = new

    @pl.when(step == pl.num_programs(1) - 1)
    def _():
        sout_ref[0] = new


def _gla(q, k, v, lg, sr, s0, g, tables, batch, blocks_per_seq):
    amat, mask, levels = tables
    c = GLA_BLOCK
    hk, hv = GLA_HEADS * GLA_DK, GLA_HEADS * GLA_DV
    row = lambda n: pl.BlockSpec((c, n), lambda b, s: (b * blocks_per_seq + s, 0))
    st = pl.BlockSpec((1, hk, GLA_DV), lambda b, s: (b, 0, 0))
    return pl.pallas_call(
        functools.partial(_gla_kernel, levels=levels),
        grid=(batch, blocks_per_seq),
        in_specs=[row(hk), row(hk), row(hv), row(hk), row(hv), st,
                  _const_spec(amat.shape), _const_spec(mask.shape), _const_spec(g.shape)],
        out_specs=[row(hv), st],
        out_shape=[jax.ShapeDtypeStruct((batch * blocks_per_seq * c, hv), BF16),
                   jax.ShapeDtypeStruct((batch, hk, GLA_DV), F32)],
        scratch_shapes=[pltpu.VMEM((hk, GLA_DV), F32)],
        compiler_params=_params(("parallel", "arbitrary")),
        name="gla",
    )(q, k, v, lg, sr, s0, amat, mask, g)


def _mla_kernel(q_ref, kt_ref, kv_ref, o_ref, s_sc, p_sc, linv_sc, acc_sc, *, tq, n_past, past_valid, own_valid,
                diag_chunks):
    kvb = MLA_KV_TILE
    qi = pl.program_id(1)
    qs = jnp.concatenate([q_ref[:, MLA_QK * h:MLA_QK * (h + 1)] for h in range(MLA_HEADS)], axis=0)
    m_rows = MLA_HEADS * tq
    n_vis = n_past + qi + 1

    def score(j, mask):
        s = _dot(qs, kt_ref[0, j])
        s_sc[j] = s if mask is None else jnp.where(mask, s, NEG_INF)

    def scores(lo, hi):
        def body(j, carry):
            score(j, None)
            return carry
        lax.fori_loop(lo, hi, body, 0)

    col = lax.broadcasted_iota(jnp.int32, (m_rows, kvb), 1)
    n_past_full = past_valid // kvb
    if n_past_full:
        scores(0, n_past_full)
    if n_past_full < n_past:
        score(n_past_full, col < past_valid - n_past_full * kvb)
    scores(n_past, n_past + qi)
    mask = col < own_valid - qi * kvb
    if diag_chunks:
        qrow = lax.broadcasted_iota(jnp.int32, (m_rows, kvb), 0) % tq
        mask = mask & (col // CHUNK <= qrow // CHUNK)
    score(n_past + qi, mask)

    rc = MLA_ROW_STRIP

    def strip(r, carry):
        rows = pl.ds(pl.multiple_of(r * rc, rc), rc)
        mx = lax.fori_loop(0, n_vis, lambda j, m: jnp.maximum(m, s_sc[j, rows, :]),
                           jnp.full((rc, kvb), -jnp.inf, F32))
        m = jnp.max(mx, axis=1, keepdims=True)

        def body(j, l):
            p = jnp.exp2(s_sc[j, rows, :] - m)
            p_sc[j, rows, :] = p.astype(BF16)
            return l + p
        l = lax.fori_loop(0, n_vis, body, jnp.zeros((rc, kvb), F32))
        linv_sc[rows, :] = 1.0 / jnp.sum(l, axis=1, keepdims=True)
        return carry
    lax.fori_loop(0, m_rows // rc, strip, 0)

    acc_sc[...] = jnp.zeros(acc_sc.shape, F32)

    def pv(j, carry):
        vblk = kv_ref[0, pl.ds(pl.multiple_of(j * kvb, kvb), kvb), :MLA_KV_LORA]
        acc_sc[...] += _dot(p_sc[j], vblk)
        return carry
    lax.fori_loop(0, n_vis, pv, 0)
    o = acc_sc[...] * linv_sc[...]
    for h in range(MLA_HEADS):
        o_ref[:, MLA_KV_LORA * h:MLA_KV_LORA * (h + 1)] = o[h * tq:(h + 1) * tq].astype(BF16)


def _mla(q_all, kt, kv, batch, nq, tq, q_block0, past_valid, own_valid):
    n_kv = kt.shape[1]
    n_past = -(-past_valid // MLA_KV_TILE)
    assert nq == 1 or tq == MLA_KV_TILE
    assert n_kv == n_past + nq
    return pl.pallas_call(
        functools.partial(_mla_kernel, tq=tq, n_past=n_past, past_valid=past_valid, own_valid=own_valid,
                          diag_chunks=nq > 1),
        grid=(batch, nq),
        in_specs=[pl.BlockSpec((tq, MLA_HEADS * MLA_QK), lambda b, i: (q_block0 + b * nq + i, 0)),
                  pl.BlockSpec((1,) + kt.shape[1:], lambda b, i: (b, 0, 0, 0)),
                  pl.BlockSpec((1,) + kv.shape[1:], lambda b, i: (b, 0, 0))],
        out_specs=pl.BlockSpec((tq, MLA_HEADS * MLA_KV_LORA), lambda b, i: (b * nq + i, 0)),
        out_shape=jax.ShapeDtypeStruct((batch * nq * tq, MLA_HEADS * MLA_KV_LORA), BF16),
        scratch_shapes=[pltpu.VMEM((n_kv, MLA_HEADS * tq, MLA_KV_TILE), F32),
                        pltpu.VMEM((n_kv, MLA_HEADS * tq, MLA_KV_TILE), BF16),
                        pltpu.VMEM((MLA_HEADS * tq, 1), F32),
                        pltpu.VMEM((MLA_HEADS * tq, MLA_KV_LORA), F32)],
        compiler_params=_params(("parallel", "arbitrary")),
        name="mla",
    )(q_all, kt, kv)


def _post_mixer(x, out, g, b, rw_ref, x1_ref, x1b_ref, lgt_ref):
    y = _layer_norm(ALPHA * x + out, g, b)
    x1_ref[...] = y
    x1b_ref[...] = y.astype(BF16)
    lgt_ref[0] = _dot_f32_nt(rw_ref[...], y)


def _outproj_kernel(og_ref, ol_ref, x_ref, wuv_ref, wo_ref, g_ref, b_ref, rw_ref, x1_ref, x1b_ref, lgt_ref):
    o_mla = _dot(ol_ref[...], wuv_ref[...]).astype(BF16)
    n_gla = og_ref.shape[1]
    out = _dot(og_ref[...], wo_ref[:n_gla, :]) + _dot(o_mla, wo_ref[n_gla:, :])
    _post_mixer(x_ref[...], out, g_ref[...], b_ref[...], rw_ref, x1_ref, x1b_ref, lgt_ref)


def _outproj(o_gla, o_lat, x_all, w_uvbd, w_out, g, b, rw):
    t, d = x_all.shape
    tm = TOK_TILE
    row = lambda n: pl.BlockSpec((tm, n), lambda i: (i, 0))
    consts = [w_uvbd, w_out, g, b, rw]
    return pl.pallas_call(
        _outproj_kernel,
        grid=(t // tm,),
        in_specs=[row(o_gla.shape[1]), row(o_lat.shape[1]), row(d)] + [_const_spec(c.shape) for c in consts],
        out_specs=[row(d), row(d), pl.BlockSpec((1, N_EXPERTS, tm), lambda i: (i, 0, 0))],
        out_shape=[jax.ShapeDtypeStruct((t, d), F32), jax.ShapeDtypeStruct((t, d), BF16),
                   jax.ShapeDtypeStruct((t // tm, N_EXPERTS, tm), F32)],
        compiler_params=_params(("parallel",)),
        name="outproj",
    )(o_gla, o_lat, x_all, *consts)


def _pool_kernel(x_ref, prev_ref, hist_ref, pw_ref, ps_ref, g_ref, b_ref, rw_ref, x1_ref, x1b_ref, lgt_ref):
    ts = x_ref.shape[0]
    pm = POOL_MAX
    x = x_ref[...]
    prev = jnp.where(pl.program_id(1) == 0, hist_ref[0], prev_ref[...])
    xc = jnp.concatenate([prev, x], axis=0)
    grp = x.shape[1] // len(POOL_WINDOWS)
    outs = []
    for gi, w in enumerate(POOL_WINDOWS):
        s = xc[:, gi * grp:(gi + 1) * grp]
        span = 1
        while span < w:
            s = s + pltpu.roll(s, span, 0)
            span *= 2
        win = s[pm:]
        mix = (win * (1.0 / w) - x[:, gi * grp:(gi + 1) * grp]).astype(BF16)
        outs.append(_dot(mix, pw_ref[gi]))
    out = jnp.concatenate(outs, axis=1) * ps_ref[...]
    _post_mixer(x, out, g_ref[...], b_ref[...], rw_ref, x1_ref, x1b_ref, lgt_ref)


def _pool(x_all, hist, pool_w, pool_scale, g, b, rw, batch, tiles_per_seq, ts, tile0):
    d = x_all.shape[1]
    pm = POOL_MAX
    per = ts // pm
    consts = [pool_w, pool_scale, g, b, rw]
    row = lambda n: pl.BlockSpec((ts, n), lambda bb, i: (bb * tiles_per_seq + i, 0))
    return pl.pallas_call(
        _pool_kernel,
        grid=(batch, tiles_per_seq),
        in_specs=[pl.BlockSpec((ts, d), lambda bb, i: (tile0 + bb * tiles_per_seq + i, 0)),
                  pl.BlockSpec((pm, d), lambda bb, i: (jnp.maximum((tile0 + bb * tiles_per_seq + i) * per - 1, 0), 0)),
                  pl.BlockSpec((1, pm, d), lambda bb, i: (bb, 0, 0))] + [_const_spec(c.shape) for c in consts],
        out_specs=[row(d), row(d), pl.BlockSpec((1, N_EXPERTS, ts), lambda bb, i: (bb * tiles_per_seq + i, 0, 0))],
        out_shape=[jax.ShapeDtypeStruct((batch * tiles_per_seq * ts, d), F32),
                   jax.ShapeDtypeStruct((batch * tiles_per_seq * ts, d), BF16),
                   jax.ShapeDtypeStruct((batch * tiles_per_seq, N_EXPERTS, ts), F32)],
        compiler_params=_params(("parallel", "arbitrary")),
        name="pool",
    )(x_all, x_all, hist, *consts)


def _experts_kernel(be_ref, nb_ref, x_ref, wg_ref, wu_ref, wd_ref, y_ref):
    @pl.when(pl.program_id(0) < nb_ref[0])
    def _():
        wg, wu, wd = wg_ref[0, 0].astype(BF16), wu_ref[0, 0].astype(BF16), wd_ref[0, 0].astype(BF16)
        sub = EXPERT_SUBTILE
        for r in range(x_ref.shape[0] // sub):
            x = x_ref[r * sub:(r + 1) * sub, :]
            hmid = (_silu(_dot(x, wg)) * _dot(x, wu)).astype(BF16)
            y_ref[r * sub:(r + 1) * sub, :] = _dot(hmid, wd).astype(BF16)

    @pl.when(pl.program_id(0) >= nb_ref[0])
    def _():
        y_ref[...] = jnp.zeros_like(y_ref)


def _experts(blk_e, n_used, x_sorted, wg, wu, wd, layer):
    p, d = x_sorted.shape
    f = wg.shape[3]
    bm = EXPERT_TILE
    grid_spec = pltpu.PrefetchScalarGridSpec(
        num_scalar_prefetch=2,
        grid=(p // bm,),
        in_specs=[pl.BlockSpec((bm, d), lambda i, be, nb: (i, 0)),
                  pl.BlockSpec((1, 1, d, f), lambda i, be, nb: (layer, be[i], 0, 0)),
                  pl.BlockSpec((1, 1, d, f), lambda i, be, nb: (layer, be[i], 0, 0)),
                  pl.BlockSpec((1, 1, f, d), lambda i, be, nb: (layer, be[i], 0, 0))],
        out_specs=pl.BlockSpec((bm, d), lambda i, be, nb: (i, 0)),
    )
    return pl.pallas_call(
        _experts_kernel,
        grid_spec=grid_spec,
        out_shape=jax.ShapeDtypeStruct((p, d), BF16),
        compiler_params=_params(("arbitrary",)),
        name="experts",
    )(blk_e, n_used, x_sorted, wg, wu, wd)


def _ffn_out_kernel(x_ref, xb_ref, yk_ref, w_ref, sg_ref, su_ref, sd_ref, g_ref, b_ref, y_ref, yb_ref):
    xb = xb_ref[...]
    hmid = (_silu(_dot(xb, sg_ref[...])) * _dot(xb, su_ref[...])).astype(BF16)
    w = w_ref[...]
    routed = yk_ref[0].astype(F32) * w[:, 0:1]
    for kk in range(1, TOP_K):
        routed = routed + yk_ref[kk].astype(F32) * w[:, kk:kk + 1]
    ffn = routed + _dot(hmid, sd_ref[...])
    y = _layer_norm(ALPHA * x_ref[...] + ffn, g_ref[...], b_ref[...])
    y_ref[...] = y
    yb_ref[...] = y.astype(BF16)


def _ffn_out(x1, x1b, y_tok, wts, sg, su, sd, g, b):
    t, d = x1.shape
    tm = FFN_TILE
    row = lambda n: pl.BlockSpec((tm, n), lambda i: (i, 0))
    consts = [sg, su, sd, g, b]
    return pl.pallas_call(
        _ffn_out_kernel,
        grid=(t // tm,),
        in_specs=[row(d), row(d), pl.BlockSpec((TOP_K, tm, d), lambda i: (0, i, 0)), row(TOP_K)]
        + [_const_spec(c.shape) for c in consts],
        out_specs=[row(d), row(d)],
        out_shape=[jax.ShapeDtypeStruct((t, d), F32), jax.ShapeDtypeStruct((t, d), BF16)],
        compiler_params=_params(("parallel",)),
        name="ffn_out",
    )(x1, x1b, y_tok, wts, *consts)


def _outranked(vals, n, ids):
    rank = jnp.zeros(vals.shape, jnp.int32)
    for j in range(n):
        row = vals[j:j + 1]
        rank = rank + ((row > vals) | ((row == vals) & (j < ids))).astype(jnp.int32)
    return rank


def _router_kernel(lg_ref, b_ref, tri_ref, low_ref, idx_ref, pos_ref, w_ref, cnt_ref, carry):
    @pl.when(pl.program_id(0) == 0)
    def _():
        carry[...] = jnp.zeros_like(carry)

    tm = lg_ref.shape[2]
    gs = N_EXPERTS // N_GROUPS
    scores = 1.0 / (1.0 + jnp.exp(-lg_ref[0]))
    biased = scores + b_ref[...]
    member = lax.broadcasted_iota(jnp.int32, (gs, tm), 0)
    group_score = []
    for g in range(N_GROUPS):
        blk = biased[g * gs:(g + 1) * gs]
        m1 = jnp.max(blk, axis=0, keepdims=True)
        first = jnp.min(jnp.where(blk == m1, member, gs), axis=0, keepdims=True)
        m2 = jnp.max(jnp.where(member == first, -jnp.inf, blk), axis=0, keepdims=True)
        group_score.append(m1 + m2)
    group_score = jnp.concatenate(group_score, axis=0)
    gid = lax.broadcasted_iota(jnp.int32, (N_GROUPS, tm), 0)
    group_ok = _outranked(group_score, N_GROUPS, gid) < TOPK_GROUPS
    masked = jnp.concatenate([jnp.where(group_ok[g:g + 1], biased[g * gs:(g + 1) * gs], -jnp.inf)
                              for g in range(N_GROUPS)], axis=0)
    eid = lax.broadcasted_iota(jnp.int32, (N_EXPERTS, tm), 0)
    sel = _outranked(masked, N_EXPERTS, eid) < TOP_K
    self = jnp.where(sel, 1.0, 0.0)
    wsel = self * scores
    wts = wsel / jnp.sum(wsel, axis=0, keepdims=True) * ROUTED_SCALE
    selb = self.astype(BF16)
    pos = carry[...] + _dot(selb, tri_ref[...])
    carry[...] = carry[...] + jnp.sum(self, axis=1, keepdims=True)
    cnt_ref[...] = carry[...]
    ordinal = _dot(low_ref[...], selb)
    eidf = eid.astype(F32)
    idx, posk, wk = [], [], []
    for kk in range(TOP_K):
        hit = sel & (ordinal == kk)
        idx.append(jnp.sum(jnp.where(hit, eidf, 0.0), axis=0, keepdims=True))
        posk.append(jnp.sum(jnp.where(hit, pos, 0.0), axis=0, keepdims=True))
        wk.append(jnp.sum(jnp.where(hit, wts, 0.0), axis=0, keepdims=True))
    idx_ref[...] = jnp.concatenate(idx, axis=0).astype(jnp.int32)
    pos_ref[...] = jnp.concatenate(posk, axis=0).astype(jnp.int32)
    w_ref[...] = jnp.concatenate(wk, axis=0)


def _router(logits, router_b):
    nt, ne, tm = logits.shape
    t = nt * tm
    tri = jnp.asarray(np.triu(np.ones((tm, tm), np.float32), 1), BF16)
    low = jnp.asarray(np.tril(np.ones((ne, ne), np.float32), -1), BF16)
    col = pl.BlockSpec((TOP_K, tm), lambda i: (0, i))
    return pl.pallas_call(
        _router_kernel,
        grid=(nt,),
        in_specs=[pl.BlockSpec((1, ne, tm), lambda i: (i, 0, 0)), _const_spec((ne, 1)),
                  _const_spec(tri.shape), _const_spec(low.shape)],
        out_specs=[col, col, col, _const_spec((ne, 1))],
        out_shape=[jax.ShapeDtypeStruct((TOP_K, t), jnp.int32), jax.ShapeDtypeStruct((TOP_K, t), jnp.int32),
                   jax.ShapeDtypeStruct((TOP_K, t), F32), jax.ShapeDtypeStruct((ne, 1), F32)],
        scratch_shapes=[pltpu.VMEM((ne, 1), F32)],
        compiler_params=_params(("arbitrary",)),
        name="router",
    )(logits, router_b.reshape(ne, 1).astype(F32), tri, low)


def _moe(x1, x1b, logits, router_b, layer, wg, wu, wd, sg, su, sd, g, b):
    t, d = x1.shape
    idx, pos, wts, cnt = _router(logits, router_b)
    a = t * TOP_K
    bm = EXPERT_TILE
    counts = cnt[:, 0].astype(jnp.int32)
    padded = (counts + bm - 1) // bm * bm
    pad_end = jnp.cumsum(padded)
    pad_start = pad_end - padded
    experts = jnp.arange(N_EXPERTS, dtype=jnp.int32)
    dest = pos + jnp.sum(jnp.where(idx[:, :, None] == experts, pad_start, 0), axis=-1)
    n_blocks = -(-a // bm) + N_EXPERTS
    tok = jnp.broadcast_to(jnp.arange(t, dtype=jnp.int32), (TOP_K, t))
    tok_buf = jnp.zeros((n_blocks * bm,), jnp.int32).at[dest.reshape(a)].set(tok.reshape(a), unique_indices=True)
    blk_start = jnp.arange(n_blocks, dtype=jnp.int32) * bm
    blk_e = jnp.minimum(jnp.sum((pad_end[None, :] <= blk_start[:, None]).astype(jnp.int32), axis=1), N_EXPERTS - 1)
    n_used = (pad_end[-1] // bm).astype(jnp.int32).reshape(1)
    x_sorted = x1b.at[tok_buf].get(mode='promise_in_bounds')
    y_sorted = _experts(blk_e, n_used, x_sorted, wg, wu, wd, layer)
    y_tok = y_sorted.at[dest.reshape(a)].get(mode='promise_in_bounds').reshape(TOP_K, t, d)
    return _ffn_out(x1, x1b, y_tok, wts.T, sg, su, sd, g, b)


def _rope_tables(pos):
    half = MLA_ROPE // 2
    inv = ROPE_THETA ** (-jnp.arange(half, dtype=F32) / half)
    ang = pos.astype(F32)[:, None] * inv
    reps = LANES // half
    return jnp.tile(jnp.cos(ang), (1, reps)), jnp.tile(jnp.sin(ang), (1, reps))


def _pack_layer0_weights(w_in0, gla_w_g2, gla_b_g, mla_q_norm_g, mla_kv_norm_g, mla_w_uq, mla_w_uk, mla_w_uv):
    d = w_in0.shape[0]
    hk, hv = GLA_HEADS * GLA_DK, GLA_HEADS * GLA_DV
    o_q, o_k, o_v, o_r = 0, hk, 2 * hk, 2 * hk + hv
    o_a = o_r + hv
    o_cq = o_a + GLA_RANK
    o_ckv = o_cq + MLA_Q_LORA
    o_kr = o_ckv + MLA_KV_LORA
    half = MLA_ROPE // 2
    zeros = lambda n: jnp.zeros((d, n), w_in0.dtype)
    kr1, kr2 = w_in0[:, o_kr:o_kr + half], w_in0[:, o_kr + half:o_kr + MLA_ROPE]
    w_in = jnp.concatenate([
        w_in0[:, o_q:o_a], w_in0[:, o_cq:o_kr],
        kr1, kr2, zeros(LANES - MLA_ROPE),
        kr2, kr1, zeros(LANES - MLA_ROPE),
        w_in0[:, o_a:o_cq], zeros(LANES - GLA_RANK)], axis=1).astype(BF16)
    assert w_in.shape[1] == _C_END
    w_g2 = jnp.concatenate([gla_w_g2, jnp.zeros((LANES - GLA_RANK, hk), gla_w_g2.dtype)], axis=0)
    uq = mla_w_uq.reshape(MLA_Q_LORA, MLA_HEADS, MLA_NOPE + MLA_ROPE)
    w_uq = jnp.concatenate([uq[:, :, :MLA_NOPE].reshape(MLA_Q_LORA, -1),
                            uq[:, :, MLA_NOPE:MLA_NOPE + half].reshape(MLA_Q_LORA, -1),
                            uq[:, :, MLA_NOPE + half:].reshape(MLA_Q_LORA, -1)], axis=1).astype(BF16)
    eye = jnp.eye(MLA_HEADS, dtype=mla_w_uk.dtype)
    w_abs = jnp.einsum('chn,hg->hngc', mla_w_uk, eye).reshape(MLA_HEADS * MLA_NOPE, MLA_HEADS * MLA_KV_LORA).astype(BF16)
    w_uvbd = jnp.einsum('chv,hg->hcgv', mla_w_uv, eye).reshape(MLA_HEADS * MLA_KV_LORA, MLA_HEADS * MLA_V).astype(BF16)
    place = np.zeros((2 * LANES, MLA_HEADS * LANES), np.float32)
    for h in range(MLA_HEADS):
        for j in range(half):
            place[h * half + j, h * LANES + j] = 1.0
            place[LANES + h * half + j, h * LANES + half + j] = 1.0
    return dict(w_in=w_in, w_g2=w_g2, b_g=gla_b_g.reshape(1, hk), q_norm=mla_q_norm_g.reshape(1, -1),
                kv_norm=mla_kv_norm_g.reshape(1, -1), w_uq=w_uq, w_abs=w_abs,
                place=jnp.asarray(place, BF16)), w_uvbd


def _pad_rows(x, n):
    return jnp.pad(x, ((0, n - x.shape[0]),) + ((0, 0),) * (x.ndim - 1))


def _kv_tiles(past, own):
    kvb = MLA_KV_TILE
    pad = lambda a: jnp.pad(a, ((0, 0), (0, -a.shape[1] % kvb), (0, 0)))
    kv = jnp.concatenate([pad(past), pad(own)], axis=1) if past.shape[1] else pad(own)
    b, n, w = kv.shape
    kt = kv.reshape(b, n // kvb, kvb, w).transpose(0, 1, 3, 2)
    return kt, kv


def kernel(x_prompt, x_sample, cache_mla_ckv, cache_mla_krope, state_gla, cache_pool, meta_tokens, w_in0, gla_w_g2, gla_b_g, gla_norm_g, mla_q_norm_g, mla_kv_norm_g, mla_w_uq, mla_w_uk, mla_w_uv, w_out0, pool_w, pool_scale, ln_g, ln_b, moe_router_w, moe_router_b, moe_w_gate, moe_w_up, moe_w_down, moe_sh_gate, moe_sh_up, moe_sh_down):
    bp, sp, d = x_prompt.shape
    bs, ss, _ = x_sample.shape
    n_meta = meta_tokens.shape[0]
    past_len = cache_mla_ckv.shape[1] - n_meta
    tp, tsm = bp * sp, bs * ss
    assert sp % MLA_Q_TILE == 0 and sp % POOL_TILE == 0 and sp % GLA_BLOCK == 0 and sp % CHUNK == 0
    assert ss <= CHUNK and past_len % CHUNK == 0 and n_meta <= CHUNK and ss <= GLA_BLOCK and n_meta <= GLA_BLOCK
    assert n_meta == POOL_MAX and ss >= POOL_MAX and tp % TOK_TILE == 0
    t_all = -(-(tp + tsm + n_meta) // TOK_TILE) * TOK_TILE
    o_s, o_m = tp, tp + tsm

    x_all = _pad_rows(jnp.concatenate([x_prompt.reshape(tp, d), x_sample.reshape(tsm, d), meta_tokens], axis=0), t_all)
    pos = jnp.concatenate([jnp.tile(n_meta + jnp.arange(sp), bp), jnp.tile(n_meta + past_len + jnp.arange(ss), bs),
                           jnp.arange(n_meta), jnp.zeros((t_all - o_m - n_meta,), jnp.int32)])
    cos_t, sin_t = _rope_tables(pos)
    w0, w_uvbd = _pack_layer0_weights(w_in0, gla_w_g2, gla_b_g, mla_q_norm_g, mla_kv_norm_g, mla_w_uq, mla_w_uk, mla_w_uv)

    q, k, v, sr, lg, q_all, ckv, kr, k_all = _inproj(x_all, cos_t, sin_t, w0)

    tables = _gla_tables(GLA_BLOCK)
    g_gla = gla_norm_g.reshape(1, GLA_DV)
    c = GLA_BLOCK
    hk = GLA_HEADS * GLA_DK

    def small(a):
        sm = jnp.pad(a[o_s:o_m].reshape(bs, ss, -1), ((0, 0), (0, c - ss), (0, 0)))
        me = jnp.pad(a[o_m:o_m + n_meta], ((0, c - n_meta), (0, 0)))[None]
        return jnp.concatenate([sm, me], axis=0).reshape((bs + 1) * c, -1)

    s0_small = jnp.concatenate([state_gla.reshape(bs, hk, GLA_DV), jnp.zeros((1, hk, GLA_DV), F32)], axis=0)
    og_small, st_small = _gla(small(q), small(k), small(v), small(lg), small(sr), s0_small, g_gla, tables, bs + 1, 1)
    s0_prompt = jnp.broadcast_to(st_small[bs:], (bp, hk, GLA_DV))
    og_prompt, st_prompt = _gla(q, k, v, lg, sr, s0_prompt, g_gla, tables, bp, sp // c)
    og_small = og_small.reshape(bs + 1, c, -1)
    o_gla = _pad_rows(jnp.concatenate([og_prompt, og_small[:bs, :ss].reshape(tsm, -1), og_small[bs, :n_meta]], axis=0), t_all)

    k_meta = k_all[o_m:o_m + n_meta]
    kt_m, kv_m = _kv_tiles(jnp.zeros((1, 0, MLA_QK), BF16), k_meta[None])
    ol_meta = _mla(q_all[o_m:o_m + n_meta], kt_m, kv_m, 1, 1, n_meta, 0, 0, n_meta)
    cache = jnp.concatenate([cache_mla_ckv, cache_mla_krope,
                             jnp.zeros(cache_mla_ckv.shape[:2] + (MLA_QK - MLA_KV_LORA - MLA_ROPE,), F32)], axis=-1).astype(BF16)
    kt_s, kv_s = _kv_tiles(cache, k_all[o_s:o_m].reshape(bs, ss, MLA_QK))
    ol_sample = _mla(q_all[o_s:o_m], kt_s, kv_s, bs, 1, ss, 0, n_meta + past_len, ss)
    kt_p, kv_p = _kv_tiles(jnp.broadcast_to(k_meta[None], (bp, n_meta, MLA_QK)), k_all[:tp].reshape(bp, sp, MLA_QK))
    ol_prompt = _mla(q_all, kt_p, kv_p, bp, sp // MLA_Q_TILE, MLA_Q_TILE, 0, n_meta, sp)
    o_lat = _pad_rows(jnp.concatenate([ol_prompt, ol_sample, ol_meta], axis=0), t_all)

    ln = lambda l, j: (ln_g[l, j].reshape(1, d), ln_b[l, j].reshape(1, d))
    rw = lambda l: moe_router_w[l].T
    moe_w = lambda l: (l, moe_w_gate, moe_w_up, moe_w_down,
                       moe_sh_gate[l].astype(BF16), moe_sh_up[l].astype(BF16), moe_sh_down[l].astype(BF16))
    x1, x1b, logits = _outproj(o_gla, o_lat, x_all, w_uvbd, w_out0.astype(BF16), *ln(0, 0), rw(0))
    x2, _ = _moe(x1, x1b, logits, moe_router_b[0], *moe_w(0), *ln(0, 1))

    pm = POOL_MAX
    zrow = jnp.zeros((1, d), F32)
    hist_p = jnp.broadcast_to(jnp.concatenate([zrow, x2[o_m + n_meta - (pm - 1):o_m + n_meta]], axis=0)[None], (bp, pm, d))
    hist_s = jnp.concatenate([jnp.zeros((bs, 1, d), F32), cache_pool], axis=1)
    pool_wb = pool_w.astype(BF16)
    ps = pool_scale.reshape(1, d)
    y1p = _pool(x2, hist_p, pool_wb, ps, *ln(1, 0), rw(1), bp, sp // POOL_TILE, POOL_TILE, 0)
    y1s = _pool(x2[o_s:o_m], hist_s, pool_wb, ps, *ln(1, 0), rw(1), bs, 1, ss, 0)
    t1 = -(-(tp + tsm) // TOK_TILE) * TOK_TILE
    x3, x3b = [_pad_rows(jnp.concatenate([a, b_], axis=0), t1) for a, b_ in zip(y1p[:2], y1s[:2])]
    by_token = lambda lt: lt.transpose(1, 0, 2).reshape(N_EXPERTS, -1)
    logits1 = jnp.pad(jnp.concatenate([by_token(y1p[2]), by_token(y1s[2])], axis=1), ((0, 0), (0, t1 - tp - tsm)))
    logits1 = logits1.reshape(N_EXPERTS, t1 // TOK_TILE, TOK_TILE).transpose(1, 0, 2)
    x4, _ = _moe(x3, x3b, logits1, moe_router_b[1], *moe_w(1), *ln(1, 1))

    y_prompt = x4[:tp].reshape(bp, sp, d)
    y_sample = x4[o_s:o_m].reshape(bs, ss, d)
    ckv_meta, kr_meta = ckv[o_m:o_m + n_meta], kr[o_m:o_m + n_meta, :MLA_ROPE]
    p_ckv = jnp.concatenate([jnp.broadcast_to(ckv_meta[None], (bp, n_meta, MLA_KV_LORA)),
                             ckv[:tp].reshape(bp, sp, MLA_KV_LORA)], axis=1)
    p_kr = jnp.concatenate([jnp.broadcast_to(kr_meta[None], (bp, n_meta, MLA_ROPE)),
                            kr[:tp, :MLA_ROPE].reshape(bp, sp, MLA_ROPE)], axis=1)
    p_gla = st_prompt.reshape(bp, GLA_HEADS, GLA_DK, GLA_DV)
    x2p = x2[:tp].reshape(bp, sp, d)
    p_pool = x2p[:, sp - (pm - 1):]
    s_ckv = ckv[o_s:o_m].reshape(bs, ss, MLA_KV_LORA)
    s_kr = kr[o_s:o_m, :MLA_ROPE].reshape(bs, ss, MLA_ROPE)
    s_gla = st_small[:bs].reshape(bs, GLA_HEADS, GLA_DK, GLA_DV)
    s_pool = x2[o_s:o_m].reshape(bs, ss, d)[:, ss - (pm - 1):]
    return (y_prompt, y_sample, p_ckv, p_kr, p_gla, p_pool, s_ckv, s_kr, s_gla, s_pool)
```

```python
import functools
import math

import jax
import jax.numpy as jnp
import numpy as np
from jax import lax
from jax.experimental import pallas as pl
from jax.experimental.pallas import tpu as pltpu

F32 = jnp.float32
BF16 = jnp.bfloat16

CHUNK = 64
DEPTH = 2
ALPHA = (2 * DEPTH) ** 0.25
LN_EPS = 1e-5
RMS_EPS = 1e-6
NEG_INF = -1e30
GLA_HEADS = 4
GLA_DK = 64
GLA_DV = 128
GLA_RANK = 16
GLA_TAU = 16.0
MLA_HEADS = 8
MLA_Q_LORA = 256
MLA_KV_LORA = 128
MLA_NOPE = 64
MLA_ROPE = 32
MLA_V = 64
MLA_SCALE = (MLA_NOPE + MLA_ROPE) ** -0.5
ROPE_THETA = 10000.0
POOL_WINDOWS = (2, 4, 8, 16)
POOL_MAX = 16
N_EXPERTS = 64
TOP_K = 8
N_GROUPS = 8
TOPK_GROUPS = 4
ROUTED_SCALE = 2.5

LANES = 128
MXU_DIM = 256

TOK_TILE = 512
GLA_BLOCK = 128
MLA_Q_TILE = 256
MLA_KV_TILE = 256
MLA_QK = 256
MLA_ROW_STRIP = 128
LOG2E = math.log2(math.e)
EXPERT_TILE = 512
EXPERT_SUBTILE = 256
POOL_TILE = 512
SEG_ROWS = 16
SORT_ROWS = 5120
NO_ROW = 8191.0
DISPATCH_STRIP = 512
COMBINE_STRIP = 1024
VMEM_LIMIT = 56 * 1024 * 1024


def _params(sem, vmem=VMEM_LIMIT):
    return pltpu.CompilerParams(dimension_semantics=sem, vmem_limit_bytes=vmem)


def _const_spec(shape):
    nd = len(shape)
    return pl.BlockSpec(shape, lambda *_: (0,) * nd)


def _split3(x):
    hi = x.astype(BF16)
    r = x - hi.astype(F32)
    mid = r.astype(BF16)
    lo = (r - mid.astype(F32)).astype(BF16)
    return hi, mid, lo


def _dot(a, b):
    return jnp.dot(a, b, preferred_element_type=F32)


def _dot_f32(a, b):
    a_hi = a.astype(BF16)
    a_lo = (a - a_hi.astype(F32)).astype(BF16)
    b_hi = b.astype(BF16)
    b_lo = (b - b_hi.astype(F32)).astype(BF16)
    return _dot(a_hi, b_hi) + (_dot(a_lo, b_hi) + _dot(a_hi, b_lo))


def _dot_f32_nt(a, b):
    nt = lambda u, v: lax.dot_general(u, v, (((1,), (1,)), ((), ())), preferred_element_type=F32)
    a_hi = a.astype(BF16)
    a_lo = (a - a_hi.astype(F32)).astype(BF16)
    b_hi = b.astype(BF16)
    b_lo = (b - b_hi.astype(F32)).astype(BF16)
    return nt(a_hi, b_hi) + (nt(a_lo, b_hi) + nt(a_hi, b_lo))


def _silu(x):
    return x * (1.0 / (1.0 + jnp.exp(-x)))


def _layer_norm(x, g, b):
    mu = jnp.mean(x, axis=-1, keepdims=True)
    xc = x - mu
    var = jnp.mean(xc * xc, axis=-1, keepdims=True)
    return xc * lax.rsqrt(var + LN_EPS) * g + b


def _rms(x, g):
    return x * lax.rsqrt(jnp.mean(x * x, axis=-1, keepdims=True) + RMS_EPS) * g


_C_Q, _C_K, _C_V, _C_R, _C_CQ, _C_CKV, _C_KR, _C_KRS, _C_A, _C_END = (
    0, 256, 512, 1024, 1536, 1792, 1920, 2048, 2176, 2304)


def _inproj_kernel(x_ref, cos_ref, sin_ref, w_ref, wg2_ref, bg_ref, qn_ref, kvn_ref, wuq_ref, wabs_ref,
                   place_ref, q_ref, k_ref, v_ref, sr_ref, lg_ref, qall_ref, ckv_ref, kr_ref, kall_ref):
    h = _dot(x_ref[...].astype(BF16), w_ref[...])
    q_ref[...] = (h[:, _C_Q:_C_K] * GLA_DK ** -0.5).astype(BF16)
    k_ref[...] = h[:, _C_K:_C_V].astype(BF16)
    v_ref[...] = h[:, _C_V:_C_R].astype(BF16)
    sr_ref[...] = _silu(h[:, _C_R:_C_CQ]).astype(BF16)
    z = _dot_f32(h[:, _C_A:_C_END], wg2_ref[...]) + bg_ref[...]
    lg_ref[...] = (jnp.minimum(z, 0.0) - jnp.log(1.0 + jnp.exp(-jnp.abs(z)))) * (1.0 / GLA_TAU)
    cqn = _rms(h[:, _C_CQ:_C_CKV], qn_ref[...]).astype(BF16)
    qh = _dot(cqn, wuq_ref[...])
    n_nope = MLA_HEADS * MLA_NOPE
    cos, sin = cos_ref[...], sin_ref[...]
    x1, x2 = qh[:, n_nope:n_nope + LANES], qh[:, n_nope + LANES:]
    qscale = MLA_SCALE * LOG2E
    rot = (jnp.concatenate([x1 * cos - x2 * sin, x1 * sin + x2 * cos], axis=1) * qscale).astype(BF16)
    qlat = (_dot(qh[:, :n_nope].astype(BF16), wabs_ref[...]) * qscale).astype(BF16)
    qrope = _dot(rot, place_ref[...]).astype(BF16)
    for hd in range(MLA_HEADS):
        qall_ref[:, MLA_QK * hd:MLA_QK * hd + LANES] = qlat[:, LANES * hd:LANES * (hd + 1)]
        qall_ref[:, MLA_QK * hd + LANES:MLA_QK * (hd + 1)] = qrope[:, LANES * hd:LANES * (hd + 1)]
    ckv = _rms(h[:, _C_CKV:_C_KR], kvn_ref[...])
    lane = lax.broadcasted_iota(jnp.int32, (1, LANES), 1)
    sgn = jnp.where(lane < MLA_ROPE // 2, -1.0, 1.0)
    kr = h[:, _C_KR:_C_KRS] * cos + h[:, _C_KRS:_C_A] * (sin * sgn)
    ckv_ref[...] = ckv
    kr_ref[...] = kr
    kall_ref[:, :LANES] = ckv.astype(BF16)
    kall_ref[:, LANES:] = kr.astype(BF16)


def _inproj(x_all, cos_t, sin_t, w):
    t = x_all.shape[0]
    d = x_all.shape[1]
    tm = TOK_TILE
    row = lambda n: pl.BlockSpec((tm, n), lambda i: (i, 0))
    consts = [w['w_in'], w['w_g2'], w['b_g'], w['q_norm'], w['kv_norm'], w['w_uq'], w['w_abs'], w['place']]
    outs = [(GLA_HEADS * GLA_DK, BF16), (GLA_HEADS * GLA_DK, BF16), (GLA_HEADS * GLA_DV, BF16),
            (GLA_HEADS * GLA_DV, BF16), (GLA_HEADS * GLA_DK, F32), (MLA_HEADS * MLA_QK, BF16),
            (LANES, F32), (LANES, F32), (MLA_QK, BF16)]
    return pl.pallas_call(
        _inproj_kernel,
        grid=(t // tm,),
        in_specs=[row(d), row(LANES), row(LANES)] + [_const_spec(c.shape) for c in consts],
        out_specs=[row(n) for n, _ in outs],
        out_shape=[jax.ShapeDtypeStruct((t, n), dt) for n, dt in outs],
        compiler_params=_params(("parallel",)),
        name="inproj",
    )(x_all, cos_t, sin_t, *consts)


def _gla_tables(c):
    levels = int(math.log2(c))
    assert 1 << levels == c
    t = np.arange(c)[:, None]
    u = np.arange(c)[None, :]
    mats, masks = [], []
    for l in range(levels):
        half = 1 << l
        base = (t // half) * half
        upper = ((t >> l) & 1) == 1
        a_q = (u >= base) & (u <= t)
        a_k = (u > t) & (u <= base + half - 1)
        mats.append(np.where(upper, a_q, a_k))
        same = (t >> (l + 1)) == (u >> (l + 1))
        masks.append(same & upper & (((u >> l) & 1) == 0))
    mats.append(u <= t)
    mats.append(u > t)
    masks.append(t == u)
    amat = np.concatenate(mats, axis=0).astype(np.float32)
    mask = np.stack([np.tile(m, (GLA_HEADS, 1)) for m in masks]).astype(np.float32)
    return jnp.asarray(amat, BF16), jnp.asarray(mask, F32), levels


def _gla_kernel(q_ref, k_ref, v_ref, lg_ref, sr_ref, s0_ref, amat_ref, mask_ref, g_ref,
                o_ref, sout_ref, state, *, levels):
    c = q_ref.shape[0]
    hk = GLA_HEADS * GLA_DK
    step = pl.program_id(1)

    @pl.when(step == 0)
    def _():
        state[...] = s0_ref[0]

    lg = lg_ref[...]
    parts = _split3(lg)
    e3 = _dot(amat_ref[...], jnp.concatenate(parts, axis=1))
    ex = jnp.exp(e3[:, :hk] + e3[:, hk:2 * hk] + e3[:, 2 * hk:])
    q = q_ref[...].astype(F32)
    k = k_ref[...].astype(F32)
    v = v_ref[...]
    head = lax.broadcasted_iota(jnp.int32, (1, hk), 1) // GLA_DK

    def per_head(xf):
        return jnp.concatenate([jnp.where(head == h, xf, 0.0) for h in range(GLA_HEADS)], axis=0).astype(BF16)

    att = jnp.zeros((GLA_HEADS * c, c), F32)
    for l in range(levels + 1):
        if l < levels:
            el = ex[l * c:(l + 1) * c]
            ql, kl = q * el, k * el
        else:
            ql, kl = q, k
        a = lax.dot_general(per_head(ql), kl.astype(BF16), (((1,), (1,)), ((), ())), preferred_element_type=F32)
        att = att + a * mask_ref[l]
    s_prev = state[...]
    o_inter = _dot(per_head(q * ex[levels * c:(levels + 1) * c]), s_prev.astype(BF16))
    att = att.astype(BF16)
    g = g_ref[...]
    for h in range(GLA_HEADS):
        o = o_inter[h * c:(h + 1) * c] + _dot(att[h * c:(h + 1) * c], v[:, GLA_DV * h:GLA_DV * (h + 1)])
        o = _rms(o, g)
        o_ref[:, GLA_DV * h:GLA_DV * (h + 1)] = (o * sr_ref[:, GLA_DV * h:GLA_DV * (h + 1)].astype(F32)).astype(BF16)
    kr = (k * ex[(levels + 1) * c:]).astype(BF16)
    upd = lax.dot_general(kr, v, (((0,), (0,)), ((), ())), preferred_element_type=F32)
    ones = jnp.ones((c, GLA_DV), BF16)
    b_last = sum(lax.dot_general(p, ones, (((0,), (0,)), ((), ())), preferred_element_type=F32) for p in parts)
    new = jnp.exp(b_last) * s_prev + jnp.concatenate(
        [upd[GLA_DK * h:GLA_DK * (h + 1), GLA_DV * h:GLA_DV * (h + 1)] for h in range(GLA_HEADS)], axis=0)
    state[...] = new

    @pl.when(step == pl.num_programs(1) - 1)
    def _():
        sout_ref[0] = new


def _gla(q, k, v, lg, sr, s0, g, tables, batch, blocks_per_seq):
    amat, mask, levels = tables
    c = GLA_BLOCK
    hk, hv = GLA_HEADS * GLA_DK, GLA_HEADS * GLA_DV
    row = lambda n: pl.BlockSpec((c, n), lambda b, s: (b * blocks_per_seq + s, 0))
    st = pl.BlockSpec((1, hk, GLA_DV), lambda b, s: (b, 0, 0))
    return pl.pallas_call(
        functools.partial(_gla_kernel, levels=levels),
        grid=(batch, blocks_per_seq),
        in_specs=[row(hk), row(hk), row(hv), row(hk), row(hv), st,
                  _const_spec(amat.shape), _const_spec(mask.shape), _const_spec(g.shape)],
        out_specs=[row(hv), st],
        out_shape=[jax.ShapeDtypeStruct((batch * blocks_per_seq * c, hv), BF16),
                   jax.ShapeDtypeStruct((batch, hk, GLA_DV), F32)],
        scratch_shapes=[pltpu.VMEM((hk, GLA_DV), F32)],
        compiler_params=_params(("parallel", "arbitrary")),
        name="gla",
    )(q, k, v, lg, sr, s0, amat, mask, g)


def _mla_kernel(q_ref, kt_ref, kv_ref, o_ref, s_sc, p_sc, linv_sc, acc_sc, *, tq, n_past, past_valid, own_valid,
                diag_chunks):
    kvb = MLA_KV_TILE
    qi = pl.program_id(1)
    qs = jnp.concatenate([q_ref[:, MLA_QK * h:MLA_QK * (h + 1)] for h in range(MLA_HEADS)], axis=0)
    m_rows = MLA_HEADS * tq
    n_vis = n_past + qi + 1

    def score(j, mask):
        s = _dot(qs, kt_ref[0, j])
        s_sc[j] = s if mask is None else jnp.where(mask, s, NEG_INF)

    def scores(lo, hi):
        def body(j, carry):
            score(j, None)
            return carry
        lax.fori_loop(lo, hi, body, 0)

    col = lax.broadcasted_iota(jnp.int32, (m_rows, kvb), 1)
    n_past_full = past_valid // kvb
    if n_past_full:
        scores(0, n_past_full)
    if n_past_full < n_past:
        score(n_past_full, col < past_valid - n_past_full * kvb)
    scores(n_past, n_past + qi)
    mask = col < own_valid - qi * kvb
    if diag_chunks:
        qrow = lax.broadcasted_iota(jnp.int32, (m_rows, kvb), 0) % tq
        mask = mask & (col // CHUNK <= qrow // CHUNK)
    score(n_past + qi, mask)

    rc = MLA_ROW_STRIP

    def strip(r, carry):
        rows = pl.ds(pl.multiple_of(r * rc, rc), rc)
        mx = lax.fori_loop(0, n_vis, lambda j, m: jnp.maximum(m, s_sc[j, rows, :]),
                           jnp.full((rc, kvb), -jnp.inf, F32))
        m = jnp.max(mx, axis=1, keepdims=True)

        def body(j, l):
            p = jnp.exp2(s_sc[j, rows, :] - m)
            p_sc[j, rows, :] = p.astype(BF16)
            return l + p
        l = lax.fori_loop(0, n_vis, body, jnp.zeros((rc, kvb), F32))
        linv_sc[rows, :] = 1.0 / jnp.sum(l, axis=1, keepdims=True)
        return carry
    lax.fori_loop(0, m_rows // rc, strip, 0)

    acc_sc[...] = jnp.zeros(acc_sc.shape, F32)

    def pv(j, carry):
        vblk = kv_ref[0, pl.ds(pl.multiple_of(j * kvb, kvb), kvb), :MLA_KV_LORA]
        acc_sc[...] += _dot(p_sc[j], vblk)
        return carry
    lax.fori_loop(0, n_vis, pv, 0)
    o = acc_sc[...] * linv_sc[...]
    for h in range(MLA_HEADS):
        o_ref[:, MLA_KV_LORA * h:MLA_KV_LORA * (h + 1)] = o[h * tq:(h + 1) * tq].astype(BF16)


def _mla(q_all, kt, kv, batch, nq, tq, q_block0, past_valid, own_valid):
    n_kv = kt.shape[1]
    n_past = -(-past_valid // MLA_KV_TILE)
    assert nq == 1 or tq == MLA_KV_TILE
    assert n_kv == n_past + nq
    return pl.pallas_call(
        functools.partial(_mla_kernel, tq=tq, n_past=n_past, past_valid=past_valid, own_valid=own_valid,
                          diag_chunks=nq > 1),
        grid=(batch, nq),
        in_specs=[pl.BlockSpec((tq, MLA_HEADS * MLA_QK), lambda b, i: (q_block0 + b * nq + i, 0)),
                  pl.BlockSpec((1,) + kt.shape[1:], lambda b, i: (b, 0, 0, 0)),
                  pl.BlockSpec((1,) + kv.shape[1:], lambda b, i: (b, 0, 0))],
        out_specs=pl.BlockSpec((tq, MLA_HEADS * MLA_KV_LORA), lambda b, i: (b * nq + i, 0)),
        out_shape=jax.ShapeDtypeStruct((batch * nq * tq, MLA_HEADS * MLA_KV_LORA), BF16),
        scratch_shapes=[pltpu.VMEM((n_kv, MLA_HEADS * tq, MLA_KV_TILE), F32),
                        pltpu.VMEM((n_kv, MLA_HEADS * tq, MLA_KV_TILE), BF16),
                        pltpu.VMEM((MLA_HEADS * tq, 1), F32),
                        pltpu.VMEM((MLA_HEADS * tq, MLA_KV_LORA), F32)],
        compiler_params=_params(("parallel", "arbitrary")),
        name="mla",
    )(q_all, kt, kv)


def _post_mixer(x, out, g, b, rw_ref, x1_ref, x1b_ref, lgt_ref):
    y = _layer_norm(ALPHA * x + out, g, b)
    x1_ref[...] = y
    x1b_ref[...] = y.astype(BF16)
    lgt_ref[0] = _dot_f32_nt(rw_ref[...], y)


def _outproj_kernel(og_ref, ol_ref, x_ref, wuv_ref, wo_ref, g_ref, b_ref, rw_ref, x1_ref, x1b_ref, lgt_ref):
    o_mla = _dot(ol_ref[...], wuv_ref[...]).astype(BF16)
    n_gla = og_ref.shape[1]
    out = _dot(og_ref[...], wo_ref[:n_gla, :]) + _dot(o_mla, wo_ref[n_gla:, :])
    _post_mixer(x_ref[...], out, g_ref[...], b_ref[...], rw_ref, x1_ref, x1b_ref, lgt_ref)


def _outproj(o_gla, o_lat, x_all, w_uvbd, w_out, g, b, rw):
    t, d = x_all.shape
    tm = TOK_TILE
    row = lambda n: pl.BlockSpec((tm, n), lambda i: (i, 0))
    consts = [w_uvbd, w_out, g, b, rw]
    return pl.pallas_call(
        _outproj_kernel,
        grid=(t // tm,),
        in_specs=[row(o_gla.shape[1]), row(o_lat.shape[1]), row(d)] + [_const_spec(c.shape) for c in consts],
        out_specs=[row(d), row(d), pl.BlockSpec((1, N_EXPERTS, tm), lambda i: (i, 0, 0))],
        out_shape=[jax.ShapeDtypeStruct((t, d), F32), jax.ShapeDtypeStruct((t, d), BF16),
                   jax.ShapeDtypeStruct((t // tm, N_EXPERTS, tm), F32)],
        compiler_params=_params(("parallel",)),
        name="outproj",
    )(o_gla, o_lat, x_all, *consts)


def _pool_kernel(x_ref, prev_ref, hist_ref, pw_ref, ps_ref, g_ref, b_ref, rw_ref, x1_ref, x1b_ref, lgt_ref):
    ts = x_ref.shape[0]
    pm = POOL_MAX
    x = x_ref[...]
    prev = jnp.where(pl.program_id(1) == 0, hist_ref[0], prev_ref[...])
    xc = jnp.concatenate([prev, x], axis=0)
    grp = x.shape[1] // len(POOL_WINDOWS)
    outs = []
    for gi, w in enumerate(POOL_WINDOWS):
        s = xc[:, gi * grp:(gi + 1) * grp]
        span = 1
        while span < w:
            s = s + pltpu.roll(s, span, 0)
            span *= 2
        win = s[pm:]
        mix = (win * (1.0 / w) - x[:, gi * grp:(gi + 1) * grp]).astype(BF16)
        outs.append(_dot(mix, pw_ref[gi]))
    out = jnp.concatenate(outs, axis=1) * ps_ref[...]
    _post_mixer(x, out, g_ref[...], b_ref[...], rw_ref, x1_ref, x1b_ref, lgt_ref)


def _pool(x_all, hist, pool_w, pool_scale, g, b, rw, batch, tiles_per_seq, ts, tile0):
    d = x_all.shape[1]
    pm = POOL_MAX
    per = ts // pm
    consts = [pool_w, pool_scale, g, b, rw]
    row = lambda n: pl.BlockSpec((ts, n), lambda bb, i: (bb * tiles_per_seq + i, 0))
    return pl.pallas_call(
        _pool_kernel,
        grid=(batch, tiles_per_seq),
        in_specs=[pl.BlockSpec((ts, d), lambda bb, i: (tile0 + bb * tiles_per_seq + i, 0)),
                  pl.BlockSpec((pm, d), lambda bb, i: (jnp.maximum((tile0 + bb * tiles_per_seq + i) * per - 1, 0), 0)),
                  pl.BlockSpec((1, pm, d), lambda bb, i: (bb, 0, 0))] + [_const_spec(c.shape) for c in consts],
        out_specs=[row(d), row(d), pl.BlockSpec((1, N_EXPERTS, ts), lambda bb, i: (bb * tiles_per_seq + i, 0, 0))],
        out_shape=[jax.ShapeDtypeStruct((batch * tiles_per_seq * ts, d), F32),
                   jax.ShapeDtypeStruct((batch * tiles_per_seq * ts, d), BF16),
                   jax.ShapeDtypeStruct((batch * tiles_per_seq, N_EXPERTS, ts), F32)],
        compiler_params=_params(("parallel", "arbitrary")),
        name="pool",
    )(x_all, x_all, hist, *consts)


def _experts_kernel(be_ref, nb_ref, x_ref, wg_ref, wu_ref, wd_ref, y_ref):
    @pl.when(pl.program_id(0) < nb_ref[0])
    def _():
        wg, wu, wd = wg_ref[0, 0].astype(BF16), wu_ref[0, 0].astype(BF16), wd_ref[0, 0].astype(BF16)
        sub = EXPERT_SUBTILE
        for r in range(x_ref.shape[0] // sub):
            x = x_ref[r * sub:(r + 1) * sub, :]
            hmid = (_silu(_dot(x, wg)) * _dot(x, wu)).astype(BF16)
            y_ref[r * sub:(r + 1) * sub, :] = _dot(hmid, wd).astype(BF16)

    @pl.when(pl.program_id(0) >= nb_ref[0])
    def _():
        y_ref[...] = jnp.zeros_like(y_ref)


def _experts(blk_e, n_used, x_sorted, wg, wu, wd, layer):
    p, d = x_sorted.shape
    f = wg.shape[3]
    bm = EXPERT_TILE
    blk = lambda i, be, nb: (i, 0)
    grid_spec = pltpu.PrefetchScalarGridSpec(
        num_scalar_prefetch=2,
        grid=(p // bm,),
        in_specs=[pl.BlockSpec((bm, d), lambda i, be, nb: (jnp.minimum(i, nb[0] - 1), 0)),
                  pl.BlockSpec((1, 1, d, f), lambda i, be, nb: (layer, be[i], 0, 0)),
                  pl.BlockSpec((1, 1, d, f), lambda i, be, nb: (layer, be[i], 0, 0)),
                  pl.BlockSpec((1, 1, f, d), lambda i, be, nb: (layer, be[i], 0, 0))],
        out_specs=pl.BlockSpec((bm, d), blk),
    )
    return pl.pallas_call(
        _experts_kernel,
        grid_spec=grid_spec,
        out_shape=jax.ShapeDtypeStruct((p, d), BF16),
        compiler_params=_params(("arbitrary",)),
        name="experts",
    )(blk_e, n_used, x_sorted, wg, wu, wd)


def _segment_copies(tile, slot, dst_ref, n16_ref, seg_ref, make, act):
    def per_expert(e, carry):
        j = tile * N_EXPERTS + e
        d0, s0 = dst_ref[j], seg_ref[j]

        def per_piece(c, carry2):
            act(make(pl.multiple_of(d0 + c * SEG_ROWS, SEG_ROWS), pl.multiple_of(s0 + c * SEG_ROWS, SEG_ROWS), slot))
            return carry2
        lax.fori_loop(0, n16_ref[j], per_piece, 0)
        return carry
    lax.fori_loop(0, N_EXPERTS, per_expert, 0)


def _split_rowid(rid):
    hi = jnp.floor(rid * (1.0 / LANES))
    return hi.astype(BF16), (rid - LANES * hi).astype(BF16)


def _ffn_out_kernel(dst_ref, n16_ref, seg_ref, x_ref, xb_ref, ridt_ref, wt_ref, segc_ref, c16c_ref, ys_ref,
                    sg_ref, su_ref, sd_ref, g_ref, b_ref, y_ref, yb_ref, ybuf, racc, sem):
    i = pl.program_id(0)
    nt = pl.num_programs(0)
    slot = i % 2
    tm = x_ref.shape[0]

    def make(d0, s0, sl):
        return pltpu.make_async_copy(ys_ref.at[pl.ds(d0, SEG_ROWS)], ybuf.at[sl, pl.ds(s0, SEG_ROWS)], sem.at[sl])
    start = lambda cp: cp.start()
    wait = lambda cp: cp.wait()

    @pl.when(i == 0)
    def _():
        ybuf[...] = jnp.zeros_like(ybuf)
        _segment_copies(0, 0, dst_ref, n16_ref, seg_ref, make, start)

    @pl.when(i + 1 < nt)
    def _():
        _segment_copies(i + 1, 1 - slot, dst_ref, n16_ref, seg_ref, make, start)

    xb = xb_ref[...]
    hmid = (_silu(_dot(xb, sg_ref[...])) * _dot(xb, su_ref[...])).astype(BF16)
    shared = _dot(hmid, sd_ref[...])
    _segment_copies(i, slot, dst_ref, n16_ref, seg_ref, make, wait)

    rid_hi, rid_lo = _split_rowid(ridt_ref[0])
    w = wt_ref[0]
    w_hi = w.astype(BF16)
    w_lo = (w - w_hi.astype(F32)).astype(BF16)
    last = i * N_EXPERTS + N_EXPERTS - 1
    used = seg_ref[last] + n16_ref[last] * SEG_ROWS
    racc[...] = jnp.zeros_like(racc)
    rs = COMBINE_STRIP
    for s in range(ybuf.shape[1] // rs):
        @pl.when(s * rs < used)
        def _():
            r = lax.broadcasted_iota(jnp.int32, (N_EXPERTS, rs), 1) + s * rs
            owner = jnp.where((r >= segc_ref[0]) & (r < segc_ref[0] + c16c_ref[0]), 1.0, 0.0).astype(BF16)
            rowid = LANES * _dot(rid_hi, owner) + _dot(rid_lo, owner)
            wrow = _dot(w_hi, owner) + _dot(w_lo, owner)
            rcol = (lax.broadcasted_iota(jnp.int32, (tm, rs), 1) + (s * rs + 1)).astype(F32)
            placed = jnp.where(rowid == rcol, wrow, 0.0).astype(BF16)
            racc[...] += _dot(placed, ybuf[slot, s * rs:(s + 1) * rs, :])
    ffn = racc[...] + shared
    y = _layer_norm(ALPHA * x_ref[...] + ffn, g_ref[...], b_ref[...])
    y_ref[...] = y
    yb_ref[...] = y.astype(BF16)


def _ffn_out(dst, n16, seg, x1, x1b, ridt, wt, segc, c16c, y_sorted, sg, su, sd, g, b):
    t, d = x1.shape
    nt, tm, ne = ridt.shape
    row = lambda n: pl.BlockSpec((tm, n), lambda i, *_: (i, 0))
    per_tile = lambda a: pl.BlockSpec((1,) + a.shape[1:], lambda i, *_: (i, 0, 0))
    consts = [sg, su, sd, g, b]
    grid_spec = pltpu.PrefetchScalarGridSpec(
        num_scalar_prefetch=3,
        grid=(nt,),
        in_specs=[row(d), row(d), per_tile(ridt), per_tile(wt), per_tile(segc), per_tile(c16c),
                  pl.BlockSpec(memory_space=pl.ANY)] + [_const_spec(c.shape) for c in consts],
        out_specs=[row(d), row(d)],
        scratch_shapes=[pltpu.VMEM((2, SORT_ROWS, d), BF16), pltpu.VMEM((tm, d), F32), pltpu.SemaphoreType.DMA((2,))],
    )
    return pl.pallas_call(
        _ffn_out_kernel,
        grid_spec=grid_spec,
        out_shape=[jax.ShapeDtypeStruct((t, d), F32), jax.ShapeDtypeStruct((t, d), BF16)],
        compiler_params=_params(("arbitrary",)),
        name="ffn_out",
    )(dst, n16, seg, x1, x1b, ridt, wt, segc, c16c, y_sorted, *consts)


def _dispatch_kernel(dst_ref, n16_ref, seg_ref, x_ref, rid_ref, segr_ref, c16r_ref, xs_in_ref, xs_ref, buf, sem):
    del xs_in_ref
    i = pl.program_id(0)
    nt = pl.num_programs(0)
    slot = i % 2
    tm = x_ref.shape[0]

    def make(d0, s0, sl):
        return pltpu.make_async_copy(buf.at[sl, pl.ds(s0, SEG_ROWS)], xs_ref.at[pl.ds(d0, SEG_ROWS)], sem.at[sl])
    start = lambda cp: cp.start()
    wait = lambda cp: cp.wait()

    @pl.when(i >= 2)
    def _():
        _segment_copies(i - 2, slot, dst_ref, n16_ref, seg_ref, make, wait)

    x = x_ref[...]
    rid2 = jnp.concatenate(_split_rowid(rid_ref[0]), axis=1)
    last = i * N_EXPERTS + N_EXPERTS - 1
    used = seg_ref[last] + n16_ref[last] * SEG_ROWS
    rs = DISPATCH_STRIP
    for s in range(buf.shape[1] // rs):
        @pl.when(s * rs < used)
        def _():
            r = lax.broadcasted_iota(jnp.int32, (rs, N_EXPERTS), 0) + s * rs
            owner = jnp.where((r >= segr_ref[0]) & (r < segr_ref[0] + c16r_ref[0]), 1.0, 0.0).astype(BF16)
            g2 = _dot(owner, rid2)
            rowid = LANES * g2[:, :tm] + g2[:, tm:]
            rrow = (lax.broadcasted_iota(jnp.int32, (rs, tm), 0) + (s * rs + 1)).astype(F32)
            onehot = jnp.where(rowid == rrow, 1.0, 0.0).astype(BF16)
            buf[slot, s * rs:(s + 1) * rs, :] = _dot(onehot, x).astype(BF16)
    _segment_copies(i, slot, dst_ref, n16_ref, seg_ref, make, start)

    @pl.when(i == nt - 1)
    def _():
        @pl.when(i >= 1)
        def _():
            _segment_copies(i - 1, 1 - slot, dst_ref, n16_ref, seg_ref, make, wait)
        _segment_copies(i, slot, dst_ref, n16_ref, seg_ref, make, wait)


def _dispatch(dst, n16, seg, x1b, rid, segr, c16r, n_rows):
    t, d = x1b.shape
    nt, ne, tm = rid.shape
    per_tile = lambda a: pl.BlockSpec((1,) + a.shape[1:], lambda i, *_: (i, 0, 0))
    grid_spec = pltpu.PrefetchScalarGridSpec(
        num_scalar_prefetch=3,
        grid=(nt,),
        in_specs=[pl.BlockSpec((tm, d), lambda i, *_: (i, 0)), per_tile(rid), per_tile(segr), per_tile(c16r),
                  pl.BlockSpec(memory_space=pl.ANY)],
        out_specs=pl.BlockSpec(memory_space=pl.ANY),
        scratch_shapes=[pltpu.VMEM((2, SORT_ROWS, d), BF16), pltpu.SemaphoreType.DMA((2,))],
    )
    return pl.pallas_call(
        _dispatch_kernel,
        grid_spec=grid_spec,
        out_shape=jax.ShapeDtypeStruct((n_rows, d), BF16),
        input_output_aliases={7: 0},
        compiler_params=_params(("arbitrary",)),
        name="dispatch",
    )(dst, n16, seg, x1b, rid, segr, c16r, jnp.zeros((n_rows, d), BF16))


def _outranked(vals, n, ids):
    rank = jnp.zeros(vals.shape, jnp.int32)
    for j in range(n):
        row = vals[j:j + 1]
        rank = rank + ((row > vals) | ((row == vals) & (j < ids))).astype(jnp.int32)
    return rank


def _router_kernel(lg_ref, b_ref, tri_ref, low_ref, rid_ref, w_ref, tab_ref, carry):
    @pl.when(pl.program_id(0) == 0)
    def _():
        carry[...] = jnp.zeros_like(carry)

    tm = lg_ref.shape[2]
    gs = N_EXPERTS // N_GROUPS
    scores = 1.0 / (1.0 + jnp.exp(-lg_ref[0]))
    biased = scores + b_ref[...]
    member = lax.broadcasted_iota(jnp.int32, (gs, tm), 0)
    group_score = []
    for g in range(N_GROUPS):
        blk = biased[g * gs:(g + 1) * gs]
        m1 = jnp.max(blk, axis=0, keepdims=True)
        first = jnp.min(jnp.where(blk == m1, member, gs), axis=0, keepdims=True)
        m2 = jnp.max(jnp.where(member == first, -jnp.inf, blk), axis=0, keepdims=True)
        group_score.append(m1 + m2)
    group_score = jnp.concatenate(group_score, axis=0)
    gid = lax.broadcasted_iota(jnp.int32, (N_GROUPS, tm), 0)
    group_ok = _outranked(group_score, N_GROUPS, gid) < TOPK_GROUPS
    masked = jnp.concatenate([jnp.where(group_ok[g:g + 1], biased[g * gs:(g + 1) * gs], -jnp.inf)
                              for g in range(N_GROUPS)], axis=0)
    eid = lax.broadcasted_iota(jnp.int32, (N_EXPERTS, tm), 0)
    sel = _outranked(masked, N_EXPERTS, eid) < TOP_K
    self = jnp.where(sel, 1.0, 0.0)
    wsel = self * scores
    wts = wsel / jnp.sum(wsel, axis=0, keepdims=True) * ROUTED_SCALE
    selb = self.astype(BF16)
    before = _dot(selb, tri_ref[...])
    pieces = jnp.floor((jnp.sum(self, axis=1, keepdims=True) + (SEG_ROWS - 1.0)) * (1.0 / SEG_ROWS))
    seg = SEG_ROWS * _dot(low_ref[...], jnp.broadcast_to(pieces, (N_EXPERTS, LANES)).astype(BF16))
    rid_ref[0] = jnp.where(sel, seg[:, :1] + before + 1.0, NO_ROW)
    w_ref[0] = wts
    lane = lax.broadcasted_iota(jnp.int32, (N_EXPERTS, LANES), 1)
    tab_ref[0] = jnp.where(lane == 0, seg, jnp.where(lane == 1, SEG_ROWS * pieces, jnp.where(lane == 2, carry[...], 0.0)))
    carry[...] = carry[...] + SEG_ROWS * pieces


def _router(logits, router_b):
    nt, ne, tm = logits.shape
    tri = jnp.asarray(np.triu(np.ones((tm, tm), np.float32), 1), BF16)
    low = jnp.asarray(np.tril(np.ones((ne, ne), np.float32), -1), BF16)
    per_tile = lambda n: pl.BlockSpec((1, ne, n), lambda i: (i, 0, 0))
    return pl.pallas_call(
        _router_kernel,
        grid=(nt,),
        in_specs=[per_tile(tm), _const_spec((ne, 1)), _const_spec(tri.shape), _const_spec(low.shape)],
        out_specs=[per_tile(tm), per_tile(tm), per_tile(LANES)],
        out_shape=[jax.ShapeDtypeStruct((nt, ne, tm), F32), jax.ShapeDtypeStruct((nt, ne, tm), F32),
                   jax.ShapeDtypeStruct((nt, ne, LANES), F32)],
        scratch_shapes=[pltpu.VMEM((ne, 1), F32)],
        compiler_params=_params(("arbitrary",)),
        name="router",
    )(logits, router_b.reshape(ne, 1).astype(F32), tri, low)


def _moe(x1, x1b, logits, router_b, layer, wg, wu, wd, sg, su, sd, g, b):
    t, d = x1.shape
    nt, ne, tm = logits.shape
    assert SORT_ROWS >= TOP_K * tm + ne * (SEG_ROWS - 1) and SORT_ROWS < NO_ROW
    rid, wdense, tab = _router(logits, router_b)
    tab = tab[:, :, :3].astype(jnp.int32)
    seg, rows, base = tab[:, :, 0], tab[:, :, 1], tab[:, :, 2]
    bm = EXPERT_TILE
    region = base[-1] + rows[-1]
    padded = (region + bm - 1) // bm * bm
    pad_end = jnp.cumsum(padded)
    dst = (pad_end - padded)[None, :] + base
    n_blocks = -(-(t * TOP_K + nt * ne * (SEG_ROWS - 1)) // bm) + ne
    blk_start = jnp.arange(n_blocks, dtype=jnp.int32) * bm
    blk_e = jnp.minimum(jnp.sum((pad_end[None, :] <= blk_start[:, None]).astype(jnp.int32), axis=1), ne - 1)
    n_used = (pad_end[-1] // bm).astype(jnp.int32).reshape(1)
    dst, n16, segf = dst.reshape(-1), (rows // SEG_ROWS).reshape(-1), seg.reshape(-1)
    x_sorted = _dispatch(dst, n16, segf, x1b, rid, seg[:, None, :], rows[:, None, :], n_blocks * bm)
    y_sorted = _experts(blk_e, n_used, x_sorted, wg, wu, wd, layer)
    return _ffn_out(dst, n16, segf, x1, x1b, rid.transpose(0, 2, 1), wdense.transpose(0, 2, 1),
                    seg[:, :, None], rows[:, :, None], y_sorted, sg, su, sd, g, b)


def _rope_tables(pos):
    half = MLA_ROPE // 2
    inv = ROPE_THETA ** (-jnp.arange(half, dtype=F32) / half)
    ang = pos.astype(F32)[:, None] * inv
    reps = LANES // half
    return jnp.tile(jnp.cos(ang), (1, reps)), jnp.tile(jnp.sin(ang), (1, reps))


def _pack_layer0_weights(w_in0, gla_w_g2, gla_b_g, mla_q_norm_g, mla_kv_norm_g, mla_w_uq, mla_w_uk, mla_w_uv):
    d = w_in0.shape[0]
    hk, hv = GLA_HEADS * GLA_DK, GLA_HEADS * GLA_DV
    o_q, o_k, o_v, o_r = 0, hk, 2 * hk, 2 * hk + hv
    o_a = o_r + hv
    o_cq = o_a + GLA_RANK
    o_ckv = o_cq + MLA_Q_LORA
    o_kr = o_ckv + MLA_KV_LORA
    half = MLA_ROPE // 2
    zeros = lambda n: jnp.zeros((d, n), w_in0.dtype)
    kr1, kr2 = w_in0[:, o_kr:o_kr + half], w_in0[:, o_kr + half:o_kr + MLA_ROPE]
    w_in = jnp.concatenate([
        w_in0[:, o_q:o_a], w_in0[:, o_cq:o_kr],
        kr1, kr2, zeros(LANES - MLA_ROPE),
        kr2, kr1, zeros(LANES - MLA_ROPE),
        w_in0[:, o_a:o_cq], zeros(LANES - GLA_RANK)], axis=1).astype(BF16)
    assert w_in.shape[1] == _C_END
    w_g2 = jnp.concatenate([gla_w_g2, jnp.zeros((LANES - GLA_RANK, hk), gla_w_g2.dtype)], axis=0)
    uq = mla_w_uq.reshape(MLA_Q_LORA, MLA_HEADS, MLA_NOPE + MLA_ROPE)
    w_uq = jnp.concatenate([uq[:, :, :MLA_NOPE].reshape(MLA_Q_LORA, -1),
                            uq[:, :, MLA_NOPE:MLA_NOPE + half].reshape(MLA_Q_LORA, -1),
                            uq[:, :, MLA_NOPE + half:].reshape(MLA_Q_LORA, -1)], axis=1).astype(BF16)
    eye = jnp.eye(MLA_HEADS, dtype=mla_w_uk.dtype)
    w_abs = jnp.einsum('chn,hg->hngc', mla_w_uk, eye).reshape(MLA_HEADS * MLA_NOPE, MLA_HEADS * MLA_KV_LORA).astype(BF16)
    w_uvbd = jnp.einsum('chv,hg->hcgv', mla_w_uv, eye).reshape(MLA_HEADS * MLA_KV_LORA, MLA_HEADS * MLA_V).astype(BF16)
    place = np.zeros((2 * LANES, MLA_HEADS * LANES), np.float32)
    for h in range(MLA_HEADS):
        for j in range(half):
            place[h * half + j, h * LANES + j] = 1.0
            place[LANES + h * half + j, h * LANES + half + j] = 1.0
    return dict(w_in=w_in, w_g2=w_g2, b_g=gla_b_g.reshape(1, hk), q_norm=mla_q_norm_g.reshape(1, -1),
                kv_norm=mla_kv_norm_g.reshape(1, -1), w_uq=w_uq, w_abs=w_abs,
                place=jnp.asarray(place, BF16)), w_uvbd


def _pad_rows(x, n):
    return jnp.pad(x, ((0, n - x.shape[0]),) + ((0, 0),) * (x.ndim - 1))


def _kv_tiles(past, own):
    kvb = MLA_KV_TILE
    pad = lambda a: jnp.pad(a, ((0, 0), (0, -a.shape[1] % kvb), (0, 0)))
    kv = jnp.concatenate([pad(past), pad(own)], axis=1) if past.shape[1] else pad(own)
    b, n, w = kv.shape
    kt = kv.reshape(b, n // kvb, kvb, w).transpose(0, 1, 3, 2)
    return kt, kv


def kernel(x_prompt, x_sample, cache_mla_ckv, cache_mla_krope, state_gla, cache_pool, meta_tokens, w_in0, gla_w_g2, gla_b_g, gla_norm_g, mla_q_norm_g, mla_kv_norm_g, mla_w_uq, mla_w_uk, mla_w_uv, w_out0, pool_w, pool_scale, ln_g, ln_b, moe_router_w, moe_router_b, moe_w_gate, moe_w_up, moe_w_down, moe_sh_gate, moe_sh_up, moe_sh_down):
    bp, sp, d = x_prompt.shape
    bs, ss, _ = x_sample.shape
    n_meta = meta_tokens.shape[0]
    past_len = cache_mla_ckv.shape[1] - n_meta
    tp, tsm = bp * sp, bs * ss
    assert sp % MLA_Q_TILE == 0 and sp % POOL_TILE == 0 and sp % GLA_BLOCK == 0 and sp % CHUNK == 0
    assert ss <= CHUNK and past_len % CHUNK == 0 and n_meta <= CHUNK and ss <= GLA_BLOCK and n_meta <= GLA_BLOCK
    assert n_meta == POOL_MAX and ss >= POOL_MAX and tp % TOK_TILE == 0
    t_all = -(-(tp + tsm + n_meta) // TOK_TILE) * TOK_TILE
    o_s, o_m = tp, tp + tsm

    x_all = _pad_rows(jnp.concatenate([x_prompt.reshape(tp, d), x_sample.reshape(tsm, d), meta_tokens], axis=0), t_all)
    pos = jnp.concatenate([jnp.tile(n_meta + jnp.arange(sp), bp), jnp.tile(n_meta + past_len + jnp.arange(ss), bs),
                           jnp.arange(n_meta), jnp.zeros((t_all - o_m - n_meta,), jnp.int32)])
    cos_t, sin_t = _rope_tables(pos)
    w0, w_uvbd = _pack_layer0_weights(w_in0, gla_w_g2, gla_b_g, mla_q_norm_g, mla_kv_norm_g, mla_w_uq, mla_w_uk, mla_w_uv)

    q, k, v, sr, lg, q_all, ckv, kr, k_all = _inproj(x_all, cos_t, sin_t, w0)

    tables = _gla_tables(GLA_BLOCK)
    g_gla = gla_norm_g.reshape(1, GLA_DV)
    c = GLA_BLOCK
    hk = GLA_HEADS * GLA_DK

    def small(a):
        sm = jnp.pad(a[o_s:o_m].reshape(bs, ss, -1), ((0, 0), (0, c - ss), (0, 0)))
        me = jnp.pad(a[o_m:o_m + n_meta], ((0, c - n_meta), (0, 0)))[None]
        return jnp.concatenate([sm, me], axis=0).reshape((bs + 1) * c, -1)

    s0_small = jnp.concatenate([state_gla.reshape(bs, hk, GLA_DV), jnp.zeros((1, hk, GLA_DV), F32)], axis=0)
    og_small, st_small = _gla(small(q), small(k), small(v), small(lg), small(sr), s0_small, g_gla, tables, bs + 1, 1)
    s0_prompt = jnp.broadcast_to(st_small[bs:], (bp, hk, GLA_DV))
    og_prompt, st_prompt = _gla(q, k, v, lg, sr, s0_prompt, g_gla, tables, bp, sp // c)
    og_small = og_small.reshape(bs + 1, c, -1)
    o_gla = _pad_rows(jnp.concatenate([og_prompt, og_small[:bs, :ss].reshape(tsm, -1), og_small[bs, :n_meta]], axis=0), t_all)

    k_meta = k_all[o_m:o_m + n_meta]
    kt_m, kv_m = _kv_tiles(jnp.zeros((1, 0, MLA_QK), BF16), k_meta[None])
    ol_meta = _mla(q_all[o_m:o_m + n_meta], kt_m, kv_m, 1, 1, n_meta, 0, 0, n_meta)
    cache = jnp.concatenate([cache_mla_ckv, cache_mla_krope,
                             jnp.zeros(cache_mla_ckv.shape[:2] + (MLA_QK - MLA_KV_LORA - MLA_ROPE,), F32)], axis=-1).astype(BF16)
    kt_s, kv_s = _kv_tiles(cache, k_all[o_s:o_m].reshape(bs, ss, MLA_QK))
    ol_sample = _mla(q_all[o_s:o_m], kt_s, kv_s, bs, 1, ss, 0, n_meta + past_len, ss)
    kt_p, kv_p = _kv_tiles(jnp.broadcast_to(k_meta[None], (bp, n_meta, MLA_QK)), k_all[:tp].reshape(bp, sp, MLA_QK))
    ol_prompt = _mla(q_all, kt_p, kv_p, bp, sp // MLA_Q_TILE, MLA_Q_TILE, 0, n_meta, sp)
    o_lat = _pad_rows(jnp.concatenate([ol_prompt, ol_sample, ol_meta], axis=0), t_all)

    ln = lambda l, j: (ln_g[l, j].reshape(1, d), ln_b[l, j].reshape(1, d))
    rw = lambda l: moe_router_w[l].T
    moe_w = lambda l: (l, moe_w_gate, moe_w_up, moe_w_down,
                       moe_sh_gate[l].astype(BF16), moe_sh_up[l].astype(BF16), moe_sh_down[l].astype(BF16))
    x1, x1b, logits = _outproj(o_gla, o_lat, x_all, w_uvbd, w_out0.astype(BF16), *ln(0, 0), rw(0))
    x2, _ = _moe(x1, x1b, logits, moe_router_b[0], *moe_w(0), *ln(0, 1))

    pm = POOL_MAX
    zrow = jnp.zeros((1, d), F32)
    hist_p = jnp.broadcast_to(jnp.concatenate([zrow, x2[o_m + n_meta - (pm - 1):o_m + n_meta]], axis=0)[None], (bp, pm, d))
    hist_s = jnp.concatenate([jnp.zeros((bs, 1, d), F32), cache_pool], axis=1)
    pool_wb = pool_w.astype(BF16)
    ps = pool_scale.reshape(1, d)
    y1p = _pool(x2, hist_p, pool_wb, ps, *ln(1, 0), rw(1), bp, sp // POOL_TILE, POOL_TILE, 0)
    y1s = _pool(x2[o_s:o_m], hist_s, pool_wb, ps, *ln(1, 0), rw(1), bs, 1, ss, 0)
    t1 = -(-(tp + tsm) // TOK_TILE) * TOK_TILE
    x3, x3b = [_pad_rows(jnp.concatenate([a, b_], axis=0), t1) for a, b_ in zip(y1p[:2], y1s[:2])]
    by_token = lambda lt: lt.transpose(1, 0, 2).reshape(N_EXPERTS, -1)
    logits1 = jnp.pad(jnp.concatenate([by_token(y1p[2]), by_token(y1s[2])], axis=1), ((0, 0), (0, t1 - tp - tsm)))
    logits1 = logits1.reshape(N_EXPERTS, t1 // TOK_TILE, TOK_TILE).transpose(1, 0, 2)
    x4, _ = _moe(x3, x3b, logits1, moe_router_b[1], *moe_w(1), *ln(1, 1))

    y_prompt = x4[:tp].reshape(bp, sp, d)
    y_sample = x4[o_s:o_m].reshape(bs, ss, d)
    ckv_meta, kr_meta = ckv[o_m:o_m + n_meta], kr[o_m:o_m + n_meta, :MLA_ROPE]
    p_ckv = jnp.concatenate([jnp.broadcast_to(ckv_meta[None], (bp, n_meta, MLA_KV_LORA)),
                             ckv[:tp].reshape(bp, sp, MLA_KV_LORA)], axis=1)
    p_kr = jnp.concatenate([jnp.broadcast_to(kr_meta[None], (bp, n_meta, MLA_ROPE)),
                            kr[:tp, :MLA_ROPE].reshape(bp, sp, MLA_ROPE)], axis=1)
    p_gla = st_prompt.reshape(bp, GLA_HEADS, GLA_DK, GLA_DV)
    x2p = x2[:tp].reshape(bp, sp, d)
    p_pool = x2p[:, sp - (pm - 1):]
    s_ckv = ckv[o_s:o_m].reshape(bs, ss, MLA_KV_LORA)
    s_kr = kr[o_s:o_m, :MLA_ROPE].reshape(bs, ss, MLA_ROPE)
    s_gla = st_small[:bs].reshape(bs, GLA_HEADS, GLA_DK, GLA_DV)
    s_pool = x2[o_s:o_m].reshape(bs, ss, d)[:, ss - (pm - 1):]
    return (y_prompt, y_sample, p_ckv, p_kr, p_gla, p_pool, s_ckv, s_kr, s_gla, s_pool)
```

```python
import functools
import math

import jax
import jax.numpy as jnp
import numpy as np
from jax import lax
from jax.experimental import pallas as pl
from jax.experimental.pallas import tpu as pltpu

F32 = jnp.float32
BF16 = jnp.bfloat16

CHUNK = 64
DEPTH = 2
ALPHA = (2 * DEPTH) ** 0.25
LN_EPS = 1e-5
RMS_EPS = 1e-6
NEG_INF = -1e30
GLA_HEADS = 4
GLA_DK = 64
GLA_DV = 128
GLA_RANK = 16
GLA_TAU = 16.0
MLA_HEADS = 8
MLA_Q_LORA = 256
MLA_KV_LORA = 128
MLA_NOPE = 64
MLA_ROPE = 32
MLA_V = 64
MLA_SCALE = (MLA_NOPE + MLA_ROPE) ** -0.5
ROPE_THETA = 10000.0
POOL_WINDOWS = (2, 4, 8, 16)
POOL_MAX = 16
N_EXPERTS = 64
TOP_K = 8
N_GROUPS = 8
TOPK_GROUPS = 4
ROUTED_SCALE = 2.5

LANES = 128
MXU_DIM = 256

TOK_TILE = 512
GLA_BLOCK = 128
MLA_Q_TILE = 256
MLA_KV_TILE = 256
MLA_QK = 256
MLA_ROW_STRIP = 128
LOG2E = math.log2(math.e)
EXPERT_TILE = 512
EXPERT_SUBTILE = 256
POOL_TILE = 512
SEG_ROWS = 16
BIG_PIECE = 4
DRAIN_UNITS = (64, 16, 4, 1)
SORT_ROWS = 5120
NO_ROW = 8191.0
DISPATCH_STRIP = 512
COMBINE_STRIP = 1024
VMEM_LIMIT = 56 * 1024 * 1024


def _params(sem, vmem=VMEM_LIMIT):
    return pltpu.CompilerParams(dimension_semantics=sem, vmem_limit_bytes=vmem)


def _const_spec(shape):
    nd = len(shape)
    return pl.BlockSpec(shape, lambda *_: (0,) * nd)


def _split3(x):
    hi = x.astype(BF16)
    r = x - hi.astype(F32)
    mid = r.astype(BF16)
    lo = (r - mid.astype(F32)).astype(BF16)
    return hi, mid, lo


def _dot(a, b):
    return jnp.dot(a, b, preferred_element_type=F32)


def _dot_f32(a, b):
    a_hi = a.astype(BF16)
    a_lo = (a - a_hi.astype(F32)).astype(BF16)
    b_hi = b.astype(BF16)
    b_lo = (b - b_hi.astype(F32)).astype(BF16)
    return _dot(a_hi, b_hi) + (_dot(a_lo, b_hi) + _dot(a_hi, b_lo))


def _dot_f32_nt(a, b):
    nt = lambda u, v: lax.dot_general(u, v, (((1,), (1,)), ((), ())), preferred_element_type=F32)
    a_hi = a.astype(BF16)
    a_lo = (a - a_hi.astype(F32)).astype(BF16)
    b_hi = b.astype(BF16)
    b_lo = (b - b_hi.astype(F32)).astype(BF16)
    return nt(a_hi, b_hi) + (nt(a_lo, b_hi) + nt(a_hi, b_lo))


def _silu(x):
    return x * (1.0 / (1.0 + jnp.exp(-x)))


def _layer_norm(x, g, b):
    mu = jnp.mean(x, axis=-1, keepdims=True)
    xc = x - mu
    var = jnp.mean(xc * xc, axis=-1, keepdims=True)
    return xc * lax.rsqrt(var + LN_EPS) * g + b


def _rms(x, g):
    return x * lax.rsqrt(jnp.mean(x * x, axis=-1, keepdims=True) + RMS_EPS) * g


_C_Q, _C_K, _C_V, _C_R, _C_CQ, _C_CKV, _C_KR, _C_KRS, _C_A, _C_END = (
    0, 256, 512, 1024, 1536, 1792, 1920, 2048, 2176, 2304)


def _inproj_kernel(x_ref, cos_ref, sin_ref, w_ref, wg2_ref, bg_ref, qn_ref, kvn_ref, wuq_ref, wabs_ref,
                   place_ref, q_ref, k_ref, v_ref, sr_ref, lg_ref, qall_ref, ckv_ref, kr_ref, kall_ref):
    h = _dot(x_ref[...].astype(BF16), w_ref[...])
    q_ref[...] = (h[:, _C_Q:_C_K] * GLA_DK ** -0.5).astype(BF16)
    k_ref[...] = h[:, _C_K:_C_V].astype(BF16)
    v_ref[...] = h[:, _C_V:_C_R].astype(BF16)
    sr_ref[...] = _silu(h[:, _C_R:_C_CQ]).astype(BF16)
    z = _dot_f32(h[:, _C_A:_C_END], wg2_ref[...]) + bg_ref[...]
    lg_ref[...] = (jnp.minimum(z, 0.0) - jnp.log(1.0 + jnp.exp(-jnp.abs(z)))) * (1.0 / GLA_TAU)
    cqn = _rms(h[:, _C_CQ:_C_CKV], qn_ref[...]).astype(BF16)
    qh = _dot(cqn, wuq_ref[...])
    n_nope = MLA_HEADS * MLA_NOPE
    cos, sin = cos_ref[...], sin_ref[...]
    x1, x2 = qh[:, n_nope:n_nope + LANES], qh[:, n_nope + LANES:]
    qscale = MLA_SCALE * LOG2E
    rot = (jnp.concatenate([x1 * cos - x2 * sin, x1 * sin + x2 * cos], axis=1) * qscale).astype(BF16)
    qlat = (_dot(qh[:, :n_nope].astype(BF16), wabs_ref[...]) * qscale).astype(BF16)
    qrope = _dot(rot, place_ref[...]).astype(BF16)
    for hd in range(MLA_HEADS):
        qall_ref[:, MLA_QK * hd:MLA_QK * hd + LANES] = qlat[:, LANES * hd:LANES * (hd + 1)]
        qall_ref[:, MLA_QK * hd + LANES:MLA_QK * (hd + 1)] = qrope[:, LANES * hd:LANES * (hd + 1)]
    ckv = _rms(h[:, _C_CKV:_C_KR], kvn_ref[...])
    lane = lax.broadcasted_iota(jnp.int32, (1, LANES), 1)
    sgn = jnp.where(lane < MLA_ROPE // 2, -1.0, 1.0)
    kr = h[:, _C_KR:_C_KRS] * cos + h[:, _C_KRS:_C_A] * (sin * sgn)
    ckv_ref[...] = ckv
    kr_ref[...] = kr
    kall_ref[:, :LANES] = ckv.astype(BF16)
    kall_ref[:, LANES:] = kr.astype(BF16)


def _inproj(x_all, cos_t, sin_t, w):
    t = x_all.shape[0]
    d = x_all.shape[1]
    tm = TOK_TILE
    row = lambda n: pl.BlockSpec((tm, n), lambda i: (i, 0))
    consts = [w['w_in'], w['w_g2'], w['b_g'], w['q_norm'], w['kv_norm'], w['w_uq'], w['w_abs'], w['place']]
    outs = [(GLA_HEADS * GLA_DK, BF16), (GLA_HEADS * GLA_DK, BF16), (GLA_HEADS * GLA_DV, BF16),
            (GLA_HEADS * GLA_DV, BF16), (GLA_HEADS * GLA_DK, F32), (MLA_HEADS * MLA_QK, BF16),
            (LANES, F32), (LANES, F32), (MLA_QK, BF16)]
    return pl.pallas_call(
        _inproj_kernel,
        grid=(t // tm,),
        in_specs=[row(d), row(LANES), row(LANES)] + [_const_spec(c.shape) for c in consts],
        out_specs=[row(n) for n, _ in outs],
        out_shape=[jax.ShapeDtypeStruct((t, n), dt) for n, dt in outs],
        compiler_params=_params(("parallel",)),
        name="inproj",
    )(x_all, cos_t, sin_t, *consts)


def _gla_tables(c):
    levels = int(math.log2(c))
    assert 1 << levels == c
    t = np.arange(c)[:, None]
    u = np.arange(c)[None, :]
    mats, masks = [], []
    for l in range(levels):
        half = 1 << l
        base = (t // half) * half
        upper = ((t >> l) & 1) == 1
        a_q = (u >= base) & (u <= t)
        a_k = (u > t) & (u <= base + half - 1)
        mats.append(np.where(upper, a_q, a_k))
        same = (t >> (l + 1)) == (u >> (l + 1))
        masks.append(same & upper & (((u >> l) & 1) == 0))
    mats.append(u <= t)
    mats.append(u > t)
    masks.append(t == u)
    amat = np.concatenate(mats, axis=0).astype(np.float32)
    mask = np.stack([np.tile(m, (GLA_HEADS, 1)) for m in masks]).astype(np.float32)
    return jnp.asarray(amat, BF16), jnp.asarray(mask, F32), levels


def _gla_kernel(q_ref, k_ref, v_ref, lg_ref, sr_ref, s0_ref, amat_ref, mask_ref, g_ref,
                o_ref, sout_ref, state, *, levels):
    c = q_ref.shape[0]
    hk = GLA_HEADS * GLA_DK
    step = pl.program_id(1)

    @pl.when(step == 0)
    def _():
        state[...] = s0_ref[0]

    lg = lg_ref[...]
    parts = _split3(lg)
    e3 = _dot(amat_ref[...], jnp.concatenate(parts, axis=1))
    ex = jnp.exp(e3[:, :hk] + e3[:, hk:2 * hk] + e3[:, 2 * hk:])
    q = q_ref[...].astype(F32)
    k = k_ref[...].astype(F32)
    v = v_ref[...]
    head = lax.broadcasted_iota(jnp.int32, (1, hk), 1) // GLA_DK

    def per_head(xf):
        return jnp.concatenate([jnp.where(head == h, xf, 0.0) for h in range(GLA_HEADS)], axis=0).astype(BF16)

    att = jnp.zeros((GLA_HEADS * c, c), F32)
    for l in range(levels + 1):
        if l < levels:
            el = ex[l * c:(l + 1) * c]
            ql, kl = q * el, k * el
        else:
            ql, kl = q, k
        a = lax.dot_general(per_head(ql), kl.astype(BF16), (((1,), (1,)), ((), ())), preferred_element_type=F32)
        att = att + a * mask_ref[l]
    s_prev = state[...]
    o_inter = _dot(per_head(q * ex[levels * c:(levels + 1) * c]), s_prev.astype(BF16))
    att = att.astype(BF16)
    g = g_ref[...]
    for h in range(GLA_HEADS):
        o = o_inter[h * c:(h + 1) * c] + _dot(att[h * c:(h + 1) * c], v[:, GLA_DV * h:GLA_DV * (h + 1)])
        o = _rms(o, g)
        o_ref[:, GLA_DV * h:GLA_DV * (h + 1)] = (o * sr_ref[:, GLA_DV * h:GLA_DV * (h + 1)].astype(F32)).astype(BF16)
    kr = (k * ex[(levels + 1) * c:]).astype(BF16)
    upd = lax.dot_general(kr, v, (((0,), (0,)), ((), ())), preferred_element_type=F32)
    ones = jnp.ones((c, GLA_DV), BF16)
    b_last = sum(lax.dot_general(p, ones, (((0,), (0,)), ((), ())), preferred_element_type=F32) for p in parts)
    new = jnp.exp(b_last) * s_prev + jnp.concatenate(
        [upd[GLA_DK * h:GLA_DK * (h + 1), GLA_DV * h:GLA_DV * (h + 1)] for h in range(GLA_HEADS)], axis=0)
    state[...] = new

    @pl.when(step == pl.num_programs(1) - 1)
    def _():
        sout_ref[0] = new


def _gla(q, k, v, lg, sr, s0, g, tables, batch, blocks_per_seq):
    amat, mask, levels = tables
    c = GLA_BLOCK
    hk, hv = GLA_HEADS * GLA_DK, GLA_HEADS * GLA_DV
    row = lambda n: pl.BlockSpec((c, n), lambda b, s: (b * blocks_per_seq + s, 0))
    st = pl.BlockSpec((1, hk, GLA_DV), lambda b, s: (b, 0, 0))
    return pl.pallas_call(
        functools.partial(_gla_kernel, levels=levels),
        grid=(batch, blocks_per_seq),
        in_specs=[row(hk), row(hk), row(hv), row(hk), row(hv), st,
                  _const_spec(amat.shape), _const_spec(mask.shape), _const_spec(g.shape)],
        out_specs=[row(hv), st],
        out_shape=[jax.ShapeDtypeStruct((batch * blocks_per_seq * c, hv), BF16),
                   jax.ShapeDtypeStruct((batch, hk, GLA_DV), F32)],
        scratch_shapes=[pltpu.VMEM((hk, GLA_DV), F32)],
        compiler_params=_params(("parallel", "arbitrary")),
        name="gla",
    )(q, k, v, lg, sr, s0, amat, mask, g)


def _mla_kernel(q_ref, kt_ref, kv_ref, o_ref, s_sc, p_sc, linv_sc, acc_sc, *, tq, n_past, past_valid, own_valid,
                diag_chunks):
    kvb = MLA_KV_TILE
    qi = pl.program_id(1)
    qs = jnp.concatenate([q_ref[:, MLA_QK * h:MLA_QK * (h + 1)] for h in range(MLA_HEADS)], axis=0)
    m_rows = MLA_HEADS * tq
    n_vis = n_past + qi + 1

    def score(j, mask):
        s = _dot(qs, kt_ref[0, j])
        s_sc[j] = s if mask is None else jnp.where(mask, s, NEG_INF)

    def scores(lo, hi):
        def body(j, carry):
            score(j, None)
            return carry
        lax.fori_loop(lo, hi, body, 0)

    col = lax.broadcasted_iota(jnp.int32, (m_rows, kvb), 1)
    n_past_full = past_valid // kvb
    if n_past_full:
        scores(0, n_past_full)
    if n_past_full < n_past:
        score(n_past_full, col < past_valid - n_past_full * kvb)
    scores(n_past, n_past + qi)
    mask = col < own_valid - qi * kvb
    if diag_chunks:
        qrow = lax.broadcasted_iota(jnp.int32, (m_rows, kvb), 0) % tq
        mask = mask & (col // CHUNK <= qrow // CHUNK)
    score(n_past + qi, mask)

    rc = MLA_ROW_STRIP

    def strip(r, carry):
        rows = pl.ds(pl.multiple_of(r * rc, rc), rc)
        mx = lax.fori_loop(0, n_vis, lambda j, m: jnp.maximum(m, s_sc[j, rows, :]),
                           jnp.full((rc, kvb), -jnp.inf, F32))
        m = jnp.max(mx, axis=1, keepdims=True)

        def body(j, l):
            p = jnp.exp2(s_sc[j, rows, :] - m)
            p_sc[j, rows, :] = p.astype(BF16)
            return l + p
        l = lax.fori_loop(0, n_vis, body, jnp.zeros((rc, kvb), F32))
        linv_sc[rows, :] = 1.0 / jnp.sum(l, axis=1, keepdims=True)
        return carry
    lax.fori_loop(0, m_rows // rc, strip, 0)

    acc_sc[...] = jnp.zeros(acc_sc.shape, F32)

    def pv(j, carry):
        vblk = kv_ref[0, pl.ds(pl.multiple_of(j * kvb, kvb), kvb), :MLA_KV_LORA]
        acc_sc[...] += _dot(p_sc[j], vblk)
        return carry
    lax.fori_loop(0, n_vis, pv, 0)
    o = acc_sc[...] * linv_sc[...]
    for h in range(MLA_HEADS):
        o_ref[:, MLA_KV_LORA * h:MLA_KV_LORA * (h + 1)] = o[h * tq:(h + 1) * tq].astype(BF16)


def _mla(q_all, kt, kv, batch, nq, tq, q_block0, past_valid, own_valid):
    n_kv = kt.shape[1]
    n_past = -(-past_valid // MLA_KV_TILE)
    assert nq == 1 or tq == MLA_KV_TILE
    assert n_kv == n_past + nq
    return pl.pallas_call(
        functools.partial(_mla_kernel, tq=tq, n_past=n_past, past_valid=past_valid, own_valid=own_valid,
                          diag_chunks=nq > 1),
        grid=(batch, nq),
        in_specs=[pl.BlockSpec((tq, MLA_HEADS * MLA_QK), lambda b, i: (q_block0 + b * nq + i, 0)),
                  pl.BlockSpec((1,) + kt.shape[1:], lambda b, i: (b, 0, 0, 0)),
                  pl.BlockSpec((1,) + kv.shape[1:], lambda b, i: (b, 0, 0))],
        out_specs=pl.BlockSpec((tq, MLA_HEADS * MLA_KV_LORA), lambda b, i: (b * nq + i, 0)),
        out_shape=jax.ShapeDtypeStruct((batch * nq * tq, MLA_HEADS * MLA_KV_LORA), BF16),
        scratch_shapes=[pltpu.VMEM((n_kv, MLA_HEADS * tq, MLA_KV_TILE), F32),
                        pltpu.VMEM((n_kv, MLA_HEADS * tq, MLA_KV_TILE), BF16),
                        pltpu.VMEM((MLA_HEADS * tq, 1), F32),
                        pltpu.VMEM((MLA_HEADS * tq, MLA_KV_LORA), F32)],
        compiler_params=_params(("parallel", "arbitrary")),
        name="mla",
    )(q_all, kt, kv)


def _post_mixer(x, out, g, b, rw_ref, x1_ref, x1b_ref, lgt_ref):
    y = _layer_norm(ALPHA * x + out, g, b)
    x1_ref[...] = y
    x1b_ref[...] = y.astype(BF16)
    lgt_ref[0] = _dot_f32_nt(rw_ref[...], y)


def _outproj_kernel(og_ref, ol_ref, x_ref, wuv_ref, wo_ref, g_ref, b_ref, rw_ref, x1_ref, x1b_ref, lgt_ref):
    o_mla = _dot(ol_ref[...], wuv_ref[...]).astype(BF16)
    n_gla = og_ref.shape[1]
    out = _dot(og_ref[...], wo_ref[:n_gla, :]) + _dot(o_mla, wo_ref[n_gla:, :])
    _post_mixer(x_ref[...], out, g_ref[...], b_ref[...], rw_ref, x1_ref, x1b_ref, lgt_ref)


def _outproj(o_gla, o_lat, x_all, w_uvbd, w_out, g, b, rw):
    t, d = x_all.shape
    tm = TOK_TILE
    row = lambda n: pl.BlockSpec((tm, n), lambda i: (i, 0))
    consts = [w_uvbd, w_out, g, b, rw]
    return pl.pallas_call(
        _outproj_kernel,
        grid=(t // tm,),
        in_specs=[row(o_gla.shape[1]), row(o_lat.shape[1]), row(d)] + [_const_spec(c.shape) for c in consts],
        out_specs=[row(d), row(d), pl.BlockSpec((1, N_EXPERTS, tm), lambda i: (i, 0, 0))],
        out_shape=[jax.ShapeDtypeStruct((t, d), F32), jax.ShapeDtypeStruct((t, d), BF16),
                   jax.ShapeDtypeStruct((t // tm, N_EXPERTS, tm), F32)],
        compiler_params=_params(("parallel",)),
        name="outproj",
    )(o_gla, o_lat, x_all, *consts)


def _pool_kernel(x_ref, prev_ref, hist_ref, pw_ref, ps_ref, g_ref, b_ref, rw_ref, x1_ref, x1b_ref, lgt_ref):
    ts = x_ref.shape[0]
    pm = POOL_MAX
    x = x_ref[...]
    prev = jnp.where(pl.program_id(1) == 0, hist_ref[0], prev_ref[...])
    xc = jnp.concatenate([prev, x], axis=0)
    grp = x.shape[1] // len(POOL_WINDOWS)
    outs = []
    for gi, w in enumerate(POOL_WINDOWS):
        s = xc[:, gi * grp:(gi + 1) * grp]
        span = 1
        while span < w:
            s = s + pltpu.roll(s, span, 0)
            span *= 2
        win = s[pm:]
        mix = (win * (1.0 / w) - x[:, gi * grp:(gi + 1) * grp]).astype(BF16)
        outs.append(_dot(mix, pw_ref[gi]))
    out = jnp.concatenate(outs, axis=1) * ps_ref[...]
    _post_mixer(x, out, g_ref[...], b_ref[...], rw_ref, x1_ref, x1b_ref, lgt_ref)


def _pool(x_all, hist, pool_w, pool_scale, g, b, rw, batch, tiles_per_seq, ts, tile0):
    d = x_all.shape[1]
    pm = POOL_MAX
    per = ts // pm
    consts = [pool_w, pool_scale, g, b, rw]
    row = lambda n: pl.BlockSpec((ts, n), lambda bb, i: (bb * tiles_per_seq + i, 0))
    return pl.pallas_call(
        _pool_kernel,
        grid=(batch, tiles_per_seq),
        in_specs=[pl.BlockSpec((ts, d), lambda bb, i: (tile0 + bb * tiles_per_seq + i, 0)),
                  pl.BlockSpec((pm, d), lambda bb, i: (jnp.maximum((tile0 + bb * tiles_per_seq + i) * per - 1, 0), 0)),
                  pl.BlockSpec((1, pm, d), lambda bb, i: (bb, 0, 0))] + [_const_spec(c.shape) for c in consts],
        out_specs=[row(d), row(d), pl.BlockSpec((1, N_EXPERTS, ts), lambda bb, i: (bb * tiles_per_seq + i, 0, 0))],
        out_shape=[jax.ShapeDtypeStruct((batch * tiles_per_seq * ts, d), F32),
                   jax.ShapeDtypeStruct((batch * tiles_per_seq * ts, d), BF16),
                   jax.ShapeDtypeStruct((batch * tiles_per_seq, N_EXPERTS, ts), F32)],
        compiler_params=_params(("parallel", "arbitrary")),
        name="pool",
    )(x_all, x_all, hist, *consts)


def _experts_kernel(be_ref, nb_ref, x_ref, wg_ref, wu_ref, wd_ref, y_ref):
    @pl.when(pl.program_id(0) < nb_ref[0])
    def _():
        wg, wu, wd = wg_ref[0, 0].astype(BF16), wu_ref[0, 0].astype(BF16), wd_ref[0, 0].astype(BF16)
        sub = EXPERT_SUBTILE
        for r in range(x_ref.shape[0] // sub):
            x = x_ref[r * sub:(r + 1) * sub, :]
            hmid = (_silu(_dot(x, wg)) * _dot(x, wu)).astype(BF16)
            y_ref[r * sub:(r + 1) * sub, :] = _dot(hmid, wd).astype(BF16)

    @pl.when(pl.program_id(0) >= nb_ref[0])
    def _():
        y_ref[...] = jnp.zeros_like(y_ref)


def _experts(blk_e, n_used, x_sorted, wg, wu, wd, layer):
    p, d = x_sorted.shape
    f = wg.shape[3]
    bm = EXPERT_TILE
    blk = lambda i, be, nb: (i, 0)
    grid_spec = pltpu.PrefetchScalarGridSpec(
        num_scalar_prefetch=2,
        grid=(p // bm,),
        in_specs=[pl.BlockSpec((bm, d), lambda i, be, nb: (jnp.minimum(i, nb[0] - 1), 0)),
                  pl.BlockSpec((1, 1, d, f), lambda i, be, nb: (layer, be[i], 0, 0)),
                  pl.BlockSpec((1, 1, d, f), lambda i, be, nb: (layer, be[i], 0, 0)),
                  pl.BlockSpec((1, 1, f, d), lambda i, be, nb: (layer, be[i], 0, 0))],
        out_specs=pl.BlockSpec((bm, d), blk),
    )
    return pl.pallas_call(
        _experts_kernel,
        grid_spec=grid_spec,
        out_shape=jax.ShapeDtypeStruct((p, d), BF16),
        compiler_params=_params(("arbitrary",)),
        name="experts",
    )(blk_e, n_used, x_sorted, wg, wu, wd)


def _segment_copies(tile, slot, dst_ref, n16_ref, seg_ref, make, entries=N_EXPERTS):
    big = SEG_ROWS * BIG_PIECE

    def per_expert(e, carry):
        j = tile * N_EXPERTS + e
        d0, s0, n = dst_ref[j], seg_ref[j], n16_ref[j]
        n_big = n // BIG_PIECE

        def big_piece(c, carry2):
            make(pl.multiple_of(d0 + c * big, SEG_ROWS), pl.multiple_of(s0 + c * big, SEG_ROWS), big, slot).start()
            return carry2
        lax.fori_loop(0, n_big, big_piece, 0)
        d1, s1 = d0 + n_big * big, s0 + n_big * big

        def small_piece(c, carry2):
            make(pl.multiple_of(d1 + c * SEG_ROWS, SEG_ROWS), pl.multiple_of(s1 + c * SEG_ROWS, SEG_ROWS),
                 SEG_ROWS, slot).start()
            return carry2
        lax.fori_loop(0, n - n_big * BIG_PIECE, small_piece, 0)
        return carry
    lax.fori_loop(0, entries, per_expert, 0)


def _tile_rows(tile, n16_ref, seg_ref):
    last = tile * N_EXPERTS + N_EXPERTS - 1
    return seg_ref[last] + n16_ref[last] * SEG_ROWS


def _drain(rows, slot, make):
    left = rows // SEG_ROWS
    for unit in DRAIN_UNITS:
        n = left // unit

        def body(c, carry, unit=unit):
            make(0, 0, unit * SEG_ROWS, slot).wait()
            return carry
        lax.fori_loop(0, n, body, 0)
        left = left - n * unit


def _ffn_out_kernel(dst_ref, n16_ref, seg_ref, x_ref, xb_ref, pw_ref, ys_ref,
                    sg_ref, su_ref, sd_ref, g_ref, b_ref, y_ref, yb_ref, ybuf, racc, sem):
    i = pl.program_id(0)
    nt = pl.num_programs(0)
    slot = i % 2

    def make(d0, s0, rows, sl):
        return pltpu.make_async_copy(ys_ref.at[pl.ds(d0, rows)], ybuf.at[sl, pl.ds(s0, rows)], sem.at[sl])

    @pl.when(i == 0)
    def _():
        ybuf[...] = jnp.zeros_like(ybuf)
        _segment_copies(0, 0, dst_ref, n16_ref, seg_ref, make)

    @pl.when(i + 1 < nt)
    def _():
        _segment_copies(i + 1, 1 - slot, dst_ref, n16_ref, seg_ref, make)

    xb = xb_ref[...]
    hmid = (_silu(_dot(xb, sg_ref[...])) * _dot(xb, su_ref[...])).astype(BF16)
    shared = _dot(hmid, sd_ref[...])
    used = _tile_rows(i, n16_ref, seg_ref)
    _drain(used, slot, make)

    racc[...] = jnp.zeros_like(racc)
    rs = COMBINE_STRIP
    for s in range(ybuf.shape[1] // rs):
        @pl.when(s * rs < used)
        def _():
            racc[...] += lax.dot_general(pw_ref[0, s * rs:(s + 1) * rs, :], ybuf[slot, s * rs:(s + 1) * rs, :],
                                         (((0,), (0,)), ((), ())), preferred_element_type=F32)
    ffn = racc[...] + shared
    y = _layer_norm(ALPHA * x_ref[...] + ffn, g_ref[...], b_ref[...])
    y_ref[...] = y
    yb_ref[...] = y.astype(BF16)


def _ffn_out(dst, n16, seg, x1, x1b, pw, y_sorted, sg, su, sd, g, b):
    t, d = x1.shape
    nt, _, tm = pw.shape
    row = lambda n: pl.BlockSpec((tm, n), lambda i, *_: (i, 0))
    per_tile = lambda a: pl.BlockSpec((1,) + a.shape[1:], lambda i, *_: (i, 0, 0))
    consts = [sg, su, sd, g, b]
    grid_spec = pltpu.PrefetchScalarGridSpec(
        num_scalar_prefetch=3,
        grid=(nt,),
        in_specs=[row(d), row(d), per_tile(pw), pl.BlockSpec(memory_space=pl.ANY)]
        + [_const_spec(c.shape) for c in consts],
        out_specs=[row(d), row(d)],
        scratch_shapes=[pltpu.VMEM((2, SORT_ROWS, d), BF16), pltpu.VMEM((tm, d), F32), pltpu.SemaphoreType.DMA((2,))],
    )
    return pl.pallas_call(
        _ffn_out_kernel,
        grid_spec=grid_spec,
        out_shape=[jax.ShapeDtypeStruct((t, d), F32), jax.ShapeDtypeStruct((t, d), BF16)],
        compiler_params=_params(("arbitrary",)),
        name="ffn_out",
    )(dst, n16, seg, x1, x1b, pw, y_sorted, *consts)


def _dispatch_kernel(dst_ref, n16_ref, seg_ref, tdst_ref, tn16_ref, x_ref, rid_ref, w_ref, segr_ref, c16r_ref,
                     xs_ref, pw_ref, buf, zbuf, sem):
    i = pl.program_id(0)
    nt = pl.num_programs(0)
    slot = i % 2
    tm = x_ref.shape[0]

    def make(d0, s0, rows, sl):
        return pltpu.make_async_copy(buf.at[sl, pl.ds(s0, rows)], xs_ref.at[pl.ds(d0, rows)], sem.at[sl])

    @pl.when(i >= 2)
    def _():
        _drain(_tile_rows(i - 2, n16_ref, seg_ref), slot, make)

    x = x_ref[...]
    rid = rid_ref[0]
    wb = w_ref[0].astype(BF16)
    used = _tile_rows(i, n16_ref, seg_ref)
    rs = DISPATCH_STRIP
    hs = MXU_DIM
    for s in range(buf.shape[1] // rs):
        @pl.when(s * rs < used)
        def _():
            hits = []
            for h0 in range(s * rs, (s + 1) * rs, hs):
                r = lax.broadcasted_iota(jnp.int32, (hs, N_EXPERTS), 0) + h0
                owner = jnp.where((r >= segr_ref[0]) & (r < segr_ref[0] + c16r_ref[0]), 1.0, 0.0).astype(BF16)
                local = jnp.where((rid > h0) & (rid <= h0 + hs), rid - h0, 0.0).astype(BF16)
                at_row = _dot(owner, jnp.concatenate([local, wb], axis=1))
                hit = at_row[:, :tm] == (lax.broadcasted_iota(jnp.int32, (hs, tm), 0) + 1).astype(F32)
                hits.append(jnp.where(hit, 1.0, 0.0).astype(BF16))
                pw_ref[0, h0:h0 + hs, :] = jnp.where(hit, at_row[:, tm:], 0.0).astype(BF16)
            buf[slot, s * rs:(s + 1) * rs, :] = _dot(jnp.concatenate(hits, axis=0), x).astype(BF16)

        @pl.when(s * rs >= used)
        def _():
            pw_ref[0, s * rs:(s + 1) * rs, :] = jnp.zeros((rs, tm), BF16)
    _segment_copies(i, slot, dst_ref, n16_ref, seg_ref, make)

    @pl.when(i == nt - 1)
    def _():
        zbuf[...] = jnp.zeros_like(zbuf)

        def zmake(d0, s0, rows, sl):
            return pltpu.make_async_copy(zbuf.at[pl.ds(0, rows)], xs_ref.at[pl.ds(d0, rows)], sem.at[sl])
        _segment_copies(0, 2, tdst_ref, tn16_ref, tdst_ref, zmake, entries=N_EXPERTS + 1)
        tail = lax.fori_loop(0, N_EXPERTS + 1, lambda e, acc: acc + tn16_ref[e], 0) * SEG_ROWS
        _drain(tail, 2, make)

        @pl.when(i >= 1)
        def _():
            _drain(_tile_rows(i - 1, n16_ref, seg_ref), 1 - slot, make)
        _drain(used, slot, make)


def _dispatch(dst, n16, seg, tail_dst, tail_n16, x1b, rid, wdense, segr, c16r, n_rows):
    t, d = x1b.shape
    nt, ne, tm = rid.shape
    per_tile = lambda a: pl.BlockSpec((1,) + a.shape[1:], lambda i, *_: (i, 0, 0))
    grid_spec = pltpu.PrefetchScalarGridSpec(
        num_scalar_prefetch=5,
        grid=(nt,),
        in_specs=[pl.BlockSpec((tm, d), lambda i, *_: (i, 0)), per_tile(rid), per_tile(wdense), per_tile(segr),
                  per_tile(c16r)],
        out_specs=[pl.BlockSpec(memory_space=pl.ANY), pl.BlockSpec((1, SORT_ROWS, tm), lambda i, *_: (i, 0, 0))],
        scratch_shapes=[pltpu.VMEM((2, SORT_ROWS, d), BF16), pltpu.VMEM((SEG_ROWS * BIG_PIECE, d), BF16),
                        pltpu.SemaphoreType.DMA((3,))],
    )
    return pl.pallas_call(
        _dispatch_kernel,
        grid_spec=grid_spec,
        out_shape=[jax.ShapeDtypeStruct((n_rows, d), BF16), jax.ShapeDtypeStruct((nt, SORT_ROWS, tm), BF16)],
        compiler_params=_params(("arbitrary",)),
        name="dispatch",
    )(dst, n16, seg, tail_dst, tail_n16, x1b, rid, wdense, segr, c16r)


def _outranked(vals, n, ids):
    rank = jnp.zeros(vals.shape, jnp.int32)
    for j in range(n):
        row = vals[j:j + 1]
        rank = rank + ((row > vals) | ((row == vals) & (j < ids))).astype(jnp.int32)
    return rank


def _router_kernel(lg_ref, b_ref, tri_ref, low_ref, rid_ref, w_ref, tab_ref, carry):
    @pl.when(pl.program_id(0) == 0)
    def _():
        carry[...] = jnp.zeros_like(carry)

    tm = lg_ref.shape[2]
    gs = N_EXPERTS // N_GROUPS
    scores = 1.0 / (1.0 + jnp.exp(-lg_ref[0]))
    biased = scores + b_ref[...]
    member = lax.broadcasted_iota(jnp.int32, (gs, tm), 0)
    group_score = []
    for g in range(N_GROUPS):
        blk = biased[g * gs:(g + 1) * gs]
        m1 = jnp.max(blk, axis=0, keepdims=True)
        first = jnp.min(jnp.where(blk == m1, member, gs), axis=0, keepdims=True)
        m2 = jnp.max(jnp.where(member == first, -jnp.inf, blk), axis=0, keepdims=True)
        group_score.append(m1 + m2)
    group_score = jnp.concatenate(group_score, axis=0)
    gid = lax.broadcasted_iota(jnp.int32, (N_GROUPS, tm), 0)
    group_ok = _outranked(group_score, N_GROUPS, gid) < TOPK_GROUPS
    masked = jnp.concatenate([jnp.where(group_ok[g:g + 1], biased[g * gs:(g + 1) * gs], -jnp.inf)
                              for g in range(N_GROUPS)], axis=0)
    eid = lax.broadcasted_iota(jnp.int32, (N_EXPERTS, tm), 0)
    sel = _outranked(masked, N_EXPERTS, eid) < TOP_K
    self = jnp.where(sel, 1.0, 0.0)
    wsel = self * scores
    wts = wsel / jnp.sum(wsel, axis=0, keepdims=True) * ROUTED_SCALE
    selb = self.astype(BF16)
    before = _dot(selb, tri_ref[...])
    pieces = jnp.floor((jnp.sum(self, axis=1, keepdims=True) + (SEG_ROWS - 1.0)) * (1.0 / SEG_ROWS))
    seg = SEG_ROWS * _dot(low_ref[...], jnp.broadcast_to(pieces, (N_EXPERTS, LANES)).astype(BF16))
    rid_ref[0] = jnp.where(sel, seg[:, :1] + before + 1.0, NO_ROW)
    w_ref[0] = wts
    lane = lax.broadcasted_iota(jnp.int32, (N_EXPERTS, LANES), 1)
    tab_ref[0] = jnp.where(lane == 0, seg, jnp.where(lane == 1, SEG_ROWS * pieces, jnp.where(lane == 2, carry[...], 0.0)))
    carry[...] = carry[...] + SEG_ROWS * pieces


def _router(logits, router_b):
    nt, ne, tm = logits.shape
    tri = jnp.asarray(np.triu(np.ones((tm, tm), np.float32), 1), BF16)
    low = jnp.asarray(np.tril(np.ones((ne, ne), np.float32), -1), BF16)
    per_tile = lambda n: pl.BlockSpec((1, ne, n), lambda i: (i, 0, 0))
    return pl.pallas_call(
        _router_kernel,
        grid=(nt,),
        in_specs=[per_tile(tm), _const_spec((ne, 1)), _const_spec(tri.shape), _const_spec(low.shape)],
        out_specs=[per_tile(tm), per_tile(tm), per_tile(LANES)],
        out_shape=[jax.ShapeDtypeStruct((nt, ne, tm), F32), jax.ShapeDtypeStruct((nt, ne, tm), F32),
                   jax.ShapeDtypeStruct((nt, ne, LANES), F32)],
        scratch_shapes=[pltpu.VMEM((ne, 1), F32)],
        compiler_params=_params(("arbitrary",)),
        name="router",
    )(logits, router_b.reshape(ne, 1).astype(F32), tri, low)


def _moe(x1, x1b, logits, router_b, layer, wg, wu, wd, sg, su, sd, g, b):
    t, d = x1.shape
    nt, ne, tm = logits.shape
    assert SORT_ROWS >= TOP_K * tm + ne * (SEG_ROWS - 1) and SORT_ROWS < NO_ROW
    rid, wdense, tab = _router(logits, router_b)
    tab = tab[:, :, :3].astype(jnp.int32)
    seg, rows, base = tab[:, :, 0], tab[:, :, 1], tab[:, :, 2]
    bm = EXPERT_TILE
    region = base[-1] + rows[-1]
    padded = (region + bm - 1) // bm * bm
    pad_end = jnp.cumsum(padded)
    dst = (pad_end - padded)[None, :] + base
    n_blocks = -(-(t * TOP_K + nt * ne * (SEG_ROWS - 1)) // bm) + ne
    blk_start = jnp.arange(n_blocks, dtype=jnp.int32) * bm
    blk_e = jnp.minimum(jnp.sum((pad_end[None, :] <= blk_start[:, None]).astype(jnp.int32), axis=1), ne - 1)
    n_used = (pad_end[-1] // bm).astype(jnp.int32).reshape(1)
    tail_dst = jnp.concatenate([pad_end - padded + region, pad_end[-1:]])
    tail_n16 = jnp.concatenate([padded - region, n_blocks * bm - pad_end[-1:]]) // SEG_ROWS
    dst, n16, segf = dst.reshape(-1), (rows // SEG_ROWS).reshape(-1), seg.reshape(-1)
    x_sorted, pw = _dispatch(dst, n16, segf, tail_dst, tail_n16, x1b, rid, wdense, seg[:, None, :], rows[:, None, :],
                             n_blocks * bm)
    y_sorted = _experts(blk_e, n_used, x_sorted, wg, wu, wd, layer)
    return _ffn_out(dst, n16, segf, x1, x1b, pw, y_sorted, sg, su, sd, g, b)


def _rope_tables(pos):
    half = MLA_ROPE // 2
    inv = ROPE_THETA ** (-jnp.arange(half, dtype=F32) / half)
    ang = pos.astype(F32)[:, None] * inv
    reps = LANES // half
    return jnp.tile(jnp.cos(ang), (1, reps)), jnp.tile(jnp.sin(ang), (1, reps))


def _pack_layer0_weights(w_in0, gla_w_g2, gla_b_g, mla_q_norm_g, mla_kv_norm_g, mla_w_uq, mla_w_uk, mla_w_uv):
    d = w_in0.shape[0]
    hk, hv = GLA_HEADS * GLA_DK, GLA_HEADS * GLA_DV
    o_q, o_k, o_v, o_r = 0, hk, 2 * hk, 2 * hk + hv
    o_a = o_r + hv
    o_cq = o_a + GLA_RANK
    o_ckv = o_cq + MLA_Q_LORA
    o_kr = o_ckv + MLA_KV_LORA
    half = MLA_ROPE // 2
    zeros = lambda n: jnp.zeros((d, n), w_in0.dtype)
    kr1, kr2 = w_in0[:, o_kr:o_kr + half], w_in0[:, o_kr + half:o_kr + MLA_ROPE]
    w_in = jnp.concatenate([
        w_in0[:, o_q:o_a], w_in0[:, o_cq:o_kr],
        kr1, kr2, zeros(LANES - MLA_ROPE),
        kr2, kr1, zeros(LANES - MLA_ROPE),
        w_in0[:, o_a:o_cq], zeros(LANES - GLA_RANK)], axis=1).astype(BF16)
    assert w_in.shape[1] == _C_END
    w_g2 = jnp.concatenate([gla_w_g2, jnp.zeros((LANES - GLA_RANK, hk), gla_w_g2.dtype)], axis=0)
    uq = mla_w_uq.reshape(MLA_Q_LORA, MLA_HEADS, MLA_NOPE + MLA_ROPE)
    w_uq = jnp.concatenate([uq[:, :, :MLA_NOPE].reshape(MLA_Q_LORA, -1),
                            uq[:, :, MLA_NOPE:MLA_NOPE + half].reshape(MLA_Q_LORA, -1),
                            uq[:, :, MLA_NOPE + half:].reshape(MLA_Q_LORA, -1)], axis=1).astype(BF16)
    eye = jnp.eye(MLA_HEADS, dtype=mla_w_uk.dtype)
    w_abs = jnp.einsum('chn,hg->hngc', mla_w_uk, eye).reshape(MLA_HEADS * MLA_NOPE, MLA_HEADS * MLA_KV_LORA).astype(BF16)
    w_uvbd = jnp.einsum('chv,hg->hcgv', mla_w_uv, eye).reshape(MLA_HEADS * MLA_KV_LORA, MLA_HEADS * MLA_V).astype(BF16)
    place = np.zeros((2 * LANES, MLA_HEADS * LANES), np.float32)
    for h in range(MLA_HEADS):
        for j in range(half):
            place[h * half + j, h * LANES + j] = 1.0
            place[LANES + h * half + j, h * LANES + half + j] = 1.0
    return dict(w_in=w_in, w_g2=w_g2, b_g=gla_b_g.reshape(1, hk), q_norm=mla_q_norm_g.reshape(1, -1),
                kv_norm=mla_kv_norm_g.reshape(1, -1), w_uq=w_uq, w_abs=w_abs,
                place=jnp.asarray(place, BF16)), w_uvbd


def _pad_rows(x, n):
    return jnp.pad(x, ((0, n - x.shape[0]),) + ((0, 0),) * (x.ndim - 1))


def _kv_tiles(past, own):
    kvb = MLA_KV_TILE
    pad = lambda a: jnp.pad(a, ((0, 0), (0, -a.shape[1] % kvb), (0, 0)))
    kv = jnp.concatenate([pad(past), pad(own)], axis=1) if past.shape[1] else pad(own)
    b, n, w = kv.shape
    kt = kv.reshape(b, n // kvb, kvb, w).transpose(0, 1, 3, 2)
    return kt, kv


def kernel(x_prompt, x_sample, cache_mla_ckv, cache_mla_krope, state_gla, cache_pool, meta_tokens, w_in0, gla_w_g2, gla_b_g, gla_norm_g, mla_q_norm_g, mla_kv_norm_g, mla_w_uq, mla_w_uk, mla_w_uv, w_out0, pool_w, pool_scale, ln_g, ln_b, moe_router_w, moe_router_b, moe_w_gate, moe_w_up, moe_w_down, moe_sh_gate, moe_sh_up, moe_sh_down):
    bp, sp, d = x_prompt.shape
    bs, ss, _ = x_sample.shape
    n_meta = meta_tokens.shape[0]
    past_len = cache_mla_ckv.shape[1] - n_meta
    tp, tsm = bp * sp, bs * ss
    assert sp % MLA_Q_TILE == 0 and sp % POOL_TILE == 0 and sp % GLA_BLOCK == 0 and sp % CHUNK == 0
    assert ss <= CHUNK and past_len % CHUNK == 0 and n_meta <= CHUNK and ss <= GLA_BLOCK and n_meta <= GLA_BLOCK
    assert n_meta == POOL_MAX and ss >= POOL_MAX and tp % TOK_TILE == 0
    t_all = -(-(tp + tsm + n_meta) // TOK_TILE) * TOK_TILE
    o_s, o_m = tp, tp + tsm

    x_all = _pad_rows(jnp.concatenate([x_prompt.reshape(tp, d), x_sample.reshape(tsm, d), meta_tokens], axis=0), t_all)
    pos = jnp.concatenate([jnp.tile(n_meta + jnp.arange(sp), bp), jnp.tile(n_meta + past_len + jnp.arange(ss), bs),
                           jnp.arange(n_meta), jnp.zeros((t_all - o_m - n_meta,), jnp.int32)])
    cos_t, sin_t = _rope_tables(pos)
    w0, w_uvbd = _pack_layer0_weights(w_in0, gla_w_g2, gla_b_g, mla_q_norm_g, mla_kv_norm_g, mla_w_uq, mla_w_uk, mla_w_uv)

    q, k, v, sr, lg, q_all, ckv, kr, k_all = _inproj(x_all, cos_t, sin_t, w0)

    tables = _gla_tables(GLA_BLOCK)
    g_gla = gla_norm_g.reshape(1, GLA_DV)
    c = GLA_BLOCK
    hk = GLA_HEADS * GLA_DK

    def small(a):
        sm = jnp.pad(a[o_s:o_m].reshape(bs, ss, -1), ((0, 0), (0, c - ss), (0, 0)))
        me = jnp.pad(a[o_m:o_m + n_meta], ((0, c - n_meta), (0, 0)))[None]
        return jnp.concatenate([sm, me], axis=0).reshape((bs + 1) * c, -1)

    s0_small = jnp.concatenate([state_gla.reshape(bs, hk, GLA_DV), jnp.zeros((1, hk, GLA_DV), F32)], axis=0)
    og_small, st_small = _gla(small(q), small(k), small(v), small(lg), small(sr), s0_small, g_gla, tables, bs + 1, 1)
    s0_prompt = jnp.broadcast_to(st_small[bs:], (bp, hk, GLA_DV))
    og_prompt, st_prompt = _gla(q, k, v, lg, sr, s0_prompt, g_gla, tables, bp, sp // c)
    og_small = og_small.reshape(bs + 1, c, -1)
    o_gla = _pad_rows(jnp.concatenate([og_prompt, og_small[:bs, :ss].reshape(tsm, -1), og_small[bs, :n_meta]], axis=0), t_all)

    k_meta = k_all[o_m:o_m + n_meta]
    kt_m, kv_m = _kv_tiles(jnp.zeros((1, 0, MLA_QK), BF16), k_meta[None])
    ol_meta = _mla(q_all[o_m:o_m + n_meta], kt_m, kv_m, 1, 1, n_meta, 0, 0, n_meta)
    cache = jnp.concatenate([cache_mla_ckv, cache_mla_krope,
                             jnp.zeros(cache_mla_ckv.shape[:2] + (MLA_QK - MLA_KV_LORA - MLA_ROPE,), F32)], axis=-1).astype(BF16)
    kt_s, kv_s = _kv_tiles(cache, k_all[o_s:o_m].reshape(bs, ss, MLA_QK))
    ol_sample = _mla(q_all[o_s:o_m], kt_s, kv_s, bs, 1, ss, 0, n_meta + past_len, ss)
    kt_p, kv_p = _kv_tiles(jnp.broadcast_to(k_meta[None], (bp, n_meta, MLA_QK)), k_all[:tp].reshape(bp, sp, MLA_QK))
    ol_prompt = _mla(q_all, kt_p, kv_p, bp, sp // MLA_Q_TILE, MLA_Q_TILE, 0, n_meta, sp)
    o_lat = _pad_rows(jnp.concatenate([ol_prompt, ol_sample, ol_meta], axis=0), t_all)

    ln = lambda l, j: (ln_g[l, j].reshape(1, d), ln_b[l, j].reshape(1, d))
    rw = lambda l: moe_router_w[l].T
    moe_w = lambda l: (l, moe_w_gate, moe_w_up, moe_w_down,
                       moe_sh_gate[l].astype(BF16), moe_sh_up[l].astype(BF16), moe_sh_down[l].astype(BF16))
    x1, x1b, logits = _outproj(o_gla, o_lat, x_all, w_uvbd, w_out0.astype(BF16), *ln(0, 0), rw(0))
    x2, _ = _moe(x1, x1b, logits, moe_router_b[0], *moe_w(0), *ln(0, 1))

    pm = POOL_MAX
    zrow = jnp.zeros((1, d), F32)
    hist_p = jnp.broadcast_to(jnp.concatenate([zrow, x2[o_m + n_meta - (pm - 1):o_m + n_meta]], axis=0)[None], (bp, pm, d))
    hist_s = jnp.concatenate([jnp.zeros((bs, 1, d), F32), cache_pool], axis=1)
    pool_wb = pool_w.astype(BF16)
    ps = pool_scale.reshape(1, d)
    y1p = _pool(x2, hist_p, pool_wb, ps, *ln(1, 0), rw(1), bp, sp // POOL_TILE, POOL_TILE, 0)
    y1s = _pool(x2[o_s:o_m], hist_s, pool_wb, ps, *ln(1, 0), rw(1), bs, 1, ss, 0)
    t1 = -(-(tp + tsm) // TOK_TILE) * TOK_TILE
    x3, x3b = [_pad_rows(jnp.concatenate([a, b_], axis=0), t1) for a, b_ in zip(y1p[:2], y1s[:2])]
    by_token = lambda lt: lt.transpose(1, 0, 2).reshape(N_EXPERTS, -1)
    logits1 = jnp.pad(jnp.concatenate([by_token(y1p[2]), by_token(y1s[2])], axis=1), ((0, 0), (0, t1 - tp - tsm)))
    logits1 = logits1.reshape(N_EXPERTS, t1 // TOK_TILE, TOK_TILE).transpose(1, 0, 2)
    x4, _ = _moe(x3, x3b, logits1, moe_router_b[1], *moe_w(1), *ln(1, 1))

    y_prompt = x4[:tp].reshape(bp, sp, d)
    y_sample = x4[o_s:o_m].reshape(bs, ss, d)
    ckv_meta, kr_meta = ckv[o_m:o_m + n_meta], kr[o_m:o_m + n_meta, :MLA_ROPE]
    p_ckv = jnp.concatenate([jnp.broadcast_to(ckv_meta[None], (bp, n_meta, MLA_KV_LORA)),
                             ckv[:tp].reshape(bp, sp, MLA_KV_LORA)], axis=1)
    p_kr = jnp.concatenate([jnp.broadcast_to(kr_meta[None], (bp, n_meta, MLA_ROPE)),
                            kr[:tp, :MLA_ROPE].reshape(bp, sp, MLA_ROPE)], axis=1)
    p_gla = st_prompt.reshape(bp, GLA_HEADS, GLA_DK, GLA_DV)
    x2p = x2[:tp].reshape(bp, sp, d)
    p_pool = x2p[:, sp - (pm - 1):]
    s_ckv = ckv[o_s:o_m].reshape(bs, ss, MLA_KV_LORA)
    s_kr = kr[o_s:o_m, :MLA_ROPE].reshape(bs, ss, MLA_ROPE)
    s_gla = st_small[:bs].reshape(bs, GLA_HEADS, GLA_DK, GLA_DV)
    s_pool = x2[o_s:o_m].reshape(bs, ss, d)[:, ss - (pm - 1):]
    return (y_prompt, y_sample, p_ckv, p_kr, p_gla, p_pool, s_ckv, s_kr, s_gla, s_pool)
```

```python
import functools
import math

import jax
import jax.numpy as jnp
import numpy as np
from jax import lax
from jax.experimental import pallas as pl
from jax.experimental.pallas import tpu as pltpu

F32 = jnp.float32
BF16 = jnp.bfloat16

CHUNK = 64
DEPTH = 2
ALPHA = (2 * DEPTH) ** 0.25
LN_EPS = 1e-5
RMS_EPS = 1e-6
NEG_INF = -1e30
GLA_HEADS = 4
GLA_DK = 64
GLA_DV = 128
GLA_RANK = 16
GLA_TAU = 16.0
MLA_HEADS = 8
MLA_Q_LORA = 256
MLA_KV_LORA = 128
MLA_NOPE = 64
MLA_ROPE = 32
MLA_V = 64
MLA_SCALE = (MLA_NOPE + MLA_ROPE) ** -0.5
ROPE_THETA = 10000.0
POOL_WINDOWS = (2, 4, 8, 16)
POOL_MAX = 16
N_EXPERTS = 64
TOP_K = 8
N_GROUPS = 8
TOPK_GROUPS = 4
ROUTED_SCALE = 2.5

LANES = 128
MXU_DIM = 256

TOK_TILE = 512
GLA_BLOCK = 128
MLA_Q_TILE = 256
MLA_KV_TILE = 256
MLA_QK = 256
MLA_ROW_STRIP = 128
LOG2E = math.log2(math.e)
EXPERT_TILE = 512
EXPERT_SUBTILE = 256
POOL_TILE = 512
SEG_ROWS = 16
BIG_PIECE = 4
DRAIN_UNITS = (64, 16, 4, 1)
ROUTE_TILE = 256
SORT_ROWS = 3072
NO_ROW = 8191.0
DISPATCH_STRIP = 512
COMBINE_STRIP = 1024
VMEM_LIMIT = 56 * 1024 * 1024


def _params(sem, vmem=VMEM_LIMIT):
    return pltpu.CompilerParams(dimension_semantics=sem, vmem_limit_bytes=vmem)


def _const_spec(shape):
    nd = len(shape)
    return pl.BlockSpec(shape, lambda *_: (0,) * nd)


def _split3(x):
    hi = x.astype(BF16)
    r = x - hi.astype(F32)
    mid = r.astype(BF16)
    lo = (r - mid.astype(F32)).astype(BF16)
    return hi, mid, lo


def _dot(a, b):
    return jnp.dot(a, b, preferred_element_type=F32)


def _dot_f32(a, b):
    a_hi = a.astype(BF16)
    a_lo = (a - a_hi.astype(F32)).astype(BF16)
    b_hi = b.astype(BF16)
    b_lo = (b - b_hi.astype(F32)).astype(BF16)
    return _dot(a_hi, b_hi) + (_dot(a_lo, b_hi) + _dot(a_hi, b_lo))


def _dot_f32_nt(a, b):
    nt = lambda u, v: lax.dot_general(u, v, (((1,), (1,)), ((), ())), preferred_element_type=F32)
    a_hi = a.astype(BF16)
    a_lo = (a - a_hi.astype(F32)).astype(BF16)
    b_hi = b.astype(BF16)
    b_lo = (b - b_hi.astype(F32)).astype(BF16)
    return nt(a_hi, b_hi) + (nt(a_lo, b_hi) + nt(a_hi, b_lo))


def _silu(x):
    return x * (1.0 / (1.0 + jnp.exp(-x)))


def _layer_norm(x, g, b):
    mu = jnp.mean(x, axis=-1, keepdims=True)
    xc = x - mu
    var = jnp.mean(xc * xc, axis=-1, keepdims=True)
    return xc * lax.rsqrt(var + LN_EPS) * g + b


def _rms(x, g):
    return x * lax.rsqrt(jnp.mean(x * x, axis=-1, keepdims=True) + RMS_EPS) * g


_C_Q, _C_K, _C_V, _C_R, _C_CQ, _C_CKV, _C_KR, _C_KRS, _C_A, _C_END = (
    0, 256, 512, 1024, 1536, 1792, 1920, 2048, 2176, 2304)


def _inproj_kernel(x_ref, cos_ref, sin_ref, w_ref, wg2_ref, bg_ref, qn_ref, kvn_ref, wuq_ref, wabs_ref,
                   place_ref, q_ref, k_ref, v_ref, sr_ref, lg_ref, qall_ref, ckv_ref, kr_ref, kall_ref):
    h = _dot(x_ref[...].astype(BF16), w_ref[...])
    q_ref[...] = (h[:, _C_Q:_C_K] * GLA_DK ** -0.5).astype(BF16)
    k_ref[...] = h[:, _C_K:_C_V].astype(BF16)
    v_ref[...] = h[:, _C_V:_C_R].astype(BF16)
    sr_ref[...] = _silu(h[:, _C_R:_C_CQ]).astype(BF16)
    z = _dot_f32(h[:, _C_A:_C_END], wg2_ref[...]) + bg_ref[...]
    lg_ref[...] = (jnp.minimum(z, 0.0) - jnp.log(1.0 + jnp.exp(-jnp.abs(z)))) * (1.0 / GLA_TAU)
    cqn = _rms(h[:, _C_CQ:_C_CKV], qn_ref[...]).astype(BF16)
    qh = _dot(cqn, wuq_ref[...])
    n_nope = MLA_HEADS * MLA_NOPE
    cos, sin = cos_ref[...], sin_ref[...]
    x1, x2 = qh[:, n_nope:n_nope + LANES], qh[:, n_nope + LANES:]
    qscale = MLA_SCALE * LOG2E
    rot = (jnp.concatenate([x1 * cos - x2 * sin, x1 * sin + x2 * cos], axis=1) * qscale).astype(BF16)
    qlat = (_dot(qh[:, :n_nope].astype(BF16), wabs_ref[...]) * qscale).astype(BF16)
    qrope = _dot(rot, place_ref[...]).astype(BF16)
    for hd in range(MLA_HEADS):
        qall_ref[:, MLA_QK * hd:MLA_QK * hd + LANES] = qlat[:, LANES * hd:LANES * (hd + 1)]
        qall_ref[:, MLA_QK * hd + LANES:MLA_QK * (hd + 1)] = qrope[:, LANES * hd:LANES * (hd + 1)]
    ckv = _rms(h[:, _C_CKV:_C_KR], kvn_ref[...])
    lane = lax.broadcasted_iota(jnp.int32, (1, LANES), 1)
    sgn = jnp.where(lane < MLA_ROPE // 2, -1.0, 1.0)
    kr = h[:, _C_KR:_C_KRS] * cos + h[:, _C_KRS:_C_A] * (sin * sgn)
    ckv_ref[...] = ckv
    kr_ref[...] = kr
    kall_ref[:, :LANES] = ckv.astype(BF16)
    kall_ref[:, LANES:] = kr.astype(BF16)


def _inproj(x_all, cos_t, sin_t, w):
    t = x_all.shape[0]
    d = x_all.shape[1]
    tm = TOK_TILE
    row = lambda n: pl.BlockSpec((tm, n), lambda i: (i, 0))
    consts = [w['w_in'], w['w_g2'], w['b_g'], w['q_norm'], w['kv_norm'], w['w_uq'], w['w_abs'], w['place']]
    outs = [(GLA_HEADS * GLA_DK, BF16), (GLA_HEADS * GLA_DK, BF16), (GLA_HEADS * GLA_DV, BF16),
            (GLA_HEADS * GLA_DV, BF16), (GLA_HEADS * GLA_DK, F32), (MLA_HEADS * MLA_QK, BF16),
            (LANES, F32), (LANES, F32), (MLA_QK, BF16)]
    return pl.pallas_call(
        _inproj_kernel,
        grid=(t // tm,),
        in_specs=[row(d), row(LANES), row(LANES)] + [_const_spec(c.shape) for c in consts],
        out_specs=[row(n) for n, _ in outs],
        out_shape=[jax.ShapeDtypeStruct((t, n), dt) for n, dt in outs],
        compiler_params=_params(("parallel",)),
        name="inproj",
    )(x_all, cos_t, sin_t, *consts)


def _gla_tables(c):
    levels = int(math.log2(c))
    assert 1 << levels == c
    t = np.arange(c)[:, None]
    u = np.arange(c)[None, :]
    mats, masks = [], []
    for l in range(levels):
        half = 1 << l
        base = (t // half) * half
        upper = ((t >> l) & 1) == 1
        a_q = (u >= base) & (u <= t)
        a_k = (u > t) & (u <= base + half - 1)
        mats.append(np.where(upper, a_q, a_k))
        same = (t >> (l + 1)) == (u >> (l + 1))
        masks.append(same & upper & (((u >> l) & 1) == 0))
    mats.append(u <= t)
    mats.append(u > t)
    masks.append(t == u)
    amat = np.concatenate(mats, axis=0).astype(np.float32)
    mask = np.stack([np.tile(m, (GLA_HEADS, 1)) for m in masks]).astype(np.float32)
    return jnp.asarray(amat, BF16), jnp.asarray(mask, F32), levels


def _gla_kernel(q_ref, k_ref, v_ref, lg_ref, sr_ref, s0_ref, amat_ref, mask_ref, g_ref,
                o_ref, sout_ref, state, *, levels):
    c = q_ref.shape[0]
    hk = GLA_HEADS * GLA_DK
    step = pl.program_id(1)

    @pl.when(step == 0)
    def _():
        state[...] = s0_ref[0]

    lg = lg_ref[...]
    parts = _split3(lg)
    e3 = _dot(amat_ref[...], jnp.concatenate(parts, axis=1))
    ex = jnp.exp(e3[:, :hk] + e3[:, hk:2 * hk] + e3[:, 2 * hk:])
    q = q_ref[...].astype(F32)
    k = k_ref[...].astype(F32)
    v = v_ref[...]
    head = lax.broadcasted_iota(jnp.int32, (1, hk), 1) // GLA_DK

    def per_head(xf):
        return jnp.concatenate([jnp.where(head == h, xf, 0.0) for h in range(GLA_HEADS)], axis=0).astype(BF16)

    att = jnp.zeros((GLA_HEADS * c, c), F32)
    for l in range(levels + 1):
        if l < levels:
            el = ex[l * c:(l + 1) * c]
            ql, kl = q * el, k * el
        else:
            ql, kl = q, k
        a = lax.dot_general(per_head(ql), kl.astype(BF16), (((1,), (1,)), ((), ())), preferred_element_type=F32)
        att = att + a * mask_ref[l]
    s_prev = state[...]
    o_inter = _dot(per_head(q * ex[levels * c:(levels + 1) * c]), s_prev.astype(BF16))
    att = att.astype(BF16)
    g = g_ref[...]
    for h in range(GLA_HEADS):
        o = o_inter[h * c:(h + 1) * c] + _dot(att[h * c:(h + 1) * c], v[:, GLA_DV * h:GLA_DV * (h + 1)])
        o = _rms(o, g)
        o_ref[:, GLA_DV * h:GLA_DV * (h + 1)] = (o * sr_ref[:, GLA_DV * h:GLA_DV * (h + 1)].astype(F32)).astype(BF16)
    kr = (k * ex[(levels + 1) * c:]).astype(BF16)
    upd = lax.dot_general(kr, v, (((0,), (0,)), ((), ())), preferred_element_type=F32)
    ones = jnp.ones((c, GLA_DV), BF16)
    b_last = sum(lax.dot_general(p, ones, (((0,), (0,)), ((), ())), preferred_element_type=F32) for p in parts)
    new = jnp.exp(b_last) * s_prev + jnp.concatenate(
        [upd[GLA_DK * h:GLA_DK * (h + 1), GLA_DV * h:GLA_DV * (h + 1)] for h in range(GLA_HEADS)], axis=0)
    state[...] = new

    @pl.when(step == pl.num_programs(1) - 1)
    def _():
        sout_ref[0] = new


def _gla(q, k, v, lg, sr, s0, g, tables, batch, blocks_per_seq):
    amat, mask, levels = tables
    c = GLA_BLOCK
    hk, hv = GLA_HEADS * GLA_DK, GLA_HEADS * GLA_DV
    row = lambda n: pl.BlockSpec((c, n), lambda b, s: (b * blocks_per_seq + s, 0))
    st = pl.BlockSpec((1, hk, GLA_DV), lambda b, s: (b, 0, 0))
    return pl.pallas_call(
        functools.partial(_gla_kernel, levels=levels),
        grid=(batch, blocks_per_seq),
        in_specs=[row(hk), row(hk), row(hv), row(hk), row(hv), st,
                  _const_spec(amat.shape), _const_spec(mask.shape), _const_spec(g.shape)],
        out_specs=[row(hv), st],
        out_shape=[jax.ShapeDtypeStruct((batch * blocks_per_seq * c, hv), BF16),
                   jax.ShapeDtypeStruct((batch, hk, GLA_DV), F32)],
        scratch_shapes=[pltpu.VMEM((hk, GLA_DV), F32)],
        compiler_params=_params(("parallel", "arbitrary")),
        name="gla",
    )(q, k, v, lg, sr, s0, amat, mask, g)


def _mla_kernel(q_ref, kt_ref, kv_ref, o_ref, s_sc, p_sc, linv_sc, acc_sc, *, tq, n_past, past_valid, own_valid,
                diag_chunks):
    kvb = MLA_KV_TILE
    qi = pl.program_id(1)
    qs = jnp.concatenate([q_ref[:, MLA_QK * h:MLA_QK * (h + 1)] for h in range(MLA_HEADS)], axis=0)
    m_rows = MLA_HEADS * tq
    n_vis = n_past + qi + 1

    def score(j, mask):
        s = _dot(qs, kt_ref[0, j])
        s_sc[j] = s if mask is None else jnp.where(mask, s, NEG_INF)

    def scores(lo, hi):
        def body(j, carry):
            score(j, None)
            return carry
        lax.fori_loop(lo, hi, body, 0)

    col = lax.broadcasted_iota(jnp.int32, (m_rows, kvb), 1)
    n_past_full = past_valid // kvb
    if n_past_full:
        scores(0, n_past_full)
    if n_past_full < n_past:
        score(n_past_full, col < past_valid - n_past_full * kvb)
    scores(n_past, n_past + qi)
    mask = col < own_valid - qi * kvb
    if diag_chunks:
        qrow = lax.broadcasted_iota(jnp.int32, (m_rows, kvb), 0) % tq
        mask = mask & (col // CHUNK <= qrow // CHUNK)
    score(n_past + qi, mask)

    rc = MLA_ROW_STRIP

    def strip(r, carry):
        rows = pl.ds(pl.multiple_of(r * rc, rc), rc)
        mx = lax.fori_loop(0, n_vis, lambda j, m: jnp.maximum(m, s_sc[j, rows, :]),
                           jnp.full((rc, kvb), -jnp.inf, F32))
        m = jnp.max(mx, axis=1, keepdims=True)

        def body(j, l):
            p = jnp.exp2(s_sc[j, rows, :] - m)
            p_sc[j, rows, :] = p.astype(BF16)
            return l + p
        l = lax.fori_loop(0, n_vis, body, jnp.zeros((rc, kvb), F32))
        linv_sc[rows, :] = 1.0 / jnp.sum(l, axis=1, keepdims=True)
        return carry
    lax.fori_loop(0, m_rows // rc, strip, 0)

    acc_sc[...] = jnp.zeros(acc_sc.shape, F32)

    def pv(j, carry):
        vblk = kv_ref[0, pl.ds(pl.multiple_of(j * kvb, kvb), kvb), :MLA_KV_LORA]
        acc_sc[...] += _dot(p_sc[j], vblk)
        return carry
    lax.fori_loop(0, n_vis, pv, 0)
    o = acc_sc[...] * linv_sc[...]
    for h in range(MLA_HEADS):
        o_ref[:, MLA_KV_LORA * h:MLA_KV_LORA * (h + 1)] = o[h * tq:(h + 1) * tq].astype(BF16)


def _mla(q_all, kt, kv, batch, nq, tq, q_block0, past_valid, own_valid):
    n_kv = kt.shape[1]
    n_past = -(-past_valid // MLA_KV_TILE)
    assert nq == 1 or tq == MLA_KV_TILE
    assert n_kv == n_past + nq
    return pl.pallas_call(
        functools.partial(_mla_kernel, tq=tq, n_past=n_past, past_valid=past_valid, own_valid=own_valid,
                          diag_chunks=nq > 1),
        grid=(batch, nq),
        in_specs=[pl.BlockSpec((tq, MLA_HEADS * MLA_QK), lambda b, i: (q_block0 + b * nq + i, 0)),
                  pl.BlockSpec((1,) + kt.shape[1:], lambda b, i: (b, 0, 0, 0)),
                  pl.BlockSpec((1,) + kv.shape[1:], lambda b, i: (b, 0, 0))],
        out_specs=pl.BlockSpec((tq, MLA_HEADS * MLA_KV_LORA), lambda b, i: (b * nq + i, 0)),
        out_shape=jax.ShapeDtypeStruct((batch * nq * tq, MLA_HEADS * MLA_KV_LORA), BF16),
        scratch_shapes=[pltpu.VMEM((n_kv, MLA_HEADS * tq, MLA_KV_TILE), F32),
                        pltpu.VMEM((n_kv, MLA_HEADS * tq, MLA_KV_TILE), BF16),
                        pltpu.VMEM((MLA_HEADS * tq, 1), F32),
                        pltpu.VMEM((MLA_HEADS * tq, MLA_KV_LORA), F32)],
        compiler_params=_params(("parallel", "arbitrary")),
        name="mla",
    )(q_all, kt, kv)


def _post_mixer(x, out, g, b, rw_ref, x1_ref, x1b_ref, lgt_ref):
    y = _layer_norm(ALPHA * x + out, g, b)
    x1_ref[...] = y
    x1b_ref[...] = y.astype(BF16)
    lgt_ref[0] = _dot_f32_nt(rw_ref[...], y)


def _outproj_kernel(og_ref, ol_ref, x_ref, wuv_ref, wo_ref, g_ref, b_ref, rw_ref, x1_ref, x1b_ref, lgt_ref):
    o_mla = _dot(ol_ref[...], wuv_ref[...]).astype(BF16)
    n_gla = og_ref.shape[1]
    out = _dot(og_ref[...], wo_ref[:n_gla, :]) + _dot(o_mla, wo_ref[n_gla:, :])
    _post_mixer(x_ref[...], out, g_ref[...], b_ref[...], rw_ref, x1_ref, x1b_ref, lgt_ref)


def _outproj(o_gla, o_lat, x_all, w_uvbd, w_out, g, b, rw):
    t, d = x_all.shape
    tm = TOK_TILE
    row = lambda n: pl.BlockSpec((tm, n), lambda i: (i, 0))
    consts = [w_uvbd, w_out, g, b, rw]
    return pl.pallas_call(
        _outproj_kernel,
        grid=(t // tm,),
        in_specs=[row(o_gla.shape[1]), row(o_lat.shape[1]), row(d)] + [_const_spec(c.shape) for c in consts],
        out_specs=[row(d), row(d), pl.BlockSpec((1, N_EXPERTS, tm), lambda i: (i, 0, 0))],
        out_shape=[jax.ShapeDtypeStruct((t, d), F32), jax.ShapeDtypeStruct((t, d), BF16),
                   jax.ShapeDtypeStruct((t // tm, N_EXPERTS, tm), F32)],
        compiler_params=_params(("parallel",)),
        name="outproj",
    )(o_gla, o_lat, x_all, *consts)


def _pool_kernel(x_ref, prev_ref, hist_ref, pw_ref, ps_ref, g_ref, b_ref, rw_ref, x1_ref, x1b_ref, lgt_ref):
    ts = x_ref.shape[0]
    pm = POOL_MAX
    x = x_ref[...]
    prev = jnp.where(pl.program_id(1) == 0, hist_ref[0], prev_ref[...])
    xc = jnp.concatenate([prev, x], axis=0)
    grp = x.shape[1] // len(POOL_WINDOWS)
    outs = []
    for gi, w in enumerate(POOL_WINDOWS):
        s = xc[:, gi * grp:(gi + 1) * grp]
        span = 1
        while span < w:
            s = s + pltpu.roll(s, span, 0)
            span *= 2
        win = s[pm:]
        mix = (win * (1.0 / w) - x[:, gi * grp:(gi + 1) * grp]).astype(BF16)
        outs.append(_dot(mix, pw_ref[gi]))
    out = jnp.concatenate(outs, axis=1) * ps_ref[...]
    _post_mixer(x, out, g_ref[...], b_ref[...], rw_ref, x1_ref, x1b_ref, lgt_ref)


def _pool(x_all, hist, pool_w, pool_scale, g, b, rw, batch, tiles_per_seq, ts, tile0):
    d = x_all.shape[1]
    pm = POOL_MAX
    per = ts // pm
    consts = [pool_w, pool_scale, g, b, rw]
    row = lambda n: pl.BlockSpec((ts, n), lambda bb, i: (bb * tiles_per_seq + i, 0))
    return pl.pallas_call(
        _pool_kernel,
        grid=(batch, tiles_per_seq),
        in_specs=[pl.BlockSpec((ts, d), lambda bb, i: (tile0 + bb * tiles_per_seq + i, 0)),
                  pl.BlockSpec((pm, d), lambda bb, i: (jnp.maximum((tile0 + bb * tiles_per_seq + i) * per - 1, 0), 0)),
                  pl.BlockSpec((1, pm, d), lambda bb, i: (bb, 0, 0))] + [_const_spec(c.shape) for c in consts],
        out_specs=[row(d), row(d), pl.BlockSpec((1, N_EXPERTS, ts), lambda bb, i: (bb * tiles_per_seq + i, 0, 0))],
        out_shape=[jax.ShapeDtypeStruct((batch * tiles_per_seq * ts, d), F32),
                   jax.ShapeDtypeStruct((batch * tiles_per_seq * ts, d), BF16),
                   jax.ShapeDtypeStruct((batch * tiles_per_seq, N_EXPERTS, ts), F32)],
        compiler_params=_params(("parallel", "arbitrary")),
        name="pool",
    )(x_all, x_all, hist, *consts)


def _experts_kernel(be_ref, nb_ref, x_ref, wg_ref, wu_ref, wd_ref, y_ref):
    @pl.when(pl.program_id(0) < nb_ref[0])
    def _():
        wg, wu, wd = wg_ref[0, 0].astype(BF16), wu_ref[0, 0].astype(BF16), wd_ref[0, 0].astype(BF16)
        sub = EXPERT_SUBTILE
        for r in range(x_ref.shape[0] // sub):
            x = x_ref[r * sub:(r + 1) * sub, :]
            hmid = (_silu(_dot(x, wg)) * _dot(x, wu)).astype(BF16)
            y_ref[r * sub:(r + 1) * sub, :] = _dot(hmid, wd).astype(BF16)

    @pl.when(pl.program_id(0) >= nb_ref[0])
    def _():
        y_ref[...] = jnp.zeros_like(y_ref)


def _experts(blk_e, n_used, x_sorted, wg, wu, wd, layer):
    p, d = x_sorted.shape
    f = wg.shape[3]
    bm = EXPERT_TILE
    blk = lambda i, be, nb: (i, 0)
    grid_spec = pltpu.PrefetchScalarGridSpec(
        num_scalar_prefetch=2,
        grid=(p // bm,),
        in_specs=[pl.BlockSpec((bm, d), lambda i, be, nb: (jnp.minimum(i, nb[0] - 1), 0)),
                  pl.BlockSpec((1, 1, d, f), lambda i, be, nb: (layer, be[i], 0, 0)),
                  pl.BlockSpec((1, 1, d, f), lambda i, be, nb: (layer, be[i], 0, 0)),
                  pl.BlockSpec((1, 1, f, d), lambda i, be, nb: (layer, be[i], 0, 0))],
        out_specs=pl.BlockSpec((bm, d), blk),
    )
    return pl.pallas_call(
        _experts_kernel,
        grid_spec=grid_spec,
        out_shape=jax.ShapeDtypeStruct((p, d), BF16),
        compiler_params=_params(("arbitrary",)),
        name="experts",
    )(blk_e, n_used, x_sorted, wg, wu, wd)


def _segment_copies(tile, slot, dst_ref, n16_ref, seg_ref, make, entries=N_EXPERTS):
    big = SEG_ROWS * BIG_PIECE

    def per_expert(e, carry):
        j = tile * N_EXPERTS + e
        d0, s0, n = dst_ref[j], seg_ref[j], n16_ref[j]
        n_big = n // BIG_PIECE

        def big_piece(c, carry2):
            make(pl.multiple_of(d0 + c * big, SEG_ROWS), pl.multiple_of(s0 + c * big, SEG_ROWS), big, slot).start()
            return carry2
        lax.fori_loop(0, n_big, big_piece, 0)
        d1, s1 = d0 + n_big * big, s0 + n_big * big

        def small_piece(c, carry2):
            make(pl.multiple_of(d1 + c * SEG_ROWS, SEG_ROWS), pl.multiple_of(s1 + c * SEG_ROWS, SEG_ROWS),
                 SEG_ROWS, slot).start()
            return carry2
        lax.fori_loop(0, n - n_big * BIG_PIECE, small_piece, 0)
        return carry
    lax.fori_loop(0, entries, per_expert, 0)


def _tile_rows(tile, n16_ref, seg_ref):
    last = tile * N_EXPERTS + N_EXPERTS - 1
    return seg_ref[last] + n16_ref[last] * SEG_ROWS


def _drain(rows, slot, make):
    left = rows // SEG_ROWS
    for unit in DRAIN_UNITS:
        n = left // unit

        def body(c, carry, unit=unit):
            make(0, 0, unit * SEG_ROWS, slot).wait()
            return carry
        lax.fori_loop(0, n, body, 0)
        left = left - n * unit


def _ffn_out_kernel(dst_ref, n16_ref, seg_ref, x_ref, xb_ref, pw_ref, ys_ref,
                    sg_ref, su_ref, sd_ref, g_ref, b_ref, y_ref, yb_ref, ybuf, racc, sem):
    i = pl.program_id(0)
    nt = pl.num_programs(0)
    slot = i % 2

    def make(d0, s0, rows, sl):
        return pltpu.make_async_copy(ys_ref.at[pl.ds(d0, rows)], ybuf.at[sl, pl.ds(s0, rows)], sem.at[sl])

    @pl.when(i == 0)
    def _():
        ybuf[...] = jnp.zeros_like(ybuf)
        _segment_copies(0, 0, dst_ref, n16_ref, seg_ref, make)

    @pl.when(i + 1 < nt)
    def _():
        _segment_copies(i + 1, 1 - slot, dst_ref, n16_ref, seg_ref, make)

    xb = xb_ref[...]
    hmid = (_silu(_dot(xb, sg_ref[...])) * _dot(xb, su_ref[...])).astype(BF16)
    shared = _dot(hmid, sd_ref[...])
    used = _tile_rows(i, n16_ref, seg_ref)
    _drain(used, slot, make)

    racc[...] = jnp.zeros_like(racc)
    rs = COMBINE_STRIP
    for s in range(ybuf.shape[1] // rs):
        @pl.when(s * rs < used)
        def _():
            racc[...] += lax.dot_general(pw_ref[0, s * rs:(s + 1) * rs, :], ybuf[slot, s * rs:(s + 1) * rs, :],
                                         (((0,), (0,)), ((), ())), preferred_element_type=F32)
    ffn = racc[...] + shared
    y = _layer_norm(ALPHA * x_ref[...] + ffn, g_ref[...], b_ref[...])
    y_ref[...] = y
    yb_ref[...] = y.astype(BF16)


def _ffn_out(dst, n16, seg, x1, x1b, pw, y_sorted, sg, su, sd, g, b):
    t, d = x1.shape
    nt, _, tm = pw.shape
    row = lambda n: pl.BlockSpec((tm, n), lambda i, *_: (i, 0))
    per_tile = lambda a: pl.BlockSpec((1,) + a.shape[1:], lambda i, *_: (i, 0, 0))
    consts = [sg, su, sd, g, b]
    grid_spec = pltpu.PrefetchScalarGridSpec(
        num_scalar_prefetch=3,
        grid=(nt,),
        in_specs=[row(d), row(d), per_tile(pw), pl.BlockSpec(memory_space=pl.ANY)]
        + [_const_spec(c.shape) for c in consts],
        out_specs=[row(d), row(d)],
        scratch_shapes=[pltpu.VMEM((2, SORT_ROWS, d), BF16), pltpu.VMEM((tm, d), F32), pltpu.SemaphoreType.DMA((2,))],
    )
    return pl.pallas_call(
        _ffn_out_kernel,
        grid_spec=grid_spec,
        out_shape=[jax.ShapeDtypeStruct((t, d), F32), jax.ShapeDtypeStruct((t, d), BF16)],
        compiler_params=_params(("arbitrary",)),
        name="ffn_out",
    )(dst, n16, seg, x1, x1b, pw, y_sorted, *consts)


def _dispatch_kernel(dst_ref, n16_ref, seg_ref, tdst_ref, tn16_ref, x_ref, rid_ref, w_ref, segr_ref, c16r_ref,
                     xs_ref, pw_ref, buf, zbuf, sem):
    i = pl.program_id(0)
    nt = pl.num_programs(0)
    slot = i % 2
    tm = x_ref.shape[0]

    def make(d0, s0, rows, sl):
        return pltpu.make_async_copy(buf.at[sl, pl.ds(s0, rows)], xs_ref.at[pl.ds(d0, rows)], sem.at[sl])

    @pl.when(i >= 2)
    def _():
        _drain(_tile_rows(i - 2, n16_ref, seg_ref), slot, make)

    x = x_ref[...]
    rid = rid_ref[0]
    wb = w_ref[0].astype(BF16)
    used = _tile_rows(i, n16_ref, seg_ref)
    rs = DISPATCH_STRIP
    hs = MXU_DIM
    for s in range(buf.shape[1] // rs):
        @pl.when(s * rs < used)
        def _():
            hits = []
            for h0 in range(s * rs, (s + 1) * rs, hs):
                r = lax.broadcasted_iota(jnp.int32, (hs, N_EXPERTS), 0) + h0
                owner = jnp.where((r >= segr_ref[0]) & (r < segr_ref[0] + c16r_ref[0]), 1.0, 0.0).astype(BF16)
                local = jnp.where((rid > h0) & (rid <= h0 + hs), rid - h0, 0.0).astype(BF16)
                at_row = _dot(owner, jnp.concatenate([local, wb], axis=1))
                hit = at_row[:, :tm] == (lax.broadcasted_iota(jnp.int32, (hs, tm), 0) + 1).astype(F32)
                hits.append(jnp.where(hit, 1.0, 0.0).astype(BF16))
                pw_ref[0, h0:h0 + hs, :] = jnp.where(hit, at_row[:, tm:], 0.0).astype(BF16)
            buf[slot, s * rs:(s + 1) * rs, :] = _dot(jnp.concatenate(hits, axis=0), x).astype(BF16)

        @pl.when(s * rs >= used)
        def _():
            pw_ref[0, s * rs:(s + 1) * rs, :] = jnp.zeros((rs, tm), BF16)
    _segment_copies(i, slot, dst_ref, n16_ref, seg_ref, make)

    @pl.when(i == nt - 1)
    def _():
        zbuf[...] = jnp.zeros_like(zbuf)

        def zmake(d0, s0, rows, sl):
            return pltpu.make_async_copy(zbuf.at[pl.ds(0, rows)], xs_ref.at[pl.ds(d0, rows)], sem.at[sl])
        _segment_copies(0, 2, tdst_ref, tn16_ref, tdst_ref, zmake, entries=N_EXPERTS + 1)
        tail = lax.fori_loop(0, N_EXPERTS + 1, lambda e, acc: acc + tn16_ref[e], 0) * SEG_ROWS
        _drain(tail, 2, make)

        @pl.when(i >= 1)
        def _():
            _drain(_tile_rows(i - 1, n16_ref, seg_ref), 1 - slot, make)
        _drain(used, slot, make)


def _dispatch(dst, n16, seg, tail_dst, tail_n16, x1b, rid, wdense, segr, c16r, n_rows):
    t, d = x1b.shape
    nt, ne, tm = rid.shape
    per_tile = lambda a: pl.BlockSpec((1,) + a.shape[1:], lambda i, *_: (i, 0, 0))
    grid_spec = pltpu.PrefetchScalarGridSpec(
        num_scalar_prefetch=5,
        grid=(nt,),
        in_specs=[pl.BlockSpec((tm, d), lambda i, *_: (i, 0)), per_tile(rid), per_tile(wdense), per_tile(segr),
                  per_tile(c16r)],
        out_specs=[pl.BlockSpec(memory_space=pl.ANY), pl.BlockSpec((1, SORT_ROWS, tm), lambda i, *_: (i, 0, 0))],
        scratch_shapes=[pltpu.VMEM((2, SORT_ROWS, d), BF16), pltpu.VMEM((SEG_ROWS * BIG_PIECE, d), BF16),
                        pltpu.SemaphoreType.DMA((3,))],
    )
    return pl.pallas_call(
        _dispatch_kernel,
        grid_spec=grid_spec,
        out_shape=[jax.ShapeDtypeStruct((n_rows, d), BF16), jax.ShapeDtypeStruct((nt, SORT_ROWS, tm), BF16)],
        compiler_params=_params(("arbitrary",)),
        name="dispatch",
    )(dst, n16, seg, tail_dst, tail_n16, x1b, rid, wdense, segr, c16r)


def _outranked(vals, n, ids):
    rank = jnp.zeros(vals.shape, jnp.int32)
    for j in range(n):
        row = vals[j:j + 1]
        rank = rank + ((row > vals) | ((row == vals) & (j < ids))).astype(jnp.int32)
    return rank


def _router_kernel(lg_ref, b_ref, tri_ref, low_ref, rid_ref, w_ref, tab_ref, carry):
    @pl.when(pl.program_id(0) == 0)
    def _():
        carry[...] = jnp.zeros_like(carry)

    tm = lg_ref.shape[2]
    gs = N_EXPERTS // N_GROUPS
    scores = 1.0 / (1.0 + jnp.exp(-lg_ref[0]))
    biased = scores + b_ref[...]
    member = lax.broadcasted_iota(jnp.int32, (gs, tm), 0)
    group_score = []
    for g in range(N_GROUPS):
        blk = biased[g * gs:(g + 1) * gs]
        m1 = jnp.max(blk, axis=0, keepdims=True)
        first = jnp.min(jnp.where(blk == m1, member, gs), axis=0, keepdims=True)
        m2 = jnp.max(jnp.where(member == first, -jnp.inf, blk), axis=0, keepdims=True)
        group_score.append(m1 + m2)
    group_score = jnp.concatenate(group_score, axis=0)
    gid = lax.broadcasted_iota(jnp.int32, (N_GROUPS, tm), 0)
    group_ok = _outranked(group_score, N_GROUPS, gid) < TOPK_GROUPS
    masked = jnp.concatenate([jnp.where(group_ok[g:g + 1], biased[g * gs:(g + 1) * gs], -jnp.inf)
                              for g in range(N_GROUPS)], axis=0)
    eid = lax.broadcasted_iota(jnp.int32, (N_EXPERTS, tm), 0)
    sel = _outranked(masked, N_EXPERTS, eid) < TOP_K
    self = jnp.where(sel, 1.0, 0.0)
    wsel = self * scores
    wts = wsel / jnp.sum(wsel, axis=0, keepdims=True) * ROUTED_SCALE
    selb = self.astype(BF16)
    before = _dot(selb, tri_ref[...])
    pieces = jnp.floor((jnp.sum(self, axis=1, keepdims=True) + (SEG_ROWS - 1.0)) * (1.0 / SEG_ROWS))
    seg = SEG_ROWS * _dot(low_ref[...], jnp.broadcast_to(pieces, (N_EXPERTS, LANES)).astype(BF16))
    rid_ref[0] = jnp.where(sel, seg[:, :1] + before + 1.0, NO_ROW)
    w_ref[0] = wts
    lane = lax.broadcasted_iota(jnp.int32, (N_EXPERTS, LANES), 1)
    tab_ref[0] = jnp.where(lane == 0, seg, jnp.where(lane == 1, SEG_ROWS * pieces, jnp.where(lane == 2, carry[...], 0.0)))
    carry[...] = carry[...] + SEG_ROWS * pieces


def _router(logits, router_b):
    nt, ne, tm = logits.shape
    tri = jnp.asarray(np.triu(np.ones((tm, tm), np.float32), 1), BF16)
    low = jnp.asarray(np.tril(np.ones((ne, ne), np.float32), -1), BF16)
    per_tile = lambda n: pl.BlockSpec((1, ne, n), lambda i: (i, 0, 0))
    return pl.pallas_call(
        _router_kernel,
        grid=(nt,),
        in_specs=[per_tile(tm), _const_spec((ne, 1)), _const_spec(tri.shape), _const_spec(low.shape)],
        out_specs=[per_tile(tm), per_tile(tm), per_tile(LANES)],
        out_shape=[jax.ShapeDtypeStruct((nt, ne, tm), F32), jax.ShapeDtypeStruct((nt, ne, tm), F32),
                   jax.ShapeDtypeStruct((nt, ne, LANES), F32)],
        scratch_shapes=[pltpu.VMEM((ne, 1), F32)],
        compiler_params=_params(("arbitrary",)),
        name="router",
    )(logits, router_b.reshape(ne, 1).astype(F32), tri, low)


def _moe(x1, x1b, logits, router_b, layer, wg, wu, wd, sg, su, sd, g, b):
    t, d = x1.shape
    nt, ne, tm = logits.shape
    if tm != ROUTE_TILE:
        split = tm // ROUTE_TILE
        logits = logits.reshape(nt, ne, split, ROUTE_TILE).transpose(0, 2, 1, 3).reshape(nt * split, ne, ROUTE_TILE)
        nt, tm = nt * split, ROUTE_TILE
    assert SORT_ROWS >= TOP_K * tm + ne * (SEG_ROWS - 1) and SORT_ROWS < NO_ROW
    rid, wdense, tab = _router(logits, router_b)
    tab = tab[:, :, :3].astype(jnp.int32)
    seg, rows, base = tab[:, :, 0], tab[:, :, 1], tab[:, :, 2]
    bm = EXPERT_TILE
    region = base[-1] + rows[-1]
    padded = (region + bm - 1) // bm * bm
    pad_end = jnp.cumsum(padded)
    dst = (pad_end - padded)[None, :] + base
    n_blocks = -(-(t * TOP_K + nt * ne * (SEG_ROWS - 1)) // bm) + ne
    blk_start = jnp.arange(n_blocks, dtype=jnp.int32) * bm
    blk_e = jnp.minimum(jnp.sum((pad_end[None, :] <= blk_start[:, None]).astype(jnp.int32), axis=1), ne - 1)
    n_used = (pad_end[-1] // bm).astype(jnp.int32).reshape(1)
    tail_dst = jnp.concatenate([pad_end - padded + region, pad_end[-1:]])
    tail_n16 = jnp.concatenate([padded - region, n_blocks * bm - pad_end[-1:]]) // SEG_ROWS
    dst, n16, segf = dst.reshape(-1), (rows // SEG_ROWS).reshape(-1), seg.reshape(-1)
    x_sorted, pw = _dispatch(dst, n16, segf, tail_dst, tail_n16, x1b, rid, wdense, seg[:, None, :], rows[:, None, :],
                             n_blocks * bm)
    y_sorted = _experts(blk_e, n_used, x_sorted, wg, wu, wd, layer)
    return _ffn_out(dst, n16, segf, x1, x1b, pw, y_sorted, sg, su, sd, g, b)


def _rope_tables(pos):
    half = MLA_ROPE // 2
    inv = ROPE_THETA ** (-jnp.arange(half, dtype=F32) / half)
    ang = pos.astype(F32)[:, None] * inv
    reps = LANES // half
    return jnp.tile(jnp.cos(ang), (1, reps)), jnp.tile(jnp.sin(ang), (1, reps))


def _pack_layer0_weights(w_in0, gla_w_g2, gla_b_g, mla_q_norm_g, mla_kv_norm_g, mla_w_uq, mla_w_uk, mla_w_uv):
    d = w_in0.shape[0]
    hk, hv = GLA_HEADS * GLA_DK, GLA_HEADS * GLA_DV
    o_q, o_k, o_v, o_r = 0, hk, 2 * hk, 2 * hk + hv
    o_a = o_r + hv
    o_cq = o_a + GLA_RANK
    o_ckv = o_cq + MLA_Q_LORA
    o_kr = o_ckv + MLA_KV_LORA
    half = MLA_ROPE // 2
    zeros = lambda n: jnp.zeros((d, n), w_in0.dtype)
    kr1, kr2 = w_in0[:, o_kr:o_kr + half], w_in0[:, o_kr + half:o_kr + MLA_ROPE]
    w_in = jnp.concatenate([
        w_in0[:, o_q:o_a], w_in0[:, o_cq:o_kr],
        kr1, kr2, zeros(LANES - MLA_ROPE),
        kr2, kr1, zeros(LANES - MLA_ROPE),
        w_in0[:, o_a:o_cq], zeros(LANES - GLA_RANK)], axis=1).astype(BF16)
    assert w_in.shape[1] == _C_END
    w_g2 = jnp.concatenate([gla_w_g2, jnp.zeros((LANES - GLA_RANK, hk), gla_w_g2.dtype)], axis=0)
    uq = mla_w_uq.reshape(MLA_Q_LORA, MLA_HEADS, MLA_NOPE + MLA_ROPE)
    w_uq = jnp.concatenate([uq[:, :, :MLA_NOPE].reshape(MLA_Q_LORA, -1),
                            uq[:, :, MLA_NOPE:MLA_NOPE + half].reshape(MLA_Q_LORA, -1),
                            uq[:, :, MLA_NOPE + half:].reshape(MLA_Q_LORA, -1)], axis=1).astype(BF16)
    eye = jnp.eye(MLA_HEADS, dtype=mla_w_uk.dtype)
    w_abs = jnp.einsum('chn,hg->hngc', mla_w_uk, eye).reshape(MLA_HEADS * MLA_NOPE, MLA_HEADS * MLA_KV_LORA).astype(BF16)
    w_uvbd = jnp.einsum('chv,hg->hcgv', mla_w_uv, eye).reshape(MLA_HEADS * MLA_KV_LORA, MLA_HEADS * MLA_V).astype(BF16)
    place = np.zeros((2 * LANES, MLA_HEADS * LANES), np.float32)
    for h in range(MLA_HEADS):
        for j in range(half):
            place[h * half + j, h * LANES + j] = 1.0
            place[LANES + h * half + j, h * LANES + half + j] = 1.0
    return dict(w_in=w_in, w_g2=w_g2, b_g=gla_b_g.reshape(1, hk), q_norm=mla_q_norm_g.reshape(1, -1),
                kv_norm=mla_kv_norm_g.reshape(1, -1), w_uq=w_uq, w_abs=w_abs,
                place=jnp.asarray(place, BF16)), w_uvbd


def _pad_rows(x, n):
    return jnp.pad(x, ((0, n - x.shape[0]),) + ((0, 0),) * (x.ndim - 1))


def _kv_tiles(past, own):
    kvb = MLA_KV_TILE
    pad = lambda a: jnp.pad(a, ((0, 0), (0, -a.shape[1] % kvb), (0, 0)))
    kv = jnp.concatenate([pad(past), pad(own)], axis=1) if past.shape[1] else pad(own)
    b, n, w = kv.shape
    kt = kv.reshape(b, n // kvb, kvb, w).transpose(0, 1, 3, 2)
    return kt, kv


def kernel(x_prompt, x_sample, cache_mla_ckv, cache_mla_krope, state_gla, cache_pool, meta_tokens, w_in0, gla_w_g2, gla_b_g, gla_norm_g, mla_q_norm_g, mla_kv_norm_g, mla_w_uq, mla_w_uk, mla_w_uv, w_out0, pool_w, pool_scale, ln_g, ln_b, moe_router_w, moe_router_b, moe_w_gate, moe_w_up, moe_w_down, moe_sh_gate, moe_sh_up, moe_sh_down):
    bp, sp, d = x_prompt.shape
    bs, ss, _ = x_sample.shape
    n_meta = meta_tokens.shape[0]
    past_len = cache_mla_ckv.shape[1] - n_meta
    tp, tsm = bp * sp, bs * ss
    assert sp % MLA_Q_TILE == 0 and sp % POOL_TILE == 0 and sp % GLA_BLOCK == 0 and sp % CHUNK == 0
    assert ss <= CHUNK and past_len % CHUNK == 0 and n_meta <= CHUNK and ss <= GLA_BLOCK and n_meta <= GLA_BLOCK
    assert n_meta == POOL_MAX and ss >= POOL_MAX and tp % TOK_TILE == 0
    t_all = -(-(tp + tsm + n_meta) // TOK_TILE) * TOK_TILE
    o_s, o_m = tp, tp + tsm

    x_all = _pad_rows(jnp.concatenate([x_prompt.reshape(tp, d), x_sample.reshape(tsm, d), meta_tokens], axis=0), t_all)
    pos = jnp.concatenate([jnp.tile(n_meta + jnp.arange(sp), bp), jnp.tile(n_meta + past_len + jnp.arange(ss), bs),
                           jnp.arange(n_meta), jnp.zeros((t_all - o_m - n_meta,), jnp.int32)])
    cos_t, sin_t = _rope_tables(pos)
    w0, w_uvbd = _pack_layer0_weights(w_in0, gla_w_g2, gla_b_g, mla_q_norm_g, mla_kv_norm_g, mla_w_uq, mla_w_uk, mla_w_uv)

    q, k, v, sr, lg, q_all, ckv, kr, k_all = _inproj(x_all, cos_t, sin_t, w0)

    tables = _gla_tables(GLA_BLOCK)
    g_gla = gla_norm_g.reshape(1, GLA_DV)
    c = GLA_BLOCK
    hk = GLA_HEADS * GLA_DK

    def small(a):
        sm = jnp.pad(a[o_s:o_m].reshape(bs, ss, -1), ((0, 0), (0, c - ss), (0, 0)))
        me = jnp.pad(a[o_m:o_m + n_meta], ((0, c - n_meta), (0, 0)))[None]
        return jnp.concatenate([sm, me], axis=0).reshape((bs + 1) * c, -1)

    s0_small = jnp.concatenate([state_gla.reshape(bs, hk, GLA_DV), jnp.zeros((1, hk, GLA_DV), F32)], axis=0)
    og_small, st_small = _gla(small(q), small(k), small(v), small(lg), small(sr), s0_small, g_gla, tables, bs + 1, 1)
    s0_prompt = jnp.broadcast_to(st_small[bs:], (bp, hk, GLA_DV))
    og_prompt, st_prompt = _gla(q, k, v, lg, sr, s0_prompt, g_gla, tables, bp, sp // c)
    og_small = og_small.reshape(bs + 1, c, -1)
    o_gla = _pad_rows(jnp.concatenate([og_prompt, og_small[:bs, :ss].reshape(tsm, -1), og_small[bs, :n_meta]], axis=0), t_all)

    k_meta = k_all[o_m:o_m + n_meta]
    kt_m, kv_m = _kv_tiles(jnp.zeros((1, 0, MLA_QK), BF16), k_meta[None])
    ol_meta = _mla(q_all[o_m:o_m + n_meta], kt_m, kv_m, 1, 1, n_meta, 0, 0, n_meta)
    cache = jnp.concatenate([cache_mla_ckv, cache_mla_krope,
                             jnp.zeros(cache_mla_ckv.shape[:2] + (MLA_QK - MLA_KV_LORA - MLA_ROPE,), F32)], axis=-1).astype(BF16)
    kt_s, kv_s = _kv_tiles(cache, k_all[o_s:o_m].reshape(bs, ss, MLA_QK))
    ol_sample = _mla(q_all[o_s:o_m], kt_s, kv_s, bs, 1, ss, 0, n_meta + past_len, ss)
    kt_p, kv_p = _kv_tiles(jnp.broadcast_to(k_meta[None], (bp, n_meta, MLA_QK)), k_all[:tp].reshape(bp, sp, MLA_QK))
    ol_prompt = _mla(q_all, kt_p, kv_p, bp, sp // MLA_Q_TILE, MLA_Q_TILE, 0, n_meta, sp)
    o_lat = _pad_rows(jnp.concatenate([ol_prompt, ol_sample, ol_meta], axis=0), t_all)

    ln = lambda l, j: (ln_g[l, j].reshape(1, d), ln_b[l, j].reshape(1, d))
    rw = lambda l: moe_router_w[l].T
    moe_w = lambda l: (l, moe_w_gate, moe_w_up, moe_w_down,
                       moe_sh_gate[l].astype(BF16), moe_sh_up[l].astype(BF16), moe_sh_down[l].astype(BF16))
    x1, x1b, logits = _outproj(o_gla, o_lat, x_all, w_uvbd, w_out0.astype(BF16), *ln(0, 0), rw(0))
    x2, _ = _moe(x1, x1b, logits, moe_router_b[0], *moe_w(0), *ln(0, 1))

    pm = POOL_MAX
    zrow = jnp.zeros((1, d), F32)
    hist_p = jnp.broadcast_to(jnp.concatenate([zrow, x2[o_m + n_meta - (pm - 1):o_m + n_meta]], axis=0)[None], (bp, pm, d))
    hist_s = jnp.concatenate([jnp.zeros((bs, 1, d), F32), cache_pool], axis=1)
    pool_wb = pool_w.astype(BF16)
    ps = pool_scale.reshape(1, d)
    y1p = _pool(x2, hist_p, pool_wb, ps, *ln(1, 0), rw(1), bp, sp // POOL_TILE, POOL_TILE, 0)
    y1s = _pool(x2[o_s:o_m], hist_s, pool_wb, ps, *ln(1, 0), rw(1), bs, 1, ss, 0)
    t1 = -(-(tp + tsm) // TOK_TILE) * TOK_TILE
    x3, x3b = [_pad_rows(jnp.concatenate([a, b_], axis=0), t1) for a, b_ in zip(y1p[:2], y1s[:2])]
    by_token = lambda lt: lt.transpose(1, 0, 2).reshape(N_EXPERTS, -1)
    logits1 = jnp.pad(jnp.concatenate([by_token(y1p[2]), by_token(y1s[2])], axis=1), ((0, 0), (0, t1 - tp - tsm)))
    logits1 = logits1.reshape(N_EXPERTS, t1 // TOK_TILE, TOK_TILE).transpose(1, 0, 2)
    x4, _ = _moe(x3, x3b, logits1, moe_router_b[1], *moe_w(1), *ln(1, 1))

    y_prompt = x4[:tp].reshape(bp, sp, d)
    y_sample = x4[o_s:o_m].reshape(bs, ss, d)
    ckv_meta, kr_meta = ckv[o_m:o_m + n_meta], kr[o_m:o_m + n_meta, :MLA_ROPE]
    p_ckv = jnp.concatenate([jnp.broadcast_to(ckv_meta[None], (bp, n_meta, MLA_KV_LORA)),
                             ckv[:tp].reshape(bp, sp, MLA_KV_LORA)], axis=1)
    p_kr = jnp.concatenate([jnp.broadcast_to(kr_meta[None], (bp, n_meta, MLA_ROPE)),
                            kr[:tp, :MLA_ROPE].reshape(bp, sp, MLA_ROPE)], axis=1)
    p_gla = st_prompt.reshape(bp, GLA_HEADS, GLA_DK, GLA_DV)
    x2p = x2[:tp].reshape(bp, sp, d)
    p_pool = x2p[:, sp - (pm - 1):]
    s_ckv = ckv[o_s:o_m].reshape(bs, ss, MLA_KV_LORA)
    s_kr = kr[o_s:o_m, :MLA_ROPE].reshape(bs, ss, MLA_ROPE)
    s_gla = st_small[:bs].reshape(bs, GLA_HEADS, GLA_DK, GLA_DV)
    s_pool = x2[o_s:o_m].reshape(bs, ss, d)[:, ss - (pm - 1):]
    return (y_prompt, y_sample, p_ckv, p_kr, p_gla, p_pool, s_ckv, s_kr, s_gla, s_pool)
```

```python
import functools
import math

import jax
import jax.numpy as jnp
import numpy as np
from jax import lax
from jax.experimental import pallas as pl
from jax.experimental.pallas import tpu as pltpu

F32 = jnp.float32
BF16 = jnp.bfloat16

CHUNK = 64
DEPTH = 2
ALPHA = (2 * DEPTH) ** 0.25
LN_EPS = 1e-5
RMS_EPS = 1e-6
NEG_INF = -1e30
GLA_HEADS = 4
GLA_DK = 64
GLA_DV = 128
GLA_RANK = 16
GLA_TAU = 16.0
MLA_HEADS = 8
MLA_Q_LORA = 256
MLA_KV_LORA = 128
MLA_NOPE = 64
MLA_ROPE = 32
MLA_V = 64
MLA_SCALE = (MLA_NOPE + MLA_ROPE) ** -0.5
ROPE_THETA = 10000.0
POOL_WINDOWS = (2, 4, 8, 16)
POOL_MAX = 16
N_EXPERTS = 64
TOP_K = 8
N_GROUPS = 8
TOPK_GROUPS = 4
ROUTED_SCALE = 2.5

LANES = 128
MXU_DIM = 256

TOK_TILE = 512
GLA_BLOCK = 128
MLA_Q_TILE = 256
MLA_KV_TILE = 256
MLA_QK = 256
MLA_ROW_STRIP = 128
LOG2E = math.log2(math.e)
EXPERT_TILE = 1024
EXPERT_SUBTILE = 256
POOL_TILE = 512
SEG_ROWS = 16
BIG_PIECE = 4
DRAIN_UNITS = (64, 16, 4, 1)
ROUTE_TILE = 512
SORT_ROWS = 5120
K_BIG = (TOP_K * ROUTE_TILE + N_EXPERTS * (SEG_ROWS - 1)) // (SEG_ROWS * BIG_PIECE)
K_SMALL = N_EXPERTS * (BIG_PIECE - 1)
PACK_BITS = 9
NO_ROW = 8191.0
DISPATCH_STRIP = 512
COMBINE_STRIP = 1024
VMEM_LIMIT = 56 * 1024 * 1024


def _params(sem, vmem=VMEM_LIMIT):
    return pltpu.CompilerParams(dimension_semantics=sem, vmem_limit_bytes=vmem)


def _const_spec(shape):
    nd = len(shape)
    return pl.BlockSpec(shape, lambda *_: (0,) * nd)


def _split3(x):
    hi = x.astype(BF16)
    r = x - hi.astype(F32)
    mid = r.astype(BF16)
    lo = (r - mid.astype(F32)).astype(BF16)
    return hi, mid, lo


def _dot(a, b):
    return jnp.dot(a, b, preferred_element_type=F32)


def _dot_f32(a, b):
    a_hi = a.astype(BF16)
    a_lo = (a - a_hi.astype(F32)).astype(BF16)
    b_hi = b.astype(BF16)
    b_lo = (b - b_hi.astype(F32)).astype(BF16)
    return _dot(a_hi, b_hi) + (_dot(a_lo, b_hi) + _dot(a_hi, b_lo))


def _dot_f32_nt(a, b):
    nt = lambda u, v: lax.dot_general(u, v, (((1,), (1,)), ((), ())), preferred_element_type=F32)
    a_hi = a.astype(BF16)
    a_lo = (a - a_hi.astype(F32)).astype(BF16)
    b_hi = b.astype(BF16)
    b_lo = (b - b_hi.astype(F32)).astype(BF16)
    return nt(a_hi, b_hi) + (nt(a_lo, b_hi) + nt(a_hi, b_lo))


def _silu(x):
    return x * (1.0 / (1.0 + jnp.exp(-x)))


def _layer_norm(x, g, b):
    mu = jnp.mean(x, axis=-1, keepdims=True)
    xc = x - mu
    var = jnp.mean(xc * xc, axis=-1, keepdims=True)
    return xc * lax.rsqrt(var + LN_EPS) * g + b


def _rms(x, g):
    return x * lax.rsqrt(jnp.mean(x * x, axis=-1, keepdims=True) + RMS_EPS) * g


_C_Q, _C_K, _C_V, _C_R, _C_CQ, _C_CKV, _C_KR, _C_KRS, _C_A, _C_END = (
    0, 256, 512, 1024, 1536, 1792, 1920, 2048, 2176, 2304)


def _inproj_kernel(x_ref, cos_ref, sin_ref, w_ref, wg2_ref, bg_ref, qn_ref, kvn_ref, wuq_ref, wabs_ref,
                   place_ref, q_ref, k_ref, v_ref, sr_ref, lg_ref, qall_ref, ckv_ref, kr_ref, kall_ref):
    h = _dot(x_ref[...].astype(BF16), w_ref[...])
    q_ref[...] = (h[:, _C_Q:_C_K] * GLA_DK ** -0.5).astype(BF16)
    k_ref[...] = h[:, _C_K:_C_V].astype(BF16)
    v_ref[...] = h[:, _C_V:_C_R].astype(BF16)
    sr_ref[...] = _silu(h[:, _C_R:_C_CQ]).astype(BF16)
    z = _dot_f32(h[:, _C_A:_C_END], wg2_ref[...]) + bg_ref[...]
    lg_ref[...] = (jnp.minimum(z, 0.0) - jnp.log(1.0 + jnp.exp(-jnp.abs(z)))) * (1.0 / GLA_TAU)
    cqn = _rms(h[:, _C_CQ:_C_CKV], qn_ref[...]).astype(BF16)
    qh = _dot(cqn, wuq_ref[...])
    n_nope = MLA_HEADS * MLA_NOPE
    cos, sin = cos_ref[...], sin_ref[...]
    x1, x2 = qh[:, n_nope:n_nope + LANES], qh[:, n_nope + LANES:]
    qscale = MLA_SCALE * LOG2E
    rot = (jnp.concatenate([x1 * cos - x2 * sin, x1 * sin + x2 * cos], axis=1) * qscale).astype(BF16)
    qlat = (_dot(qh[:, :n_nope].astype(BF16), wabs_ref[...]) * qscale).astype(BF16)
    qrope = _dot(rot, place_ref[...]).astype(BF16)
    for hd in range(MLA_HEADS):
        qall_ref[:, MLA_QK * hd:MLA_QK * hd + LANES] = qlat[:, LANES * hd:LANES * (hd + 1)]
        qall_ref[:, MLA_QK * hd + LANES:MLA_QK * (hd + 1)] = qrope[:, LANES * hd:LANES * (hd + 1)]
    ckv = _rms(h[:, _C_CKV:_C_KR], kvn_ref[...])
    lane = lax.broadcasted_iota(jnp.int32, (1, LANES), 1)
    sgn = jnp.where(lane < MLA_ROPE // 2, -1.0, 1.0)
    kr = h[:, _C_KR:_C_KRS] * cos + h[:, _C_KRS:_C_A] * (sin * sgn)
    ckv_ref[...] = ckv
    kr_ref[...] = kr
    kall_ref[:, :LANES] = ckv.astype(BF16)
    kall_ref[:, LANES:] = kr.astype(BF16)


def _inproj(x_all, cos_t, sin_t, w):
    t = x_all.shape[0]
    d = x_all.shape[1]
    tm = TOK_TILE
    row = lambda n: pl.BlockSpec((tm, n), lambda i: (i, 0))
    consts = [w['w_in'], w['w_g2'], w['b_g'], w['q_norm'], w['kv_norm'], w['w_uq'], w['w_abs'], w['place']]
    outs = [(GLA_HEADS * GLA_DK, BF16), (GLA_HEADS * GLA_DK, BF16), (GLA_HEADS * GLA_DV, BF16),
            (GLA_HEADS * GLA_DV, BF16), (GLA_HEADS * GLA_DK, F32), (MLA_HEADS * MLA_QK, BF16),
            (LANES, F32), (LANES, F32), (MLA_QK, BF16)]
    return pl.pallas_call(
        _inproj_kernel,
        grid=(t // tm,),
        in_specs=[row(d), row(LANES), row(LANES)] + [_const_spec(c.shape) for c in consts],
        out_specs=[row(n) for n, _ in outs],
        out_shape=[jax.ShapeDtypeStruct((t, n), dt) for n, dt in outs],
        compiler_params=_params(("parallel",)),
        name="inproj",
    )(x_all, cos_t, sin_t, *consts)


def _gla_tables(c):
    levels = int(math.log2(c))
    assert 1 << levels == c
    t = np.arange(c)[:, None]
    u = np.arange(c)[None, :]
    mats, masks = [], []
    for l in range(levels):
        half = 1 << l
        base = (t // half) * half
        upper = ((t >> l) & 1) == 1
        a_q = (u >= base) & (u <= t)
        a_k = (u > t) & (u <= base + half - 1)
        mats.append(np.where(upper, a_q, a_k))
        same = (t >> (l + 1)) == (u >> (l + 1))
        masks.append(same & upper & (((u >> l) & 1) == 0))
    mats.append(u <= t)
    mats.append(u > t)
    masks.append(t == u)
    amat = np.concatenate(mats, axis=0).astype(np.float32)
    mask = np.stack([np.tile(m, (GLA_HEADS, 1)) for m in masks]).astype(np.float32)
    return jnp.asarray(amat, BF16), jnp.asarray(mask, F32), levels


def _gla_kernel(q_ref, k_ref, v_ref, lg_ref, sr_ref, s0_ref, amat_ref, mask_ref, g_ref,
                o_ref, sout_ref, state, *, levels):
    c = q_ref.shape[0]
    hk = GLA_HEADS * GLA_DK
    step = pl.program_id(1)

    @pl.when(step == 0)
    def _():
        state[...] = s0_ref[0]

    lg = lg_ref[...]
    parts = _split3(lg)
    e3 = _dot(amat_ref[...], jnp.concatenate(parts, axis=1))
    ex = jnp.exp(e3[:, :hk] + e3[:, hk:2 * hk] + e3[:, 2 * hk:])
    q = q_ref[...].astype(F32)
    k = k_ref[...].astype(F32)
    v = v_ref[...]
    head = lax.broadcasted_iota(jnp.int32, (1, hk), 1) // GLA_DK

    def per_head(xf):
        return jnp.concatenate([jnp.where(head == h, xf, 0.0) for h in range(GLA_HEADS)], axis=0).astype(BF16)

    att = jnp.zeros((GLA_HEADS * c, c), F32)
    for l in range(levels + 1):
        if l < levels:
            el = ex[l * c:(l + 1) * c]
            ql, kl = q * el, k * el
        else:
            ql, kl = q, k
        a = lax.dot_general(per_head(ql), kl.astype(BF16), (((1,), (1,)), ((), ())), preferred_element_type=F32)
        att = att + a * mask_ref[l]
    s_prev = state[...]
    o_inter = _dot(per_head(q * ex[levels * c:(levels + 1) * c]), s_prev.astype(BF16))
    att = att.astype(BF16)
    g = g_ref[...]
    for h in range(GLA_HEADS):
        o = o_inter[h * c:(h + 1) * c] + _dot(att[h * c:(h + 1) * c], v[:, GLA_DV * h:GLA_DV * (h + 1)])
        o = _rms(o, g)
        o_ref[:, GLA_DV * h:GLA_DV * (h + 1)] = (o * sr_ref[:, GLA_DV * h:GLA_DV * (h + 1)].astype(F32)).astype(BF16)
    kr = (k * ex[(levels + 1) * c:]).astype(BF16)
    upd = lax.dot_general(kr, v, (((0,), (0,)), ((), ())), preferred_element_type=F32)
    ones = jnp.ones((c, GLA_DV), BF16)
    b_last = sum(lax.dot_general(p, ones, (((0,), (0,)), ((), ())), preferred_element_type=F32) for p in parts)
    new = jnp.exp(b_last) * s_prev + jnp.concatenate(
        [upd[GLA_DK * h:GLA_DK * (h + 1), GLA_DV * h:GLA_DV * (h + 1)] for h in range(GLA_HEADS)], axis=0)
    state[...] = new

    @pl.when(step == pl.num_programs(1) - 1)
    def _():
        sout_ref[0] = new


def _gla(q, k, v, lg, sr, s0, g, tables, batch, blocks_per_seq, out_rows=None):
    amat, mask, levels = tables
    c = GLA_BLOCK
    hk, hv = GLA_HEADS * GLA_DK, GLA_HEADS * GLA_DV
    row = lambda n: pl.BlockSpec((c, n), lambda b, s: (b * blocks_per_seq + s, 0))
    st = pl.BlockSpec((1, hk, GLA_DV), lambda b, s: (b, 0, 0))
    return pl.pallas_call(
        functools.partial(_gla_kernel, levels=levels),
        grid=(batch, blocks_per_seq),
        in_specs=[row(hk), row(hk), row(hv), row(hk), row(hv), st,
                  _const_spec(amat.shape), _const_spec(mask.shape), _const_spec(g.shape)],
        out_specs=[row(hv), st],
        out_shape=[jax.ShapeDtypeStruct((out_rows or batch * blocks_per_seq * c, hv), BF16),
                   jax.ShapeDtypeStruct((batch, hk, GLA_DV), F32)],
        scratch_shapes=[pltpu.VMEM((hk, GLA_DV), F32)],
        compiler_params=_params(("parallel", "arbitrary")),
        name="gla",
    )(q, k, v, lg, sr, s0, amat, mask, g)


def _mla_kernel(q_ref, kt_ref, kv_ref, o_ref, s_sc, p_sc, linv_sc, acc_sc, *, tq, n_past, past_valid, own_valid,
                diag_chunks):
    kvb = MLA_KV_TILE
    qi = pl.program_id(1)
    qs = jnp.concatenate([q_ref[:, MLA_QK * h:MLA_QK * (h + 1)] for h in range(MLA_HEADS)], axis=0)
    m_rows = MLA_HEADS * tq
    n_vis = n_past + qi + 1

    def score(j, mask):
        s = _dot(qs, kt_ref[0, j])
        s_sc[j] = s if mask is None else jnp.where(mask, s, NEG_INF)

    def scores(lo, hi):
        def body(j, carry):
            score(j, None)
            return carry
        lax.fori_loop(lo, hi, body, 0)

    col = lax.broadcasted_iota(jnp.int32, (m_rows, kvb), 1)
    n_past_full = past_valid // kvb
    if n_past_full:
        scores(0, n_past_full)
    if n_past_full < n_past:
        score(n_past_full, col < past_valid - n_past_full * kvb)
    scores(n_past, n_past + qi)
    mask = col < own_valid - qi * kvb
    if diag_chunks:
        qrow = lax.broadcasted_iota(jnp.int32, (m_rows, kvb), 0) % tq
        mask = mask & (col // CHUNK <= qrow // CHUNK)
    score(n_past + qi, mask)

    rc = MLA_ROW_STRIP

    def strip(r, carry):
        rows = pl.ds(pl.multiple_of(r * rc, rc), rc)
        mx = lax.fori_loop(0, n_vis, lambda j, m: jnp.maximum(m, s_sc[j, rows, :]),
                           jnp.full((rc, kvb), -jnp.inf, F32))
        m = jnp.max(mx, axis=1, keepdims=True)

        def body(j, l):
            p = jnp.exp2(s_sc[j, rows, :] - m)
            p_sc[j, rows, :] = p.astype(BF16)
            return l + p
        l = lax.fori_loop(0, n_vis, body, jnp.zeros((rc, kvb), F32))
        linv_sc[rows, :] = 1.0 / jnp.sum(l, axis=1, keepdims=True)
        return carry
    lax.fori_loop(0, m_rows // rc, strip, 0)

    acc_sc[...] = jnp.zeros(acc_sc.shape, F32)

    def pv(j, carry):
        vblk = kv_ref[0, pl.ds(pl.multiple_of(j * kvb, kvb), kvb), :MLA_KV_LORA]
        acc_sc[...] += _dot(p_sc[j], vblk)
        return carry
    lax.fori_loop(0, n_vis, pv, 0)
    o = acc_sc[...] * linv_sc[...]
    for h in range(MLA_HEADS):
        o_ref[:, MLA_KV_LORA * h:MLA_KV_LORA * (h + 1)] = o[h * tq:(h + 1) * tq].astype(BF16)


def _mla(q_all, kt, kv, batch, nq, tq, q_block0, past_valid, own_valid, out_rows=None):
    n_kv = kt.shape[1]
    n_past = -(-past_valid // MLA_KV_TILE)
    assert nq == 1 or tq == MLA_KV_TILE
    assert n_kv == n_past + nq
    return pl.pallas_call(
        functools.partial(_mla_kernel, tq=tq, n_past=n_past, past_valid=past_valid, own_valid=own_valid,
                          diag_chunks=nq > 1),
        grid=(batch, nq),
        in_specs=[pl.BlockSpec((tq, MLA_HEADS * MLA_QK), lambda b, i: (q_block0 + b * nq + i, 0)),
                  pl.BlockSpec((1,) + kt.shape[1:], lambda b, i: (b, 0, 0, 0)),
                  pl.BlockSpec((1,) + kv.shape[1:], lambda b, i: (b, 0, 0))],
        out_specs=pl.BlockSpec((tq, MLA_HEADS * MLA_KV_LORA), lambda b, i: (b * nq + i, 0)),
        out_shape=jax.ShapeDtypeStruct((out_rows or batch * nq * tq, MLA_HEADS * MLA_KV_LORA), BF16),
        scratch_shapes=[pltpu.VMEM((n_kv, MLA_HEADS * tq, MLA_KV_TILE), F32),
                        pltpu.VMEM((n_kv, MLA_HEADS * tq, MLA_KV_TILE), BF16),
                        pltpu.VMEM((MLA_HEADS * tq, 1), F32),
                        pltpu.VMEM((MLA_HEADS * tq, MLA_KV_LORA), F32)],
        compiler_params=_params(("parallel", "arbitrary")),
        name="mla",
    )(q_all, kt, kv)


def _post_mixer(x, out, g, b, rw_ref, x1_ref, x1b_ref, lgt_ref):
    y = _layer_norm(ALPHA * x + out, g, b)
    x1_ref[...] = y
    x1b_ref[...] = y.astype(BF16)
    lgt_ref[0] = _dot_f32_nt(rw_ref[...], y)


def _outproj_kernel(og_ref, ol_ref, x_ref, wuv_ref, wo_ref, g_ref, b_ref, rw_ref, x1_ref, x1b_ref, lgt_ref):
    o_mla = _dot(ol_ref[...], wuv_ref[...]).astype(BF16)
    n_gla = og_ref.shape[1]
    out = _dot(og_ref[...], wo_ref[:n_gla, :]) + _dot(o_mla, wo_ref[n_gla:, :])
    _post_mixer(x_ref[...], out, g_ref[...], b_ref[...], rw_ref, x1_ref, x1b_ref, lgt_ref)


def _outproj(o_gla, o_lat, x_all, w_uvbd, w_out, g, b, rw):
    t, d = x_all.shape
    tm = TOK_TILE
    row = lambda n: pl.BlockSpec((tm, n), lambda i: (i, 0))
    consts = [w_uvbd, w_out, g, b, rw]
    return pl.pallas_call(
        _outproj_kernel,
        grid=(t // tm,),
        in_specs=[row(o_gla.shape[1]), row(o_lat.shape[1]), row(d)] + [_const_spec(c.shape) for c in consts],
        out_specs=[row(d), row(d), pl.BlockSpec((1, N_EXPERTS, tm), lambda i: (i, 0, 0))],
        out_shape=[jax.ShapeDtypeStruct((t, d), F32), jax.ShapeDtypeStruct((t, d), BF16),
                   jax.ShapeDtypeStruct((t // tm, N_EXPERTS, tm), F32)],
        compiler_params=_params(("parallel",)),
        name="outproj",
    )(o_gla, o_lat, x_all, *consts)


def _pool_kernel(x_ref, prev_ref, hist_ref, pw_ref, ps_ref, g_ref, b_ref, rw_ref, x1_ref, x1b_ref, lgt_ref):
    ts = x_ref.shape[0]
    pm = POOL_MAX
    x = x_ref[...]
    prev = jnp.where(pl.program_id(1) == 0, hist_ref[0], prev_ref[...])
    xc = jnp.concatenate([prev, x], axis=0)
    grp = x.shape[1] // len(POOL_WINDOWS)
    outs = []
    for gi, w in enumerate(POOL_WINDOWS):
        s = xc[:, gi * grp:(gi + 1) * grp]
        span = 1
        while span < w:
            s = s + pltpu.roll(s, span, 0)
            span *= 2
        win = s[pm:]
        mix = (win * (1.0 / w) - x[:, gi * grp:(gi + 1) * grp]).astype(BF16)
        outs.append(_dot(mix, pw_ref[gi]))
    out = jnp.concatenate(outs, axis=1) * ps_ref[...]
    _post_mixer(x, out, g_ref[...], b_ref[...], rw_ref, x1_ref, x1b_ref, lgt_ref)


def _pool(x_all, hist, pool_w, pool_scale, g, b, rw, batch, tiles_per_seq, ts, tile0, out_rows=None):
    d = x_all.shape[1]
    pm = POOL_MAX
    per = ts // pm
    consts = [pool_w, pool_scale, g, b, rw]
    row = lambda n: pl.BlockSpec((ts, n), lambda bb, i: (bb * tiles_per_seq + i, 0))
    return pl.pallas_call(
        _pool_kernel,
        grid=(batch, tiles_per_seq),
        in_specs=[pl.BlockSpec((ts, d), lambda bb, i: (tile0 + bb * tiles_per_seq + i, 0)),
                  pl.BlockSpec((pm, d), lambda bb, i: (jnp.maximum((tile0 + bb * tiles_per_seq + i) * per - 1, 0), 0)),
                  pl.BlockSpec((1, pm, d), lambda bb, i: (bb, 0, 0))] + [_const_spec(c.shape) for c in consts],
        out_specs=[row(d), row(d), pl.BlockSpec((1, N_EXPERTS, ts), lambda bb, i: (bb * tiles_per_seq + i, 0, 0))],
        out_shape=[jax.ShapeDtypeStruct((out_rows or batch * tiles_per_seq * ts, d), F32),
                   jax.ShapeDtypeStruct((out_rows or batch * tiles_per_seq * ts, d), BF16),
                   jax.ShapeDtypeStruct((batch * tiles_per_seq, N_EXPERTS, ts), F32)],
        compiler_params=_params(("parallel", "arbitrary")),
        name="pool",
    )(x_all, x_all, hist, *consts)


def _experts_kernel(be_ref, nb_ref, x_ref, wg_ref, wu_ref, wd_ref, y_ref):
    @pl.when(pl.program_id(0) < nb_ref[0])
    def _():
        wg, wu, wd = wg_ref[0, 0].astype(BF16), wu_ref[0, 0].astype(BF16), wd_ref[0, 0].astype(BF16)
        sub = EXPERT_SUBTILE
        for r in range(x_ref.shape[0] // sub):
            x = x_ref[r * sub:(r + 1) * sub, :]
            hmid = (_silu(_dot(x, wg)) * _dot(x, wu)).astype(BF16)
            y_ref[r * sub:(r + 1) * sub, :] = _dot(hmid, wd).astype(BF16)

    @pl.when(pl.program_id(0) >= nb_ref[0])
    def _():
        y_ref[...] = jnp.zeros_like(y_ref)


def _experts(blk_e, n_used, x_sorted, wg, wu, wd, layer):
    p, d = x_sorted.shape
    f = wg.shape[3]
    bm = EXPERT_TILE
    blk = lambda i, be, nb: (i, 0)
    grid_spec = pltpu.PrefetchScalarGridSpec(
        num_scalar_prefetch=2,
        grid=(p // bm,),
        in_specs=[pl.BlockSpec((bm, d), lambda i, be, nb: (jnp.minimum(i, nb[0] - 1), 0)),
                  pl.BlockSpec((1, 1, d, f), lambda i, be, nb: (layer, be[i], 0, 0)),
                  pl.BlockSpec((1, 1, d, f), lambda i, be, nb: (layer, be[i], 0, 0)),
                  pl.BlockSpec((1, 1, f, d), lambda i, be, nb: (layer, be[i], 0, 0))],
        out_specs=pl.BlockSpec((bm, d), blk),
    )
    return pl.pallas_call(
        _experts_kernel,
        grid_spec=grid_spec,
        out_shape=jax.ShapeDtypeStruct((p, d), BF16),
        compiler_params=_params(("arbitrary",)),
        name="experts",
    )(blk_e, n_used, x_sorted, wg, wu, wd)


def _segment_copies(tile, slot, dst_ref, n16_ref, seg_ref, make, entries=N_EXPERTS):
    big = SEG_ROWS * BIG_PIECE

    def per_expert(e, carry):
        j = tile * N_EXPERTS + e
        d0, s0, n = dst_ref[j], seg_ref[j], n16_ref[j]
        n_big = n // BIG_PIECE

        def big_piece(c, carry2):
            make(pl.multiple_of(d0 + c * big, SEG_ROWS), pl.multiple_of(s0 + c * big, SEG_ROWS), big, slot).start()
            return carry2
        lax.fori_loop(0, n_big, big_piece, 0)
        d1, s1 = d0 + n_big * big, s0 + n_big * big

        def small_piece(c, carry2):
            make(pl.multiple_of(d1 + c * SEG_ROWS, SEG_ROWS), pl.multiple_of(s1 + c * SEG_ROWS, SEG_ROWS),
                 SEG_ROWS, slot).start()
            return carry2
        lax.fori_loop(0, n - n_big * BIG_PIECE, small_piece, 0)
        return carry
    lax.fori_loop(0, entries, per_expert, 0)


def _start_pieces(tile, slot, big_ref, small_ref, cnt_ref, make):
    def run(tab_ref, kmax, n, rows):
        def body(k, carry):
            v = tab_ref[tile * kmax + k]
            make(pl.multiple_of((v >> PACK_BITS) * SEG_ROWS, SEG_ROWS),
                 pl.multiple_of((v & ((1 << PACK_BITS) - 1)) * SEG_ROWS, SEG_ROWS), rows, slot).start()
            return carry
        lax.fori_loop(0, n, body, 0)
    run(big_ref, K_BIG, cnt_ref[2 * tile], SEG_ROWS * BIG_PIECE)
    run(small_ref, K_SMALL, cnt_ref[2 * tile + 1], SEG_ROWS)


def _tile_rows(tile, cnt_ref):
    return cnt_ref[2 * tile] * (SEG_ROWS * BIG_PIECE) + cnt_ref[2 * tile + 1] * SEG_ROWS


def _piece_tables(dst, n16, seg):
    nt, ne = n16.shape
    big_rows = SEG_ROWS * BIG_PIECE
    n_big, n_small = n16 // BIG_PIECE, n16 % BIG_PIECE

    def flat(counts, first_dst, first_src, step, kmax):
        cum = jnp.cumsum(counts, axis=1)
        k = jnp.arange(kmax, dtype=jnp.int32)
        e = jnp.minimum(jnp.sum((cum[:, None, :] <= k[None, :, None]).astype(jnp.int32), axis=2), ne - 1)
        off = step * (k[None, :] - jnp.take_along_axis(cum - counts, e, axis=1))
        d = jnp.take_along_axis(first_dst, e, axis=1) + off
        s = jnp.take_along_axis(first_src, e, axis=1) + off
        return (((d // SEG_ROWS) << PACK_BITS) | (s // SEG_ROWS)).reshape(-1).astype(jnp.int32), cum[:, -1]
    big, cnt_big = flat(n_big, dst, seg, big_rows, K_BIG)
    small, cnt_small = flat(n_small, dst + n_big * big_rows, seg + n_big * big_rows, SEG_ROWS, K_SMALL)
    return big, small, jnp.stack([cnt_big, cnt_small], axis=1).reshape(-1).astype(jnp.int32)


def _drain(rows, slot, make):
    left = rows // SEG_ROWS
    for unit in DRAIN_UNITS:
        n = left // unit

        def body(c, carry, unit=unit):
            make(0, 0, unit * SEG_ROWS, slot).wait()
            return carry
        lax.fori_loop(0, n, body, 0)
        left = left - n * unit


def _ffn_out_kernel(big_ref, small_ref, cnt_ref, x_ref, xb_ref, pw_ref, ys_ref,
                    sg_ref, su_ref, sd_ref, g_ref, b_ref, y_ref, yb_ref, ybuf, racc, sem):
    i = pl.program_id(0)
    nt = pl.num_programs(0)
    slot = i % 2

    def make(d0, s0, rows, sl):
        return pltpu.make_async_copy(ys_ref.at[pl.ds(d0, rows)], ybuf.at[sl, pl.ds(s0, rows)], sem.at[sl])

    @pl.when(i == 0)
    def _():
        ybuf[...] = jnp.zeros_like(ybuf)
        _start_pieces(0, 0, big_ref, small_ref, cnt_ref, make)

    @pl.when(i + 1 < nt)
    def _():
        _start_pieces(i + 1, 1 - slot, big_ref, small_ref, cnt_ref, make)

    xb = xb_ref[...]
    hmid = (_silu(_dot(xb, sg_ref[...])) * _dot(xb, su_ref[...])).astype(BF16)
    shared = _dot(hmid, sd_ref[...])
    used = _tile_rows(i, cnt_ref)
    _drain(used, slot, make)

    racc[...] = jnp.zeros_like(racc)
    rs = COMBINE_STRIP
    for s in range(ybuf.shape[1] // rs):
        @pl.when(s * rs < used)
        def _():
            racc[...] += lax.dot_general(pw_ref[0, s * rs:(s + 1) * rs, :], ybuf[slot, s * rs:(s + 1) * rs, :],
                                         (((0,), (0,)), ((), ())), preferred_element_type=F32)
    ffn = racc[...] + shared
    y = _layer_norm(ALPHA * x_ref[...] + ffn, g_ref[...], b_ref[...])
    y_ref[...] = y
    yb_ref[...] = y.astype(BF16)


def _ffn_out(big, small, cnt, x1, x1b, pw, y_sorted, sg, su, sd, g, b):
    t, d = x1.shape
    nt, _, tm = pw.shape
    row = lambda n: pl.BlockSpec((tm, n), lambda i, *_: (i, 0))
    per_tile = lambda a: pl.BlockSpec((1,) + a.shape[1:], lambda i, *_: (i, 0, 0))
    consts = [sg, su, sd, g, b]
    grid_spec = pltpu.PrefetchScalarGridSpec(
        num_scalar_prefetch=3,
        grid=(nt,),
        in_specs=[row(d), row(d), per_tile(pw), pl.BlockSpec(memory_space=pl.ANY)]
        + [_const_spec(c.shape) for c in consts],
        out_specs=[row(d), row(d)],
        scratch_shapes=[pltpu.VMEM((2, SORT_ROWS, d), BF16), pltpu.VMEM((tm, d), F32), pltpu.SemaphoreType.DMA((2,))],
    )
    return pl.pallas_call(
        _ffn_out_kernel,
        grid_spec=grid_spec,
        out_shape=[jax.ShapeDtypeStruct((t, d), F32), jax.ShapeDtypeStruct((t, d), BF16)],
        compiler_params=_params(("arbitrary",)),
        name="ffn_out",
    )(big, small, cnt, x1, x1b, pw, y_sorted, *consts)


def _dispatch_kernel(big_ref, small_ref, cnt_ref, tdst_ref, tn16_ref, x_ref, rid_ref, w_ref, segr_ref, c16r_ref,
                     xs_ref, pw_ref, buf, zbuf, sem):
    i = pl.program_id(0)
    nt = pl.num_programs(0)
    slot = i % 2
    tm = x_ref.shape[0]

    def make(d0, s0, rows, sl):
        return pltpu.make_async_copy(buf.at[sl, pl.ds(s0, rows)], xs_ref.at[pl.ds(d0, rows)], sem.at[sl])

    @pl.when(i >= 2)
    def _():
        _drain(_tile_rows(i - 2, cnt_ref), slot, make)

    x = x_ref[...]
    rid = rid_ref[0]
    wb = w_ref[0].astype(BF16)
    used = _tile_rows(i, cnt_ref)
    rs = DISPATCH_STRIP
    hs = MXU_DIM
    for s in range(buf.shape[1] // rs):
        @pl.when(s * rs < used)
        def _():
            hits = []
            for h0 in range(s * rs, (s + 1) * rs, hs):
                r = lax.broadcasted_iota(jnp.int32, (hs, N_EXPERTS), 0) + h0
                owner = jnp.where((r >= segr_ref[0]) & (r < segr_ref[0] + c16r_ref[0]), 1.0, 0.0).astype(BF16)
                local = jnp.where((rid > h0) & (rid <= h0 + hs), rid - h0, 0.0).astype(BF16)
                at_row = _dot(owner, jnp.concatenate([local, wb], axis=1))
                hit = at_row[:, :tm] == (lax.broadcasted_iota(jnp.int32, (hs, tm), 0) + 1).astype(F32)
                hits.append(jnp.where(hit, 1.0, 0.0).astype(BF16))
                pw_ref[0, h0:h0 + hs, :] = jnp.where(hit, at_row[:, tm:], 0.0).astype(BF16)
            buf[slot, s * rs:(s + 1) * rs, :] = _dot(jnp.concatenate(hits, axis=0), x).astype(BF16)

        @pl.when(s * rs >= used)
        def _():
            pw_ref[0, s * rs:(s + 1) * rs, :] = jnp.zeros((rs, tm), BF16)
    _start_pieces(i, slot, big_ref, small_ref, cnt_ref, make)

    @pl.when(i == nt - 1)
    def _():
        zbuf[...] = jnp.zeros_like(zbuf)

        def zmake(d0, s0, rows, sl):
            return pltpu.make_async_copy(zbuf.at[pl.ds(0, rows)], xs_ref.at[pl.ds(d0, rows)], sem.at[sl])
        _segment_copies(0, 2, tdst_ref, tn16_ref, tdst_ref, zmake, entries=N_EXPERTS + 1)
        tail = lax.fori_loop(0, N_EXPERTS + 1, lambda e, acc: acc + tn16_ref[e], 0) * SEG_ROWS
        _drain(tail, 2, make)

        @pl.when(i >= 1)
        def _():
            _drain(_tile_rows(i - 1, cnt_ref), 1 - slot, make)
        _drain(used, slot, make)


def _dispatch(big, small, cnt, tail_dst, tail_n16, x1b, rid, wdense, segr, c16r, n_rows):
    t, d = x1b.shape
    nt, ne, tm = rid.shape
    per_tile = lambda a: pl.BlockSpec((1,) + a.shape[1:], lambda i, *_: (i, 0, 0))
    grid_spec = pltpu.PrefetchScalarGridSpec(
        num_scalar_prefetch=5,
        grid=(nt,),
        in_specs=[pl.BlockSpec((tm, d), lambda i, *_: (i, 0)), per_tile(rid), per_tile(wdense), per_tile(segr),
                  per_tile(c16r)],
        out_specs=[pl.BlockSpec(memory_space=pl.ANY), pl.BlockSpec((1, SORT_ROWS, tm), lambda i, *_: (i, 0, 0))],
        scratch_shapes=[pltpu.VMEM((2, SORT_ROWS, d), BF16), pltpu.VMEM((SEG_ROWS * BIG_PIECE, d), BF16),
                        pltpu.SemaphoreType.DMA((3,))],
    )
    return pl.pallas_call(
        _dispatch_kernel,
        grid_spec=grid_spec,
        out_shape=[jax.ShapeDtypeStruct((n_rows, d), BF16), jax.ShapeDtypeStruct((nt, SORT_ROWS, tm), BF16)],
        compiler_params=_params(("arbitrary",)),
        name="dispatch",
    )(big, small, cnt, tail_dst, tail_n16, x1b, rid, wdense, segr, c16r)


def _outranked(vals, n, ids):
    rank = jnp.zeros(vals.shape, jnp.int32)
    for j in range(n):
        row = vals[j:j + 1]
        rank = rank + ((row > vals) | ((row == vals) & (j < ids))).astype(jnp.int32)
    return rank


def _router_kernel(lg_ref, b_ref, tri_ref, low_ref, rid_ref, w_ref, tab_ref, carry):
    @pl.when(pl.program_id(0) == 0)
    def _():
        carry[...] = jnp.zeros_like(carry)

    tm = lg_ref.shape[2]
    gs = N_EXPERTS // N_GROUPS
    scores = 1.0 / (1.0 + jnp.exp(-lg_ref[0]))
    biased = scores + b_ref[...]
    member = lax.broadcasted_iota(jnp.int32, (gs, tm), 0)
    group_score = []
    for g in range(N_GROUPS):
        blk = biased[g * gs:(g + 1) * gs]
        m1 = jnp.max(blk, axis=0, keepdims=True)
        first = jnp.min(jnp.where(blk == m1, member, gs), axis=0, keepdims=True)
        m2 = jnp.max(jnp.where(member == first, -jnp.inf, blk), axis=0, keepdims=True)
        group_score.append(m1 + m2)
    group_score = jnp.concatenate(group_score, axis=0)
    gid = lax.broadcasted_iota(jnp.int32, (N_GROUPS, tm), 0)
    group_ok = _outranked(group_score, N_GROUPS, gid) < TOPK_GROUPS
    masked = jnp.concatenate([jnp.where(group_ok[g:g + 1], biased[g * gs:(g + 1) * gs], -jnp.inf)
                              for g in range(N_GROUPS)], axis=0)
    eid = lax.broadcasted_iota(jnp.int32, (N_EXPERTS, tm), 0)
    sel = _outranked(masked, N_EXPERTS, eid) < TOP_K
    self = jnp.where(sel, 1.0, 0.0)
    wsel = self * scores
    wts = wsel / jnp.sum(wsel, axis=0, keepdims=True) * ROUTED_SCALE
    selb = self.astype(BF16)
    before = _dot(selb, tri_ref[...])
    pieces = jnp.floor((jnp.sum(self, axis=1, keepdims=True) + (SEG_ROWS - 1.0)) * (1.0 / SEG_ROWS))
    seg = SEG_ROWS * _dot(low_ref[...], jnp.broadcast_to(pieces, (N_EXPERTS, LANES)).astype(BF16))
    rid_ref[0] = jnp.where(sel, seg[:, :1] + before + 1.0, NO_ROW)
    w_ref[0] = wts
    lane = lax.broadcasted_iota(jnp.int32, (N_EXPERTS, LANES), 1)
    tab_ref[0] = jnp.where(lane == 0, seg, jnp.where(lane == 1, SEG_ROWS * pieces, jnp.where(lane == 2, carry[...], 0.0)))
    carry[...] = carry[...] + SEG_ROWS * pieces


def _router(logits, router_b):
    nt, ne, tm = logits.shape
    tri = jnp.asarray(np.triu(np.ones((tm, tm), np.float32), 1), BF16)
    low = jnp.asarray(np.tril(np.ones((ne, ne), np.float32), -1), BF16)
    per_tile = lambda n: pl.BlockSpec((1, ne, n), lambda i: (i, 0, 0))
    return pl.pallas_call(
        _router_kernel,
        grid=(nt,),
        in_specs=[per_tile(tm), _const_spec((ne, 1)), _const_spec(tri.shape), _const_spec(low.shape)],
        out_specs=[per_tile(tm), per_tile(tm), per_tile(LANES)],
        out_shape=[jax.ShapeDtypeStruct((nt, ne, tm), F32), jax.ShapeDtypeStruct((nt, ne, tm), F32),
                   jax.ShapeDtypeStruct((nt, ne, LANES), F32)],
        scratch_shapes=[pltpu.VMEM((ne, 1), F32)],
        compiler_params=_params(("arbitrary",)),
        name="router",
    )(logits, router_b.reshape(ne, 1).astype(F32), tri, low)


def _moe(x1, x1b, logits, router_b, layer, wg, wu, wd, sg, su, sd, g, b):
    t, d = x1.shape
    nt, ne, tm = logits.shape
    if tm != ROUTE_TILE:
        split = tm // ROUTE_TILE
        logits = logits.reshape(nt, ne, split, ROUTE_TILE).transpose(0, 2, 1, 3).reshape(nt * split, ne, ROUTE_TILE)
        nt, tm = nt * split, ROUTE_TILE
    assert SORT_ROWS >= TOP_K * tm + ne * (SEG_ROWS - 1) and SORT_ROWS < NO_ROW
    rid, wdense, tab = _router(logits, router_b)
    tab = tab[:, :, :3].astype(jnp.int32)
    seg, rows, base = tab[:, :, 0], tab[:, :, 1], tab[:, :, 2]
    bm = EXPERT_TILE
    region = base[-1] + rows[-1]
    padded = (region + bm - 1) // bm * bm
    pad_end = jnp.cumsum(padded)
    dst = (pad_end - padded)[None, :] + base
    n_blocks = -(-(t * TOP_K + nt * ne * (SEG_ROWS - 1)) // bm) + ne
    blk_start = jnp.arange(n_blocks, dtype=jnp.int32) * bm
    blk_e = jnp.minimum(jnp.sum((pad_end[None, :] <= blk_start[:, None]).astype(jnp.int32), axis=1), ne - 1)
    n_used = (pad_end[-1] // bm).astype(jnp.int32).reshape(1)
    tail_dst = jnp.concatenate([pad_end - padded + region, pad_end[-1:]])
    tail_n16 = jnp.concatenate([padded - region, n_blocks * bm - pad_end[-1:]]) // SEG_ROWS
    big, small, cnt = _piece_tables(dst, rows // SEG_ROWS, seg)
    x_sorted, pw = _dispatch(big, small, cnt, tail_dst, tail_n16, x1b, rid, wdense, seg[:, None, :], rows[:, None, :],
                             n_blocks * bm)
    y_sorted = _experts(blk_e, n_used, x_sorted, wg, wu, wd, layer)
    return _ffn_out(big, small, cnt, x1, x1b, pw, y_sorted, sg, su, sd, g, b)


def _rope_tables(pos):
    half = MLA_ROPE // 2
    inv = ROPE_THETA ** (-jnp.arange(half, dtype=F32) / half)
    ang = pos.astype(F32)[:, None] * inv
    reps = LANES // half
    return jnp.tile(jnp.cos(ang), (1, reps)), jnp.tile(jnp.sin(ang), (1, reps))


def _pack_layer0_weights(w_in0, gla_w_g2, gla_b_g, mla_q_norm_g, mla_kv_norm_g, mla_w_uq, mla_w_uk, mla_w_uv):
    d = w_in0.shape[0]
    hk, hv = GLA_HEADS * GLA_DK, GLA_HEADS * GLA_DV
    o_q, o_k, o_v, o_r = 0, hk, 2 * hk, 2 * hk + hv
    o_a = o_r + hv
    o_cq = o_a + GLA_RANK
    o_ckv = o_cq + MLA_Q_LORA
    o_kr = o_ckv + MLA_KV_LORA
    half = MLA_ROPE // 2
    zeros = lambda n: jnp.zeros((d, n), w_in0.dtype)
    kr1, kr2 = w_in0[:, o_kr:o_kr + half], w_in0[:, o_kr + half:o_kr + MLA_ROPE]
    w_in = jnp.concatenate([
        w_in0[:, o_q:o_a], w_in0[:, o_cq:o_kr],
        kr1, kr2, zeros(LANES - MLA_ROPE),
        kr2, kr1, zeros(LANES - MLA_ROPE),
        w_in0[:, o_a:o_cq], zeros(LANES - GLA_RANK)], axis=1).astype(BF16)
    assert w_in.shape[1] == _C_END
    w_g2 = jnp.concatenate([gla_w_g2, jnp.zeros((LANES - GLA_RANK, hk), gla_w_g2.dtype)], axis=0)
    uq = mla_w_uq.reshape(MLA_Q_LORA, MLA_HEADS, MLA_NOPE + MLA_ROPE)
    w_uq = jnp.concatenate([uq[:, :, :MLA_NOPE].reshape(MLA_Q_LORA, -1),
                            uq[:, :, MLA_NOPE:MLA_NOPE + half].reshape(MLA_Q_LORA, -1),
                            uq[:, :, MLA_NOPE + half:].reshape(MLA_Q_LORA, -1)], axis=1).astype(BF16)
    eye = jnp.eye(MLA_HEADS, dtype=mla_w_uk.dtype)
    w_abs = jnp.einsum('chn,hg->hngc', mla_w_uk, eye).reshape(MLA_HEADS * MLA_NOPE, MLA_HEADS * MLA_KV_LORA).astype(BF16)
    w_uvbd = jnp.einsum('chv,hg->hcgv', mla_w_uv, eye).reshape(MLA_HEADS * MLA_KV_LORA, MLA_HEADS * MLA_V).astype(BF16)
    place = np.zeros((2 * LANES, MLA_HEADS * LANES), np.float32)
    for h in range(MLA_HEADS):
        for j in range(half):
            place[h * half + j, h * LANES + j] = 1.0
            place[LANES + h * half + j, h * LANES + half + j] = 1.0
    return dict(w_in=w_in, w_g2=w_g2, b_g=gla_b_g.reshape(1, hk), q_norm=mla_q_norm_g.reshape(1, -1),
                kv_norm=mla_kv_norm_g.reshape(1, -1), w_uq=w_uq, w_abs=w_abs,
                place=jnp.asarray(place, BF16)), w_uvbd


def _pad_rows(x, n):
    return jnp.pad(x, ((0, n - x.shape[0]),) + ((0, 0),) * (x.ndim - 1))


def _kv_tiles(past, own):
    kvb = MLA_KV_TILE
    pad = lambda a: jnp.pad(a, ((0, 0), (0, -a.shape[1] % kvb), (0, 0)))
    kv = jnp.concatenate([pad(past), pad(own)], axis=1) if past.shape[1] else pad(own)
    b, n, w = kv.shape
    kt = kv.reshape(b, n // kvb, kvb, w).transpose(0, 1, 3, 2)
    return kt, kv


def kernel(x_prompt, x_sample, cache_mla_ckv, cache_mla_krope, state_gla, cache_pool, meta_tokens, w_in0, gla_w_g2, gla_b_g, gla_norm_g, mla_q_norm_g, mla_kv_norm_g, mla_w_uq, mla_w_uk, mla_w_uv, w_out0, pool_w, pool_scale, ln_g, ln_b, moe_router_w, moe_router_b, moe_w_gate, moe_w_up, moe_w_down, moe_sh_gate, moe_sh_up, moe_sh_down):
    bp, sp, d = x_prompt.shape
    bs, ss, _ = x_sample.shape
    n_meta = meta_tokens.shape[0]
    past_len = cache_mla_ckv.shape[1] - n_meta
    tp, tsm = bp * sp, bs * ss
    assert sp % MLA_Q_TILE == 0 and sp % POOL_TILE == 0 and sp % GLA_BLOCK == 0 and sp % CHUNK == 0
    assert ss <= CHUNK and past_len % CHUNK == 0 and n_meta <= CHUNK and ss <= GLA_BLOCK and n_meta <= GLA_BLOCK
    assert n_meta == POOL_MAX and ss >= POOL_MAX and tp % TOK_TILE == 0
    t_all = -(-(tp + tsm + n_meta) // TOK_TILE) * TOK_TILE
    o_s, o_m = tp, tp + tsm

    x_all = _pad_rows(jnp.concatenate([x_prompt.reshape(tp, d), x_sample.reshape(tsm, d), meta_tokens], axis=0), t_all)
    pos = jnp.concatenate([jnp.tile(n_meta + jnp.arange(sp), bp), jnp.tile(n_meta + past_len + jnp.arange(ss), bs),
                           jnp.arange(n_meta), jnp.zeros((t_all - o_m - n_meta,), jnp.int32)])
    cos_t, sin_t = _rope_tables(pos)
    w0, w_uvbd = _pack_layer0_weights(w_in0, gla_w_g2, gla_b_g, mla_q_norm_g, mla_kv_norm_g, mla_w_uq, mla_w_uk, mla_w_uv)

    q, k, v, sr, lg, q_all, ckv, kr, k_all = _inproj(x_all, cos_t, sin_t, w0)

    tables = _gla_tables(GLA_BLOCK)
    g_gla = gla_norm_g.reshape(1, GLA_DV)
    c = GLA_BLOCK
    hk = GLA_HEADS * GLA_DK

    def small(a):
        sm = jnp.pad(a[o_s:o_m].reshape(bs, ss, -1), ((0, 0), (0, c - ss), (0, 0)))
        me = jnp.pad(a[o_m:o_m + n_meta], ((0, c - n_meta), (0, 0)))[None]
        return jnp.concatenate([sm, me], axis=0).reshape((bs + 1) * c, -1)

    s0_small = jnp.concatenate([state_gla.reshape(bs, hk, GLA_DV), jnp.zeros((1, hk, GLA_DV), F32)], axis=0)
    og_small, st_small = _gla(small(q), small(k), small(v), small(lg), small(sr), s0_small, g_gla, tables, bs + 1, 1)
    s0_prompt = jnp.broadcast_to(st_small[bs:], (bp, hk, GLA_DV))
    og_prompt, st_prompt = _gla(q, k, v, lg, sr, s0_prompt, g_gla, tables, bp, sp // c)
    og_small = og_small.reshape(bs + 1, c, -1)
    o_gla = _pad_rows(jnp.concatenate([og_prompt, og_small[:bs, :ss].reshape(tsm, -1), og_small[bs, :n_meta]], axis=0), t_all)

    k_meta = k_all[o_m:o_m + n_meta]
    kt_m, kv_m = _kv_tiles(jnp.zeros((1, 0, MLA_QK), BF16), k_meta[None])
    ol_meta = _mla(q_all[o_m:o_m + n_meta], kt_m, kv_m, 1, 1, n_meta, 0, 0, n_meta)
    cache = jnp.concatenate([cache_mla_ckv, cache_mla_krope,
                             jnp.zeros(cache_mla_ckv.shape[:2] + (MLA_QK - MLA_KV_LORA - MLA_ROPE,), F32)], axis=-1).astype(BF16)
    kt_s, kv_s = _kv_tiles(cache, k_all[o_s:o_m].reshape(bs, ss, MLA_QK))
    ol_sample = _mla(q_all[o_s:o_m], kt_s, kv_s, bs, 1, ss, 0, n_meta + past_len, ss)
    kt_p, kv_p = _kv_tiles(jnp.broadcast_to(k_meta[None], (bp, n_meta, MLA_QK)), k_all[:tp].reshape(bp, sp, MLA_QK))
    ol_prompt = _mla(q_all, kt_p, kv_p, bp, sp // MLA_Q_TILE, MLA_Q_TILE, 0, n_meta, sp)
    o_lat = _pad_rows(jnp.concatenate([ol_prompt, ol_sample, ol_meta], axis=0), t_all)

    ln = lambda l, j: (ln_g[l, j].reshape(1, d), ln_b[l, j].reshape(1, d))
    rw = lambda l: moe_router_w[l].T
    moe_w = lambda l: (l, moe_w_gate, moe_w_up, moe_w_down,
                       moe_sh_gate[l].astype(BF16), moe_sh_up[l].astype(BF16), moe_sh_down[l].astype(BF16))
    x1, x1b, logits = _outproj(o_gla, o_lat, x_all, w_uvbd, w_out0.astype(BF16), *ln(0, 0), rw(0))
    x2, _ = _moe(x1, x1b, logits, moe_router_b[0], *moe_w(0), *ln(0, 1))

    pm = POOL_MAX
    zrow = jnp.zeros((1, d), F32)
    hist_p = jnp.broadcast_to(jnp.concatenate([zrow, x2[o_m + n_meta - (pm - 1):o_m + n_meta]], axis=0)[None], (bp, pm, d))
    hist_s = jnp.concatenate([jnp.zeros((bs, 1, d), F32), cache_pool], axis=1)
    pool_wb = pool_w.astype(BF16)
    ps = pool_scale.reshape(1, d)
    t1 = -(-(tp + tsm) // TOK_TILE) * TOK_TILE
    y1p = _pool(x2, hist_p, pool_wb, ps, *ln(1, 0), rw(1), bp, sp // POOL_TILE, POOL_TILE, 0)
    y1s = _pool(x2[o_s:o_m], hist_s, pool_wb, ps, *ln(1, 0), rw(1), bs, 1, ss, 0)
    x3, x3b = [_pad_rows(jnp.concatenate([a, b_], axis=0), t1) for a, b_ in zip(y1p[:2], y1s[:2])]
    by_token = lambda lt: lt.transpose(1, 0, 2).reshape(N_EXPERTS, -1)
    logits1 = jnp.pad(jnp.concatenate([by_token(y1p[2]), by_token(y1s[2])], axis=1), ((0, 0), (0, t1 - tp - tsm)))
    logits1 = logits1.reshape(N_EXPERTS, t1 // TOK_TILE, TOK_TILE).transpose(1, 0, 2)
    x4, _ = _moe(x3, x3b, logits1, moe_router_b[1], *moe_w(1), *ln(1, 1))

    y_prompt = x4[:tp].reshape(bp, sp, d)
    y_sample = x4[o_s:o_m].reshape(bs, ss, d)
    ckv_meta, kr_meta = ckv[o_m:o_m + n_meta], kr[o_m:o_m + n_meta, :MLA_ROPE]
    p_ckv = jnp.concatenate([jnp.broadcast_to(ckv_meta[None], (bp, n_meta, MLA_KV_LORA)),
                             ckv[:tp].reshape(bp, sp, MLA_KV_LORA)], axis=1)
    p_kr = jnp.concatenate([jnp.broadcast_to(kr_meta[None], (bp, n_meta, MLA_ROPE)),
                            kr[:tp, :MLA_ROPE].reshape(bp, sp, MLA_ROPE)], axis=1)
    p_gla = st_prompt.reshape(bp, GLA_HEADS, GLA_DK, GLA_DV)
    x2p = x2[:tp].reshape(bp, sp, d)
    p_pool = x2p[:, sp - (pm - 1):]
    s_ckv = ckv[o_s:o_m].reshape(bs, ss, MLA_KV_LORA)
    s_kr = kr[o_s:o_m, :MLA_ROPE].reshape(bs, ss, MLA_ROPE)
    s_gla = st_small[:bs].reshape(bs, GLA_HEADS, GLA_DK, GLA_DV)
    s_pool = x2[o_s:o_m].reshape(bs, ss, d)[:, ss - (pm - 1):]
    return (y_prompt, y_sample, p_ckv, p_kr, p_gla, p_pool, s_ckv, s_kr, s_gla, s_pool)
```

```python
import functools
import math

import jax
import jax.numpy as jnp
import numpy as np
from jax import lax
from jax.experimental import pallas as pl
from jax.experimental.pallas import tpu as pltpu

F32 = jnp.float32
BF16 = jnp.bfloat16

CHUNK = 64
DEPTH = 2
ALPHA = (2 * DEPTH) ** 0.25
LN_EPS = 1e-5
RMS_EPS = 1e-6
NEG_INF = -1e30
GLA_HEADS = 4
GLA_DK = 64
GLA_DV = 128
GLA_RANK = 16
GLA_TAU = 16.0
MLA_HEADS = 8
MLA_Q_LORA = 256
MLA_KV_LORA = 128
MLA_NOPE = 64
MLA_ROPE = 32
MLA_V = 64
MLA_SCALE = (MLA_NOPE + MLA_ROPE) ** -0.5
ROPE_THETA = 10000.0
POOL_WINDOWS = (2, 4, 8, 16)
POOL_MAX = 16
N_EXPERTS = 64
TOP_K = 8
N_GROUPS = 8
TOPK_GROUPS = 4
ROUTED_SCALE = 2.5

LANES = 128
MXU_DIM = 256

TOK_TILE = 512
GLA_BLOCK = 128
MLA_Q_TILE = 256
MLA_KV_TILE = 256
MLA_QK = 256
MLA_ROW_STRIP = 128
LOG2E = math.log2(math.e)
EXPERT_TILE = 1024
EXPERT_SUBTILE = 256
POOL_TILE = 512
SEG_ROWS = 16
BIG_PIECE = 4
DRAIN_UNITS = (64, 16, 4, 1)
ROUTE_TILE = 512
SORT_ROWS = 5120
K_BIG = (TOP_K * ROUTE_TILE + N_EXPERTS * (SEG_ROWS - 1)) // (SEG_ROWS * BIG_PIECE)
K_SMALL = N_EXPERTS * (BIG_PIECE - 1)
PACK_BITS = 9
NO_ROW = 8191.0
DISPATCH_STRIP = 512
COMBINE_STRIP = 1024
VMEM_LIMIT = 56 * 1024 * 1024


def _params(sem, vmem=VMEM_LIMIT):
    return pltpu.CompilerParams(dimension_semantics=sem, vmem_limit_bytes=vmem)


def _const_spec(shape):
    nd = len(shape)
    return pl.BlockSpec(shape, lambda *_: (0,) * nd)


def _split3(x):
    hi = x.astype(BF16)
    r = x - hi.astype(F32)
    mid = r.astype(BF16)
    lo = (r - mid.astype(F32)).astype(BF16)
    return hi, mid, lo


def _dot(a, b):
    return jnp.dot(a, b, preferred_element_type=F32)


def _dot_f32(a, b):
    a_hi = a.astype(BF16)
    a_lo = (a - a_hi.astype(F32)).astype(BF16)
    b_hi = b.astype(BF16)
    b_lo = (b - b_hi.astype(F32)).astype(BF16)
    return _dot(a_hi, b_hi) + (_dot(a_lo, b_hi) + _dot(a_hi, b_lo))


def _dot_f32_nt(a, b):
    nt = lambda u, v: lax.dot_general(u, v, (((1,), (1,)), ((), ())), preferred_element_type=F32)
    a_hi = a.astype(BF16)
    a_lo = (a - a_hi.astype(F32)).astype(BF16)
    b_hi = b.astype(BF16)
    b_lo = (b - b_hi.astype(F32)).astype(BF16)
    return nt(a_hi, b_hi) + (nt(a_lo, b_hi) + nt(a_hi, b_lo))


def _silu(x):
    return x * (1.0 / (1.0 + jnp.exp(-x)))


def _layer_norm(x, g, b):
    mu = jnp.mean(x, axis=-1, keepdims=True)
    xc = x - mu
    var = jnp.mean(xc * xc, axis=-1, keepdims=True)
    return xc * lax.rsqrt(var + LN_EPS) * g + b


def _rms(x, g):
    return x * lax.rsqrt(jnp.mean(x * x, axis=-1, keepdims=True) + RMS_EPS) * g


_C_Q, _C_K, _C_V, _C_R, _C_CQ, _C_CKV, _C_KR, _C_KRS, _C_A, _C_END = (
    0, 256, 512, 1024, 1536, 1792, 1920, 2048, 2176, 2304)


def _two_part_specs(tm, n, n_main):
    return [pl.BlockSpec((tm, n), lambda i: (jnp.minimum(i, n_main - 1), 0)),
            pl.BlockSpec((tm, n), lambda i: (jnp.maximum(i - n_main, 0), 0))]


def _pick(main_ref, rest_ref, n_main):
    return jnp.where(pl.program_id(0) < n_main, main_ref[...], rest_ref[...])


def _inproj_kernel(xm_ref, xr_ref, cos_ref, sin_ref, w_ref, wg2_ref, bg_ref, qn_ref, kvn_ref, wuq_ref, wabs_ref,
                   place_ref, q_ref, k_ref, v_ref, sr_ref, lg_ref, qall_ref, ckv_ref, kr_ref, kall_ref, *, n_main):
    h = _dot(_pick(xm_ref, xr_ref, n_main).astype(BF16), w_ref[...])
    q_ref[...] = (h[:, _C_Q:_C_K] * GLA_DK ** -0.5).astype(BF16)
    k_ref[...] = h[:, _C_K:_C_V].astype(BF16)
    v_ref[...] = h[:, _C_V:_C_R].astype(BF16)
    sr_ref[...] = _silu(h[:, _C_R:_C_CQ]).astype(BF16)
    z = _dot_f32(h[:, _C_A:_C_END], wg2_ref[...]) + bg_ref[...]
    lg_ref[...] = (jnp.minimum(z, 0.0) - jnp.log(1.0 + jnp.exp(-jnp.abs(z)))) * (1.0 / GLA_TAU)
    cqn = _rms(h[:, _C_CQ:_C_CKV], qn_ref[...]).astype(BF16)
    qh = _dot(cqn, wuq_ref[...])
    n_nope = MLA_HEADS * MLA_NOPE
    cos, sin = cos_ref[...], sin_ref[...]
    x1, x2 = qh[:, n_nope:n_nope + LANES], qh[:, n_nope + LANES:]
    qscale = MLA_SCALE * LOG2E
    rot = (jnp.concatenate([x1 * cos - x2 * sin, x1 * sin + x2 * cos], axis=1) * qscale).astype(BF16)
    qlat = (_dot(qh[:, :n_nope].astype(BF16), wabs_ref[...]) * qscale).astype(BF16)
    qrope = _dot(rot, place_ref[...]).astype(BF16)
    for hd in range(MLA_HEADS):
        qall_ref[:, MLA_QK * hd:MLA_QK * hd + LANES] = qlat[:, LANES * hd:LANES * (hd + 1)]
        qall_ref[:, MLA_QK * hd + LANES:MLA_QK * (hd + 1)] = qrope[:, LANES * hd:LANES * (hd + 1)]
    ckv = _rms(h[:, _C_CKV:_C_KR], kvn_ref[...])
    lane = lax.broadcasted_iota(jnp.int32, (1, LANES), 1)
    sgn = jnp.where(lane < MLA_ROPE // 2, -1.0, 1.0)
    kr = h[:, _C_KR:_C_KRS] * cos + h[:, _C_KRS:_C_A] * (sin * sgn)
    ckv_ref[...] = ckv
    kr_ref[...] = kr
    kall_ref[:, :LANES] = ckv.astype(BF16)
    kall_ref[:, LANES:] = kr.astype(BF16)


def _inproj(x_main, x_rest, cos_t, sin_t, w):
    t = x_main.shape[0] + x_rest.shape[0]
    d = x_main.shape[1]
    tm = TOK_TILE
    n_main = x_main.shape[0] // tm
    row = lambda n: pl.BlockSpec((tm, n), lambda i: (i, 0))
    consts = [w['w_in'], w['w_g2'], w['b_g'], w['q_norm'], w['kv_norm'], w['w_uq'], w['w_abs'], w['place']]
    outs = [(GLA_HEADS * GLA_DK, BF16), (GLA_HEADS * GLA_DK, BF16), (GLA_HEADS * GLA_DV, BF16),
            (GLA_HEADS * GLA_DV, BF16), (GLA_HEADS * GLA_DK, F32), (MLA_HEADS * MLA_QK, BF16),
            (LANES, F32), (LANES, F32), (MLA_QK, BF16)]
    return pl.pallas_call(
        functools.partial(_inproj_kernel, n_main=n_main),
        grid=(t // tm,),
        in_specs=_two_part_specs(tm, d, n_main) + [row(LANES), row(LANES)] + [_const_spec(c.shape) for c in consts],
        out_specs=[row(n) for n, _ in outs],
        out_shape=[jax.ShapeDtypeStruct((t, n), dt) for n, dt in outs],
        compiler_params=_params(("parallel",)),
        name="inproj",
    )(x_main, x_rest, cos_t, sin_t, *consts)


def _gla_tables(c):
    levels = int(math.log2(c))
    assert 1 << levels == c
    t = np.arange(c)[:, None]
    u = np.arange(c)[None, :]
    mats, masks = [], []
    for l in range(levels):
        half = 1 << l
        base = (t // half) * half
        upper = ((t >> l) & 1) == 1
        a_q = (u >= base) & (u <= t)
        a_k = (u > t) & (u <= base + half - 1)
        mats.append(np.where(upper, a_q, a_k))
        same = (t >> (l + 1)) == (u >> (l + 1))
        masks.append(same & upper & (((u >> l) & 1) == 0))
    mats.append(u <= t)
    mats.append(u > t)
    masks.append(t == u)
    amat = np.concatenate(mats, axis=0).astype(np.float32)
    mask = np.stack([np.tile(m, (GLA_HEADS, 1)) for m in masks]).astype(np.float32)
    return jnp.asarray(amat, BF16), jnp.asarray(mask, F32), levels


def _gla_kernel(q_ref, k_ref, v_ref, lg_ref, sr_ref, s0_ref, amat_ref, mask_ref, g_ref,
                o_ref, sout_ref, state, *, levels):
    c = q_ref.shape[0]
    hk = GLA_HEADS * GLA_DK
    step = pl.program_id(1)

    @pl.when(step == 0)
    def _():
        state[...] = s0_ref[0]

    lg = lg_ref[...]
    parts = _split3(lg)
    e3 = _dot(amat_ref[...], jnp.concatenate(parts, axis=1))
    ex = jnp.exp(e3[:, :hk] + e3[:, hk:2 * hk] + e3[:, 2 * hk:])
    q = q_ref[...].astype(F32)
    k = k_ref[...].astype(F32)
    v = v_ref[...]
    head = lax.broadcasted_iota(jnp.int32, (1, hk), 1) // GLA_DK

    def per_head(xf):
        return jnp.concatenate([jnp.where(head == h, xf, 0.0) for h in range(GLA_HEADS)], axis=0).astype(BF16)

    att = jnp.zeros((GLA_HEADS * c, c), F32)
    for l in range(levels + 1):
        if l < levels:
            el = ex[l * c:(l + 1) * c]
            ql, kl = q * el, k * el
        else:
            ql, kl = q, k
        a = lax.dot_general(per_head(ql), kl.astype(BF16), (((1,), (1,)), ((), ())), preferred_element_type=F32)
        att = att + a * mask_ref[l]
    s_prev = state[...]
    o_inter = _dot(per_head(q * ex[levels * c:(levels + 1) * c]), s_prev.astype(BF16))
    att = att.astype(BF16)
    g = g_ref[...]
    for h in range(GLA_HEADS):
        o = o_inter[h * c:(h + 1) * c] + _dot(att[h * c:(h + 1) * c], v[:, GLA_DV * h:GLA_DV * (h + 1)])
        o = _rms(o, g)
        o_ref[:, GLA_DV * h:GLA_DV * (h + 1)] = (o * sr_ref[:, GLA_DV * h:GLA_DV * (h + 1)].astype(F32)).astype(BF16)
    kr = (k * ex[(levels + 1) * c:]).astype(BF16)
    upd = lax.dot_general(kr, v, (((0,), (0,)), ((), ())), preferred_element_type=F32)
    ones = jnp.ones((c, GLA_DV), BF16)
    b_last = sum(lax.dot_general(p, ones, (((0,), (0,)), ((), ())), preferred_element_type=F32) for p in parts)
    new = jnp.exp(b_last) * s_prev + jnp.concatenate(
        [upd[GLA_DK * h:GLA_DK * (h + 1), GLA_DV * h:GLA_DV * (h + 1)] for h in range(GLA_HEADS)], axis=0)
    state[...] = new

    @pl.when(step == pl.num_programs(1) - 1)
    def _():
        sout_ref[0] = new


def _gla(q, k, v, lg, sr, s0, g, tables, batch, blocks_per_seq, out_rows=None):
    amat, mask, levels = tables
    c = GLA_BLOCK
    hk, hv = GLA_HEADS * GLA_DK, GLA_HEADS * GLA_DV
    row = lambda n: pl.BlockSpec((c, n), lambda b, s: (b * blocks_per_seq + s, 0))
    st = pl.BlockSpec((1, hk, GLA_DV), lambda b, s: (b, 0, 0))
    return pl.pallas_call(
        functools.partial(_gla_kernel, levels=levels),
        grid=(batch, blocks_per_seq),
        in_specs=[row(hk), row(hk), row(hv), row(hk), row(hv), st,
                  _const_spec(amat.shape), _const_spec(mask.shape), _const_spec(g.shape)],
        out_specs=[row(hv), st],
        out_shape=[jax.ShapeDtypeStruct((out_rows or batch * blocks_per_seq * c, hv), BF16),
                   jax.ShapeDtypeStruct((batch, hk, GLA_DV), F32)],
        scratch_shapes=[pltpu.VMEM((hk, GLA_DV), F32)],
        compiler_params=_params(("parallel", "arbitrary")),
        name="gla",
    )(q, k, v, lg, sr, s0, amat, mask, g)


def _mla_kernel(q_ref, kt_ref, kv_ref, o_ref, s_sc, p_sc, linv_sc, acc_sc, *, tq, n_past, past_valid, own_valid,
                diag_chunks):
    kvb = MLA_KV_TILE
    qi = pl.program_id(1)
    qs = jnp.concatenate([q_ref[:, MLA_QK * h:MLA_QK * (h + 1)] for h in range(MLA_HEADS)], axis=0)
    m_rows = MLA_HEADS * tq
    n_vis = n_past + qi + 1

    def score(j, mask):
        s = _dot(qs, kt_ref[0, j])
        s_sc[j] = s if mask is None else jnp.where(mask, s, NEG_INF)

    def scores(lo, hi):
        def body(j, carry):
            score(j, None)
            return carry
        lax.fori_loop(lo, hi, body, 0)

    col = lax.broadcasted_iota(jnp.int32, (m_rows, kvb), 1)
    n_past_full = past_valid // kvb
    if n_past_full:
        scores(0, n_past_full)
    if n_past_full < n_past:
        score(n_past_full, col < past_valid - n_past_full * kvb)
    scores(n_past, n_past + qi)
    mask = col < own_valid - qi * kvb
    if diag_chunks:
        qrow = lax.broadcasted_iota(jnp.int32, (m_rows, kvb), 0) % tq
        mask = mask & (col // CHUNK <= qrow // CHUNK)
    score(n_past + qi, mask)

    rc = MLA_ROW_STRIP

    def strip(r, carry):
        rows = pl.ds(pl.multiple_of(r * rc, rc), rc)
        mx = lax.fori_loop(0, n_vis, lambda j, m: jnp.maximum(m, s_sc[j, rows, :]),
                           jnp.full((rc, kvb), -jnp.inf, F32))
        m = jnp.max(mx, axis=1, keepdims=True)

        def body(j, l):
            p = jnp.exp2(s_sc[j, rows, :] - m)
            p_sc[j, rows, :] = p.astype(BF16)
            return l + p
        l = lax.fori_loop(0, n_vis, body, jnp.zeros((rc, kvb), F32))
        linv_sc[rows, :] = 1.0 / jnp.sum(l, axis=1, keepdims=True)
        return carry
    lax.fori_loop(0, m_rows // rc, strip, 0)

    acc_sc[...] = jnp.zeros(acc_sc.shape, F32)

    def pv(j, carry):
        vblk = kv_ref[0, pl.ds(pl.multiple_of(j * kvb, kvb), kvb), :MLA_KV_LORA]
        acc_sc[...] += _dot(p_sc[j], vblk)
        return carry
    lax.fori_loop(0, n_vis, pv, 0)
    o = acc_sc[...] * linv_sc[...]
    for h in range(MLA_HEADS):
        o_ref[:, MLA_KV_LORA * h:MLA_KV_LORA * (h + 1)] = o[h * tq:(h + 1) * tq].astype(BF16)


def _mla(q_all, kt, kv, batch, nq, tq, q_block0, past_valid, own_valid, out_rows=None):
    n_kv = kt.shape[1]
    n_past = -(-past_valid // MLA_KV_TILE)
    assert nq == 1 or tq == MLA_KV_TILE
    assert n_kv == n_past + nq
    return pl.pallas_call(
        functools.partial(_mla_kernel, tq=tq, n_past=n_past, past_valid=past_valid, own_valid=own_valid,
                          diag_chunks=nq > 1),
        grid=(batch, nq),
        in_specs=[pl.BlockSpec((tq, MLA_HEADS * MLA_QK), lambda b, i: (q_block0 + b * nq + i, 0)),
                  pl.BlockSpec((1,) + kt.shape[1:], lambda b, i: (b, 0, 0, 0)),
                  pl.BlockSpec((1,) + kv.shape[1:], lambda b, i: (b, 0, 0))],
        out_specs=pl.BlockSpec((tq, MLA_HEADS * MLA_KV_LORA), lambda b, i: (b * nq + i, 0)),
        out_shape=jax.ShapeDtypeStruct((out_rows or batch * nq * tq, MLA_HEADS * MLA_KV_LORA), BF16),
        scratch_shapes=[pltpu.VMEM((n_kv, MLA_HEADS * tq, MLA_KV_TILE), F32),
                        pltpu.VMEM((n_kv, MLA_HEADS * tq, MLA_KV_TILE), BF16),
                        pltpu.VMEM((MLA_HEADS * tq, 1), F32),
                        pltpu.VMEM((MLA_HEADS * tq, MLA_KV_LORA), F32)],
        compiler_params=_params(("parallel", "arbitrary")),
        name="mla",
    )(q_all, kt, kv)


def _post_mixer(x, out, g, b, rw_ref, x1_ref, x1b_ref, lgt_ref):
    y = _layer_norm(ALPHA * x + out, g, b)
    x1_ref[...] = y
    x1b_ref[...] = y.astype(BF16)
    lgt_ref[0] = _dot_f32_nt(rw_ref[...], y)


def _outproj_kernel(ogm_ref, ogr_ref, olm_ref, olr_ref, xm_ref, xr_ref, wuv_ref, wo_ref, g_ref, b_ref, rw_ref,
                    x1_ref, x1b_ref, lgt_ref, *, n_main):
    og = _pick(ogm_ref, ogr_ref, n_main)
    o_mla = _dot(_pick(olm_ref, olr_ref, n_main), wuv_ref[...]).astype(BF16)
    n_gla = og.shape[1]
    out = _dot(og, wo_ref[:n_gla, :]) + _dot(o_mla, wo_ref[n_gla:, :])
    _post_mixer(_pick(xm_ref, xr_ref, n_main), out, g_ref[...], b_ref[...], rw_ref, x1_ref, x1b_ref, lgt_ref)


def _outproj(o_gla, o_lat, x, w_uvbd, w_out, g, b, rw):
    t, d = x[0].shape[0] + x[1].shape[0], x[0].shape[1]
    tm = TOK_TILE
    n_main = x[0].shape[0] // tm
    row = lambda n: pl.BlockSpec((tm, n), lambda i: (i, 0))
    consts = [w_uvbd, w_out, g, b, rw]
    parts = [*o_gla, *o_lat, *x]
    return pl.pallas_call(
        functools.partial(_outproj_kernel, n_main=n_main),
        grid=(t // tm,),
        in_specs=[s for a in (o_gla, o_lat, x) for s in _two_part_specs(tm, a[0].shape[1], n_main)]
        + [_const_spec(c.shape) for c in consts],
        out_specs=[row(d), row(d), pl.BlockSpec((1, N_EXPERTS, tm), lambda i: (i, 0, 0))],
        out_shape=[jax.ShapeDtypeStruct((t, d), F32), jax.ShapeDtypeStruct((t, d), BF16),
                   jax.ShapeDtypeStruct((t // tm, N_EXPERTS, tm), F32)],
        compiler_params=_params(("parallel",)),
        name="outproj",
    )(*parts, *consts)


def _pool_kernel(x_ref, prev_ref, hist_ref, pw_ref, ps_ref, g_ref, b_ref, rw_ref, x1_ref, x1b_ref, lgt_ref):
    ts = x_ref.shape[0]
    pm = POOL_MAX
    x = x_ref[...]
    prev = jnp.where(pl.program_id(1) == 0, hist_ref[0], prev_ref[...])
    xc = jnp.concatenate([prev, x], axis=0)
    grp = x.shape[1] // len(POOL_WINDOWS)
    outs = []
    for gi, w in enumerate(POOL_WINDOWS):
        s = xc[:, gi * grp:(gi + 1) * grp]
        span = 1
        while span < w:
            s = s + pltpu.roll(s, span, 0)
            span *= 2
        win = s[pm:]
        mix = (win * (1.0 / w) - x[:, gi * grp:(gi + 1) * grp]).astype(BF16)
        outs.append(_dot(mix, pw_ref[gi]))
    out = jnp.concatenate(outs, axis=1) * ps_ref[...]
    _post_mixer(x, out, g_ref[...], b_ref[...], rw_ref, x1_ref, x1b_ref, lgt_ref)


def _pool(x_all, hist, pool_w, pool_scale, g, b, rw, batch, tiles_per_seq, ts, tile0, out_rows=None):
    d = x_all.shape[1]
    pm = POOL_MAX
    per = ts // pm
    consts = [pool_w, pool_scale, g, b, rw]
    row = lambda n: pl.BlockSpec((ts, n), lambda bb, i: (bb * tiles_per_seq + i, 0))
    return pl.pallas_call(
        _pool_kernel,
        grid=(batch, tiles_per_seq),
        in_specs=[pl.BlockSpec((ts, d), lambda bb, i: (tile0 + bb * tiles_per_seq + i, 0)),
                  pl.BlockSpec((pm, d), lambda bb, i: (jnp.maximum((tile0 + bb * tiles_per_seq + i) * per - 1, 0), 0)),
                  pl.BlockSpec((1, pm, d), lambda bb, i: (bb, 0, 0))] + [_const_spec(c.shape) for c in consts],
        out_specs=[row(d), row(d), pl.BlockSpec((1, N_EXPERTS, ts), lambda bb, i: (bb * tiles_per_seq + i, 0, 0))],
        out_shape=[jax.ShapeDtypeStruct((out_rows or batch * tiles_per_seq * ts, d), F32),
                   jax.ShapeDtypeStruct((out_rows or batch * tiles_per_seq * ts, d), BF16),
                   jax.ShapeDtypeStruct((batch * tiles_per_seq, N_EXPERTS, ts), F32)],
        compiler_params=_params(("parallel", "arbitrary")),
        name="pool",
    )(x_all, x_all, hist, *consts)


def _experts_kernel(be_ref, nb_ref, x_ref, wg_ref, wu_ref, wd_ref, y_ref):
    @pl.when(pl.program_id(0) < nb_ref[0])
    def _():
        wg, wu, wd = wg_ref[0, 0].astype(BF16), wu_ref[0, 0].astype(BF16), wd_ref[0, 0].astype(BF16)
        sub = EXPERT_SUBTILE
        for r in range(x_ref.shape[0] // sub):
            x = x_ref[r * sub:(r + 1) * sub, :]
            hmid = (_silu(_dot(x, wg)) * _dot(x, wu)).astype(BF16)
            y_ref[r * sub:(r + 1) * sub, :] = _dot(hmid, wd).astype(BF16)

    @pl.when(pl.program_id(0) >= nb_ref[0])
    def _():
        y_ref[...] = jnp.zeros_like(y_ref)


def _experts(blk_e, n_used, x_sorted, wg, wu, wd, layer):
    p, d = x_sorted.shape
    f = wg.shape[3]
    bm = EXPERT_TILE
    blk = lambda i, be, nb: (i, 0)
    grid_spec = pltpu.PrefetchScalarGridSpec(
        num_scalar_prefetch=2,
        grid=(p // bm,),
        in_specs=[pl.BlockSpec((bm, d), lambda i, be, nb: (jnp.minimum(i, nb[0] - 1), 0)),
                  pl.BlockSpec((1, 1, d, f), lambda i, be, nb: (layer, be[i], 0, 0)),
                  pl.BlockSpec((1, 1, d, f), lambda i, be, nb: (layer, be[i], 0, 0)),
                  pl.BlockSpec((1, 1, f, d), lambda i, be, nb: (layer, be[i], 0, 0))],
        out_specs=pl.BlockSpec((bm, d), blk),
    )
    return pl.pallas_call(
        _experts_kernel,
        grid_spec=grid_spec,
        out_shape=jax.ShapeDtypeStruct((p, d), BF16),
        compiler_params=_params(("arbitrary",)),
        name="experts",
    )(blk_e, n_used, x_sorted, wg, wu, wd)


def _segment_copies(tile, slot, dst_ref, n16_ref, seg_ref, make, entries=N_EXPERTS):
    big = SEG_ROWS * BIG_PIECE

    def per_expert(e, carry):
        j = tile * N_EXPERTS + e
        d0, s0, n = dst_ref[j], seg_ref[j], n16_ref[j]
        n_big = n // BIG_PIECE

        def big_piece(c, carry2):
            make(pl.multiple_of(d0 + c * big, SEG_ROWS), pl.multiple_of(s0 + c * big, SEG_ROWS), big, slot).start()
            return carry2
        lax.fori_loop(0, n_big, big_piece, 0)
        d1, s1 = d0 + n_big * big, s0 + n_big * big

        def small_piece(c, carry2):
            make(pl.multiple_of(d1 + c * SEG_ROWS, SEG_ROWS), pl.multiple_of(s1 + c * SEG_ROWS, SEG_ROWS),
                 SEG_ROWS, slot).start()
            return carry2
        lax.fori_loop(0, n - n_big * BIG_PIECE, small_piece, 0)
        return carry
    lax.fori_loop(0, entries, per_expert, 0)


def _start_pieces(tile, slot, big_ref, small_ref, cnt_ref, make):
    def run(tab_ref, kmax, n, rows):
        def body(k, carry):
            v = tab_ref[tile * kmax + k]
            make(pl.multiple_of((v >> PACK_BITS) * SEG_ROWS, SEG_ROWS),
                 pl.multiple_of((v & ((1 << PACK_BITS) - 1)) * SEG_ROWS, SEG_ROWS), rows, slot).start()
            return carry
        lax.fori_loop(0, n, body, 0)
    run(big_ref, K_BIG, cnt_ref[2 * tile], SEG_ROWS * BIG_PIECE)
    run(small_ref, K_SMALL, cnt_ref[2 * tile + 1], SEG_ROWS)


def _tile_rows(tile, cnt_ref):
    return cnt_ref[2 * tile] * (SEG_ROWS * BIG_PIECE) + cnt_ref[2 * tile + 1] * SEG_ROWS


def _piece_tables(dst, n16, seg):
    nt, ne = n16.shape
    big_rows = SEG_ROWS * BIG_PIECE
    n_big, n_small = n16 // BIG_PIECE, n16 % BIG_PIECE

    def flat(counts, first_dst, first_src, step, kmax):
        cum = jnp.cumsum(counts, axis=1)[:, None, :]
        before = cum - counts[:, None, :]
        k = jnp.arange(kmax, dtype=jnp.int32)[None, :, None]
        mine = (before <= k) & (k < cum)
        off = step * (k - before)
        d = jnp.sum(jnp.where(mine, first_dst[:, None, :] + off, 0), axis=2)
        s = jnp.sum(jnp.where(mine, first_src[:, None, :] + off, 0), axis=2)
        return (((d // SEG_ROWS) << PACK_BITS) | (s // SEG_ROWS)).reshape(-1).astype(jnp.int32), cum[:, 0, -1]
    big, cnt_big = flat(n_big, dst, seg, big_rows, K_BIG)
    small, cnt_small = flat(n_small, dst + n_big * big_rows, seg + n_big * big_rows, SEG_ROWS, K_SMALL)
    return big, small, jnp.stack([cnt_big, cnt_small], axis=1).reshape(-1).astype(jnp.int32)


def _drain(rows, slot, make):
    left = rows // SEG_ROWS
    for unit in DRAIN_UNITS:
        n = left // unit

        def body(c, carry, unit=unit):
            make(0, 0, unit * SEG_ROWS, slot).wait()
            return carry
        lax.fori_loop(0, n, body, 0)
        left = left - n * unit


def _ffn_out_kernel(big_ref, small_ref, cnt_ref, x_ref, xb_ref, pw_ref, ys_ref,
                    sg_ref, su_ref, sd_ref, g_ref, b_ref, y_ref, yb_ref, ybuf, racc, sem, *, tile0):
    i = pl.program_id(0)
    nt = pl.num_programs(0)
    slot = i % 2
    tile = i + tile0

    def make(d0, s0, rows, sl):
        return pltpu.make_async_copy(ys_ref.at[pl.ds(d0, rows)], ybuf.at[sl, pl.ds(s0, rows)], sem.at[sl])

    @pl.when(i == 0)
    def _():
        ybuf[...] = jnp.zeros_like(ybuf)
        _start_pieces(tile, 0, big_ref, small_ref, cnt_ref, make)

    @pl.when(i + 1 < nt)
    def _():
        _start_pieces(tile + 1, 1 - slot, big_ref, small_ref, cnt_ref, make)

    xb = xb_ref[...]
    hmid = (_silu(_dot(xb, sg_ref[...])) * _dot(xb, su_ref[...])).astype(BF16)
    shared = _dot(hmid, sd_ref[...])
    used = _tile_rows(tile, cnt_ref)
    _drain(used, slot, make)

    racc[...] = jnp.zeros_like(racc)
    rs = COMBINE_STRIP
    for s in range(ybuf.shape[1] // rs):
        @pl.when(s * rs < used)
        def _():
            racc[...] += lax.dot_general(pw_ref[0, s * rs:(s + 1) * rs, :], ybuf[slot, s * rs:(s + 1) * rs, :],
                                         (((0,), (0,)), ((), ())), preferred_element_type=F32)
    ffn = racc[...] + shared
    y = _layer_norm(ALPHA * x_ref[...] + ffn, g_ref[...], b_ref[...])
    y_ref[...] = y
    yb_ref[...] = y.astype(BF16)


def _ffn_out(big, small, cnt, x1, x1b, pw, y_sorted, sg, su, sd, g, b, tile0=0, n_tiles=None):
    d = x1.shape[1]
    nt, _, tm = pw.shape
    n_tiles = nt - tile0 if n_tiles is None else n_tiles
    row_in = pl.BlockSpec((tm, d), lambda i, *_: (i + tile0, 0))
    row_out = pl.BlockSpec((tm, d), lambda i, *_: (i, 0))
    consts = [sg, su, sd, g, b]
    grid_spec = pltpu.PrefetchScalarGridSpec(
        num_scalar_prefetch=3,
        grid=(n_tiles,),
        in_specs=[row_in, row_in, pl.BlockSpec((1,) + pw.shape[1:], lambda i, *_: (i + tile0, 0, 0)),
                  pl.BlockSpec(memory_space=pl.ANY)] + [_const_spec(c.shape) for c in consts],
        out_specs=[row_out, row_out],
        scratch_shapes=[pltpu.VMEM((2, SORT_ROWS, d), BF16), pltpu.VMEM((tm, d), F32), pltpu.SemaphoreType.DMA((2,))],
    )
    return pl.pallas_call(
        functools.partial(_ffn_out_kernel, tile0=tile0),
        grid_spec=grid_spec,
        out_shape=[jax.ShapeDtypeStruct((n_tiles * tm, d), F32), jax.ShapeDtypeStruct((n_tiles * tm, d), BF16)],
        compiler_params=_params(("arbitrary",)),
        name="ffn_out",
    )(big, small, cnt, x1, x1b, pw, y_sorted, *consts)


def _dispatch_kernel(big_ref, small_ref, cnt_ref, tdst_ref, tn16_ref, x_ref, rid_ref, w_ref, segr_ref, c16r_ref,
                     xs_ref, pw_ref, buf, zbuf, sem):
    i = pl.program_id(0)
    nt = pl.num_programs(0)
    slot = i % 2
    tm = x_ref.shape[0]

    def make(d0, s0, rows, sl):
        return pltpu.make_async_copy(buf.at[sl, pl.ds(s0, rows)], xs_ref.at[pl.ds(d0, rows)], sem.at[sl])

    @pl.when(i >= 2)
    def _():
        _drain(_tile_rows(i - 2, cnt_ref), slot, make)

    x = x_ref[...]
    rid = rid_ref[0]
    wb = w_ref[0].astype(BF16)
    used = _tile_rows(i, cnt_ref)
    rs = DISPATCH_STRIP
    hs = MXU_DIM
    for s in range(buf.shape[1] // rs):
        @pl.when(s * rs < used)
        def _():
            hits = []
            for h0 in range(s * rs, (s + 1) * rs, hs):
                r = lax.broadcasted_iota(jnp.int32, (hs, N_EXPERTS), 0) + h0
                owner = jnp.where((r >= segr_ref[0]) & (r < segr_ref[0] + c16r_ref[0]), 1.0, 0.0).astype(BF16)
                local = jnp.where((rid > h0) & (rid <= h0 + hs), rid - h0, 0.0).astype(BF16)
                at_row = _dot(owner, jnp.concatenate([local, wb], axis=1))
                hit = at_row[:, :tm] == (lax.broadcasted_iota(jnp.int32, (hs, tm), 0) + 1).astype(F32)
                hits.append(jnp.where(hit, 1.0, 0.0).astype(BF16))
                pw_ref[0, h0:h0 + hs, :] = jnp.where(hit, at_row[:, tm:], 0.0).astype(BF16)
            buf[slot, s * rs:(s + 1) * rs, :] = _dot(jnp.concatenate(hits, axis=0), x).astype(BF16)

        @pl.when(s * rs >= used)
        def _():
            pw_ref[0, s * rs:(s + 1) * rs, :] = jnp.zeros((rs, tm), BF16)
    _start_pieces(i, slot, big_ref, small_ref, cnt_ref, make)

    @pl.when(i == nt - 1)
    def _():
        zbuf[...] = jnp.zeros_like(zbuf)

        def zmake(d0, s0, rows, sl):
            return pltpu.make_async_copy(zbuf.at[pl.ds(0, rows)], xs_ref.at[pl.ds(d0, rows)], sem.at[sl])
        _segment_copies(0, 2, tdst_ref, tn16_ref, tdst_ref, zmake, entries=N_EXPERTS + 1)
        tail = lax.fori_loop(0, N_EXPERTS + 1, lambda e, acc: acc + tn16_ref[e], 0) * SEG_ROWS
        _drain(tail, 2, make)

        @pl.when(i >= 1)
        def _():
            _drain(_tile_rows(i - 1, cnt_ref), 1 - slot, make)
        _drain(used, slot, make)


def _dispatch(big, small, cnt, tail_dst, tail_n16, x1b, rid, wdense, segr, c16r, n_rows):
    t, d = x1b.shape
    nt, ne, tm = rid.shape
    per_tile = lambda a: pl.BlockSpec((1,) + a.shape[1:], lambda i, *_: (i, 0, 0))
    grid_spec = pltpu.PrefetchScalarGridSpec(
        num_scalar_prefetch=5,
        grid=(nt,),
        in_specs=[pl.BlockSpec((tm, d), lambda i, *_: (i, 0)), per_tile(rid), per_tile(wdense), per_tile(segr),
                  per_tile(c16r)],
        out_specs=[pl.BlockSpec(memory_space=pl.ANY), pl.BlockSpec((1, SORT_ROWS, tm), lambda i, *_: (i, 0, 0))],
        scratch_shapes=[pltpu.VMEM((2, SORT_ROWS, d), BF16), pltpu.VMEM((SEG_ROWS * BIG_PIECE, d), BF16),
                        pltpu.SemaphoreType.DMA((3,))],
    )
    return pl.pallas_call(
        _dispatch_kernel,
        grid_spec=grid_spec,
        out_shape=[jax.ShapeDtypeStruct((n_rows, d), BF16), jax.ShapeDtypeStruct((nt, SORT_ROWS, tm), BF16)],
        compiler_params=_params(("arbitrary",)),
        name="dispatch",
    )(big, small, cnt, tail_dst, tail_n16, x1b, rid, wdense, segr, c16r)


def _outranked(vals, n, ids):
    rank = jnp.zeros(vals.shape, jnp.int32)
    for j in range(n):
        row = vals[j:j + 1]
        rank = rank + ((row > vals) | ((row == vals) & (j < ids))).astype(jnp.int32)
    return rank


def _router_kernel(lg_ref, b_ref, tri_ref, low_ref, rid_ref, w_ref, tab_ref, carry):
    @pl.when(pl.program_id(0) == 0)
    def _():
        carry[...] = jnp.zeros_like(carry)

    tm = lg_ref.shape[2]
    gs = N_EXPERTS // N_GROUPS
    scores = 1.0 / (1.0 + jnp.exp(-lg_ref[0]))
    biased = scores + b_ref[...]
    member = lax.broadcasted_iota(jnp.int32, (gs, tm), 0)
    group_score = []
    for g in range(N_GROUPS):
        blk = biased[g * gs:(g + 1) * gs]
        m1 = jnp.max(blk, axis=0, keepdims=True)
        first = jnp.min(jnp.where(blk == m1, member, gs), axis=0, keepdims=True)
        m2 = jnp.max(jnp.where(member == first, -jnp.inf, blk), axis=0, keepdims=True)
        group_score.append(m1 + m2)
    group_score = jnp.concatenate(group_score, axis=0)
    gid = lax.broadcasted_iota(jnp.int32, (N_GROUPS, tm), 0)
    group_ok = _outranked(group_score, N_GROUPS, gid) < TOPK_GROUPS
    masked = jnp.concatenate([jnp.where(group_ok[g:g + 1], biased[g * gs:(g + 1) * gs], -jnp.inf)
                              for g in range(N_GROUPS)], axis=0)
    eid = lax.broadcasted_iota(jnp.int32, (N_EXPERTS, tm), 0)
    sel = _outranked(masked, N_EXPERTS, eid) < TOP_K
    self = jnp.where(sel, 1.0, 0.0)
    wsel = self * scores
    wts = wsel / jnp.sum(wsel, axis=0, keepdims=True) * ROUTED_SCALE
    selb = self.astype(BF16)
    before = _dot(selb, tri_ref[...])
    pieces = jnp.floor((jnp.sum(self, axis=1, keepdims=True) + (SEG_ROWS - 1.0)) * (1.0 / SEG_ROWS))
    seg = SEG_ROWS * _dot(low_ref[...], jnp.broadcast_to(pieces, (N_EXPERTS, LANES)).astype(BF16))
    rid_ref[0] = jnp.where(sel, seg[:, :1] + before + 1.0, NO_ROW)
    w_ref[0] = wts
    lane = lax.broadcasted_iota(jnp.int32, (N_EXPERTS, LANES), 1)
    tab_ref[0] = jnp.where(lane == 0, seg, jnp.where(lane == 1, SEG_ROWS * pieces, jnp.where(lane == 2, carry[...], 0.0)))
    carry[...] = carry[...] + SEG_ROWS * pieces


def _router(logits, router_b):
    nt, ne, tm = logits.shape
    tri = jnp.asarray(np.triu(np.ones((tm, tm), np.float32), 1), BF16)
    low = jnp.asarray(np.tril(np.ones((ne, ne), np.float32), -1), BF16)
    per_tile = lambda n: pl.BlockSpec((1, ne, n), lambda i: (i, 0, 0))
    return pl.pallas_call(
        _router_kernel,
        grid=(nt,),
        in_specs=[per_tile(tm), _const_spec((ne, 1)), _const_spec(tri.shape), _const_spec(low.shape)],
        out_specs=[per_tile(tm), per_tile(tm), per_tile(LANES)],
        out_shape=[jax.ShapeDtypeStruct((nt, ne, tm), F32), jax.ShapeDtypeStruct((nt, ne, tm), F32),
                   jax.ShapeDtypeStruct((nt, ne, LANES), F32)],
        scratch_shapes=[pltpu.VMEM((ne, 1), F32)],
        compiler_params=_params(("arbitrary",)),
        name="router",
    )(logits, router_b.reshape(ne, 1).astype(F32), tri, low)


def _moe(x1, x1b, logits, router_b, layer, wg, wu, wd, sg, su, sd, g, b, split_rows=None):
    t, d = x1.shape
    nt, ne, tm = logits.shape
    if tm != ROUTE_TILE:
        split = tm // ROUTE_TILE
        logits = logits.reshape(nt, ne, split, ROUTE_TILE).transpose(0, 2, 1, 3).reshape(nt * split, ne, ROUTE_TILE)
        nt, tm = nt * split, ROUTE_TILE
    assert SORT_ROWS >= TOP_K * tm + ne * (SEG_ROWS - 1) and SORT_ROWS < NO_ROW
    rid, wdense, tab = _router(logits, router_b)
    tab = tab[:, :, :3].astype(jnp.int32)
    seg, rows, base = tab[:, :, 0], tab[:, :, 1], tab[:, :, 2]
    bm = EXPERT_TILE
    region = base[-1] + rows[-1]
    padded = (region + bm - 1) // bm * bm
    pad_end = jnp.cumsum(padded)
    dst = (pad_end - padded)[None, :] + base
    n_blocks = -(-(t * TOP_K + nt * ne * (SEG_ROWS - 1)) // bm) + ne
    blk_start = jnp.arange(n_blocks, dtype=jnp.int32) * bm
    blk_e = jnp.minimum(jnp.sum((pad_end[None, :] <= blk_start[:, None]).astype(jnp.int32), axis=1), ne - 1)
    n_used = (pad_end[-1] // bm).astype(jnp.int32).reshape(1)
    tail_dst = jnp.concatenate([pad_end - padded + region, pad_end[-1:]])
    tail_n16 = jnp.concatenate([padded - region, n_blocks * bm - pad_end[-1:]]) // SEG_ROWS
    big, small, cnt = _piece_tables(dst, rows // SEG_ROWS, seg)
    x_sorted, pw = _dispatch(big, small, cnt, tail_dst, tail_n16, x1b, rid, wdense, seg[:, None, :], rows[:, None, :],
                             n_blocks * bm)
    y_sorted = _experts(blk_e, n_used, x_sorted, wg, wu, wd, layer)
    finish = functools.partial(_ffn_out, big, small, cnt, x1, x1b, pw, y_sorted, sg, su, sd, g, b)
    if split_rows is None:
        return finish()
    return finish(0, split_rows // tm), finish(split_rows // tm)


def _rope_tables(pos):
    half = MLA_ROPE // 2
    inv = ROPE_THETA ** (-jnp.arange(half, dtype=F32) / half)
    ang = pos.astype(F32)[:, None] * inv
    reps = LANES // half
    return jnp.tile(jnp.cos(ang), (1, reps)), jnp.tile(jnp.sin(ang), (1, reps))


def _pack_layer0_weights(w_in0, gla_w_g2, gla_b_g, mla_q_norm_g, mla_kv_norm_g, mla_w_uq, mla_w_uk, mla_w_uv):
    d = w_in0.shape[0]
    hk, hv = GLA_HEADS * GLA_DK, GLA_HEADS * GLA_DV
    o_q, o_k, o_v, o_r = 0, hk, 2 * hk, 2 * hk + hv
    o_a = o_r + hv
    o_cq = o_a + GLA_RANK
    o_ckv = o_cq + MLA_Q_LORA
    o_kr = o_ckv + MLA_KV_LORA
    half = MLA_ROPE // 2
    zeros = lambda n: jnp.zeros((d, n), w_in0.dtype)
    kr1, kr2 = w_in0[:, o_kr:o_kr + half], w_in0[:, o_kr + half:o_kr + MLA_ROPE]
    w_in = jnp.concatenate([
        w_in0[:, o_q:o_a], w_in0[:, o_cq:o_kr],
        kr1, kr2, zeros(LANES - MLA_ROPE),
        kr2, kr1, zeros(LANES - MLA_ROPE),
        w_in0[:, o_a:o_cq], zeros(LANES - GLA_RANK)], axis=1).astype(BF16)
    assert w_in.shape[1] == _C_END
    w_g2 = jnp.concatenate([gla_w_g2, jnp.zeros((LANES - GLA_RANK, hk), gla_w_g2.dtype)], axis=0)
    uq = mla_w_uq.reshape(MLA_Q_LORA, MLA_HEADS, MLA_NOPE + MLA_ROPE)
    w_uq = jnp.concatenate([uq[:, :, :MLA_NOPE].reshape(MLA_Q_LORA, -1),
                            uq[:, :, MLA_NOPE:MLA_NOPE + half].reshape(MLA_Q_LORA, -1),
                            uq[:, :, MLA_NOPE + half:].reshape(MLA_Q_LORA, -1)], axis=1).astype(BF16)
    eye = jnp.eye(MLA_HEADS, dtype=mla_w_uk.dtype)
    w_abs = jnp.einsum('chn,hg->hngc', mla_w_uk, eye).reshape(MLA_HEADS * MLA_NOPE, MLA_HEADS * MLA_KV_LORA).astype(BF16)
    w_uvbd = jnp.einsum('chv,hg->hcgv', mla_w_uv, eye).reshape(MLA_HEADS * MLA_KV_LORA, MLA_HEADS * MLA_V).astype(BF16)
    place = np.zeros((2 * LANES, MLA_HEADS * LANES), np.float32)
    for h in range(MLA_HEADS):
        for j in range(half):
            place[h * half + j, h * LANES + j] = 1.0
            place[LANES + h * half + j, h * LANES + half + j] = 1.0
    return dict(w_in=w_in, w_g2=w_g2, b_g=gla_b_g.reshape(1, hk), q_norm=mla_q_norm_g.reshape(1, -1),
                kv_norm=mla_kv_norm_g.reshape(1, -1), w_uq=w_uq, w_abs=w_abs,
                place=jnp.asarray(place, BF16)), w_uvbd


def _pad_rows(x, n):
    return jnp.pad(x, ((0, n - x.shape[0]),) + ((0, 0),) * (x.ndim - 1))


def _kv_tiles(past, own):
    kvb = MLA_KV_TILE
    pad = lambda a: jnp.pad(a, ((0, 0), (0, -a.shape[1] % kvb), (0, 0)))
    kv = jnp.concatenate([pad(past), pad(own)], axis=1) if past.shape[1] else pad(own)
    b, n, w = kv.shape
    kt = kv.reshape(b, n // kvb, kvb, w).transpose(0, 1, 3, 2)
    return kt, kv


def kernel(x_prompt, x_sample, cache_mla_ckv, cache_mla_krope, state_gla, cache_pool, meta_tokens, w_in0, gla_w_g2, gla_b_g, gla_norm_g, mla_q_norm_g, mla_kv_norm_g, mla_w_uq, mla_w_uk, mla_w_uv, w_out0, pool_w, pool_scale, ln_g, ln_b, moe_router_w, moe_router_b, moe_w_gate, moe_w_up, moe_w_down, moe_sh_gate, moe_sh_up, moe_sh_down):
    bp, sp, d = x_prompt.shape
    bs, ss, _ = x_sample.shape
    n_meta = meta_tokens.shape[0]
    past_len = cache_mla_ckv.shape[1] - n_meta
    tp, tsm = bp * sp, bs * ss
    assert sp % MLA_Q_TILE == 0 and sp % POOL_TILE == 0 and sp % GLA_BLOCK == 0 and sp % CHUNK == 0
    assert ss <= CHUNK and past_len % CHUNK == 0 and n_meta <= CHUNK and ss <= GLA_BLOCK and n_meta <= GLA_BLOCK
    assert n_meta == POOL_MAX and ss >= POOL_MAX and tp % TOK_TILE == 0
    t_all = -(-(tp + tsm + n_meta) // TOK_TILE) * TOK_TILE
    o_s, o_m = tp, tp + tsm

    x_parts = (x_prompt.reshape(tp, d),
               _pad_rows(jnp.concatenate([x_sample.reshape(tsm, d), meta_tokens], axis=0), t_all - tp))
    pos = jnp.concatenate([jnp.tile(n_meta + jnp.arange(sp), bp), jnp.tile(n_meta + past_len + jnp.arange(ss), bs),
                           jnp.arange(n_meta), jnp.zeros((t_all - o_m - n_meta,), jnp.int32)])
    cos_t, sin_t = _rope_tables(pos)
    w0, w_uvbd = _pack_layer0_weights(w_in0, gla_w_g2, gla_b_g, mla_q_norm_g, mla_kv_norm_g, mla_w_uq, mla_w_uk, mla_w_uv)

    q, k, v, sr, lg, q_all, ckv, kr, k_all = _inproj(*x_parts, cos_t, sin_t, w0)

    tables = _gla_tables(GLA_BLOCK)
    g_gla = gla_norm_g.reshape(1, GLA_DV)
    c = GLA_BLOCK
    hk = GLA_HEADS * GLA_DK

    def small(a):
        sm = jnp.pad(a[o_s:o_m].reshape(bs, ss, -1), ((0, 0), (0, c - ss), (0, 0)))
        me = jnp.pad(a[o_m:o_m + n_meta], ((0, c - n_meta), (0, 0)))[None]
        return jnp.concatenate([sm, me], axis=0).reshape((bs + 1) * c, -1)

    s0_small = jnp.concatenate([state_gla.reshape(bs, hk, GLA_DV), jnp.zeros((1, hk, GLA_DV), F32)], axis=0)
    og_small, st_small = _gla(small(q), small(k), small(v), small(lg), small(sr), s0_small, g_gla, tables, bs + 1, 1)
    s0_prompt = jnp.broadcast_to(st_small[bs:], (bp, hk, GLA_DV))
    og_prompt, st_prompt = _gla(q, k, v, lg, sr, s0_prompt, g_gla, tables, bp, sp // c)
    og_small = og_small.reshape(bs + 1, c, -1)
    o_gla = (og_prompt, _pad_rows(jnp.concatenate([og_small[:bs, :ss].reshape(tsm, -1), og_small[bs, :n_meta]], axis=0),
                                  t_all - tp))

    k_meta = k_all[o_m:o_m + n_meta]
    kt_m, kv_m = _kv_tiles(jnp.zeros((1, 0, MLA_QK), BF16), k_meta[None])
    ol_meta = _mla(q_all[o_m:o_m + n_meta], kt_m, kv_m, 1, 1, n_meta, 0, 0, n_meta)
    cache = jnp.concatenate([cache_mla_ckv, cache_mla_krope,
                             jnp.zeros(cache_mla_ckv.shape[:2] + (MLA_QK - MLA_KV_LORA - MLA_ROPE,), F32)], axis=-1).astype(BF16)
    kt_s, kv_s = _kv_tiles(cache, k_all[o_s:o_m].reshape(bs, ss, MLA_QK))
    ol_sample = _mla(q_all[o_s:o_m], kt_s, kv_s, bs, 1, ss, 0, n_meta + past_len, ss)
    kt_p, kv_p = _kv_tiles(jnp.broadcast_to(k_meta[None], (bp, n_meta, MLA_QK)), k_all[:tp].reshape(bp, sp, MLA_QK))
    ol_prompt = _mla(q_all, kt_p, kv_p, bp, sp // MLA_Q_TILE, MLA_Q_TILE, 0, n_meta, sp)
    o_lat = (ol_prompt, _pad_rows(jnp.concatenate([ol_sample, ol_meta], axis=0), t_all - tp))

    ln = lambda l, j: (ln_g[l, j].reshape(1, d), ln_b[l, j].reshape(1, d))
    rw = lambda l: moe_router_w[l].T
    moe_w = lambda l: (l, moe_w_gate, moe_w_up, moe_w_down,
                       moe_sh_gate[l].astype(BF16), moe_sh_up[l].astype(BF16), moe_sh_down[l].astype(BF16))
    x1, x1b, logits = _outproj(o_gla, o_lat, x_parts, w_uvbd, w_out0.astype(BF16), *ln(0, 0), rw(0))
    x2, _ = _moe(x1, x1b, logits, moe_router_b[0], *moe_w(0), *ln(0, 1))

    pm = POOL_MAX
    zrow = jnp.zeros((1, d), F32)
    hist_p = jnp.broadcast_to(jnp.concatenate([zrow, x2[o_m + n_meta - (pm - 1):o_m + n_meta]], axis=0)[None], (bp, pm, d))
    hist_s = jnp.concatenate([jnp.zeros((bs, 1, d), F32), cache_pool], axis=1)
    pool_wb = pool_w.astype(BF16)
    ps = pool_scale.reshape(1, d)
    t1 = -(-(tp + tsm) // TOK_TILE) * TOK_TILE
    y1p = _pool(x2, hist_p, pool_wb, ps, *ln(1, 0), rw(1), bp, sp // POOL_TILE, POOL_TILE, 0)
    y1s = _pool(x2[o_s:o_m], hist_s, pool_wb, ps, *ln(1, 0), rw(1), bs, 1, ss, 0)
    x3, x3b = [_pad_rows(jnp.concatenate([a, b_], axis=0), t1) for a, b_ in zip(y1p[:2], y1s[:2])]
    by_token = lambda lt: lt.transpose(1, 0, 2).reshape(N_EXPERTS, -1)
    logits1 = jnp.pad(jnp.concatenate([by_token(y1p[2]), by_token(y1s[2])], axis=1), ((0, 0), (0, t1 - tp - tsm)))
    logits1 = logits1.reshape(N_EXPERTS, t1 // TOK_TILE, TOK_TILE).transpose(1, 0, 2)
    (x4_prompt, _), (x4_rest, _) = _moe(x3, x3b, logits1, moe_router_b[1], *moe_w(1), *ln(1, 1), split_rows=tp)

    y_prompt = x4_prompt.reshape(bp, sp, d)
    y_sample = x4_rest[:tsm].reshape(bs, ss, d)
    ckv_meta, kr_meta = ckv[o_m:o_m + n_meta], kr[o_m:o_m + n_meta, :MLA_ROPE]
    p_ckv = jnp.concatenate([jnp.broadcast_to(ckv_meta[None], (bp, n_meta, MLA_KV_LORA)),
                             ckv[:tp].reshape(bp, sp, MLA_KV_LORA)], axis=1)
    p_kr = jnp.concatenate([jnp.broadcast_to(kr_meta[None], (bp, n_meta, MLA_ROPE)),
                            kr[:tp, :MLA_ROPE].reshape(bp, sp, MLA_ROPE)], axis=1)
    p_gla = st_prompt.reshape(bp, GLA_HEADS, GLA_DK, GLA_DV)
    p_pool = jnp.stack([x2[(b_ + 1) * sp - (pm - 1):(b_ + 1) * sp] for b_ in range(bp)])
    s_ckv = ckv[o_s:o_m].reshape(bs, ss, MLA_KV_LORA)
    s_kr = kr[o_s:o_m, :MLA_ROPE].reshape(bs, ss, MLA_ROPE)
    s_gla = st_small[:bs].reshape(bs, GLA_HEADS, GLA_DK, GLA_DV)
    s_pool = x2[o_s:o_m].reshape(bs, ss, d)[:, ss - (pm - 1):]
    return (y_prompt, y_sample, p_ckv, p_kr, p_gla, p_pool, s_ckv, s_kr, s_gla, s_pool)
```

```python
import functools
import math

import jax
import jax.numpy as jnp
import numpy as np
from jax import lax
from jax.experimental import pallas as pl
from jax.experimental.pallas import tpu as pltpu

F32 = jnp.float32
BF16 = jnp.bfloat16

CHUNK = 64
DEPTH = 2
ALPHA = (2 * DEPTH) ** 0.25
LN_EPS = 1e-5
RMS_EPS = 1e-6
NEG_INF = -1e30
GLA_HEADS = 4
GLA_DK = 64
GLA_DV = 128
GLA_RANK = 16
GLA_TAU = 16.0
MLA_HEADS = 8
MLA_Q_LORA = 256
MLA_KV_LORA = 128
MLA_NOPE = 64
MLA_ROPE = 32
MLA_V = 64
MLA_SCALE = (MLA_NOPE + MLA_ROPE) ** -0.5
ROPE_THETA = 10000.0
POOL_WINDOWS = (2, 4, 8, 16)
POOL_MAX = 16
N_EXPERTS = 64
TOP_K = 8
N_GROUPS = 8
TOPK_GROUPS = 4
ROUTED_SCALE = 2.5

LANES = 128
MXU_DIM = 256

TOK_TILE = 512
GLA_BLOCK = 128
MLA_Q_TILE = 256
MLA_KV_TILE = 256
MLA_QK = 256
MLA_ROW_STRIP = 128
LOG2E = math.log2(math.e)
EXPERT_TILE = 1024
EXPERT_SUBTILE = 256
POOL_TILE = 512
SEG_ROWS = 16
BIG_PIECE = 4
DRAIN_UNITS = (64, 16, 4, 1)
ROUTE_TILE = 512
SORT_ROWS = 5120
K_BIG = (TOP_K * ROUTE_TILE + N_EXPERTS * (SEG_ROWS - 1)) // (SEG_ROWS * BIG_PIECE)
K_SMALL = N_EXPERTS * (BIG_PIECE - 1)
PACK_BITS = 9
NO_ROW = 8191.0
DISPATCH_STRIP = 512
COMBINE_STRIP = 1024
VMEM_LIMIT = 56 * 1024 * 1024


def _params(sem, vmem=VMEM_LIMIT):
    return pltpu.CompilerParams(dimension_semantics=sem, vmem_limit_bytes=vmem)


def _const_spec(shape):
    nd = len(shape)
    return pl.BlockSpec(shape, lambda *_: (0,) * nd)


def _split3(x):
    hi = x.astype(BF16)
    r = x - hi.astype(F32)
    mid = r.astype(BF16)
    lo = (r - mid.astype(F32)).astype(BF16)
    return hi, mid, lo


def _dot(a, b):
    return jnp.dot(a, b, preferred_element_type=F32)


def _dot_f32(a, b):
    a_hi = a.astype(BF16)
    a_lo = (a - a_hi.astype(F32)).astype(BF16)
    b_hi = b.astype(BF16)
    b_lo = (b - b_hi.astype(F32)).astype(BF16)
    return _dot(a_hi, b_hi) + (_dot(a_lo, b_hi) + _dot(a_hi, b_lo))


def _dot_f32_nt(a, b):
    nt = lambda u, v: lax.dot_general(u, v, (((1,), (1,)), ((), ())), preferred_element_type=F32)
    a_hi = a.astype(BF16)
    a_lo = (a - a_hi.astype(F32)).astype(BF16)
    b_hi = b.astype(BF16)
    b_lo = (b - b_hi.astype(F32)).astype(BF16)
    return nt(a_hi, b_hi) + (nt(a_lo, b_hi) + nt(a_hi, b_lo))


def _silu(x):
    return x * (1.0 / (1.0 + jnp.exp(-x)))


def _layer_norm(x, g, b):
    mu = jnp.mean(x, axis=-1, keepdims=True)
    xc = x - mu
    var = jnp.mean(xc * xc, axis=-1, keepdims=True)
    return xc * lax.rsqrt(var + LN_EPS) * g + b


def _rms(x, g):
    return x * lax.rsqrt(jnp.mean(x * x, axis=-1, keepdims=True) + RMS_EPS) * g


_C_Q, _C_K, _C_V, _C_R, _C_CQ, _C_CKV, _C_KR, _C_KRS, _C_A, _C_END = (
    0, 256, 512, 1024, 1536, 1792, 1920, 2048, 2176, 2304)


def _two_part_specs(tm, n, n_main):
    return [pl.BlockSpec((tm, n), lambda i, *_: (jnp.minimum(i, n_main - 1), 0)),
            pl.BlockSpec((tm, n), lambda i, *_: (jnp.maximum(i - n_main, 0), 0))]


def _pick(main_ref, rest_ref, n_main):
    return jnp.where(pl.program_id(0) < n_main, main_ref[...], rest_ref[...])


def _inproj_kernel(xm_ref, xr_ref, cos_ref, sin_ref, w_ref, wg2_ref, bg_ref, qn_ref, kvn_ref, wuq_ref, wabs_ref,
                   place_ref, q_ref, k_ref, v_ref, sr_ref, lg_ref, qall_ref, ckv_ref, kr_ref, kall_ref, *, n_main):
    h = _dot(_pick(xm_ref, xr_ref, n_main).astype(BF16), w_ref[...])
    q_ref[...] = (h[:, _C_Q:_C_K] * GLA_DK ** -0.5).astype(BF16)
    k_ref[...] = h[:, _C_K:_C_V].astype(BF16)
    v_ref[...] = h[:, _C_V:_C_R].astype(BF16)
    sr_ref[...] = _silu(h[:, _C_R:_C_CQ]).astype(BF16)
    z = _dot_f32(h[:, _C_A:_C_END], wg2_ref[...]) + bg_ref[...]
    lg_ref[...] = (jnp.minimum(z, 0.0) - jnp.log(1.0 + jnp.exp(-jnp.abs(z)))) * (1.0 / GLA_TAU)
    cqn = _rms(h[:, _C_CQ:_C_CKV], qn_ref[...]).astype(BF16)
    qh = _dot(cqn, wuq_ref[...])
    n_nope = MLA_HEADS * MLA_NOPE
    cos, sin = cos_ref[...], sin_ref[...]
    x1, x2 = qh[:, n_nope:n_nope + LANES], qh[:, n_nope + LANES:]
    qscale = MLA_SCALE * LOG2E
    rot = (jnp.concatenate([x1 * cos - x2 * sin, x1 * sin + x2 * cos], axis=1) * qscale).astype(BF16)
    qlat = (_dot(qh[:, :n_nope].astype(BF16), wabs_ref[...]) * qscale).astype(BF16)
    qrope = _dot(rot, place_ref[...]).astype(BF16)
    for hd in range(MLA_HEADS):
        qall_ref[:, MLA_QK * hd:MLA_QK * hd + LANES] = qlat[:, LANES * hd:LANES * (hd + 1)]
        qall_ref[:, MLA_QK * hd + LANES:MLA_QK * (hd + 1)] = qrope[:, LANES * hd:LANES * (hd + 1)]
    ckv = _rms(h[:, _C_CKV:_C_KR], kvn_ref[...])
    lane = lax.broadcasted_iota(jnp.int32, (1, LANES), 1)
    sgn = jnp.where(lane < MLA_ROPE // 2, -1.0, 1.0)
    kr = h[:, _C_KR:_C_KRS] * cos + h[:, _C_KRS:_C_A] * (sin * sgn)
    ckv_ref[...] = ckv
    kr_ref[...] = kr
    kall_ref[:, :LANES] = ckv.astype(BF16)
    kall_ref[:, LANES:] = kr.astype(BF16)


def _inproj(x_main, x_rest, cos_t, sin_t, w):
    t = x_main.shape[0] + x_rest.shape[0]
    d = x_main.shape[1]
    tm = TOK_TILE
    n_main = x_main.shape[0] // tm
    row = lambda n: pl.BlockSpec((tm, n), lambda i: (i, 0))
    consts = [w['w_in'], w['w_g2'], w['b_g'], w['q_norm'], w['kv_norm'], w['w_uq'], w['w_abs'], w['place']]
    outs = [(GLA_HEADS * GLA_DK, BF16), (GLA_HEADS * GLA_DK, BF16), (GLA_HEADS * GLA_DV, BF16),
            (GLA_HEADS * GLA_DV, BF16), (GLA_HEADS * GLA_DK, F32), (MLA_HEADS * MLA_QK, BF16),
            (LANES, F32), (LANES, F32), (MLA_QK, BF16)]
    return pl.pallas_call(
        functools.partial(_inproj_kernel, n_main=n_main),
        grid=(t // tm,),
        in_specs=_two_part_specs(tm, d, n_main) + [row(LANES), row(LANES)] + [_const_spec(c.shape) for c in consts],
        out_specs=[row(n) for n, _ in outs],
        out_shape=[jax.ShapeDtypeStruct((t, n), dt) for n, dt in outs],
        compiler_params=_params(("parallel",)),
        name="inproj",
    )(x_main, x_rest, cos_t, sin_t, *consts)


def _gla_tables(c):
    levels = int(math.log2(c))
    assert 1 << levels == c
    t = np.arange(c)[:, None]
    u = np.arange(c)[None, :]
    mats, masks = [], []
    for l in range(levels):
        half = 1 << l
        base = (t // half) * half
        upper = ((t >> l) & 1) == 1
        a_q = (u >= base) & (u <= t)
        a_k = (u > t) & (u <= base + half - 1)
        mats.append(np.where(upper, a_q, a_k))
        same = (t >> (l + 1)) == (u >> (l + 1))
        masks.append(same & upper & (((u >> l) & 1) == 0))
    mats.append(u <= t)
    mats.append(u > t)
    masks.append(t == u)
    amat = np.concatenate(mats, axis=0).astype(np.float32)
    mask = np.stack([np.tile(m, (GLA_HEADS, 1)) for m in masks]).astype(np.float32)
    return jnp.asarray(amat, BF16), jnp.asarray(mask, F32), levels


def _gla_kernel(q_ref, k_ref, v_ref, lg_ref, sr_ref, s0_ref, amat_ref, mask_ref, g_ref,
                o_ref, sout_ref, state, *, levels):
    c = q_ref.shape[0]
    hk = GLA_HEADS * GLA_DK
    step = pl.program_id(1)

    @pl.when(step == 0)
    def _():
        state[...] = s0_ref[0]

    lg = lg_ref[...]
    parts = _split3(lg)
    e3 = _dot(amat_ref[...], jnp.concatenate(parts, axis=1))
    ex = jnp.exp(e3[:, :hk] + e3[:, hk:2 * hk] + e3[:, 2 * hk:])
    q = q_ref[...].astype(F32)
    k = k_ref[...].astype(F32)
    v = v_ref[...]
    head = lax.broadcasted_iota(jnp.int32, (1, hk), 1) // GLA_DK

    def per_head(xf):
        return jnp.concatenate([jnp.where(head == h, xf, 0.0) for h in range(GLA_HEADS)], axis=0).astype(BF16)

    att = jnp.zeros((GLA_HEADS * c, c), F32)
    for l in range(levels + 1):
        if l < levels:
            el = ex[l * c:(l + 1) * c]
            ql, kl = q * el, k * el
        else:
            ql, kl = q, k
        a = lax.dot_general(per_head(ql), kl.astype(BF16), (((1,), (1,)), ((), ())), preferred_element_type=F32)
        att = att + a * mask_ref[l]
    s_prev = state[...]
    o_inter = _dot(per_head(q * ex[levels * c:(levels + 1) * c]), s_prev.astype(BF16))
    att = att.astype(BF16)
    g = g_ref[...]
    for h in range(GLA_HEADS):
        o = o_inter[h * c:(h + 1) * c] + _dot(att[h * c:(h + 1) * c], v[:, GLA_DV * h:GLA_DV * (h + 1)])
        o = _rms(o, g)
        o_ref[:, GLA_DV * h:GLA_DV * (h + 1)] = (o * sr_ref[:, GLA_DV * h:GLA_DV * (h + 1)].astype(F32)).astype(BF16)
    kr = (k * ex[(levels + 1) * c:]).astype(BF16)
    upd = lax.dot_general(kr, v, (((0,), (0,)), ((), ())), preferred_element_type=F32)
    ones = jnp.ones((c, GLA_DV), BF16)
    b_last = sum(lax.dot_general(p, ones, (((0,), (0,)), ((), ())), preferred_element_type=F32) for p in parts)
    new = jnp.exp(b_last) * s_prev + jnp.concatenate(
        [upd[GLA_DK * h:GLA_DK * (h + 1), GLA_DV * h:GLA_DV * (h + 1)] for h in range(GLA_HEADS)], axis=0)
    state[...] = new

    @pl.when(step == pl.num_programs(1) - 1)
    def _():
        sout_ref[0] = new


def _gla(q, k, v, lg, sr, s0, g, tables, batch, blocks_per_seq, out_rows=None):
    amat, mask, levels = tables
    c = GLA_BLOCK
    hk, hv = GLA_HEADS * GLA_DK, GLA_HEADS * GLA_DV
    row = lambda n: pl.BlockSpec((c, n), lambda b, s: (b * blocks_per_seq + s, 0))
    st = pl.BlockSpec((1, hk, GLA_DV), lambda b, s: (b, 0, 0))
    return pl.pallas_call(
        functools.partial(_gla_kernel, levels=levels),
        grid=(batch, blocks_per_seq),
        in_specs=[row(hk), row(hk), row(hv), row(hk), row(hv), st,
                  _const_spec(amat.shape), _const_spec(mask.shape), _const_spec(g.shape)],
        out_specs=[row(hv), st],
        out_shape=[jax.ShapeDtypeStruct((out_rows or batch * blocks_per_seq * c, hv), BF16),
                   jax.ShapeDtypeStruct((batch, hk, GLA_DV), F32)],
        scratch_shapes=[pltpu.VMEM((hk, GLA_DV), F32)],
        compiler_params=_params(("parallel", "arbitrary")),
        name="gla",
    )(q, k, v, lg, sr, s0, amat, mask, g)


def _mla_kernel(q_ref, kt_ref, kv_ref, o_ref, s_sc, p_sc, linv_sc, acc_sc, *, tq, n_past, past_valid, own_valid,
                diag_chunks):
    kvb = MLA_KV_TILE
    qi = pl.program_id(1)
    qs = jnp.concatenate([q_ref[:, MLA_QK * h:MLA_QK * (h + 1)] for h in range(MLA_HEADS)], axis=0)
    m_rows = MLA_HEADS * tq
    n_vis = n_past + qi + 1

    def score(j, mask):
        s = _dot(qs, kt_ref[0, j])
        s_sc[j] = s if mask is None else jnp.where(mask, s, NEG_INF)

    def scores(lo, hi):
        def body(j, carry):
            score(j, None)
            return carry
        lax.fori_loop(lo, hi, body, 0)

    col = lax.broadcasted_iota(jnp.int32, (m_rows, kvb), 1)
    n_past_full = past_valid // kvb
    if n_past_full:
        scores(0, n_past_full)
    if n_past_full < n_past:
        score(n_past_full, col < past_valid - n_past_full * kvb)
    scores(n_past, n_past + qi)
    mask = col < own_valid - qi * kvb
    if diag_chunks:
        qrow = lax.broadcasted_iota(jnp.int32, (m_rows, kvb), 0) % tq
        mask = mask & (col // CHUNK <= qrow // CHUNK)
    score(n_past + qi, mask)

    rc = MLA_ROW_STRIP

    def strip(r, carry):
        rows = pl.ds(pl.multiple_of(r * rc, rc), rc)
        mx = lax.fori_loop(0, n_vis, lambda j, m: jnp.maximum(m, s_sc[j, rows, :]),
                           jnp.full((rc, kvb), -jnp.inf, F32))
        m = jnp.max(mx, axis=1, keepdims=True)

        def body(j, l):
            p = jnp.exp2(s_sc[j, rows, :] - m)
            p_sc[j, rows, :] = p.astype(BF16)
            return l + p
        l = lax.fori_loop(0, n_vis, body, jnp.zeros((rc, kvb), F32))
        linv_sc[rows, :] = 1.0 / jnp.sum(l, axis=1, keepdims=True)
        return carry
    lax.fori_loop(0, m_rows // rc, strip, 0)

    acc_sc[...] = jnp.zeros(acc_sc.shape, F32)

    def pv(j, carry):
        vblk = kv_ref[0, pl.ds(pl.multiple_of(j * kvb, kvb), kvb), :MLA_KV_LORA]
        acc_sc[...] += _dot(p_sc[j], vblk)
        return carry
    lax.fori_loop(0, n_vis, pv, 0)
    o = acc_sc[...] * linv_sc[...]
    for h in range(MLA_HEADS):
        o_ref[:, MLA_KV_LORA * h:MLA_KV_LORA * (h + 1)] = o[h * tq:(h + 1) * tq].astype(BF16)


def _mla(q_all, kt, kv, batch, nq, tq, q_block0, past_valid, own_valid, out_rows=None):
    n_kv = kt.shape[1]
    n_past = -(-past_valid // MLA_KV_TILE)
    assert nq == 1 or tq == MLA_KV_TILE
    assert n_kv == n_past + nq
    return pl.pallas_call(
        functools.partial(_mla_kernel, tq=tq, n_past=n_past, past_valid=past_valid, own_valid=own_valid,
                          diag_chunks=nq > 1),
        grid=(batch, nq),
        in_specs=[pl.BlockSpec((tq, MLA_HEADS * MLA_QK), lambda b, i: (q_block0 + b * nq + i, 0)),
                  pl.BlockSpec((1,) + kt.shape[1:], lambda b, i: (b, 0, 0, 0)),
                  pl.BlockSpec((1,) + kv.shape[1:], lambda b, i: (b, 0, 0))],
        out_specs=pl.BlockSpec((tq, MLA_HEADS * MLA_KV_LORA), lambda b, i: (b * nq + i, 0)),
        out_shape=jax.ShapeDtypeStruct((out_rows or batch * nq * tq, MLA_HEADS * MLA_KV_LORA), BF16),
        scratch_shapes=[pltpu.VMEM((n_kv, MLA_HEADS * tq, MLA_KV_TILE), F32),
                        pltpu.VMEM((n_kv, MLA_HEADS * tq, MLA_KV_TILE), BF16),
                        pltpu.VMEM((MLA_HEADS * tq, 1), F32),
                        pltpu.VMEM((MLA_HEADS * tq, MLA_KV_LORA), F32)],
        compiler_params=_params(("parallel", "arbitrary")),
        name="mla",
    )(q_all, kt, kv)


def _post_mixer(x, out, g, b, rw_ref, x1_ref, x1b_ref, lgt_ref):
    y = _layer_norm(ALPHA * x + out, g, b)
    x1_ref[...] = y
    x1b_ref[...] = y.astype(BF16)
    lgt_ref[0] = _dot_f32_nt(rw_ref[...], y)


def _outproj_kernel(ogm_ref, ogr_ref, olm_ref, olr_ref, xm_ref, xr_ref, wuv_ref, wo_ref, g_ref, b_ref, rw_ref,
                    x1_ref, x1b_ref, lgt_ref, *, n_main):
    og = _pick(ogm_ref, ogr_ref, n_main)
    o_mla = _dot(_pick(olm_ref, olr_ref, n_main), wuv_ref[...]).astype(BF16)
    n_gla = og.shape[1]
    out = _dot(og, wo_ref[:n_gla, :]) + _dot(o_mla, wo_ref[n_gla:, :])
    _post_mixer(_pick(xm_ref, xr_ref, n_main), out, g_ref[...], b_ref[...], rw_ref, x1_ref, x1b_ref, lgt_ref)


def _outproj(o_gla, o_lat, x, w_uvbd, w_out, g, b, rw):
    t, d = x[0].shape[0] + x[1].shape[0], x[0].shape[1]
    tm = TOK_TILE
    n_main = x[0].shape[0] // tm
    row = lambda n: pl.BlockSpec((tm, n), lambda i: (i, 0))
    consts = [w_uvbd, w_out, g, b, rw]
    parts = [*o_gla, *o_lat, *x]
    return pl.pallas_call(
        functools.partial(_outproj_kernel, n_main=n_main),
        grid=(t // tm,),
        in_specs=[s for a in (o_gla, o_lat, x) for s in _two_part_specs(tm, a[0].shape[1], n_main)]
        + [_const_spec(c.shape) for c in consts],
        out_specs=[row(d), row(d), pl.BlockSpec((1, N_EXPERTS, tm), lambda i: (i, 0, 0))],
        out_shape=[jax.ShapeDtypeStruct((t, d), F32), jax.ShapeDtypeStruct((t, d), BF16),
                   jax.ShapeDtypeStruct((t // tm, N_EXPERTS, tm), F32)],
        compiler_params=_params(("parallel",)),
        name="outproj",
    )(*parts, *consts)


def _pool_kernel(x_ref, prev_ref, hist_ref, pw_ref, ps_ref, g_ref, b_ref, rw_ref, x1_ref, x1b_ref, lgt_ref):
    ts = x_ref.shape[0]
    pm = POOL_MAX
    x = x_ref[...]
    prev = jnp.where(pl.program_id(1) == 0, hist_ref[0], prev_ref[...])
    xc = jnp.concatenate([prev, x], axis=0)
    grp = x.shape[1] // len(POOL_WINDOWS)
    outs = []
    for gi, w in enumerate(POOL_WINDOWS):
        s = xc[:, gi * grp:(gi + 1) * grp]
        span = 1
        while span < w:
            s = s + pltpu.roll(s, span, 0)
            span *= 2
        win = s[pm:]
        mix = (win * (1.0 / w) - x[:, gi * grp:(gi + 1) * grp]).astype(BF16)
        outs.append(_dot(mix, pw_ref[gi]))
    out = jnp.concatenate(outs, axis=1) * ps_ref[...]
    _post_mixer(x, out, g_ref[...], b_ref[...], rw_ref, x1_ref, x1b_ref, lgt_ref)


def _pool(x_all, hist, pool_w, pool_scale, g, b, rw, batch, tiles_per_seq, ts, tile0, out_rows=None):
    d = x_all.shape[1]
    pm = POOL_MAX
    per = ts // pm
    consts = [pool_w, pool_scale, g, b, rw]
    row = lambda n: pl.BlockSpec((ts, n), lambda bb, i: (bb * tiles_per_seq + i, 0))
    return pl.pallas_call(
        _pool_kernel,
        grid=(batch, tiles_per_seq),
        in_specs=[pl.BlockSpec((ts, d), lambda bb, i: (tile0 + bb * tiles_per_seq + i, 0)),
                  pl.BlockSpec((pm, d), lambda bb, i: (jnp.maximum((tile0 + bb * tiles_per_seq + i) * per - 1, 0), 0)),
                  pl.BlockSpec((1, pm, d), lambda bb, i: (bb, 0, 0))] + [_const_spec(c.shape) for c in consts],
        out_specs=[row(d), row(d), pl.BlockSpec((1, N_EXPERTS, ts), lambda bb, i: (bb * tiles_per_seq + i, 0, 0))],
        out_shape=[jax.ShapeDtypeStruct((out_rows or batch * tiles_per_seq * ts, d), F32),
                   jax.ShapeDtypeStruct((out_rows or batch * tiles_per_seq * ts, d), BF16),
                   jax.ShapeDtypeStruct((batch * tiles_per_seq, N_EXPERTS, ts), F32)],
        compiler_params=_params(("parallel", "arbitrary")),
        name="pool",
    )(x_all, x_all, hist, *consts)


def _experts_kernel(be_ref, nb_ref, x_ref, wg_ref, wu_ref, wd_ref, y_ref):
    @pl.when(pl.program_id(0) < nb_ref[0])
    def _():
        wg, wu, wd = wg_ref[0, 0].astype(BF16), wu_ref[0, 0].astype(BF16), wd_ref[0, 0].astype(BF16)
        sub = EXPERT_SUBTILE
        for r in range(x_ref.shape[0] // sub):
            x = x_ref[r * sub:(r + 1) * sub, :]
            hmid = (_silu(_dot(x, wg)) * _dot(x, wu)).astype(BF16)
            y_ref[r * sub:(r + 1) * sub, :] = _dot(hmid, wd).astype(BF16)

    @pl.when(pl.program_id(0) >= nb_ref[0])
    def _():
        y_ref[...] = jnp.zeros_like(y_ref)


def _experts(blk_e, n_used, x_sorted, wg, wu, wd, layer):
    p, d = x_sorted.shape
    f = wg.shape[3]
    bm = EXPERT_TILE
    blk = lambda i, be, nb: (i, 0)
    grid_spec = pltpu.PrefetchScalarGridSpec(
        num_scalar_prefetch=2,
        grid=(p // bm,),
        in_specs=[pl.BlockSpec((bm, d), lambda i, be, nb: (jnp.minimum(i, nb[0] - 1), 0)),
                  pl.BlockSpec((1, 1, d, f), lambda i, be, nb: (layer, be[i], 0, 0)),
                  pl.BlockSpec((1, 1, d, f), lambda i, be, nb: (layer, be[i], 0, 0)),
                  pl.BlockSpec((1, 1, f, d), lambda i, be, nb: (layer, be[i], 0, 0))],
        out_specs=pl.BlockSpec((bm, d), blk),
    )
    return pl.pallas_call(
        _experts_kernel,
        grid_spec=grid_spec,
        out_shape=jax.ShapeDtypeStruct((p, d), BF16),
        compiler_params=_params(("arbitrary",)),
        name="experts",
    )(blk_e, n_used, x_sorted, wg, wu, wd)


def _segment_copies(tile, slot, dst_ref, n16_ref, seg_ref, make, entries=N_EXPERTS):
    big = SEG_ROWS * BIG_PIECE

    def per_expert(e, carry):
        j = tile * N_EXPERTS + e
        d0, s0, n = dst_ref[j], seg_ref[j], n16_ref[j]
        n_big = n // BIG_PIECE

        def big_piece(c, carry2):
            make(pl.multiple_of(d0 + c * big, SEG_ROWS), pl.multiple_of(s0 + c * big, SEG_ROWS), big, slot).start()
            return carry2
        lax.fori_loop(0, n_big, big_piece, 0)
        d1, s1 = d0 + n_big * big, s0 + n_big * big

        def small_piece(c, carry2):
            make(pl.multiple_of(d1 + c * SEG_ROWS, SEG_ROWS), pl.multiple_of(s1 + c * SEG_ROWS, SEG_ROWS),
                 SEG_ROWS, slot).start()
            return carry2
        lax.fori_loop(0, n - n_big * BIG_PIECE, small_piece, 0)
        return carry
    lax.fori_loop(0, entries, per_expert, 0)


def _start_pieces(tile, slot, big_ref, small_ref, cnt_ref, make):
    def run(tab_ref, kmax, n, rows):
        def body(k, carry):
            v = tab_ref[tile * kmax + k]
            make(pl.multiple_of((v >> PACK_BITS) * SEG_ROWS, SEG_ROWS),
                 pl.multiple_of((v & ((1 << PACK_BITS) - 1)) * SEG_ROWS, SEG_ROWS), rows, slot).start()
            return carry
        lax.fori_loop(0, n, body, 0)
    run(big_ref, K_BIG, cnt_ref[2 * tile], SEG_ROWS * BIG_PIECE)
    run(small_ref, K_SMALL, cnt_ref[2 * tile + 1], SEG_ROWS)


def _tile_rows(tile, cnt_ref):
    return cnt_ref[2 * tile] * (SEG_ROWS * BIG_PIECE) + cnt_ref[2 * tile + 1] * SEG_ROWS


def _piece_tables(dst, n16, seg):
    nt, ne = n16.shape
    big_rows = SEG_ROWS * BIG_PIECE
    n_big, n_small = n16 // BIG_PIECE, n16 % BIG_PIECE

    def flat(counts, first_dst, first_src, step, kmax):
        cum = jnp.cumsum(counts, axis=1)[:, None, :]
        before = cum - counts[:, None, :]
        k = jnp.arange(kmax, dtype=jnp.int32)[None, :, None]
        mine = (before <= k) & (k < cum)
        off = step * (k - before)
        d = jnp.sum(jnp.where(mine, first_dst[:, None, :] + off, 0), axis=2)
        s = jnp.sum(jnp.where(mine, first_src[:, None, :] + off, 0), axis=2)
        return (((d // SEG_ROWS) << PACK_BITS) | (s // SEG_ROWS)).reshape(-1).astype(jnp.int32), cum[:, 0, -1]
    big, cnt_big = flat(n_big, dst, seg, big_rows, K_BIG)
    small, cnt_small = flat(n_small, dst + n_big * big_rows, seg + n_big * big_rows, SEG_ROWS, K_SMALL)
    return big, small, jnp.stack([cnt_big, cnt_small], axis=1).reshape(-1).astype(jnp.int32)


def _drain(rows, slot, make):
    left = rows // SEG_ROWS
    for unit in DRAIN_UNITS:
        n = left // unit

        def body(c, carry, unit=unit):
            make(0, 0, unit * SEG_ROWS, slot).wait()
            return carry
        lax.fori_loop(0, n, body, 0)
        left = left - n * unit


def _ffn_out_kernel(big_ref, small_ref, cnt_ref, x_ref, xb_ref, pw_ref, ys_ref,
                    sg_ref, su_ref, sd_ref, g_ref, b_ref, y_ref, yb_ref, ybuf, racc, sem, *, tile0):
    i = pl.program_id(0)
    nt = pl.num_programs(0)
    slot = i % 2
    tile = i + tile0

    def make(d0, s0, rows, sl):
        return pltpu.make_async_copy(ys_ref.at[pl.ds(d0, rows)], ybuf.at[sl, pl.ds(s0, rows)], sem.at[sl])

    @pl.when(i == 0)
    def _():
        ybuf[...] = jnp.zeros_like(ybuf)
        _start_pieces(tile, 0, big_ref, small_ref, cnt_ref, make)

    @pl.when(i + 1 < nt)
    def _():
        _start_pieces(tile + 1, 1 - slot, big_ref, small_ref, cnt_ref, make)

    xb = xb_ref[...]
    hmid = (_silu(_dot(xb, sg_ref[...])) * _dot(xb, su_ref[...])).astype(BF16)
    shared = _dot(hmid, sd_ref[...])
    used = _tile_rows(tile, cnt_ref)
    _drain(used, slot, make)

    racc[...] = jnp.zeros_like(racc)
    rs = COMBINE_STRIP
    for s in range(ybuf.shape[1] // rs):
        @pl.when(s * rs < used)
        def _():
            racc[...] += lax.dot_general(pw_ref[0, s * rs:(s + 1) * rs, :], ybuf[slot, s * rs:(s + 1) * rs, :],
                                         (((0,), (0,)), ((), ())), preferred_element_type=F32)
    ffn = racc[...] + shared
    y = _layer_norm(ALPHA * x_ref[...] + ffn, g_ref[...], b_ref[...])
    y_ref[...] = y
    yb_ref[...] = y.astype(BF16)


def _ffn_out(big, small, cnt, x1, x1b, pw, y_sorted, sg, su, sd, g, b, tile0=0, n_tiles=None, row0=None):
    d = x1.shape[1]
    nt, _, tm = pw.shape
    n_tiles = nt - tile0 if n_tiles is None else n_tiles
    row0 = tile0 if row0 is None else row0
    row_in = pl.BlockSpec((tm, d), lambda i, *_: (i + row0, 0))
    row_out = pl.BlockSpec((tm, d), lambda i, *_: (i, 0))
    consts = [sg, su, sd, g, b]
    grid_spec = pltpu.PrefetchScalarGridSpec(
        num_scalar_prefetch=3,
        grid=(n_tiles,),
        in_specs=[row_in, row_in, pl.BlockSpec((1,) + pw.shape[1:], lambda i, *_: (i + tile0, 0, 0)),
                  pl.BlockSpec(memory_space=pl.ANY)] + [_const_spec(c.shape) for c in consts],
        out_specs=[row_out, row_out],
        scratch_shapes=[pltpu.VMEM((2, SORT_ROWS, d), BF16), pltpu.VMEM((tm, d), F32), pltpu.SemaphoreType.DMA((2,))],
    )
    return pl.pallas_call(
        functools.partial(_ffn_out_kernel, tile0=tile0),
        grid_spec=grid_spec,
        out_shape=[jax.ShapeDtypeStruct((n_tiles * tm, d), F32), jax.ShapeDtypeStruct((n_tiles * tm, d), BF16)],
        compiler_params=_params(("arbitrary",)),
        name="ffn_out",
    )(big, small, cnt, x1, x1b, pw, y_sorted, *consts)


def _dispatch_kernel(big_ref, small_ref, cnt_ref, tdst_ref, tn16_ref, xm_ref, xr_ref, rid_ref, w_ref, segr_ref, c16r_ref,
                     xs_ref, pw_ref, buf, zbuf, sem, *, n_main):
    i = pl.program_id(0)
    nt = pl.num_programs(0)
    slot = i % 2
    tm = xm_ref.shape[0]

    def make(d0, s0, rows, sl):
        return pltpu.make_async_copy(buf.at[sl, pl.ds(s0, rows)], xs_ref.at[pl.ds(d0, rows)], sem.at[sl])

    @pl.when(i >= 2)
    def _():
        _drain(_tile_rows(i - 2, cnt_ref), slot, make)

    x = _pick(xm_ref, xr_ref, n_main)
    rid = rid_ref[0]
    wb = w_ref[0].astype(BF16)
    used = _tile_rows(i, cnt_ref)
    rs = DISPATCH_STRIP
    hs = MXU_DIM
    for s in range(buf.shape[1] // rs):
        @pl.when(s * rs < used)
        def _():
            hits = []
            for h0 in range(s * rs, (s + 1) * rs, hs):
                r = lax.broadcasted_iota(jnp.int32, (hs, N_EXPERTS), 0) + h0
                owner = jnp.where((r >= segr_ref[0]) & (r < segr_ref[0] + c16r_ref[0]), 1.0, 0.0).astype(BF16)
                local = jnp.where((rid > h0) & (rid <= h0 + hs), rid - h0, 0.0).astype(BF16)
                at_row = _dot(owner, jnp.concatenate([local, wb], axis=1))
                hit = at_row[:, :tm] == (lax.broadcasted_iota(jnp.int32, (hs, tm), 0) + 1).astype(F32)
                hits.append(jnp.where(hit, 1.0, 0.0).astype(BF16))
                pw_ref[0, h0:h0 + hs, :] = jnp.where(hit, at_row[:, tm:], 0.0).astype(BF16)
            buf[slot, s * rs:(s + 1) * rs, :] = _dot(jnp.concatenate(hits, axis=0), x).astype(BF16)

        @pl.when(s * rs >= used)
        def _():
            pw_ref[0, s * rs:(s + 1) * rs, :] = jnp.zeros((rs, tm), BF16)
    _start_pieces(i, slot, big_ref, small_ref, cnt_ref, make)

    @pl.when(i == nt - 1)
    def _():
        zbuf[...] = jnp.zeros_like(zbuf)

        def zmake(d0, s0, rows, sl):
            return pltpu.make_async_copy(zbuf.at[pl.ds(0, rows)], xs_ref.at[pl.ds(d0, rows)], sem.at[sl])
        _segment_copies(0, 2, tdst_ref, tn16_ref, tdst_ref, zmake, entries=N_EXPERTS + 1)
        tail = lax.fori_loop(0, N_EXPERTS + 1, lambda e, acc: acc + tn16_ref[e], 0) * SEG_ROWS
        _drain(tail, 2, make)

        @pl.when(i >= 1)
        def _():
            _drain(_tile_rows(i - 1, cnt_ref), 1 - slot, make)
        _drain(used, slot, make)


def _dispatch(big, small, cnt, tail_dst, tail_n16, xb, rid, wdense, segr, c16r, n_rows):
    d = xb[0].shape[1]
    nt, ne, tm = rid.shape
    n_main = xb[0].shape[0] // tm
    per_tile = lambda a: pl.BlockSpec((1,) + a.shape[1:], lambda i, *_: (i, 0, 0))
    grid_spec = pltpu.PrefetchScalarGridSpec(
        num_scalar_prefetch=5,
        grid=(nt,),
        in_specs=_two_part_specs(tm, d, n_main) + [per_tile(rid), per_tile(wdense), per_tile(segr), per_tile(c16r)],
        out_specs=[pl.BlockSpec(memory_space=pl.ANY), pl.BlockSpec((1, SORT_ROWS, tm), lambda i, *_: (i, 0, 0))],
        scratch_shapes=[pltpu.VMEM((2, SORT_ROWS, d), BF16), pltpu.VMEM((SEG_ROWS * BIG_PIECE, d), BF16),
                        pltpu.SemaphoreType.DMA((3,))],
    )
    return pl.pallas_call(
        functools.partial(_dispatch_kernel, n_main=n_main),
        grid_spec=grid_spec,
        out_shape=[jax.ShapeDtypeStruct((n_rows, d), BF16), jax.ShapeDtypeStruct((nt, SORT_ROWS, tm), BF16)],
        compiler_params=_params(("arbitrary",)),
        name="dispatch",
    )(big, small, cnt, tail_dst, tail_n16, *xb, rid, wdense, segr, c16r)


def _top_rows(vals, n, k, ids):
    sel = jnp.zeros(vals.shape, jnp.bool_)
    for _ in range(k):
        m = jnp.max(vals, axis=0, keepdims=True)
        hit = ids == jnp.min(jnp.where(vals == m, ids, n), axis=0, keepdims=True)
        sel = sel | hit
        vals = jnp.where(hit, -jnp.inf, vals)
    return sel


def _router_kernel(lg_ref, b_ref, tri_ref, low_ref, rid_ref, w_ref, tab_ref, carry):
    @pl.when(pl.program_id(0) == 0)
    def _():
        carry[...] = jnp.zeros_like(carry)

    tm = lg_ref.shape[2]
    gs = N_EXPERTS // N_GROUPS
    scores = 1.0 / (1.0 + jnp.exp(-lg_ref[0]))
    biased = scores + b_ref[...]
    member = lax.broadcasted_iota(jnp.int32, (gs, tm), 0)
    group_score = []
    for g in range(N_GROUPS):
        blk = biased[g * gs:(g + 1) * gs]
        m1 = jnp.max(blk, axis=0, keepdims=True)
        first = jnp.min(jnp.where(blk == m1, member, gs), axis=0, keepdims=True)
        m2 = jnp.max(jnp.where(member == first, -jnp.inf, blk), axis=0, keepdims=True)
        group_score.append(m1 + m2)
    group_score = jnp.concatenate(group_score, axis=0)
    gid = lax.broadcasted_iota(jnp.int32, (N_GROUPS, tm), 0)
    group_ok = _top_rows(group_score, N_GROUPS, TOPK_GROUPS, gid)
    masked = jnp.concatenate([jnp.where(group_ok[g:g + 1], biased[g * gs:(g + 1) * gs], -jnp.inf)
                              for g in range(N_GROUPS)], axis=0)
    eid = lax.broadcasted_iota(jnp.int32, (N_EXPERTS, tm), 0)
    sel = _top_rows(masked, N_EXPERTS, TOP_K, eid)
    self = jnp.where(sel, 1.0, 0.0)
    wsel = self * scores
    wts = wsel / jnp.sum(wsel, axis=0, keepdims=True) * ROUTED_SCALE
    selb = self.astype(BF16)
    before = _dot(selb, tri_ref[...])
    pieces = jnp.floor((jnp.sum(self, axis=1, keepdims=True) + (SEG_ROWS - 1.0)) * (1.0 / SEG_ROWS))
    seg = SEG_ROWS * _dot(low_ref[...], jnp.broadcast_to(pieces, (N_EXPERTS, LANES)).astype(BF16))
    rid_ref[0] = jnp.where(sel, seg[:, :1] + before + 1.0, NO_ROW)
    w_ref[0] = wts
    lane = lax.broadcasted_iota(jnp.int32, (N_EXPERTS, LANES), 1)
    tab_ref[0] = jnp.where(lane == 0, seg, jnp.where(lane == 1, SEG_ROWS * pieces, jnp.where(lane == 2, carry[...], 0.0)))
    carry[...] = carry[...] + SEG_ROWS * pieces


def _router(logits, router_b):
    nt, ne, tm = logits.shape
    tri = jnp.asarray(np.triu(np.ones((tm, tm), np.float32), 1), BF16)
    low = jnp.asarray(np.tril(np.ones((ne, ne), np.float32), -1), BF16)
    per_tile = lambda n: pl.BlockSpec((1, ne, n), lambda i: (i, 0, 0))
    return pl.pallas_call(
        _router_kernel,
        grid=(nt,),
        in_specs=[per_tile(tm), _const_spec((ne, 1)), _const_spec(tri.shape), _const_spec(low.shape)],
        out_specs=[per_tile(tm), per_tile(tm), per_tile(LANES)],
        out_shape=[jax.ShapeDtypeStruct((nt, ne, tm), F32), jax.ShapeDtypeStruct((nt, ne, tm), F32),
                   jax.ShapeDtypeStruct((nt, ne, LANES), F32)],
        scratch_shapes=[pltpu.VMEM((ne, 1), F32)],
        compiler_params=_params(("arbitrary",)),
        name="router",
    )(logits, router_b.reshape(ne, 1).astype(F32), tri, low)


def _moe(x1, x1b, logits, router_b, layer, wg, wu, wd, sg, su, sd, g, b):
    two_parts = isinstance(x1, tuple)
    nt, ne, tm = logits.shape
    t = nt * tm
    if tm != ROUTE_TILE:
        split = tm // ROUTE_TILE
        logits = logits.reshape(nt, ne, split, ROUTE_TILE).transpose(0, 2, 1, 3).reshape(nt * split, ne, ROUTE_TILE)
        nt, tm = nt * split, ROUTE_TILE
    assert SORT_ROWS >= TOP_K * tm + ne * (SEG_ROWS - 1) and SORT_ROWS < NO_ROW
    rid, wdense, tab = _router(logits, router_b)
    tab = tab[:, :, :3].astype(jnp.int32)
    seg, rows, base = tab[:, :, 0], tab[:, :, 1], tab[:, :, 2]
    bm = EXPERT_TILE
    region = base[-1] + rows[-1]
    padded = (region + bm - 1) // bm * bm
    pad_end = jnp.cumsum(padded)
    dst = (pad_end - padded)[None, :] + base
    n_blocks = -(-(t * TOP_K + nt * ne * (SEG_ROWS - 1)) // bm) + ne
    blk_start = jnp.arange(n_blocks, dtype=jnp.int32) * bm
    blk_e = jnp.minimum(jnp.sum((pad_end[None, :] <= blk_start[:, None]).astype(jnp.int32), axis=1), ne - 1)
    n_used = (pad_end[-1] // bm).astype(jnp.int32).reshape(1)
    tail_dst = jnp.concatenate([pad_end - padded + region, pad_end[-1:]])
    tail_n16 = jnp.concatenate([padded - region, n_blocks * bm - pad_end[-1:]]) // SEG_ROWS
    big, small, cnt = _piece_tables(dst, rows // SEG_ROWS, seg)
    x_sorted, pw = _dispatch(big, small, cnt, tail_dst, tail_n16, x1b if two_parts else (x1b, x1b), rid, wdense,
                             seg[:, None, :], rows[:, None, :], n_blocks * bm)
    y_sorted = _experts(blk_e, n_used, x_sorted, wg, wu, wd, layer)
    finish = lambda xf, xb, tile0, n_tiles: _ffn_out(big, small, cnt, xf, xb, pw, y_sorted, sg, su, sd, g, b,
                                                     tile0, n_tiles, row0=0)
    if not two_parts:
        return finish(x1, x1b, 0, nt)
    n_main = x1[0].shape[0] // tm
    return finish(x1[0], x1b[0], 0, n_main), finish(x1[1], x1b[1], n_main, nt - n_main)


def _rope_tables(pos):
    half = MLA_ROPE // 2
    inv = ROPE_THETA ** (-jnp.arange(half, dtype=F32) / half)
    ang = pos.astype(F32)[:, None] * inv
    reps = LANES // half
    return jnp.tile(jnp.cos(ang), (1, reps)), jnp.tile(jnp.sin(ang), (1, reps))


def _pack_layer0_weights(w_in0, gla_w_g2, gla_b_g, mla_q_norm_g, mla_kv_norm_g, mla_w_uq, mla_w_uk, mla_w_uv):
    d = w_in0.shape[0]
    hk, hv = GLA_HEADS * GLA_DK, GLA_HEADS * GLA_DV
    o_q, o_k, o_v, o_r = 0, hk, 2 * hk, 2 * hk + hv
    o_a = o_r + hv
    o_cq = o_a + GLA_RANK
    o_ckv = o_cq + MLA_Q_LORA
    o_kr = o_ckv + MLA_KV_LORA
    half = MLA_ROPE // 2
    zeros = lambda n: jnp.zeros((d, n), w_in0.dtype)
    kr1, kr2 = w_in0[:, o_kr:o_kr + half], w_in0[:, o_kr + half:o_kr + MLA_ROPE]
    w_in = jnp.concatenate([
        w_in0[:, o_q:o_a], w_in0[:, o_cq:o_kr],
        kr1, kr2, zeros(LANES - MLA_ROPE),
        kr2, kr1, zeros(LANES - MLA_ROPE),
        w_in0[:, o_a:o_cq], zeros(LANES - GLA_RANK)], axis=1).astype(BF16)
    assert w_in.shape[1] == _C_END
    w_g2 = jnp.concatenate([gla_w_g2, jnp.zeros((LANES - GLA_RANK, hk), gla_w_g2.dtype)], axis=0)
    uq = mla_w_uq.reshape(MLA_Q_LORA, MLA_HEADS, MLA_NOPE + MLA_ROPE)
    w_uq = jnp.concatenate([uq[:, :, :MLA_NOPE].reshape(MLA_Q_LORA, -1),
                            uq[:, :, MLA_NOPE:MLA_NOPE + half].reshape(MLA_Q_LORA, -1),
                            uq[:, :, MLA_NOPE + half:].reshape(MLA_Q_LORA, -1)], axis=1).astype(BF16)
    eye = jnp.eye(MLA_HEADS, dtype=mla_w_uk.dtype)
    w_abs = jnp.einsum('chn,hg->hngc', mla_w_uk, eye).reshape(MLA_HEADS * MLA_NOPE, MLA_HEADS * MLA_KV_LORA).astype(BF16)
    w_uvbd = jnp.einsum('chv,hg->hcgv', mla_w_uv, eye).reshape(MLA_HEADS * MLA_KV_LORA, MLA_HEADS * MLA_V).astype(BF16)
    place = np.zeros((2 * LANES, MLA_HEADS * LANES), np.float32)
    for h in range(MLA_HEADS):
        for j in range(half):
            place[h * half + j, h * LANES + j] = 1.0
            place[LANES + h * half + j, h * LANES + half + j] = 1.0
    return dict(w_in=w_in, w_g2=w_g2, b_g=gla_b_g.reshape(1, hk), q_norm=mla_q_norm_g.reshape(1, -1),
                kv_norm=mla_kv_norm_g.reshape(1, -1), w_uq=w_uq, w_abs=w_abs,
                place=jnp.asarray(place, BF16)), w_uvbd


def _pad_rows(x, n):
    return jnp.pad(x, ((0, n - x.shape[0]),) + ((0, 0),) * (x.ndim - 1))


def _kv_tiles(past, own):
    kvb = MLA_KV_TILE
    pad = lambda a: jnp.pad(a, ((0, 0), (0, -a.shape[1] % kvb), (0, 0)))
    kv = jnp.concatenate([pad(past), pad(own)], axis=1) if past.shape[1] else pad(own)
    b, n, w = kv.shape
    kt = kv.reshape(b, n // kvb, kvb, w).transpose(0, 1, 3, 2)
    return kt, kv


def kernel(x_prompt, x_sample, cache_mla_ckv, cache_mla_krope, state_gla, cache_pool, meta_tokens, w_in0, gla_w_g2, gla_b_g, gla_norm_g, mla_q_norm_g, mla_kv_norm_g, mla_w_uq, mla_w_uk, mla_w_uv, w_out0, pool_w, pool_scale, ln_g, ln_b, moe_router_w, moe_router_b, moe_w_gate, moe_w_up, moe_w_down, moe_sh_gate, moe_sh_up, moe_sh_down):
    bp, sp, d = x_prompt.shape
    bs, ss, _ = x_sample.shape
    n_meta = meta_tokens.shape[0]
    past_len = cache_mla_ckv.shape[1] - n_meta
    tp, tsm = bp * sp, bs * ss
    assert sp % MLA_Q_TILE == 0 and sp % POOL_TILE == 0 and sp % GLA_BLOCK == 0 and sp % CHUNK == 0
    assert ss <= CHUNK and past_len % CHUNK == 0 and n_meta <= CHUNK and ss <= GLA_BLOCK and n_meta <= GLA_BLOCK
    assert n_meta == POOL_MAX and ss >= POOL_MAX and tp % TOK_TILE == 0
    t_all = -(-(tp + tsm + n_meta) // TOK_TILE) * TOK_TILE
    o_s, o_m = tp, tp + tsm

    x_parts = (x_prompt.reshape(tp, d),
               _pad_rows(jnp.concatenate([x_sample.reshape(tsm, d), meta_tokens], axis=0), t_all - tp))
    pos = jnp.concatenate([jnp.tile(n_meta + jnp.arange(sp), bp), jnp.tile(n_meta + past_len + jnp.arange(ss), bs),
                           jnp.arange(n_meta), jnp.zeros((t_all - o_m - n_meta,), jnp.int32)])
    cos_t, sin_t = _rope_tables(pos)
    w0, w_uvbd = _pack_layer0_weights(w_in0, gla_w_g2, gla_b_g, mla_q_norm_g, mla_kv_norm_g, mla_w_uq, mla_w_uk, mla_w_uv)

    q, k, v, sr, lg, q_all, ckv, kr, k_all = _inproj(*x_parts, cos_t, sin_t, w0)

    tables = _gla_tables(GLA_BLOCK)
    g_gla = gla_norm_g.reshape(1, GLA_DV)
    c = GLA_BLOCK
    hk = GLA_HEADS * GLA_DK

    def small(a):
        sm = jnp.pad(a[o_s:o_m].reshape(bs, ss, -1), ((0, 0), (0, c - ss), (0, 0)))
        me = jnp.pad(a[o_m:o_m + n_meta], ((0, c - n_meta), (0, 0)))[None]
        return jnp.concatenate([sm, me], axis=0).reshape((bs + 1) * c, -1)

    s0_small = jnp.concatenate([state_gla.reshape(bs, hk, GLA_DV), jnp.zeros((1, hk, GLA_DV), F32)], axis=0)
    og_small, st_small = _gla(small(q), small(k), small(v), small(lg), small(sr), s0_small, g_gla, tables, bs + 1, 1)
    s0_prompt = jnp.broadcast_to(st_small[bs:], (bp, hk, GLA_DV))
    og_prompt, st_prompt = _gla(q, k, v, lg, sr, s0_prompt, g_gla, tables, bp, sp // c)
    og_small = og_small.reshape(bs + 1, c, -1)
    o_gla = (og_prompt, _pad_rows(jnp.concatenate([og_small[:bs, :ss].reshape(tsm, -1), og_small[bs, :n_meta]], axis=0),
                                  t_all - tp))

    k_meta = k_all[o_m:o_m + n_meta]
    kt_m, kv_m = _kv_tiles(jnp.zeros((1, 0, MLA_QK), BF16), k_meta[None])
    ol_meta = _mla(q_all[o_m:o_m + n_meta], kt_m, kv_m, 1, 1, n_meta, 0, 0, n_meta)
    cache = jnp.concatenate([cache_mla_ckv, cache_mla_krope,
                             jnp.zeros(cache_mla_ckv.shape[:2] + (MLA_QK - MLA_KV_LORA - MLA_ROPE,), F32)], axis=-1).astype(BF16)
    kt_s, kv_s = _kv_tiles(cache, k_all[o_s:o_m].reshape(bs, ss, MLA_QK))
    ol_sample = _mla(q_all[o_s:o_m], kt_s, kv_s, bs, 1, ss, 0, n_meta + past_len, ss)
    kt_p, kv_p = _kv_tiles(jnp.broadcast_to(k_meta[None], (bp, n_meta, MLA_QK)), k_all[:tp].reshape(bp, sp, MLA_QK))
    ol_prompt = _mla(q_all, kt_p, kv_p, bp, sp // MLA_Q_TILE, MLA_Q_TILE, 0, n_meta, sp)
    o_lat = (ol_prompt, _pad_rows(jnp.concatenate([ol_sample, ol_meta], axis=0), t_all - tp))

    ln = lambda l, j: (ln_g[l, j].reshape(1, d), ln_b[l, j].reshape(1, d))
    rw = lambda l: moe_router_w[l].T
    moe_w = lambda l: (l, moe_w_gate, moe_w_up, moe_w_down,
                       moe_sh_gate[l].astype(BF16), moe_sh_up[l].astype(BF16), moe_sh_down[l].astype(BF16))
    x1, x1b, logits = _outproj(o_gla, o_lat, x_parts, w_uvbd, w_out0.astype(BF16), *ln(0, 0), rw(0))
    x2, _ = _moe(x1, x1b, logits, moe_router_b[0], *moe_w(0), *ln(0, 1))

    pm = POOL_MAX
    zrow = jnp.zeros((1, d), F32)
    hist_p = jnp.broadcast_to(jnp.concatenate([zrow, x2[o_m + n_meta - (pm - 1):o_m + n_meta]], axis=0)[None], (bp, pm, d))
    hist_s = jnp.concatenate([jnp.zeros((bs, 1, d), F32), cache_pool], axis=1)
    pool_wb = pool_w.astype(BF16)
    ps = pool_scale.reshape(1, d)
    t1 = -(-(tp + tsm) // TOK_TILE) * TOK_TILE
    y1p = _pool(x2, hist_p, pool_wb, ps, *ln(1, 0), rw(1), bp, sp // POOL_TILE, POOL_TILE, 0)
    y1s = _pool(x2[o_s:o_m], hist_s, pool_wb, ps, *ln(1, 0), rw(1), bs, 1, ss, 0)
    x3, x3b = [(a, _pad_rows(b_, t1 - tp)) for a, b_ in zip(y1p[:2], y1s[:2])]
    by_token = lambda lt: lt.transpose(1, 0, 2).reshape(N_EXPERTS, -1)
    logits1 = jnp.pad(jnp.concatenate([by_token(y1p[2]), by_token(y1s[2])], axis=1), ((0, 0), (0, t1 - tp - tsm)))
    logits1 = logits1.reshape(N_EXPERTS, t1 // TOK_TILE, TOK_TILE).transpose(1, 0, 2)
    (x4_prompt, _), (x4_rest, _) = _moe(x3, x3b, logits1, moe_router_b[1], *moe_w(1), *ln(1, 1))

    y_prompt = x4_prompt.reshape(bp, sp, d)
    y_sample = x4_rest[:tsm].reshape(bs, ss, d)
    ckv_meta, kr_meta = ckv[o_m:o_m + n_meta], kr[o_m:o_m + n_meta, :MLA_ROPE]
    p_ckv = jnp.concatenate([jnp.broadcast_to(ckv_meta[None], (bp, n_meta, MLA_KV_LORA)),
                             ckv[:tp].reshape(bp, sp, MLA_KV_LORA)], axis=1)
    p_kr = jnp.concatenate([jnp.broadcast_to(kr_meta[None], (bp, n_meta, MLA_ROPE)),
                            kr[:tp, :MLA_ROPE].reshape(bp, sp, MLA_ROPE)], axis=1)
    p_gla = st_prompt.reshape(bp, GLA_HEADS, GLA_DK, GLA_DV)
    p_pool = jnp.stack([x2[(b_ + 1) * sp - (pm - 1):(b_ + 1) * sp] for b_ in range(bp)])
    s_ckv = ckv[o_s:o_m].reshape(bs, ss, MLA_KV_LORA)
    s_kr = kr[o_s:o_m, :MLA_ROPE].reshape(bs, ss, MLA_ROPE)
    s_gla = st_small[:bs].reshape(bs, GLA_HEADS, GLA_DK, GLA_DV)
    s_pool = x2[o_s:o_m].reshape(bs, ss, d)[:, ss - (pm - 1):]
    return (y_prompt, y_sample, p_ckv, p_kr, p_gla, p_pool, s_ckv, s_kr, s_gla, s_pool)
```

```python
import functools
import math

import jax
import jax.numpy as jnp
import numpy as np
from jax import lax
from jax.experimental import pallas as pl
from jax.experimental.pallas import tpu as pltpu

F32 = jnp.float32
BF16 = jnp.bfloat16

CHUNK = 64
DEPTH = 2
ALPHA = (2 * DEPTH) ** 0.25
LN_EPS = 1e-5
RMS_EPS = 1e-6
NEG_INF = -1e30
GLA_HEADS = 4
GLA_DK = 64
GLA_DV = 128
GLA_RANK = 16
GLA_TAU = 16.0
MLA_HEADS = 8
MLA_Q_LORA = 256
MLA_KV_LORA = 128
MLA_NOPE = 64
MLA_ROPE = 32
MLA_V = 64
MLA_SCALE = (MLA_NOPE + MLA_ROPE) ** -0.5
ROPE_THETA = 10000.0
POOL_WINDOWS = (2, 4, 8, 16)
POOL_MAX = 16
N_EXPERTS = 64
TOP_K = 8
N_GROUPS = 8
TOPK_GROUPS = 4
ROUTED_SCALE = 2.5

LANES = 128
MXU_DIM = 256

TOK_TILE = 512
GLA_BLOCK = 128
MLA_Q_TILE = 256
MLA_KV_TILE = 256
MLA_QK = 256
MLA_ROW_STRIP = 128
LOG2E = math.log2(math.e)
EXPERT_TILE = 1024
EXPERT_SUBTILE = 512
POOL_TILE = 512
SEG_ROWS = 16
BIG_PIECE = 4
DRAIN_UNITS = (64, 16, 4, 1)
ROUTE_TILE = 512
SORT_ROWS = 5120
K_BIG = (TOP_K * ROUTE_TILE + N_EXPERTS * (SEG_ROWS - 1)) // (SEG_ROWS * BIG_PIECE)
K_SMALL = N_EXPERTS * (BIG_PIECE - 1)
PACK_BITS = 9
NO_ROW = 8191.0
DISPATCH_STRIP = 512
COMBINE_STRIP = 1024
VMEM_LIMIT = 56 * 1024 * 1024


def _params(sem, vmem=VMEM_LIMIT):
    return pltpu.CompilerParams(dimension_semantics=sem, vmem_limit_bytes=vmem)


def _const_spec(shape):
    nd = len(shape)
    return pl.BlockSpec(shape, lambda *_: (0,) * nd)


def _split3(x):
    hi = x.astype(BF16)
    r = x - hi.astype(F32)
    mid = r.astype(BF16)
    lo = (r - mid.astype(F32)).astype(BF16)
    return hi, mid, lo


def _dot(a, b):
    return jnp.dot(a, b, preferred_element_type=F32)


def _dot_f32(a, b):
    a_hi = a.astype(BF16)
    a_lo = (a - a_hi.astype(F32)).astype(BF16)
    b_hi = b.astype(BF16)
    b_lo = (b - b_hi.astype(F32)).astype(BF16)
    return _dot(a_hi, b_hi) + (_dot(a_lo, b_hi) + _dot(a_hi, b_lo))


def _dot_f32_nt(a, b):
    nt = lambda u, v: lax.dot_general(u, v, (((1,), (1,)), ((), ())), preferred_element_type=F32)
    a_hi = a.astype(BF16)
    a_lo = (a - a_hi.astype(F32)).astype(BF16)
    b_hi = b.astype(BF16)
    b_lo = (b - b_hi.astype(F32)).astype(BF16)
    return nt(a_hi, b_hi) + (nt(a_lo, b_hi) + nt(a_hi, b_lo))


def _silu(x):
    return x * (1.0 / (1.0 + jnp.exp(-x)))


def _layer_norm(x, g, b):
    mu = jnp.mean(x, axis=-1, keepdims=True)
    xc = x - mu
    var = jnp.mean(xc * xc, axis=-1, keepdims=True)
    return xc * lax.rsqrt(var + LN_EPS) * g + b


def _rms(x, g):
    return x * lax.rsqrt(jnp.mean(x * x, axis=-1, keepdims=True) + RMS_EPS) * g


_C_Q, _C_K, _C_V, _C_R, _C_CQ, _C_CKV, _C_KR, _C_KRS, _C_A, _C_END = (
    0, 256, 512, 1024, 1536, 1792, 1920, 2048, 2176, 2304)


def _two_part_specs(tm, n, n_main):
    return [pl.BlockSpec((tm, n), lambda i, *_: (jnp.minimum(i, n_main - 1), 0)),
            pl.BlockSpec((tm, n), lambda i, *_: (jnp.maximum(i - n_main, 0), 0))]


def _pick(main_ref, rest_ref, n_main):
    return jnp.where(pl.program_id(0) < n_main, main_ref[...], rest_ref[...])


def _inproj_kernel(xm_ref, xr_ref, cos_ref, sin_ref, w_ref, wg2_ref, bg_ref, qn_ref, kvn_ref, wuq_ref, wabs_ref,
                   place_ref, q_ref, k_ref, v_ref, sr_ref, lg_ref, qall_ref, ckv_ref, kr_ref, kall_ref, *, n_main):
    h = _dot(_pick(xm_ref, xr_ref, n_main).astype(BF16), w_ref[...])
    q_ref[...] = (h[:, _C_Q:_C_K] * GLA_DK ** -0.5).astype(BF16)
    k_ref[...] = h[:, _C_K:_C_V].astype(BF16)
    v_ref[...] = h[:, _C_V:_C_R].astype(BF16)
    sr_ref[...] = _silu(h[:, _C_R:_C_CQ]).astype(BF16)
    z = _dot_f32(h[:, _C_A:_C_END], wg2_ref[...]) + bg_ref[...]
    lg_ref[...] = (jnp.minimum(z, 0.0) - jnp.log(1.0 + jnp.exp(-jnp.abs(z)))) * (1.0 / GLA_TAU)
    cqn = _rms(h[:, _C_CQ:_C_CKV], qn_ref[...]).astype(BF16)
    qh = _dot(cqn, wuq_ref[...])
    n_nope = MLA_HEADS * MLA_NOPE
    cos, sin = cos_ref[...], sin_ref[...]
    x1, x2 = qh[:, n_nope:n_nope + LANES], qh[:, n_nope + LANES:]
    qscale = MLA_SCALE * LOG2E
    rot = (jnp.concatenate([x1 * cos - x2 * sin, x1 * sin + x2 * cos], axis=1) * qscale).astype(BF16)
    qlat = (_dot(qh[:, :n_nope].astype(BF16), wabs_ref[...]) * qscale).astype(BF16)
    qrope = _dot(rot, place_ref[...]).astype(BF16)
    for hd in range(MLA_HEADS):
        qall_ref[:, MLA_QK * hd:MLA_QK * hd + LANES] = qlat[:, LANES * hd:LANES * (hd + 1)]
        qall_ref[:, MLA_QK * hd + LANES:MLA_QK * (hd + 1)] = qrope[:, LANES * hd:LANES * (hd + 1)]
    ckv = _rms(h[:, _C_CKV:_C_KR], kvn_ref[...])
    lane = lax.broadcasted_iota(jnp.int32, (1, LANES), 1)
    sgn = jnp.where(lane < MLA_ROPE // 2, -1.0, 1.0)
    kr = h[:, _C_KR:_C_KRS] * cos + h[:, _C_KRS:_C_A] * (sin * sgn)
    ckv_ref[...] = ckv
    kr_ref[...] = kr
    kall_ref[:, :LANES] = ckv.astype(BF16)
    kall_ref[:, LANES:] = kr.astype(BF16)


def _inproj(x_main, x_rest, cos_t, sin_t, w):
    t = x_main.shape[0] + x_rest.shape[0]
    d = x_main.shape[1]
    tm = TOK_TILE
    n_main = x_main.shape[0] // tm
    row = lambda n: pl.BlockSpec((tm, n), lambda i: (i, 0))
    consts = [w['w_in'], w['w_g2'], w['b_g'], w['q_norm'], w['kv_norm'], w['w_uq'], w['w_abs'], w['place']]
    outs = [(GLA_HEADS * GLA_DK, BF16), (GLA_HEADS * GLA_DK, BF16), (GLA_HEADS * GLA_DV, BF16),
            (GLA_HEADS * GLA_DV, BF16), (GLA_HEADS * GLA_DK, F32), (MLA_HEADS * MLA_QK, BF16),
            (LANES, F32), (LANES, F32), (MLA_QK, BF16)]
    return pl.pallas_call(
        functools.partial(_inproj_kernel, n_main=n_main),
        grid=(t // tm,),
        in_specs=_two_part_specs(tm, d, n_main) + [row(LANES), row(LANES)] + [_const_spec(c.shape) for c in consts],
        out_specs=[row(n) for n, _ in outs],
        out_shape=[jax.ShapeDtypeStruct((t, n), dt) for n, dt in outs],
        compiler_params=_params(("parallel",)),
        name="inproj",
    )(x_main, x_rest, cos_t, sin_t, *consts)


def _gla_tables(c):
    levels = int(math.log2(c))
    assert 1 << levels == c
    t = np.arange(c)[:, None]
    u = np.arange(c)[None, :]
    mats, masks = [], []
    for l in range(levels):
        half = 1 << l
        base = (t // half) * half
        upper = ((t >> l) & 1) == 1
        a_q = (u >= base) & (u <= t)
        a_k = (u > t) & (u <= base + half - 1)
        mats.append(np.where(upper, a_q, a_k))
        same = (t >> (l + 1)) == (u >> (l + 1))
        masks.append(same & upper & (((u >> l) & 1) == 0))
    mats.append(u <= t)
    mats.append(u > t)
    masks.append(t == u)
    amat = np.concatenate(mats, axis=0).astype(np.float32)
    mask = np.stack([np.tile(m, (GLA_HEADS, 1)) for m in masks]).astype(np.float32)
    return jnp.asarray(amat, BF16), jnp.asarray(mask, F32), levels


def _gla_kernel(q_ref, k_ref, v_ref, lg_ref, sr_ref, s0_ref, amat_ref, mask_ref, g_ref,
                o_ref, sout_ref, state, *, levels):
    c = q_ref.shape[0]
    hk = GLA_HEADS * GLA_DK
    step = pl.program_id(1)

    @pl.when(step == 0)
    def _():
        state[...] = s0_ref[0]

    lg = lg_ref[...]
    parts = _split3(lg)
    e3 = _dot(amat_ref[...], jnp.concatenate(parts, axis=1))
    ex = jnp.exp(e3[:, :hk] + e3[:, hk:2 * hk] + e3[:, 2 * hk:])
    q = q_ref[...].astype(F32)
    k = k_ref[...].astype(F32)
    v = v_ref[...]
    head = lax.broadcasted_iota(jnp.int32, (1, hk), 1) // GLA_DK

    def per_head(xf):
        return jnp.concatenate([jnp.where(head == h, xf, 0.0) for h in range(GLA_HEADS)], axis=0).astype(BF16)

    att = jnp.zeros((GLA_HEADS * c, c), F32)
    for l in range(levels + 1):
        if l < levels:
            el = ex[l * c:(l + 1) * c]
            ql, kl = q * el, k * el
        else:
            ql, kl = q, k
        a = lax.dot_general(per_head(ql), kl.astype(BF16), (((1,), (1,)), ((), ())), preferred_element_type=F32)
        att = att + a * mask_ref[l]
    s_prev = state[...]
    o_inter = _dot(per_head(q * ex[levels * c:(levels + 1) * c]), s_prev.astype(BF16))
    att = att.astype(BF16)
    g = g_ref[...]
    for h in range(GLA_HEADS):
        o = o_inter[h * c:(h + 1) * c] + _dot(att[h * c:(h + 1) * c], v[:, GLA_DV * h:GLA_DV * (h + 1)])
        o = _rms(o, g)
        o_ref[:, GLA_DV * h:GLA_DV * (h + 1)] = (o * sr_ref[:, GLA_DV * h:GLA_DV * (h + 1)].astype(F32)).astype(BF16)
    kr = (k * ex[(levels + 1) * c:]).astype(BF16)
    upd = lax.dot_general(kr, v, (((0,), (0,)), ((), ())), preferred_element_type=F32)
    ones = jnp.ones((c, GLA_DV), BF16)
    b_last = sum(lax.dot_general(p, ones, (((0,), (0,)), ((), ())), preferred_element_type=F32) for p in parts)
    new = jnp.exp(b_last) * s_prev + jnp.concatenate(
        [upd[GLA_DK * h:GLA_DK * (h + 1), GLA_DV * h:GLA_DV * (h + 1)] for h in range(GLA_HEADS)], axis=0)
    state[...] = new

    @pl.when(step == pl.num_programs(1) - 1)
    def _():
        sout_ref[0] = new


def _gla(q, k, v, lg, sr, s0, g, tables, batch, blocks_per_seq, out_rows=None):
    amat, mask, levels = tables
    c = GLA_BLOCK
    hk, hv = GLA_HEADS * GLA_DK, GLA_HEADS * GLA_DV
    row = lambda n: pl.BlockSpec((c, n), lambda b, s: (b * blocks_per_seq + s, 0))
    st = pl.BlockSpec((1, hk, GLA_DV), lambda b, s: (b, 0, 0))
    return pl.pallas_call(
        functools.partial(_gla_kernel, levels=levels),
        grid=(batch, blocks_per_seq),
        in_specs=[row(hk), row(hk), row(hv), row(hk), row(hv), st,
                  _const_spec(amat.shape), _const_spec(mask.shape), _const_spec(g.shape)],
        out_specs=[row(hv), st],
        out_shape=[jax.ShapeDtypeStruct((out_rows or batch * blocks_per_seq * c, hv), BF16),
                   jax.ShapeDtypeStruct((batch, hk, GLA_DV), F32)],
        scratch_shapes=[pltpu.VMEM((hk, GLA_DV), F32)],
        compiler_params=_params(("parallel", "arbitrary")),
        name="gla",
    )(q, k, v, lg, sr, s0, amat, mask, g)


def _mla_kernel(q_ref, kt_ref, kv_ref, o_ref, s_sc, p_sc, linv_sc, acc_sc, *, tq, n_past, past_valid, own_valid,
                diag_chunks):
    kvb = MLA_KV_TILE
    qi = pl.program_id(1)
    qs = jnp.concatenate([q_ref[:, MLA_QK * h:MLA_QK * (h + 1)] for h in range(MLA_HEADS)], axis=0)
    m_rows = MLA_HEADS * tq
    n_vis = n_past + qi + 1

    def score(j, mask):
        s = _dot(qs, kt_ref[0, j])
        s_sc[j] = s if mask is None else jnp.where(mask, s, NEG_INF)

    def scores(lo, hi):
        def body(j, carry):
            score(j, None)
            return carry
        lax.fori_loop(lo, hi, body, 0)

    col = lax.broadcasted_iota(jnp.int32, (m_rows, kvb), 1)
    n_past_full = past_valid // kvb
    if n_past_full:
        scores(0, n_past_full)
    if n_past_full < n_past:
        score(n_past_full, col < past_valid - n_past_full * kvb)
    scores(n_past, n_past + qi)
    mask = col < own_valid - qi * kvb
    if diag_chunks:
        qrow = lax.broadcasted_iota(jnp.int32, (m_rows, kvb), 0) % tq
        mask = mask & (col // CHUNK <= qrow // CHUNK)
    score(n_past + qi, mask)

    rc = MLA_ROW_STRIP

    def strip(r, carry):
        rows = pl.ds(pl.multiple_of(r * rc, rc), rc)
        mx = lax.fori_loop(0, n_vis, lambda j, m: jnp.maximum(m, s_sc[j, rows, :]),
                           jnp.full((rc, kvb), -jnp.inf, F32))
        m = jnp.max(mx, axis=1, keepdims=True)

        def body(j, l):
            p = jnp.exp2(s_sc[j, rows, :] - m)
            p_sc[j, rows, :] = p.astype(BF16)
            return l + p
        l = lax.fori_loop(0, n_vis, body, jnp.zeros((rc, kvb), F32))
        linv_sc[rows, :] = 1.0 / jnp.sum(l, axis=1, keepdims=True)
        return carry
    lax.fori_loop(0, m_rows // rc, strip, 0)

    acc_sc[...] = jnp.zeros(acc_sc.shape, F32)

    def pv(j, carry):
        vblk = kv_ref[0, pl.ds(pl.multiple_of(j * kvb, kvb), kvb), :MLA_KV_LORA]
        acc_sc[...] += _dot(p_sc[j], vblk)
        return carry
    lax.fori_loop(0, n_vis, pv, 0)
    o = acc_sc[...] * linv_sc[...]
    for h in range(MLA_HEADS):
        o_ref[:, MLA_KV_LORA * h:MLA_KV_LORA * (h + 1)] = o[h * tq:(h + 1) * tq].astype(BF16)


def _mla(q_all, kt, kv, batch, nq, tq, q_block0, past_valid, own_valid, out_rows=None):
    n_kv = kt.shape[1]
    n_past = -(-past_valid // MLA_KV_TILE)
    assert nq == 1 or tq == MLA_KV_TILE
    assert n_kv == n_past + nq
    return pl.pallas_call(
        functools.partial(_mla_kernel, tq=tq, n_past=n_past, past_valid=past_valid, own_valid=own_valid,
                          diag_chunks=nq > 1),
        grid=(batch, nq),
        in_specs=[pl.BlockSpec((tq, MLA_HEADS * MLA_QK), lambda b, i: (q_block0 + b * nq + i, 0)),
                  pl.BlockSpec((1,) + kt.shape[1:], lambda b, i: (b, 0, 0, 0)),
                  pl.BlockSpec((1,) + kv.shape[1:], lambda b, i: (b, 0, 0))],
        out_specs=pl.BlockSpec((tq, MLA_HEADS * MLA_KV_LORA), lambda b, i: (b * nq + i, 0)),
        out_shape=jax.ShapeDtypeStruct((out_rows or batch * nq * tq, MLA_HEADS * MLA_KV_LORA), BF16),
        scratch_shapes=[pltpu.VMEM((n_kv, MLA_HEADS * tq, MLA_KV_TILE), F32),
                        pltpu.VMEM((n_kv, MLA_HEADS * tq, MLA_KV_TILE), BF16),
                        pltpu.VMEM((MLA_HEADS * tq, 1), F32),
                        pltpu.VMEM((MLA_HEADS * tq, MLA_KV_LORA), F32)],
        compiler_params=_params(("parallel", "arbitrary")),
        name="mla",
    )(q_all, kt, kv)


def _post_mixer(x, out, g, b, rw_ref, x1_ref, x1b_ref, lgt_ref):
    y = _layer_norm(ALPHA * x + out, g, b)
    x1_ref[...] = y
    x1b_ref[...] = y.astype(BF16)
    lgt_ref[0] = _dot_f32_nt(rw_ref[...], y)


def _outproj_kernel(ogm_ref, ogr_ref, olm_ref, olr_ref, xm_ref, xr_ref, wuv_ref, wo_ref, g_ref, b_ref, rw_ref,
                    x1_ref, x1b_ref, lgt_ref, *, n_main):
    og = _pick(ogm_ref, ogr_ref, n_main)
    o_mla = _dot(_pick(olm_ref, olr_ref, n_main), wuv_ref[...]).astype(BF16)
    n_gla = og.shape[1]
    out = _dot(og, wo_ref[:n_gla, :]) + _dot(o_mla, wo_ref[n_gla:, :])
    _post_mixer(_pick(xm_ref, xr_ref, n_main), out, g_ref[...], b_ref[...], rw_ref, x1_ref, x1b_ref, lgt_ref)


def _outproj(o_gla, o_lat, x, w_uvbd, w_out, g, b, rw):
    t, d = x[0].shape[0] + x[1].shape[0], x[0].shape[1]
    tm = TOK_TILE
    n_main = x[0].shape[0] // tm
    row = lambda n: pl.BlockSpec((tm, n), lambda i: (i, 0))
    consts = [w_uvbd, w_out, g, b, rw]
    parts = [*o_gla, *o_lat, *x]
    return pl.pallas_call(
        functools.partial(_outproj_kernel, n_main=n_main),
        grid=(t // tm,),
        in_specs=[s for a in (o_gla, o_lat, x) for s in _two_part_specs(tm, a[0].shape[1], n_main)]
        + [_const_spec(c.shape) for c in consts],
        out_specs=[row(d), row(d), pl.BlockSpec((1, N_EXPERTS, tm), lambda i: (i, 0, 0))],
        out_shape=[jax.ShapeDtypeStruct((t, d), F32), jax.ShapeDtypeStruct((t, d), BF16),
                   jax.ShapeDtypeStruct((t // tm, N_EXPERTS, tm), F32)],
        compiler_params=_params(("parallel",)),
        name="outproj",
    )(*parts, *consts)


def _pool_kernel(x_ref, prev_ref, hist_ref, pw_ref, ps_ref, g_ref, b_ref, rw_ref, x1_ref, x1b_ref, lgt_ref):
    ts = x_ref.shape[0]
    pm = POOL_MAX
    x = x_ref[...]
    prev = jnp.where(pl.program_id(1) == 0, hist_ref[0], prev_ref[...])
    xc = jnp.concatenate([prev, x], axis=0)
    grp = x.shape[1] // len(POOL_WINDOWS)
    outs = []
    for gi, w in enumerate(POOL_WINDOWS):
        s = xc[:, gi * grp:(gi + 1) * grp]
        span = 1
        while span < w:
            s = s + pltpu.roll(s, span, 0)
            span *= 2
        win = s[pm:]
        mix = (win * (1.0 / w) - x[:, gi * grp:(gi + 1) * grp]).astype(BF16)
        outs.append(_dot(mix, pw_ref[gi]))
    out = jnp.concatenate(outs, axis=1) * ps_ref[...]
    _post_mixer(x, out, g_ref[...], b_ref[...], rw_ref, x1_ref, x1b_ref, lgt_ref)


def _pool(x_all, hist, pool_w, pool_scale, g, b, rw, batch, tiles_per_seq, ts, tile0, out_rows=None):
    d = x_all.shape[1]
    pm = POOL_MAX
    per = ts // pm
    consts = [pool_w, pool_scale, g, b, rw]
    row = lambda n: pl.BlockSpec((ts, n), lambda bb, i: (bb * tiles_per_seq + i, 0))
    return pl.pallas_call(
        _pool_kernel,
        grid=(batch, tiles_per_seq),
        in_specs=[pl.BlockSpec((ts, d), lambda bb, i: (tile0 + bb * tiles_per_seq + i, 0)),
                  pl.BlockSpec((pm, d), lambda bb, i: (jnp.maximum((tile0 + bb * tiles_per_seq + i) * per - 1, 0), 0)),
                  pl.BlockSpec((1, pm, d), lambda bb, i: (bb, 0, 0))] + [_const_spec(c.shape) for c in consts],
        out_specs=[row(d), row(d), pl.BlockSpec((1, N_EXPERTS, ts), lambda bb, i: (bb * tiles_per_seq + i, 0, 0))],
        out_shape=[jax.ShapeDtypeStruct((out_rows or batch * tiles_per_seq * ts, d), F32),
                   jax.ShapeDtypeStruct((out_rows or batch * tiles_per_seq * ts, d), BF16),
                   jax.ShapeDtypeStruct((batch * tiles_per_seq, N_EXPERTS, ts), F32)],
        compiler_params=_params(("parallel", "arbitrary")),
        name="pool",
    )(x_all, x_all, hist, *consts)


def _experts_kernel(be_ref, nb_ref, x_ref, wg_ref, wu_ref, wd_ref, y_ref, wg_sc, wu_sc, wd_sc):
    i = pl.program_id(0)

    @pl.when((i == 0) | (be_ref[i] != be_ref[jnp.maximum(i - 1, 0)]))
    def _():
        wg_sc[...] = wg_ref[0, 0].astype(BF16)
        wu_sc[...] = wu_ref[0, 0].astype(BF16)
        wd_sc[...] = wd_ref[0, 0].astype(BF16)

    @pl.when(i < nb_ref[0])
    def _():
        sub = EXPERT_SUBTILE
        for r in range(x_ref.shape[0] // sub):
            x = x_ref[r * sub:(r + 1) * sub, :]
            hmid = (_silu(_dot(x, wg_sc[...])) * _dot(x, wu_sc[...])).astype(BF16)
            y_ref[r * sub:(r + 1) * sub, :] = _dot(hmid, wd_sc[...]).astype(BF16)

    @pl.when(pl.program_id(0) >= nb_ref[0])
    def _():
        y_ref[...] = jnp.zeros_like(y_ref)


def _experts(blk_e, n_used, x_sorted, wg, wu, wd, layer):
    p, d = x_sorted.shape
    f = wg.shape[3]
    bm = EXPERT_TILE
    blk = lambda i, be, nb: (i, 0)
    grid_spec = pltpu.PrefetchScalarGridSpec(
        num_scalar_prefetch=2,
        grid=(p // bm,),
        in_specs=[pl.BlockSpec((bm, d), lambda i, be, nb: (jnp.minimum(i, nb[0] - 1), 0)),
                  pl.BlockSpec((1, 1, d, f), lambda i, be, nb: (layer, be[i], 0, 0)),
                  pl.BlockSpec((1, 1, d, f), lambda i, be, nb: (layer, be[i], 0, 0)),
                  pl.BlockSpec((1, 1, f, d), lambda i, be, nb: (layer, be[i], 0, 0))],
        out_specs=pl.BlockSpec((bm, d), blk),
        scratch_shapes=[pltpu.VMEM((d, f), BF16), pltpu.VMEM((d, f), BF16), pltpu.VMEM((f, d), BF16)],
    )
    return pl.pallas_call(
        _experts_kernel,
        grid_spec=grid_spec,
        out_shape=jax.ShapeDtypeStruct((p, d), BF16),
        compiler_params=_params(("arbitrary",)),
        name="experts",
    )(blk_e, n_used, x_sorted, wg, wu, wd)


def _segment_copies(tile, slot, dst_ref, n16_ref, seg_ref, make, entries=N_EXPERTS):
    big = SEG_ROWS * BIG_PIECE

    def per_expert(e, carry):
        j = tile * N_EXPERTS + e
        d0, s0, n = dst_ref[j], seg_ref[j], n16_ref[j]
        n_big = n // BIG_PIECE

        def big_piece(c, carry2):
            make(pl.multiple_of(d0 + c * big, SEG_ROWS), pl.multiple_of(s0 + c * big, SEG_ROWS), big, slot).start()
            return carry2
        lax.fori_loop(0, n_big, big_piece, 0)
        d1, s1 = d0 + n_big * big, s0 + n_big * big

        def small_piece(c, carry2):
            make(pl.multiple_of(d1 + c * SEG_ROWS, SEG_ROWS), pl.multiple_of(s1 + c * SEG_ROWS, SEG_ROWS),
                 SEG_ROWS, slot).start()
            return carry2
        lax.fori_loop(0, n - n_big * BIG_PIECE, small_piece, 0)
        return carry
    lax.fori_loop(0, entries, per_expert, 0)


def _start_pieces(tile, slot, big_ref, small_ref, cnt_ref, make):
    def run(tab_ref, kmax, n, rows):
        def body(k, carry):
            v = tab_ref[tile * kmax + k]
            make(pl.multiple_of((v >> PACK_BITS) * SEG_ROWS, SEG_ROWS),
                 pl.multiple_of((v & ((1 << PACK_BITS) - 1)) * SEG_ROWS, SEG_ROWS), rows, slot).start()
            return carry
        lax.fori_loop(0, n, body, 0)
    run(big_ref, K_BIG, cnt_ref[2 * tile], SEG_ROWS * BIG_PIECE)
    run(small_ref, K_SMALL, cnt_ref[2 * tile + 1], SEG_ROWS)


def _tile_rows(tile, cnt_ref):
    return cnt_ref[2 * tile] * (SEG_ROWS * BIG_PIECE) + cnt_ref[2 * tile + 1] * SEG_ROWS


def _piece_tables(dst, n16, seg):
    nt, ne = n16.shape
    big_rows = SEG_ROWS * BIG_PIECE
    n_big, n_small = n16 // BIG_PIECE, n16 % BIG_PIECE

    def flat(counts, first_dst, first_src, step, kmax):
        cum = jnp.cumsum(counts, axis=1)[:, None, :]
        before = cum - counts[:, None, :]
        k = jnp.arange(kmax, dtype=jnp.int32)[None, :, None]
        mine = (before <= k) & (k < cum)
        off = step * (k - before)
        d = jnp.sum(jnp.where(mine, first_dst[:, None, :] + off, 0), axis=2)
        s = jnp.sum(jnp.where(mine, first_src[:, None, :] + off, 0), axis=2)
        return (((d // SEG_ROWS) << PACK_BITS) | (s // SEG_ROWS)).reshape(-1).astype(jnp.int32), cum[:, 0, -1]
    big, cnt_big = flat(n_big, dst, seg, big_rows, K_BIG)
    small, cnt_small = flat(n_small, dst + n_big * big_rows, seg + n_big * big_rows, SEG_ROWS, K_SMALL)
    return big, small, jnp.stack([cnt_big, cnt_small], axis=1).reshape(-1).astype(jnp.int32)


def _drain(rows, slot, make):
    left = rows // SEG_ROWS
    for unit in DRAIN_UNITS:
        n = left // unit

        def body(c, carry, unit=unit):
            make(0, 0, unit * SEG_ROWS, slot).wait()
            return carry
        lax.fori_loop(0, n, body, 0)
        left = left - n * unit


def _ffn_out_kernel(big_ref, small_ref, cnt_ref, x_ref, xb_ref, pw_ref, ys_ref,
                    sg_ref, su_ref, sd_ref, g_ref, b_ref, y_ref, yb_ref, ybuf, racc, sem, *, tile0):
    i = pl.program_id(0)
    nt = pl.num_programs(0)
    slot = i % 2
    tile = i + tile0

    def make(d0, s0, rows, sl):
        return pltpu.make_async_copy(ys_ref.at[pl.ds(d0, rows)], ybuf.at[sl, pl.ds(s0, rows)], sem.at[sl])

    @pl.when(i == 0)
    def _():
        ybuf[...] = jnp.zeros_like(ybuf)
        _start_pieces(tile, 0, big_ref, small_ref, cnt_ref, make)

    @pl.when(i + 1 < nt)
    def _():
        _start_pieces(tile + 1, 1 - slot, big_ref, small_ref, cnt_ref, make)

    xb = xb_ref[...]
    hmid = (_silu(_dot(xb, sg_ref[...])) * _dot(xb, su_ref[...])).astype(BF16)
    shared = _dot(hmid, sd_ref[...])
    used = _tile_rows(tile, cnt_ref)
    _drain(used, slot, make)

    racc[...] = jnp.zeros_like(racc)
    rs = COMBINE_STRIP
    for s in range(ybuf.shape[1] // rs):
        @pl.when(s * rs < used)
        def _():
            racc[...] += lax.dot_general(pw_ref[0, s * rs:(s + 1) * rs, :], ybuf[slot, s * rs:(s + 1) * rs, :],
                                         (((0,), (0,)), ((), ())), preferred_element_type=F32)
    ffn = racc[...] + shared
    y = _layer_norm(ALPHA * x_ref[...] + ffn, g_ref[...], b_ref[...])
    y_ref[...] = y
    yb_ref[...] = y.astype(BF16)


def _ffn_out(big, small, cnt, x1, x1b, pw, y_sorted, sg, su, sd, g, b, tile0=0, n_tiles=None, row0=None):
    d = x1.shape[1]
    nt, _, tm = pw.shape
    n_tiles = nt - tile0 if n_tiles is None else n_tiles
    row0 = tile0 if row0 is None else row0
    row_in = pl.BlockSpec((tm, d), lambda i, *_: (i + row0, 0))
    row_out = pl.BlockSpec((tm, d), lambda i, *_: (i, 0))
    consts = [sg, su, sd, g, b]
    grid_spec = pltpu.PrefetchScalarGridSpec(
        num_scalar_prefetch=3,
        grid=(n_tiles,),
        in_specs=[row_in, row_in, pl.BlockSpec((1,) + pw.shape[1:], lambda i, *_: (i + tile0, 0, 0)),
                  pl.BlockSpec(memory_space=pl.ANY)] + [_const_spec(c.shape) for c in consts],
        out_specs=[row_out, row_out],
        scratch_shapes=[pltpu.VMEM((2, SORT_ROWS, d), BF16), pltpu.VMEM((tm, d), F32), pltpu.SemaphoreType.DMA((2,))],
    )
    return pl.pallas_call(
        functools.partial(_ffn_out_kernel, tile0=tile0),
        grid_spec=grid_spec,
        out_shape=[jax.ShapeDtypeStruct((n_tiles * tm, d), F32), jax.ShapeDtypeStruct((n_tiles * tm, d), BF16)],
        compiler_params=_params(("arbitrary",)),
        name="ffn_out",
    )(big, small, cnt, x1, x1b, pw, y_sorted, *consts)


def _dispatch_kernel(big_ref, small_ref, cnt_ref, tdst_ref, tn16_ref, xm_ref, xr_ref, rid_ref, w_ref, segr_ref, c16r_ref,
                     xs_ref, pw_ref, buf, zbuf, sem, *, n_main):
    i = pl.program_id(0)
    nt = pl.num_programs(0)
    slot = i % 2
    tm = xm_ref.shape[0]

    def make(d0, s0, rows, sl):
        return pltpu.make_async_copy(buf.at[sl, pl.ds(s0, rows)], xs_ref.at[pl.ds(d0, rows)], sem.at[sl])

    @pl.when(i >= 2)
    def _():
        _drain(_tile_rows(i - 2, cnt_ref), slot, make)

    x = _pick(xm_ref, xr_ref, n_main)
    rid = rid_ref[0]
    wb = w_ref[0].astype(BF16)
    used = _tile_rows(i, cnt_ref)
    rs = DISPATCH_STRIP
    hs = MXU_DIM
    for s in range(buf.shape[1] // rs):
        @pl.when(s * rs < used)
        def _():
            hits = []
            for h0 in range(s * rs, (s + 1) * rs, hs):
                r = lax.broadcasted_iota(jnp.int32, (hs, N_EXPERTS), 0) + h0
                owner = jnp.where((r >= segr_ref[0]) & (r < segr_ref[0] + c16r_ref[0]), 1.0, 0.0).astype(BF16)
                local = jnp.where((rid > h0) & (rid <= h0 + hs), rid - h0, 0.0).astype(BF16)
                at_row = _dot(owner, jnp.concatenate([local, wb], axis=1))
                hit = at_row[:, :tm] == (lax.broadcasted_iota(jnp.int32, (hs, tm), 0) + 1).astype(F32)
                hits.append(jnp.where(hit, 1.0, 0.0).astype(BF16))
                pw_ref[0, h0:h0 + hs, :] = jnp.where(hit, at_row[:, tm:], 0.0).astype(BF16)
            buf[slot, s * rs:(s + 1) * rs, :] = _dot(jnp.concatenate(hits, axis=0), x).astype(BF16)

        @pl.when(s * rs >= used)
        def _():
            pw_ref[0, s * rs:(s + 1) * rs, :] = jnp.zeros((rs, tm), BF16)
    _start_pieces(i, slot, big_ref, small_ref, cnt_ref, make)

    @pl.when(i == nt - 1)
    def _():
        zbuf[...] = jnp.zeros_like(zbuf)

        def zmake(d0, s0, rows, sl):
            return pltpu.make_async_copy(zbuf.at[pl.ds(0, rows)], xs_ref.at[pl.ds(d0, rows)], sem.at[sl])
        _segment_copies(0, 2, tdst_ref, tn16_ref, tdst_ref, zmake, entries=N_EXPERTS + 1)
        tail = lax.fori_loop(0, N_EXPERTS + 1, lambda e, acc: acc + tn16_ref[e], 0) * SEG_ROWS
        _drain(tail, 2, make)

        @pl.when(i >= 1)
        def _():
            _drain(_tile_rows(i - 1, cnt_ref), 1 - slot, make)
        _drain(used, slot, make)


def _dispatch(big, small, cnt, tail_dst, tail_n16, xb, rid, wdense, segr, c16r, n_rows):
    d = xb[0].shape[1]
    nt, ne, tm = rid.shape
    n_main = xb[0].shape[0] // tm
    per_tile = lambda a: pl.BlockSpec((1,) + a.shape[1:], lambda i, *_: (i, 0, 0))
    grid_spec = pltpu.PrefetchScalarGridSpec(
        num_scalar_prefetch=5,
        grid=(nt,),
        in_specs=_two_part_specs(tm, d, n_main) + [per_tile(rid), per_tile(wdense), per_tile(segr), per_tile(c16r)],
        out_specs=[pl.BlockSpec(memory_space=pl.ANY), pl.BlockSpec((1, SORT_ROWS, tm), lambda i, *_: (i, 0, 0))],
        scratch_shapes=[pltpu.VMEM((2, SORT_ROWS, d), BF16), pltpu.VMEM((SEG_ROWS * BIG_PIECE, d), BF16),
                        pltpu.SemaphoreType.DMA((3,))],
    )
    return pl.pallas_call(
        functools.partial(_dispatch_kernel, n_main=n_main),
        grid_spec=grid_spec,
        out_shape=[jax.ShapeDtypeStruct((n_rows, d), BF16), jax.ShapeDtypeStruct((nt, SORT_ROWS, tm), BF16)],
        compiler_params=_params(("arbitrary",)),
        name="dispatch",
    )(big, small, cnt, tail_dst, tail_n16, *xb, rid, wdense, segr, c16r)


def _top_rows(vals, n, k, ids):
    sel = jnp.zeros(vals.shape, jnp.bool_)
    for _ in range(k):
        m = jnp.max(vals, axis=0, keepdims=True)
        hit = ids == jnp.min(jnp.where(vals == m, ids, n), axis=0, keepdims=True)
        sel = sel | hit
        vals = jnp.where(hit, -jnp.inf, vals)
    return sel


def _router_kernel(lg_ref, b_ref, tri_ref, low_ref, rid_ref, w_ref, tab_ref, carry):
    @pl.when(pl.program_id(0) == 0)
    def _():
        carry[...] = jnp.zeros_like(carry)

    tm = lg_ref.shape[2]
    gs = N_EXPERTS // N_GROUPS
    scores = 1.0 / (1.0 + jnp.exp(-lg_ref[0]))
    biased = scores + b_ref[...]
    member = lax.broadcasted_iota(jnp.int32, (gs, tm), 0)
    group_score = []
    for g in range(N_GROUPS):
        blk = biased[g * gs:(g + 1) * gs]
        m1 = jnp.max(blk, axis=0, keepdims=True)
        first = jnp.min(jnp.where(blk == m1, member, gs), axis=0, keepdims=True)
        m2 = jnp.max(jnp.where(member == first, -jnp.inf, blk), axis=0, keepdims=True)
        group_score.append(m1 + m2)
    group_score = jnp.concatenate(group_score, axis=0)
    gid = lax.broadcasted_iota(jnp.int32, (N_GROUPS, tm), 0)
    group_ok = _top_rows(group_score, N_GROUPS, TOPK_GROUPS, gid)
    masked = jnp.concatenate([jnp.where(group_ok[g:g + 1], biased[g * gs:(g + 1) * gs], -jnp.inf)
                              for g in range(N_GROUPS)], axis=0)
    eid = lax.broadcasted_iota(jnp.int32, (N_EXPERTS, tm), 0)
    sel = _top_rows(masked, N_EXPERTS, TOP_K, eid)
    self = jnp.where(sel, 1.0, 0.0)
    wsel = self * scores
    wts = wsel / jnp.sum(wsel, axis=0, keepdims=True) * ROUTED_SCALE
    selb = self.astype(BF16)
    before = _dot(selb, tri_ref[...])
    pieces = jnp.floor((jnp.sum(self, axis=1, keepdims=True) + (SEG_ROWS - 1.0)) * (1.0 / SEG_ROWS))
    seg = SEG_ROWS * _dot(low_ref[...], jnp.broadcast_to(pieces, (N_EXPERTS, LANES)).astype(BF16))
    rid_ref[0] = jnp.where(sel, seg[:, :1] + before + 1.0, NO_ROW)
    w_ref[0] = wts
    lane = lax.broadcasted_iota(jnp.int32, (N_EXPERTS, LANES), 1)
    tab_ref[0] = jnp.where(lane == 0, seg, jnp.where(lane == 1, SEG_ROWS * pieces, jnp.where(lane == 2, carry[...], 0.0)))
    carry[...] = carry[...] + SEG_ROWS * pieces


def _router(logits, router_b):
    nt, ne, tm = logits.shape
    tri = jnp.asarray(np.triu(np.ones((tm, tm), np.float32), 1), BF16)
    low = jnp.asarray(np.tril(np.ones((ne, ne), np.float32), -1), BF16)
    per_tile = lambda n: pl.BlockSpec((1, ne, n), lambda i: (i, 0, 0))
    return pl.pallas_call(
        _router_kernel,
        grid=(nt,),
        in_specs=[per_tile(tm), _const_spec((ne, 1)), _const_spec(tri.shape), _const_spec(low.shape)],
        out_specs=[per_tile(tm), per_tile(tm), per_tile(LANES)],
        out_shape=[jax.ShapeDtypeStruct((nt, ne, tm), F32), jax.ShapeDtypeStruct((nt, ne, tm), F32),
                   jax.ShapeDtypeStruct((nt, ne, LANES), F32)],
        scratch_shapes=[pltpu.VMEM((ne, 1), F32)],
        compiler_params=_params(("arbitrary",)),
        name="router",
    )(logits, router_b.reshape(ne, 1).astype(F32), tri, low)


def _moe(x1, x1b, logits, router_b, layer, wg, wu, wd, sg, su, sd, g, b):
    two_parts = isinstance(x1, tuple)
    nt, ne, tm = logits.shape
    t = nt * tm
    if tm != ROUTE_TILE:
        split = tm // ROUTE_TILE
        logits = logits.reshape(nt, ne, split, ROUTE_TILE).transpose(0, 2, 1, 3).reshape(nt * split, ne, ROUTE_TILE)
        nt, tm = nt * split, ROUTE_TILE
    assert SORT_ROWS >= TOP_K * tm + ne * (SEG_ROWS - 1) and SORT_ROWS < NO_ROW
    rid, wdense, tab = _router(logits, router_b)
    tab = tab[:, :, :3].astype(jnp.int32)
    seg, rows, base = tab[:, :, 0], tab[:, :, 1], tab[:, :, 2]
    bm = EXPERT_TILE
    region = base[-1] + rows[-1]
    padded = (region + bm - 1) // bm * bm
    pad_end = jnp.cumsum(padded)
    dst = (pad_end - padded)[None, :] + base
    n_blocks = -(-(t * TOP_K + nt * ne * (SEG_ROWS - 1)) // bm) + ne
    blk_start = jnp.arange(n_blocks, dtype=jnp.int32) * bm
    blk_e = jnp.minimum(jnp.sum((pad_end[None, :] <= blk_start[:, None]).astype(jnp.int32), axis=1), ne - 1)
    n_used = (pad_end[-1] // bm).astype(jnp.int32).reshape(1)
    tail_dst = jnp.concatenate([pad_end - padded + region, pad_end[-1:]])
    tail_n16 = jnp.concatenate([padded - region, n_blocks * bm - pad_end[-1:]]) // SEG_ROWS
    big, small, cnt = _piece_tables(dst, rows // SEG_ROWS, seg)
    x_sorted, pw = _dispatch(big, small, cnt, tail_dst, tail_n16, x1b if two_parts else (x1b, x1b), rid, wdense,
                             seg[:, None, :], rows[:, None, :], n_blocks * bm)
    y_sorted = _experts(blk_e, n_used, x_sorted, wg, wu, wd, layer)
    finish = lambda xf, xb, tile0, n_tiles: _ffn_out(big, small, cnt, xf, xb, pw, y_sorted, sg, su, sd, g, b,
                                                     tile0, n_tiles, row0=0)
    if not two_parts:
        return finish(x1, x1b, 0, nt)
    n_main = x1[0].shape[0] // tm
    return finish(x1[0], x1b[0], 0, n_main), finish(x1[1], x1b[1], n_main, nt - n_main)


def _rope_tables(pos):
    half = MLA_ROPE // 2
    inv = ROPE_THETA ** (-jnp.arange(half, dtype=F32) / half)
    ang = pos.astype(F32)[:, None] * inv
    reps = LANES // half
    return jnp.tile(jnp.cos(ang), (1, reps)), jnp.tile(jnp.sin(ang), (1, reps))


def _pack_layer0_weights(w_in0, gla_w_g2, gla_b_g, mla_q_norm_g, mla_kv_norm_g, mla_w_uq, mla_w_uk, mla_w_uv):
    d = w_in0.shape[0]
    hk, hv = GLA_HEADS * GLA_DK, GLA_HEADS * GLA_DV
    o_q, o_k, o_v, o_r = 0, hk, 2 * hk, 2 * hk + hv
    o_a = o_r + hv
    o_cq = o_a + GLA_RANK
    o_ckv = o_cq + MLA_Q_LORA
    o_kr = o_ckv + MLA_KV_LORA
    half = MLA_ROPE // 2
    zeros = lambda n: jnp.zeros((d, n), w_in0.dtype)
    kr1, kr2 = w_in0[:, o_kr:o_kr + half], w_in0[:, o_kr + half:o_kr + MLA_ROPE]
    w_in = jnp.concatenate([
        w_in0[:, o_q:o_a], w_in0[:, o_cq:o_kr],
        kr1, kr2, zeros(LANES - MLA_ROPE),
        kr2, kr1, zeros(LANES - MLA_ROPE),
        w_in0[:, o_a:o_cq], zeros(LANES - GLA_RANK)], axis=1).astype(BF16)
    assert w_in.shape[1] == _C_END
    w_g2 = jnp.concatenate([gla_w_g2, jnp.zeros((LANES - GLA_RANK, hk), gla_w_g2.dtype)], axis=0)
    uq = mla_w_uq.reshape(MLA_Q_LORA, MLA_HEADS, MLA_NOPE + MLA_ROPE)
    w_uq = jnp.concatenate([uq[:, :, :MLA_NOPE].reshape(MLA_Q_LORA, -1),
                            uq[:, :, MLA_NOPE:MLA_NOPE + half].reshape(MLA_Q_LORA, -1),
                            uq[:, :, MLA_NOPE + half:].reshape(MLA_Q_LORA, -1)], axis=1).astype(BF16)
    eye = jnp.eye(MLA_HEADS, dtype=mla_w_uk.dtype)
    w_abs = jnp.einsum('chn,hg->hngc', mla_w_uk, eye).reshape(MLA_HEADS * MLA_NOPE, MLA_HEADS * MLA_KV_LORA).astype(BF16)
    w_uvbd = jnp.einsum('chv,hg->hcgv', mla_w_uv, eye).reshape(MLA_HEADS * MLA_KV_LORA, MLA_HEADS * MLA_V).astype(BF16)
    place = np.zeros((2 * LANES, MLA_HEADS * LANES), np.float32)
    for h in range(MLA_HEADS):
        for j in range(half):
            place[h * half + j, h * LANES + j] = 1.0
            place[LANES + h * half + j, h * LANES + half + j] = 1.0
    return dict(w_in=w_in, w_g2=w_g2, b_g=gla_b_g.reshape(1, hk), q_norm=mla_q_norm_g.reshape(1, -1),
                kv_norm=mla_kv_norm_g.reshape(1, -1), w_uq=w_uq, w_abs=w_abs,
                place=jnp.asarray(place, BF16)), w_uvbd


def _pad_rows(x, n):
    return jnp.pad(x, ((0, n - x.shape[0]),) + ((0, 0),) * (x.ndim - 1))


def _kv_tiles(past, own):
    kvb = MLA_KV_TILE
    pad = lambda a: jnp.pad(a, ((0, 0), (0, -a.shape[1] % kvb), (0, 0)))
    kv = jnp.concatenate([pad(past), pad(own)], axis=1) if past.shape[1] else pad(own)
    b, n, w = kv.shape
    kt = kv.reshape(b, n // kvb, kvb, w).transpose(0, 1, 3, 2)
    return kt, kv


def kernel(x_prompt, x_sample, cache_mla_ckv, cache_mla_krope, state_gla, cache_pool, meta_tokens, w_in0, gla_w_g2, gla_b_g, gla_norm_g, mla_q_norm_g, mla_kv_norm_g, mla_w_uq, mla_w_uk, mla_w_uv, w_out0, pool_w, pool_scale, ln_g, ln_b, moe_router_w, moe_router_b, moe_w_gate, moe_w_up, moe_w_down, moe_sh_gate, moe_sh_up, moe_sh_down):
    bp, sp, d = x_prompt.shape
    bs, ss, _ = x_sample.shape
    n_meta = meta_tokens.shape[0]
    past_len = cache_mla_ckv.shape[1] - n_meta
    tp, tsm = bp * sp, bs * ss
    assert sp % MLA_Q_TILE == 0 and sp % POOL_TILE == 0 and sp % GLA_BLOCK == 0 and sp % CHUNK == 0
    assert ss <= CHUNK and past_len % CHUNK == 0 and n_meta <= CHUNK and ss <= GLA_BLOCK and n_meta <= GLA_BLOCK
    assert n_meta == POOL_MAX and ss >= POOL_MAX and tp % TOK_TILE == 0
    t_all = -(-(tp + tsm + n_meta) // TOK_TILE) * TOK_TILE
    o_s, o_m = tp, tp + tsm

    x_parts = (x_prompt.reshape(tp, d),
               _pad_rows(jnp.concatenate([x_sample.reshape(tsm, d), meta_tokens], axis=0), t_all - tp))
    pos = jnp.concatenate([jnp.tile(n_meta + jnp.arange(sp), bp), jnp.tile(n_meta + past_len + jnp.arange(ss), bs),
                           jnp.arange(n_meta), jnp.zeros((t_all - o_m - n_meta,), jnp.int32)])
    cos_t, sin_t = _rope_tables(pos)
    w0, w_uvbd = _pack_layer0_weights(w_in0, gla_w_g2, gla_b_g, mla_q_norm_g, mla_kv_norm_g, mla_w_uq, mla_w_uk, mla_w_uv)

    q, k, v, sr, lg, q_all, ckv, kr, k_all = _inproj(*x_parts, cos_t, sin_t, w0)

    tables = _gla_tables(GLA_BLOCK)
    g_gla = gla_norm_g.reshape(1, GLA_DV)
    c = GLA_BLOCK
    hk = GLA_HEADS * GLA_DK

    def small(a):
        sm = jnp.pad(a[o_s:o_m].reshape(bs, ss, -1), ((0, 0), (0, c - ss), (0, 0)))
        me = jnp.pad(a[o_m:o_m + n_meta], ((0, c - n_meta), (0, 0)))[None]
        return jnp.concatenate([sm, me], axis=0).reshape((bs + 1) * c, -1)

    s0_small = jnp.concatenate([state_gla.reshape(bs, hk, GLA_DV), jnp.zeros((1, hk, GLA_DV), F32)], axis=0)
    og_small, st_small = _gla(small(q), small(k), small(v), small(lg), small(sr), s0_small, g_gla, tables, bs + 1, 1)
    s0_prompt = jnp.broadcast_to(st_small[bs:], (bp, hk, GLA_DV))
    og_prompt, st_prompt = _gla(q, k, v, lg, sr, s0_prompt, g_gla, tables, bp, sp // c)
    og_small = og_small.reshape(bs + 1, c, -1)
    o_gla = (og_prompt, _pad_rows(jnp.concatenate([og_small[:bs, :ss].reshape(tsm, -1), og_small[bs, :n_meta]], axis=0),
                                  t_all - tp))

    k_meta = k_all[o_m:o_m + n_meta]
    kt_m, kv_m = _kv_tiles(jnp.zeros((1, 0, MLA_QK), BF16), k_meta[None])
    ol_meta = _mla(q_all[o_m:o_m + n_meta], kt_m, kv_m, 1, 1, n_meta, 0, 0, n_meta)
    cache = jnp.concatenate([cache_mla_ckv, cache_mla_krope,
                             jnp.zeros(cache_mla_ckv.shape[:2] + (MLA_QK - MLA_KV_LORA - MLA_ROPE,), F32)], axis=-1).astype(BF16)
    kt_s, kv_s = _kv_tiles(cache, k_all[o_s:o_m].reshape(bs, ss, MLA_QK))
    ol_sample = _mla(q_all[o_s:o_m], kt_s, kv_s, bs, 1, ss, 0, n_meta + past_len, ss)
    kt_p, kv_p = _kv_tiles(jnp.broadcast_to(k_meta[None], (bp, n_meta, MLA_QK)), k_all[:tp].reshape(bp, sp, MLA_QK))
    ol_prompt = _mla(q_all, kt_p, kv_p, bp, sp // MLA_Q_TILE, MLA_Q_TILE, 0, n_meta, sp)
    o_lat = (ol_prompt, _pad_rows(jnp.concatenate([ol_sample, ol_meta], axis=0), t_all - tp))

    ln = lambda l, j: (ln_g[l, j].reshape(1, d), ln_b[l, j].reshape(1, d))
    rw = lambda l: moe_router_w[l].T
    moe_w = lambda l: (l, moe_w_gate, moe_w_up, moe_w_down,
                       moe_sh_gate[l].astype(BF16), moe_sh_up[l].astype(BF16), moe_sh_down[l].astype(BF16))
    x1, x1b, logits = _outproj(o_gla, o_lat, x_parts, w_uvbd, w_out0.astype(BF16), *ln(0, 0), rw(0))
    x2, _ = _moe(x1, x1b, logits, moe_router_b[0], *moe_w(0), *ln(0, 1))

    pm = POOL_MAX
    zrow = jnp.zeros((1, d), F32)
    hist_p = jnp.broadcast_to(jnp.concatenate([zrow, x2[o_m + n_meta - (pm - 1):o_m + n_meta]], axis=0)[None], (bp, pm, d))
    hist_s = jnp.concatenate([jnp.zeros((bs, 1, d), F32), cache_pool], axis=1)
    pool_wb = pool_w.astype(BF16)
    ps = pool_scale.reshape(1, d)
    t1 = -(-(tp + tsm) // TOK_TILE) * TOK_TILE
    y1p = _pool(x2, hist_p, pool_wb, ps, *ln(1, 0), rw(1), bp, sp // POOL_TILE, POOL_TILE, 0)
    y1s = _pool(x2[o_s:o_m], hist_s, pool_wb, ps, *ln(1, 0), rw(1), bs, 1, ss, 0)
    x3, x3b = [(a, _pad_rows(b_, t1 - tp)) for a, b_ in zip(y1p[:2], y1s[:2])]
    assert POOL_TILE == TOK_TILE
    rest_logits = jnp.pad(y1s[2].transpose(1, 0, 2).reshape(N_EXPERTS, tsm), ((0, 0), (0, t1 - tp - tsm)))
    logits1 = jnp.concatenate([y1p[2], rest_logits.reshape(N_EXPERTS, -1, TOK_TILE).transpose(1, 0, 2)], axis=0)
    (x4_prompt, _), (x4_rest, _) = _moe(x3, x3b, logits1, moe_router_b[1], *moe_w(1), *ln(1, 1))

    y_prompt = x4_prompt.reshape(bp, sp, d)
    y_sample = x4_rest[:tsm].reshape(bs, ss, d)
    ckv_meta, kr_meta = ckv[o_m:o_m + n_meta], kr[o_m:o_m + n_meta, :MLA_ROPE]
    p_ckv = jnp.concatenate([jnp.broadcast_to(ckv_meta[None], (bp, n_meta, MLA_KV_LORA)),
                             ckv[:tp].reshape(bp, sp, MLA_KV_LORA)], axis=1)
    p_kr = jnp.concatenate([jnp.broadcast_to(kr_meta[None], (bp, n_meta, MLA_ROPE)),
                            kr[:tp, :MLA_ROPE].reshape(bp, sp, MLA_ROPE)], axis=1)
    p_gla = st_prompt.reshape(bp, GLA_HEADS, GLA_DK, GLA_DV)
    p_pool = jnp.stack([x2[(b_ + 1) * sp - (pm - 1):(b_ + 1) * sp] for b_ in range(bp)])
    s_ckv = ckv[o_s:o_m].reshape(bs, ss, MLA_KV_LORA)
    s_kr = kr[o_s:o_m, :MLA_ROPE].reshape(bs, ss, MLA_ROPE)
    s_gla = st_small[:bs].reshape(bs, GLA_HEADS, GLA_DK, GLA_DV)
    s_pool = x2[o_s:o_m].reshape(bs, ss, d)[:, ss - (pm - 1):]
    return (y_prompt, y_sample, p_ckv, p_kr, p_gla, p_pool, s_ckv, s_kr, s_gla, s_pool)
```

```python
import functools
import math

import jax
import jax.numpy as jnp
import numpy as np
from jax import lax
from jax.experimental import pallas as pl
from jax.experimental.pallas import tpu as pltpu

F32 = jnp.float32
BF16 = jnp.bfloat16

CHUNK = 64
DEPTH = 2
ALPHA = (2 * DEPTH) ** 0.25
LN_EPS = 1e-5
RMS_EPS = 1e-6
NEG_INF = -1e30
GLA_HEADS = 4
GLA_DK = 64
GLA_DV = 128
GLA_RANK = 16
GLA_TAU = 16.0
MLA_HEADS = 8
MLA_Q_LORA = 256
MLA_KV_LORA = 128
MLA_NOPE = 64
MLA_ROPE = 32
MLA_V = 64
MLA_SCALE = (MLA_NOPE + MLA_ROPE) ** -0.5
ROPE_THETA = 10000.0
POOL_WINDOWS = (2, 4, 8, 16)
POOL_MAX = 16
N_EXPERTS = 64
TOP_K = 8
N_GROUPS = 8
TOPK_GROUPS = 4
ROUTED_SCALE = 2.5

LANES = 128
MXU_DIM = 256

TOK_TILE = 512
GLA_BLOCK = 128
MLA_Q_TILE = 256
MLA_KV_TILE = 256
MLA_QK = 256
MLA_ROW_STRIP = 128
LOG2E = math.log2(math.e)
EXPERT_TILE = 1024
EXPERT_SUBTILE = 512
POOL_TILE = 512
SEG_ROWS = 16
BIG_PIECE = 4
DRAIN_UNITS = (64, 16, 4, 1)
ROUTE_TILE = 512
SORT_ROWS = 5120
K_BIG = (TOP_K * ROUTE_TILE + N_EXPERTS * (SEG_ROWS - 1)) // (SEG_ROWS * BIG_PIECE)
K_SMALL = N_EXPERTS * (BIG_PIECE - 1)
PACK_BITS = 9
NO_ROW = 8191.0
DISPATCH_STRIP = 1024
COMBINE_STRIP = 2560
VMEM_LIMIT = 56 * 1024 * 1024


def _params(sem, vmem=VMEM_LIMIT):
    return pltpu.CompilerParams(dimension_semantics=sem, vmem_limit_bytes=vmem)


def _const_spec(shape):
    nd = len(shape)
    return pl.BlockSpec(shape, lambda *_: (0,) * nd)


def _split3(x):
    hi = x.astype(BF16)
    r = x - hi.astype(F32)
    mid = r.astype(BF16)
    lo = (r - mid.astype(F32)).astype(BF16)
    return hi, mid, lo


def _dot(a, b):
    return jnp.dot(a, b, preferred_element_type=F32)


def _dot_f32(a, b):
    a_hi = a.astype(BF16)
    a_lo = (a - a_hi.astype(F32)).astype(BF16)
    b_hi = b.astype(BF16)
    b_lo = (b - b_hi.astype(F32)).astype(BF16)
    return _dot(a_hi, b_hi) + (_dot(a_lo, b_hi) + _dot(a_hi, b_lo))


def _dot_f32_nt(a, b):
    nt = lambda u, v: lax.dot_general(u, v, (((1,), (1,)), ((), ())), preferred_element_type=F32)
    a_hi = a.astype(BF16)
    a_lo = (a - a_hi.astype(F32)).astype(BF16)
    b_hi = b.astype(BF16)
    b_lo = (b - b_hi.astype(F32)).astype(BF16)
    return nt(a_hi, b_hi) + (nt(a_lo, b_hi) + nt(a_hi, b_lo))


def _silu(x):
    return x * (1.0 / (1.0 + jnp.exp(-x)))


def _layer_norm(x, g, b):
    mu = jnp.mean(x, axis=-1, keepdims=True)
    xc = x - mu
    var = jnp.mean(xc * xc, axis=-1, keepdims=True)
    return xc * lax.rsqrt(var + LN_EPS) * g + b


def _rms(x, g):
    return x * lax.rsqrt(jnp.mean(x * x, axis=-1, keepdims=True) + RMS_EPS) * g


_C_Q, _C_K, _C_V, _C_R, _C_CQ, _C_CKV, _C_KR, _C_KRS, _C_A, _C_END = (
    0, 256, 512, 1024, 1536, 1792, 1920, 2048, 2176, 2304)


def _two_part_specs(tm, n, n_main):
    return [pl.BlockSpec((tm, n), lambda i, *_: (jnp.minimum(i, n_main - 1), 0)),
            pl.BlockSpec((tm, n), lambda i, *_: (jnp.maximum(i - n_main, 0), 0))]


def _pick(main_ref, rest_ref, n_main):
    return jnp.where(pl.program_id(0) < n_main, main_ref[...], rest_ref[...])


def _inproj_kernel(xm_ref, xr_ref, cos_ref, sin_ref, w_ref, wg2_ref, bg_ref, qn_ref, kvn_ref, wuq_ref, wabs_ref,
                   place_ref, q_ref, k_ref, v_ref, sr_ref, lg_ref, qall_ref, ckv_ref, kr_ref, kall_ref, *, n_main):
    h = _dot(_pick(xm_ref, xr_ref, n_main).astype(BF16), w_ref[...])
    q_ref[...] = (h[:, _C_Q:_C_K] * GLA_DK ** -0.5).astype(BF16)
    k_ref[...] = h[:, _C_K:_C_V].astype(BF16)
    v_ref[...] = h[:, _C_V:_C_R].astype(BF16)
    sr_ref[...] = _silu(h[:, _C_R:_C_CQ]).astype(BF16)
    z = _dot_f32(h[:, _C_A:_C_END], wg2_ref[...]) + bg_ref[...]
    lg_ref[...] = (jnp.minimum(z, 0.0) - jnp.log(1.0 + jnp.exp(-jnp.abs(z)))) * (1.0 / GLA_TAU)
    cqn = _rms(h[:, _C_CQ:_C_CKV], qn_ref[...]).astype(BF16)
    qh = _dot(cqn, wuq_ref[...])
    n_nope = MLA_HEADS * MLA_NOPE
    cos, sin = cos_ref[...], sin_ref[...]
    x1, x2 = qh[:, n_nope:n_nope + LANES], qh[:, n_nope + LANES:]
    qscale = MLA_SCALE * LOG2E
    rot = (jnp.concatenate([x1 * cos - x2 * sin, x1 * sin + x2 * cos], axis=1) * qscale).astype(BF16)
    qlat = (_dot(qh[:, :n_nope].astype(BF16), wabs_ref[...]) * qscale).astype(BF16)
    qrope = _dot(rot, place_ref[...]).astype(BF16)
    for hd in range(MLA_HEADS):
        qall_ref[:, MLA_QK * hd:MLA_QK * hd + LANES] = qlat[:, LANES * hd:LANES * (hd + 1)]
        qall_ref[:, MLA_QK * hd + LANES:MLA_QK * (hd + 1)] = qrope[:, LANES * hd:LANES * (hd + 1)]
    ckv = _rms(h[:, _C_CKV:_C_KR], kvn_ref[...])
    lane = lax.broadcasted_iota(jnp.int32, (1, LANES), 1)
    sgn = jnp.where(lane < MLA_ROPE // 2, -1.0, 1.0)
    kr = h[:, _C_KR:_C_KRS] * cos + h[:, _C_KRS:_C_A] * (sin * sgn)
    ckv_ref[...] = ckv
    kr_ref[...] = kr
    kall_ref[:, :LANES] = ckv.astype(BF16)
    kall_ref[:, LANES:] = kr.astype(BF16)


def _inproj(x_main, x_rest, cos_t, sin_t, w):
    t = x_main.shape[0] + x_rest.shape[0]
    d = x_main.shape[1]
    tm = TOK_TILE
    n_main = x_main.shape[0] // tm
    row = lambda n: pl.BlockSpec((tm, n), lambda i: (i, 0))
    consts = [w['w_in'], w['w_g2'], w['b_g'], w['q_norm'], w['kv_norm'], w['w_uq'], w['w_abs'], w['place']]
    outs = [(GLA_HEADS * GLA_DK, BF16), (GLA_HEADS * GLA_DK, BF16), (GLA_HEADS * GLA_DV, BF16),
            (GLA_HEADS * GLA_DV, BF16), (GLA_HEADS * GLA_DK, F32), (MLA_HEADS * MLA_QK, BF16),
            (LANES, F32), (LANES, F32), (MLA_QK, BF16)]
    return pl.pallas_call(
        functools.partial(_inproj_kernel, n_main=n_main),
        grid=(t // tm,),
        in_specs=_two_part_specs(tm, d, n_main) + [row(LANES), row(LANES)] + [_const_spec(c.shape) for c in consts],
        out_specs=[row(n) for n, _ in outs],
        out_shape=[jax.ShapeDtypeStruct((t, n), dt) for n, dt in outs],
        compiler_params=_params(("parallel",)),
        name="inproj",
    )(x_main, x_rest, cos_t, sin_t, *consts)


def _gla_tables(c):
    levels = int(math.log2(c))
    assert 1 << levels == c
    t = np.arange(c)[:, None]
    u = np.arange(c)[None, :]
    mats, masks = [], []
    for l in range(levels):
        half = 1 << l
        base = (t // half) * half
        upper = ((t >> l) & 1) == 1
        a_q = (u >= base) & (u <= t)
        a_k = (u > t) & (u <= base + half - 1)
        mats.append(np.where(upper, a_q, a_k))
        same = (t >> (l + 1)) == (u >> (l + 1))
        masks.append(same & upper & (((u >> l) & 1) == 0))
    mats.append(u <= t)
    mats.append(u > t)
    masks.append(t == u)
    amat = np.concatenate(mats, axis=0).astype(np.float32)
    mask = np.stack([np.tile(m, (GLA_HEADS, 1)) for m in masks]).astype(np.float32)
    return jnp.asarray(amat, BF16), jnp.asarray(mask, F32), levels


def _gla_kernel(q_ref, k_ref, v_ref, lg_ref, sr_ref, s0_ref, amat_ref, mask_ref, g_ref,
                o_ref, sout_ref, state, *, levels):
    c = q_ref.shape[0]
    hk = GLA_HEADS * GLA_DK
    step = pl.program_id(1)

    @pl.when(step == 0)
    def _():
        state[...] = s0_ref[0]

    lg = lg_ref[...]
    parts = _split3(lg)
    e3 = _dot(amat_ref[...], jnp.concatenate(parts, axis=1))
    ex = jnp.exp(e3[:, :hk] + e3[:, hk:2 * hk] + e3[:, 2 * hk:])
    q = q_ref[...].astype(F32)
    k = k_ref[...].astype(F32)
    v = v_ref[...]
    head = lax.broadcasted_iota(jnp.int32, (1, hk), 1) // GLA_DK

    def per_head(xf):
        return jnp.concatenate([jnp.where(head == h, xf, 0.0) for h in range(GLA_HEADS)], axis=0).astype(BF16)

    att = jnp.zeros((GLA_HEADS * c, c), F32)
    for l in range(levels + 1):
        if l < levels:
            el = ex[l * c:(l + 1) * c]
            ql, kl = q * el, k * el
        else:
            ql, kl = q, k
        a = lax.dot_general(per_head(ql), kl.astype(BF16), (((1,), (1,)), ((), ())), preferred_element_type=F32)
        att = att + a * mask_ref[l]
    s_prev = state[...]
    o_inter = _dot(per_head(q * ex[levels * c:(levels + 1) * c]), s_prev.astype(BF16))
    att = att.astype(BF16)
    g = g_ref[...]
    for h in range(GLA_HEADS):
        o = o_inter[h * c:(h + 1) * c] + _dot(att[h * c:(h + 1) * c], v[:, GLA_DV * h:GLA_DV * (h + 1)])
        o = _rms(o, g)
        o_ref[:, GLA_DV * h:GLA_DV * (h + 1)] = (o * sr_ref[:, GLA_DV * h:GLA_DV * (h + 1)].astype(F32)).astype(BF16)
    kr = (k * ex[(levels + 1) * c:]).astype(BF16)
    upd = lax.dot_general(kr, v, (((0,), (0,)), ((), ())), preferred_element_type=F32)
    ones = jnp.ones((c, GLA_DV), BF16)
    b_last = sum(lax.dot_general(p, ones, (((0,), (0,)), ((), ())), preferred_element_type=F32) for p in parts)
    new = jnp.exp(b_last) * s_prev + jnp.concatenate(
        [upd[GLA_DK * h:GLA_DK * (h + 1), GLA_DV * h:GLA_DV * (h + 1)] for h in range(GLA_HEADS)], axis=0)
    state[...] = new

    @pl.when(step == pl.num_programs(1) - 1)
    def _():
        sout_ref[0] = new


def _gla(q, k, v, lg, sr, s0, g, tables, batch, blocks_per_seq, out_rows=None):
    amat, mask, levels = tables
    c = GLA_BLOCK
    hk, hv = GLA_HEADS * GLA_DK, GLA_HEADS * GLA_DV
    row = lambda n: pl.BlockSpec((c, n), lambda b, s: (b * blocks_per_seq + s, 0))
    st = pl.BlockSpec((1, hk, GLA_DV), lambda b, s: (b, 0, 0))
    return pl.pallas_call(
        functools.partial(_gla_kernel, levels=levels),
        grid=(batch, blocks_per_seq),
        in_specs=[row(hk), row(hk), row(hv), row(hk), row(hv), st,
                  _const_spec(amat.shape), _const_spec(mask.shape), _const_spec(g.shape)],
        out_specs=[row(hv), st],
        out_shape=[jax.ShapeDtypeStruct((out_rows or batch * blocks_per_seq * c, hv), BF16),
                   jax.ShapeDtypeStruct((batch, hk, GLA_DV), F32)],
        scratch_shapes=[pltpu.VMEM((hk, GLA_DV), F32)],
        compiler_params=_params(("parallel", "arbitrary")),
        name="gla",
    )(q, k, v, lg, sr, s0, amat, mask, g)


def _mla_kernel(q_ref, kt_ref, kv_ref, o_ref, s_sc, p_sc, linv_sc, acc_sc, *, tq, n_past, past_valid, own_valid,
                diag_chunks):
    kvb = MLA_KV_TILE
    qi = pl.program_id(1)
    qs = jnp.concatenate([q_ref[:, MLA_QK * h:MLA_QK * (h + 1)] for h in range(MLA_HEADS)], axis=0)
    m_rows = MLA_HEADS * tq
    n_vis = n_past + qi + 1

    def score(j, mask):
        s = _dot(qs, kt_ref[0, j])
        s_sc[j] = s if mask is None else jnp.where(mask, s, NEG_INF)

    def scores(lo, hi):
        def body(j, carry):
            score(j, None)
            return carry
        lax.fori_loop(lo, hi, body, 0)

    col = lax.broadcasted_iota(jnp.int32, (m_rows, kvb), 1)
    n_past_full = past_valid // kvb
    if n_past_full:
        scores(0, n_past_full)
    if n_past_full < n_past:
        score(n_past_full, col < past_valid - n_past_full * kvb)
    scores(n_past, n_past + qi)
    mask = col < own_valid - qi * kvb
    if diag_chunks:
        qrow = lax.broadcasted_iota(jnp.int32, (m_rows, kvb), 0) % tq
        mask = mask & (col // CHUNK <= qrow // CHUNK)
    score(n_past + qi, mask)

    rc = MLA_ROW_STRIP

    def strip(r, carry):
        rows = pl.ds(pl.multiple_of(r * rc, rc), rc)
        mx = lax.fori_loop(0, n_vis, lambda j, m: jnp.maximum(m, s_sc[j, rows, :]),
                           jnp.full((rc, kvb), -jnp.inf, F32))
        m = jnp.max(mx, axis=1, keepdims=True)

        def body(j, l):
            p = jnp.exp2(s_sc[j, rows, :] - m)
            p_sc[j, rows, :] = p.astype(BF16)
            return l + p
        l = lax.fori_loop(0, n_vis, body, jnp.zeros((rc, kvb), F32))
        linv_sc[rows, :] = 1.0 / jnp.sum(l, axis=1, keepdims=True)
        return carry
    lax.fori_loop(0, m_rows // rc, strip, 0)

    acc_sc[...] = jnp.zeros(acc_sc.shape, F32)

    def pv(j, carry):
        vblk = kv_ref[0, pl.ds(pl.multiple_of(j * kvb, kvb), kvb), :MLA_KV_LORA]
        acc_sc[...] += _dot(p_sc[j], vblk)
        return carry
    lax.fori_loop(0, n_vis, pv, 0)
    o = acc_sc[...] * linv_sc[...]
    for h in range(MLA_HEADS):
        o_ref[:, MLA_KV_LORA * h:MLA_KV_LORA * (h + 1)] = o[h * tq:(h + 1) * tq].astype(BF16)


def _mla(q_all, kt, kv, batch, nq, tq, q_block0, past_valid, own_valid, out_rows=None):
    n_kv = kt.shape[1]
    n_past = -(-past_valid // MLA_KV_TILE)
    assert nq == 1 or tq == MLA_KV_TILE
    assert n_kv == n_past + nq
    return pl.pallas_call(
        functools.partial(_mla_kernel, tq=tq, n_past=n_past, past_valid=past_valid, own_valid=own_valid,
                          diag_chunks=nq > 1),
        grid=(batch, nq),
        in_specs=[pl.BlockSpec((tq, MLA_HEADS * MLA_QK), lambda b, i: (q_block0 + b * nq + i, 0)),
                  pl.BlockSpec((1,) + kt.shape[1:], lambda b, i: (b, 0, 0, 0)),
                  pl.BlockSpec((1,) + kv.shape[1:], lambda b, i: (b, 0, 0))],
        out_specs=pl.BlockSpec((tq, MLA_HEADS * MLA_KV_LORA), lambda b, i: (b * nq + i, 0)),
        out_shape=jax.ShapeDtypeStruct((out_rows or batch * nq * tq, MLA_HEADS * MLA_KV_LORA), BF16),
        scratch_shapes=[pltpu.VMEM((n_kv, MLA_HEADS * tq, MLA_KV_TILE), F32),
                        pltpu.VMEM((n_kv, MLA_HEADS * tq, MLA_KV_TILE), BF16),
                        pltpu.VMEM((MLA_HEADS * tq, 1), F32),
                        pltpu.VMEM((MLA_HEADS * tq, MLA_KV_LORA), F32)],
        compiler_params=_params(("parallel", "arbitrary")),
        name="mla",
    )(q_all, kt, kv)


def _post_mixer(x, out, g, b, rw_ref, x1_ref, x1b_ref, lgt_ref):
    y = _layer_norm(ALPHA * x + out, g, b)
    x1_ref[...] = y
    x1b_ref[...] = y.astype(BF16)
    lgt_ref[0] = _dot_f32_nt(rw_ref[...], y)


def _outproj_kernel(ogm_ref, ogr_ref, olm_ref, olr_ref, xm_ref, xr_ref, wuv_ref, wo_ref, g_ref, b_ref, rw_ref,
                    x1_ref, x1b_ref, lgt_ref, *, n_main):
    og = _pick(ogm_ref, ogr_ref, n_main)
    o_mla = _dot(_pick(olm_ref, olr_ref, n_main), wuv_ref[...]).astype(BF16)
    n_gla = og.shape[1]
    out = _dot(og, wo_ref[:n_gla, :]) + _dot(o_mla, wo_ref[n_gla:, :])
    _post_mixer(_pick(xm_ref, xr_ref, n_main), out, g_ref[...], b_ref[...], rw_ref, x1_ref, x1b_ref, lgt_ref)


def _outproj(o_gla, o_lat, x, w_uvbd, w_out, g, b, rw):
    t, d = x[0].shape[0] + x[1].shape[0], x[0].shape[1]
    tm = TOK_TILE
    n_main = x[0].shape[0] // tm
    row = lambda n: pl.BlockSpec((tm, n), lambda i: (i, 0))
    consts = [w_uvbd, w_out, g, b, rw]
    parts = [*o_gla, *o_lat, *x]
    return pl.pallas_call(
        functools.partial(_outproj_kernel, n_main=n_main),
        grid=(t // tm,),
        in_specs=[s for a in (o_gla, o_lat, x) for s in _two_part_specs(tm, a[0].shape[1], n_main)]
        + [_const_spec(c.shape) for c in consts],
        out_specs=[row(d), row(d), pl.BlockSpec((1, N_EXPERTS, tm), lambda i: (i, 0, 0))],
        out_shape=[jax.ShapeDtypeStruct((t, d), F32), jax.ShapeDtypeStruct((t, d), BF16),
                   jax.ShapeDtypeStruct((t // tm, N_EXPERTS, tm), F32)],
        compiler_params=_params(("parallel",)),
        name="outproj",
    )(*parts, *consts)


def _pool_kernel(x_ref, prev_ref, hist_ref, pw_ref, ps_ref, g_ref, b_ref, rw_ref, x1_ref, x1b_ref, lgt_ref):
    ts = x_ref.shape[0]
    pm = POOL_MAX
    x = x_ref[...]
    prev = jnp.where(pl.program_id(1) == 0, hist_ref[0], prev_ref[...])
    xc = jnp.concatenate([prev, x], axis=0)
    grp = x.shape[1] // len(POOL_WINDOWS)
    outs = []
    for gi, w in enumerate(POOL_WINDOWS):
        s = xc[:, gi * grp:(gi + 1) * grp]
        span = 1
        while span < w:
            s = s + pltpu.roll(s, span, 0)
            span *= 2
        win = s[pm:]
        mix = (win * (1.0 / w) - x[:, gi * grp:(gi + 1) * grp]).astype(BF16)
        outs.append(_dot(mix, pw_ref[gi]))
    out = jnp.concatenate(outs, axis=1) * ps_ref[...]
    _post_mixer(x, out, g_ref[...], b_ref[...], rw_ref, x1_ref, x1b_ref, lgt_ref)


def _pool(x_all, hist, pool_w, pool_scale, g, b, rw, batch, tiles_per_seq, ts, tile0, out_rows=None):
    d = x_all.shape[1]
    pm = POOL_MAX
    per = ts // pm
    consts = [pool_w, pool_scale, g, b, rw]
    row = lambda n: pl.BlockSpec((ts, n), lambda bb, i: (bb * tiles_per_seq + i, 0))
    return pl.pallas_call(
        _pool_kernel,
        grid=(batch, tiles_per_seq),
        in_specs=[pl.BlockSpec((ts, d), lambda bb, i: (tile0 + bb * tiles_per_seq + i, 0)),
                  pl.BlockSpec((pm, d), lambda bb, i: (jnp.maximum((tile0 + bb * tiles_per_seq + i) * per - 1, 0), 0)),
                  pl.BlockSpec((1, pm, d), lambda bb, i: (bb, 0, 0))] + [_const_spec(c.shape) for c in consts],
        out_specs=[row(d), row(d), pl.BlockSpec((1, N_EXPERTS, ts), lambda bb, i: (bb * tiles_per_seq + i, 0, 0))],
        out_shape=[jax.ShapeDtypeStruct((out_rows or batch * tiles_per_seq * ts, d), F32),
                   jax.ShapeDtypeStruct((out_rows or batch * tiles_per_seq * ts, d), BF16),
                   jax.ShapeDtypeStruct((batch * tiles_per_seq, N_EXPERTS, ts), F32)],
        compiler_params=_params(("parallel", "arbitrary")),
        name="pool",
    )(x_all, x_all, hist, *consts)


def _experts_kernel(be_ref, nb_ref, x_ref, wg_ref, wu_ref, wd_ref, y_ref, wg_sc, wu_sc, wd_sc):
    i = pl.program_id(0)

    @pl.when((i == 0) | (be_ref[i] != be_ref[jnp.maximum(i - 1, 0)]))
    def _():
        wg_sc[...] = wg_ref[0, 0].astype(BF16)
        wu_sc[...] = wu_ref[0, 0].astype(BF16)
        wd_sc[...] = wd_ref[0, 0].astype(BF16)

    @pl.when(i < nb_ref[0])
    def _():
        sub = EXPERT_SUBTILE
        for r in range(x_ref.shape[0] // sub):
            x = x_ref[r * sub:(r + 1) * sub, :]
            hmid = (_silu(_dot(x, wg_sc[...])) * _dot(x, wu_sc[...])).astype(BF16)
            y_ref[r * sub:(r + 1) * sub, :] = _dot(hmid, wd_sc[...]).astype(BF16)

    @pl.when(pl.program_id(0) >= nb_ref[0])
    def _():
        y_ref[...] = jnp.zeros_like(y_ref)


def _experts(blk_e, n_used, x_sorted, wg, wu, wd, layer):
    p, d = x_sorted.shape
    f = wg.shape[3]
    bm = EXPERT_TILE
    blk = lambda i, be, nb: (i, 0)
    grid_spec = pltpu.PrefetchScalarGridSpec(
        num_scalar_prefetch=2,
        grid=(p // bm,),
        in_specs=[pl.BlockSpec((bm, d), lambda i, be, nb: (jnp.minimum(i, nb[0] - 1), 0)),
                  pl.BlockSpec((1, 1, d, f), lambda i, be, nb: (layer, be[i], 0, 0)),
                  pl.BlockSpec((1, 1, d, f), lambda i, be, nb: (layer, be[i], 0, 0)),
                  pl.BlockSpec((1, 1, f, d), lambda i, be, nb: (layer, be[i], 0, 0))],
        out_specs=pl.BlockSpec((bm, d), blk),
        scratch_shapes=[pltpu.VMEM((d, f), BF16), pltpu.VMEM((d, f), BF16), pltpu.VMEM((f, d), BF16)],
    )
    return pl.pallas_call(
        _experts_kernel,
        grid_spec=grid_spec,
        out_shape=jax.ShapeDtypeStruct((p, d), BF16),
        compiler_params=_params(("arbitrary",)),
        name="experts",
    )(blk_e, n_used, x_sorted, wg, wu, wd)


def _segment_copies(tile, slot, dst_ref, n16_ref, seg_ref, make, entries=N_EXPERTS):
    big = SEG_ROWS * BIG_PIECE

    def per_expert(e, carry):
        j = tile * N_EXPERTS + e
        d0, s0, n = dst_ref[j], seg_ref[j], n16_ref[j]
        n_big = n // BIG_PIECE

        def big_piece(c, carry2):
            make(pl.multiple_of(d0 + c * big, SEG_ROWS), pl.multiple_of(s0 + c * big, SEG_ROWS), big, slot).start()
            return carry2
        lax.fori_loop(0, n_big, big_piece, 0)
        d1, s1 = d0 + n_big * big, s0 + n_big * big

        def small_piece(c, carry2):
            make(pl.multiple_of(d1 + c * SEG_ROWS, SEG_ROWS), pl.multiple_of(s1 + c * SEG_ROWS, SEG_ROWS),
                 SEG_ROWS, slot).start()
            return carry2
        lax.fori_loop(0, n - n_big * BIG_PIECE, small_piece, 0)
        return carry
    lax.fori_loop(0, entries, per_expert, 0)


def _start_pieces(tile, slot, big_ref, small_ref, cnt_ref, make):
    def run(tab_ref, kmax, n, rows):
        def body(k, carry):
            v = tab_ref[tile * kmax + k]
            make(pl.multiple_of((v >> PACK_BITS) * SEG_ROWS, SEG_ROWS),
                 pl.multiple_of((v & ((1 << PACK_BITS) - 1)) * SEG_ROWS, SEG_ROWS), rows, slot).start()
            return carry
        lax.fori_loop(0, n, body, 0)
    run(big_ref, K_BIG, cnt_ref[2 * tile], SEG_ROWS * BIG_PIECE)
    run(small_ref, K_SMALL, cnt_ref[2 * tile + 1], SEG_ROWS)


def _tile_rows(tile, cnt_ref):
    return cnt_ref[2 * tile] * (SEG_ROWS * BIG_PIECE) + cnt_ref[2 * tile + 1] * SEG_ROWS


def _piece_tables(dst, n16, seg):
    nt, ne = n16.shape
    big_rows = SEG_ROWS * BIG_PIECE
    n_big, n_small = n16 // BIG_PIECE, n16 % BIG_PIECE

    def flat(counts, first_dst, first_src, step, kmax):
        cum = jnp.cumsum(counts, axis=1)[:, None, :]
        before = cum - counts[:, None, :]
        k = jnp.arange(kmax, dtype=jnp.int32)[None, :, None]
        mine = (before <= k) & (k < cum)
        off = step * (k - before)
        d = jnp.sum(jnp.where(mine, first_dst[:, None, :] + off, 0), axis=2)
        s = jnp.sum(jnp.where(mine, first_src[:, None, :] + off, 0), axis=2)
        return (((d // SEG_ROWS) << PACK_BITS) | (s // SEG_ROWS)).reshape(-1).astype(jnp.int32), cum[:, 0, -1]
    big, cnt_big = flat(n_big, dst, seg, big_rows, K_BIG)
    small, cnt_small = flat(n_small, dst + n_big * big_rows, seg + n_big * big_rows, SEG_ROWS, K_SMALL)
    return big, small, jnp.stack([cnt_big, cnt_small], axis=1).reshape(-1).astype(jnp.int32)


def _drain(rows, slot, make):
    left = rows // SEG_ROWS
    for unit in DRAIN_UNITS:
        n = left // unit

        def body(c, carry, unit=unit):
            make(0, 0, unit * SEG_ROWS, slot).wait()
            return carry
        lax.fori_loop(0, n, body, 0)
        left = left - n * unit


def _ffn_out_kernel(big_ref, small_ref, cnt_ref, x_ref, xb_ref, pw_ref, ys_ref,
                    sg_ref, su_ref, sd_ref, g_ref, b_ref, y_ref, yb_ref, ybuf, racc, sem, *, tile0):
    i = pl.program_id(0)
    nt = pl.num_programs(0)
    slot = i % 2
    tile = i + tile0

    def make(d0, s0, rows, sl):
        return pltpu.make_async_copy(ys_ref.at[pl.ds(d0, rows)], ybuf.at[sl, pl.ds(s0, rows)], sem.at[sl])

    @pl.when(i == 0)
    def _():
        ybuf[...] = jnp.zeros_like(ybuf)
        _start_pieces(tile, 0, big_ref, small_ref, cnt_ref, make)

    @pl.when(i + 1 < nt)
    def _():
        _start_pieces(tile + 1, 1 - slot, big_ref, small_ref, cnt_ref, make)

    xb = xb_ref[...]
    hmid = (_silu(_dot(xb, sg_ref[...])) * _dot(xb, su_ref[...])).astype(BF16)
    shared = _dot(hmid, sd_ref[...])
    used = _tile_rows(tile, cnt_ref)
    _drain(used, slot, make)

    racc[...] = jnp.zeros_like(racc)
    rs = COMBINE_STRIP
    for s in range(ybuf.shape[1] // rs):
        @pl.when(s * rs < used)
        def _():
            racc[...] += lax.dot_general(pw_ref[0, s * rs:(s + 1) * rs, :], ybuf[slot, s * rs:(s + 1) * rs, :],
                                         (((0,), (0,)), ((), ())), preferred_element_type=F32)
    ffn = racc[...] + shared
    y = _layer_norm(ALPHA * x_ref[...] + ffn, g_ref[...], b_ref[...])
    y_ref[...] = y
    yb_ref[...] = y.astype(BF16)


def _ffn_out(big, small, cnt, x1, x1b, pw, y_sorted, sg, su, sd, g, b, tile0=0, n_tiles=None, row0=None):
    d = x1.shape[1]
    nt, _, tm = pw.shape
    n_tiles = nt - tile0 if n_tiles is None else n_tiles
    row0 = tile0 if row0 is None else row0
    row_in = pl.BlockSpec((tm, d), lambda i, *_: (i + row0, 0))
    row_out = pl.BlockSpec((tm, d), lambda i, *_: (i, 0))
    consts = [sg, su, sd, g, b]
    grid_spec = pltpu.PrefetchScalarGridSpec(
        num_scalar_prefetch=3,
        grid=(n_tiles,),
        in_specs=[row_in, row_in, pl.BlockSpec((1,) + pw.shape[1:], lambda i, *_: (i + tile0, 0, 0)),
                  pl.BlockSpec(memory_space=pl.ANY)] + [_const_spec(c.shape) for c in consts],
        out_specs=[row_out, row_out],
        scratch_shapes=[pltpu.VMEM((2, SORT_ROWS, d), BF16), pltpu.VMEM((tm, d), F32), pltpu.SemaphoreType.DMA((2,))],
    )
    return pl.pallas_call(
        functools.partial(_ffn_out_kernel, tile0=tile0),
        grid_spec=grid_spec,
        out_shape=[jax.ShapeDtypeStruct((n_tiles * tm, d), F32), jax.ShapeDtypeStruct((n_tiles * tm, d), BF16)],
        compiler_params=_params(("arbitrary",)),
        name="ffn_out",
    )(big, small, cnt, x1, x1b, pw, y_sorted, *consts)


def _dispatch_kernel(big_ref, small_ref, cnt_ref, tdst_ref, tn16_ref, xm_ref, xr_ref, rid_ref, w_ref, segr_ref, c16r_ref,
                     xs_ref, pw_ref, buf, zbuf, sem, *, n_main):
    i = pl.program_id(0)
    nt = pl.num_programs(0)
    slot = i % 2
    tm = xm_ref.shape[0]

    def make(d0, s0, rows, sl):
        return pltpu.make_async_copy(buf.at[sl, pl.ds(s0, rows)], xs_ref.at[pl.ds(d0, rows)], sem.at[sl])

    @pl.when(i >= 2)
    def _():
        _drain(_tile_rows(i - 2, cnt_ref), slot, make)

    x = _pick(xm_ref, xr_ref, n_main)
    rid = rid_ref[0]
    wb = w_ref[0].astype(BF16)
    used = _tile_rows(i, cnt_ref)
    rs = DISPATCH_STRIP
    hs = MXU_DIM
    for s in range(buf.shape[1] // rs):
        @pl.when(s * rs < used)
        def _():
            hits = []
            for h0 in range(s * rs, (s + 1) * rs, hs):
                r = lax.broadcasted_iota(jnp.int32, (hs, N_EXPERTS), 0) + h0
                owner = jnp.where((r >= segr_ref[0]) & (r < segr_ref[0] + c16r_ref[0]), 1.0, 0.0).astype(BF16)
                local = jnp.where((rid > h0) & (rid <= h0 + hs), rid - h0, 0.0).astype(BF16)
                at_row = _dot(owner, jnp.concatenate([local, wb], axis=1))
                hit = at_row[:, :tm] == (lax.broadcasted_iota(jnp.int32, (hs, tm), 0) + 1).astype(F32)
                hits.append(jnp.where(hit, 1.0, 0.0).astype(BF16))
                pw_ref[0, h0:h0 + hs, :] = jnp.where(hit, at_row[:, tm:], 0.0).astype(BF16)
            buf[slot, s * rs:(s + 1) * rs, :] = _dot(jnp.concatenate(hits, axis=0), x).astype(BF16)

        @pl.when(s * rs >= used)
        def _():
            pw_ref[0, s * rs:(s + 1) * rs, :] = jnp.zeros((rs, tm), BF16)
    _start_pieces(i, slot, big_ref, small_ref, cnt_ref, make)

    @pl.when(i == nt - 1)
    def _():
        zbuf[...] = jnp.zeros_like(zbuf)

        def zmake(d0, s0, rows, sl):
            return pltpu.make_async_copy(zbuf.at[pl.ds(0, rows)], xs_ref.at[pl.ds(d0, rows)], sem.at[sl])
        _segment_copies(0, 2, tdst_ref, tn16_ref, tdst_ref, zmake, entries=N_EXPERTS + 1)
        tail = lax.fori_loop(0, N_EXPERTS + 1, lambda e, acc: acc + tn16_ref[e], 0) * SEG_ROWS
        _drain(tail, 2, make)

        @pl.when(i >= 1)
        def _():
            _drain(_tile_rows(i - 1, cnt_ref), 1 - slot, make)
        _drain(used, slot, make)


def _dispatch(big, small, cnt, tail_dst, tail_n16, xb, rid, wdense, segr, c16r, n_rows):
    d = xb[0].shape[1]
    nt, ne, tm = rid.shape
    n_main = xb[0].shape[0] // tm
    per_tile = lambda a: pl.BlockSpec((1,) + a.shape[1:], lambda i, *_: (i, 0, 0))
    grid_spec = pltpu.PrefetchScalarGridSpec(
        num_scalar_prefetch=5,
        grid=(nt,),
        in_specs=_two_part_specs(tm, d, n_main) + [per_tile(rid), per_tile(wdense), per_tile(segr), per_tile(c16r)],
        out_specs=[pl.BlockSpec(memory_space=pl.ANY), pl.BlockSpec((1, SORT_ROWS, tm), lambda i, *_: (i, 0, 0))],
        scratch_shapes=[pltpu.VMEM((2, SORT_ROWS, d), BF16), pltpu.VMEM((SEG_ROWS * BIG_PIECE, d), BF16),
                        pltpu.SemaphoreType.DMA((3,))],
    )
    return pl.pallas_call(
        functools.partial(_dispatch_kernel, n_main=n_main),
        grid_spec=grid_spec,
        out_shape=[jax.ShapeDtypeStruct((n_rows, d), BF16), jax.ShapeDtypeStruct((nt, SORT_ROWS, tm), BF16)],
        compiler_params=_params(("arbitrary",)),
        name="dispatch",
    )(big, small, cnt, tail_dst, tail_n16, *xb, rid, wdense, segr, c16r)


def _top_rows(vals, n, k, ids):
    sel = jnp.zeros(vals.shape, jnp.bool_)
    for _ in range(k):
        m = jnp.max(vals, axis=0, keepdims=True)
        hit = ids == jnp.min(jnp.where(vals == m, ids, n), axis=0, keepdims=True)
        sel = sel | hit
        vals = jnp.where(hit, -jnp.inf, vals)
    return sel


def _router_kernel(lg_ref, b_ref, tri_ref, low_ref, rid_ref, w_ref, tab_ref, carry):
    @pl.when(pl.program_id(0) == 0)
    def _():
        carry[...] = jnp.zeros_like(carry)

    tm = lg_ref.shape[2]
    gs = N_EXPERTS // N_GROUPS
    scores = 1.0 / (1.0 + jnp.exp(-lg_ref[0]))
    biased = scores + b_ref[...]
    member = lax.broadcasted_iota(jnp.int32, (gs, tm), 0)
    group_score = []
    for g in range(N_GROUPS):
        blk = biased[g * gs:(g + 1) * gs]
        m1 = jnp.max(blk, axis=0, keepdims=True)
        first = jnp.min(jnp.where(blk == m1, member, gs), axis=0, keepdims=True)
        m2 = jnp.max(jnp.where(member == first, -jnp.inf, blk), axis=0, keepdims=True)
        group_score.append(m1 + m2)
    group_score = jnp.concatenate(group_score, axis=0)
    gid = lax.broadcasted_iota(jnp.int32, (N_GROUPS, tm), 0)
    group_ok = _top_rows(group_score, N_GROUPS, TOPK_GROUPS, gid)
    masked = jnp.concatenate([jnp.where(group_ok[g:g + 1], biased[g * gs:(g + 1) * gs], -jnp.inf)
                              for g in range(N_GROUPS)], axis=0)
    eid = lax.broadcasted_iota(jnp.int32, (N_EXPERTS, tm), 0)
    sel = _top_rows(masked, N_EXPERTS, TOP_K, eid)
    self = jnp.where(sel, 1.0, 0.0)
    wsel = self * scores
    wts = wsel / jnp.sum(wsel, axis=0, keepdims=True) * ROUTED_SCALE
    selb = self.astype(BF16)
    before = _dot(selb, tri_ref[...])
    pieces = jnp.floor((jnp.sum(self, axis=1, keepdims=True) + (SEG_ROWS - 1.0)) * (1.0 / SEG_ROWS))
    seg = SEG_ROWS * _dot(low_ref[...], jnp.broadcast_to(pieces, (N_EXPERTS, LANES)).astype(BF16))
    rid_ref[0] = jnp.where(sel, seg[:, :1] + before + 1.0, NO_ROW)
    w_ref[0] = wts
    lane = lax.broadcasted_iota(jnp.int32, (N_EXPERTS, LANES), 1)
    tab_ref[0] = jnp.where(lane == 0, seg, jnp.where(lane == 1, SEG_ROWS * pieces, jnp.where(lane == 2, carry[...], 0.0)))
    carry[...] = carry[...] + SEG_ROWS * pieces


def _router(logits, router_b):
    nt, ne, tm = logits.shape
    tri = jnp.asarray(np.triu(np.ones((tm, tm), np.float32), 1), BF16)
    low = jnp.asarray(np.tril(np.ones((ne, ne), np.float32), -1), BF16)
    per_tile = lambda n: pl.BlockSpec((1, ne, n), lambda i: (i, 0, 0))
    return pl.pallas_call(
        _router_kernel,
        grid=(nt,),
        in_specs=[per_tile(tm), _const_spec((ne, 1)), _const_spec(tri.shape), _const_spec(low.shape)],
        out_specs=[per_tile(tm), per_tile(tm), per_tile(LANES)],
        out_shape=[jax.ShapeDtypeStruct((nt, ne, tm), F32), jax.ShapeDtypeStruct((nt, ne, tm), F32),
                   jax.ShapeDtypeStruct((nt, ne, LANES), F32)],
        scratch_shapes=[pltpu.VMEM((ne, 1), F32)],
        compiler_params=_params(("arbitrary",)),
        name="router",
    )(logits, router_b.reshape(ne, 1).astype(F32), tri, low)


def _moe(x1, x1b, logits, router_b, layer, wg, wu, wd, sg, su, sd, g, b):
    two_parts = isinstance(x1, tuple)
    nt, ne, tm = logits.shape
    t = nt * tm
    if tm != ROUTE_TILE:
        split = tm // ROUTE_TILE
        logits = logits.reshape(nt, ne, split, ROUTE_TILE).transpose(0, 2, 1, 3).reshape(nt * split, ne, ROUTE_TILE)
        nt, tm = nt * split, ROUTE_TILE
    assert SORT_ROWS >= TOP_K * tm + ne * (SEG_ROWS - 1) and SORT_ROWS < NO_ROW
    rid, wdense, tab = _router(logits, router_b)
    tab = tab[:, :, :3].astype(jnp.int32)
    seg, rows, base = tab[:, :, 0], tab[:, :, 1], tab[:, :, 2]
    bm = EXPERT_TILE
    region = base[-1] + rows[-1]
    padded = (region + bm - 1) // bm * bm
    pad_end = jnp.cumsum(padded)
    dst = (pad_end - padded)[None, :] + base
    n_blocks = -(-(t * TOP_K + nt * ne * (SEG_ROWS - 1)) // bm) + ne
    blk_start = jnp.arange(n_blocks, dtype=jnp.int32) * bm
    blk_e = jnp.minimum(jnp.sum((pad_end[None, :] <= blk_start[:, None]).astype(jnp.int32), axis=1), ne - 1)
    n_used = (pad_end[-1] // bm).astype(jnp.int32).reshape(1)
    tail_dst = jnp.concatenate([pad_end - padded + region, pad_end[-1:]])
    tail_n16 = jnp.concatenate([padded - region, n_blocks * bm - pad_end[-1:]]) // SEG_ROWS
    big, small, cnt = _piece_tables(dst, rows // SEG_ROWS, seg)
    x_sorted, pw = _dispatch(big, small, cnt, tail_dst, tail_n16, x1b if two_parts else (x1b, x1b), rid, wdense,
                             seg[:, None, :], rows[:, None, :], n_blocks * bm)
    y_sorted = _experts(blk_e, n_used, x_sorted, wg, wu, wd, layer)
    finish = lambda xf, xb, tile0, n_tiles: _ffn_out(big, small, cnt, xf, xb, pw, y_sorted, sg, su, sd, g, b,
                                                     tile0, n_tiles, row0=0)
    if not two_parts:
        return finish(x1, x1b, 0, nt)
    n_main = x1[0].shape[0] // tm
    return finish(x1[0], x1b[0], 0, n_main), finish(x1[1], x1b[1], n_main, nt - n_main)


def _rope_tables(pos):
    half = MLA_ROPE // 2
    inv = ROPE_THETA ** (-jnp.arange(half, dtype=F32) / half)
    ang = pos.astype(F32)[:, None] * inv
    reps = LANES // half
    return jnp.tile(jnp.cos(ang), (1, reps)), jnp.tile(jnp.sin(ang), (1, reps))


def _pack_layer0_weights(w_in0, gla_w_g2, gla_b_g, mla_q_norm_g, mla_kv_norm_g, mla_w_uq, mla_w_uk, mla_w_uv):
    d = w_in0.shape[0]
    hk, hv = GLA_HEADS * GLA_DK, GLA_HEADS * GLA_DV
    o_q, o_k, o_v, o_r = 0, hk, 2 * hk, 2 * hk + hv
    o_a = o_r + hv
    o_cq = o_a + GLA_RANK
    o_ckv = o_cq + MLA_Q_LORA
    o_kr = o_ckv + MLA_KV_LORA
    half = MLA_ROPE // 2
    zeros = lambda n: jnp.zeros((d, n), w_in0.dtype)
    kr1, kr2 = w_in0[:, o_kr:o_kr + half], w_in0[:, o_kr + half:o_kr + MLA_ROPE]
    w_in = jnp.concatenate([
        w_in0[:, o_q:o_a], w_in0[:, o_cq:o_kr],
        kr1, kr2, zeros(LANES - MLA_ROPE),
        kr2, kr1, zeros(LANES - MLA_ROPE),
        w_in0[:, o_a:o_cq], zeros(LANES - GLA_RANK)], axis=1).astype(BF16)
    assert w_in.shape[1] == _C_END
    w_g2 = jnp.concatenate([gla_w_g2, jnp.zeros((LANES - GLA_RANK, hk), gla_w_g2.dtype)], axis=0)
    uq = mla_w_uq.reshape(MLA_Q_LORA, MLA_HEADS, MLA_NOPE + MLA_ROPE)
    w_uq = jnp.concatenate([uq[:, :, :MLA_NOPE].reshape(MLA_Q_LORA, -1),
                            uq[:, :, MLA_NOPE:MLA_NOPE + half].reshape(MLA_Q_LORA, -1),
                            uq[:, :, MLA_NOPE + half:].reshape(MLA_Q_LORA, -1)], axis=1).astype(BF16)
    eye = jnp.eye(MLA_HEADS, dtype=mla_w_uk.dtype)
    w_abs = jnp.einsum('chn,hg->hngc', mla_w_uk, eye).reshape(MLA_HEADS * MLA_NOPE, MLA_HEADS * MLA_KV_LORA).astype(BF16)
    w_uvbd = jnp.einsum('chv,hg->hcgv', mla_w_uv, eye).reshape(MLA_HEADS * MLA_KV_LORA, MLA_HEADS * MLA_V).astype(BF16)
    place = np.zeros((2 * LANES, MLA_HEADS * LANES), np.float32)
    for h in range(MLA_HEADS):
        for j in range(half):
            place[h * half + j, h * LANES + j] = 1.0
            place[LANES + h * half + j, h * LANES + half + j] = 1.0
    return dict(w_in=w_in, w_g2=w_g2, b_g=gla_b_g.reshape(1, hk), q_norm=mla_q_norm_g.reshape(1, -1),
                kv_norm=mla_kv_norm_g.reshape(1, -1), w_uq=w_uq, w_abs=w_abs,
                place=jnp.asarray(place, BF16)), w_uvbd


def _pad_rows(x, n):
    return jnp.pad(x, ((0, n - x.shape[0]),) + ((0, 0),) * (x.ndim - 1))


def _kv_tiles(past, own):
    kvb = MLA_KV_TILE
    pad = lambda a: jnp.pad(a, ((0, 0), (0, -a.shape[1] % kvb), (0, 0)))
    kv = jnp.concatenate([pad(past), pad(own)], axis=1) if past.shape[1] else pad(own)
    b, n, w = kv.shape
    kt = kv.reshape(b, n // kvb, kvb, w).transpose(0, 1, 3, 2)
    return kt, kv


def kernel(x_prompt, x_sample, cache_mla_ckv, cache_mla_krope, state_gla, cache_pool, meta_tokens, w_in0, gla_w_g2, gla_b_g, gla_norm_g, mla_q_norm_g, mla_kv_norm_g, mla_w_uq, mla_w_uk, mla_w_uv, w_out0, pool_w, pool_scale, ln_g, ln_b, moe_router_w, moe_router_b, moe_w_gate, moe_w_up, moe_w_down, moe_sh_gate, moe_sh_up, moe_sh_down):
    bp, sp, d = x_prompt.shape
    bs, ss, _ = x_sample.shape
    n_meta = meta_tokens.shape[0]
    past_len = cache_mla_ckv.shape[1] - n_meta
    tp, tsm = bp * sp, bs * ss
    assert sp % MLA_Q_TILE == 0 and sp % POOL_TILE == 0 and sp % GLA_BLOCK == 0 and sp % CHUNK == 0
    assert ss <= CHUNK and past_len % CHUNK == 0 and n_meta <= CHUNK and ss <= GLA_BLOCK and n_meta <= GLA_BLOCK
    assert n_meta == POOL_MAX and ss >= POOL_MAX and tp % TOK_TILE == 0
    t_all = -(-(tp + tsm + n_meta) // TOK_TILE) * TOK_TILE
    o_s, o_m = tp, tp + tsm

    x_parts = (x_prompt.reshape(tp, d),
               _pad_rows(jnp.concatenate([x_sample.reshape(tsm, d), meta_tokens], axis=0), t_all - tp))
    pos = jnp.concatenate([jnp.tile(n_meta + jnp.arange(sp), bp), jnp.tile(n_meta + past_len + jnp.arange(ss), bs),
                           jnp.arange(n_meta), jnp.zeros((t_all - o_m - n_meta,), jnp.int32)])
    cos_t, sin_t = _rope_tables(pos)
    w0, w_uvbd = _pack_layer0_weights(w_in0, gla_w_g2, gla_b_g, mla_q_norm_g, mla_kv_norm_g, mla_w_uq, mla_w_uk, mla_w_uv)

    q, k, v, sr, lg, q_all, ckv, kr, k_all = _inproj(*x_parts, cos_t, sin_t, w0)

    tables = _gla_tables(GLA_BLOCK)
    g_gla = gla_norm_g.reshape(1, GLA_DV)
    c = GLA_BLOCK
    hk = GLA_HEADS * GLA_DK

    def small(a):
        sm = jnp.pad(a[o_s:o_m].reshape(bs, ss, -1), ((0, 0), (0, c - ss), (0, 0)))
        me = jnp.pad(a[o_m:o_m + n_meta], ((0, c - n_meta), (0, 0)))[None]
        return jnp.concatenate([sm, me], axis=0).reshape((bs + 1) * c, -1)

    s0_small = jnp.concatenate([state_gla.reshape(bs, hk, GLA_DV), jnp.zeros((1, hk, GLA_DV), F32)], axis=0)
    og_small, st_small = _gla(small(q), small(k), small(v), small(lg), small(sr), s0_small, g_gla, tables, bs + 1, 1)
    s0_prompt = jnp.broadcast_to(st_small[bs:], (bp, hk, GLA_DV))
    og_prompt, st_prompt = _gla(q, k, v, lg, sr, s0_prompt, g_gla, tables, bp, sp // c)
    og_small = og_small.reshape(bs + 1, c, -1)
    o_gla = (og_prompt, _pad_rows(jnp.concatenate([og_small[:bs, :ss].reshape(tsm, -1), og_small[bs, :n_meta]], axis=0),
                                  t_all - tp))

    k_meta = k_all[o_m:o_m + n_meta]
    kt_m, kv_m = _kv_tiles(jnp.zeros((1, 0, MLA_QK), BF16), k_meta[None])
    ol_meta = _mla(q_all[o_m:o_m + n_meta], kt_m, kv_m, 1, 1, n_meta, 0, 0, n_meta)
    cache = jnp.concatenate([cache_mla_ckv, cache_mla_krope,
                             jnp.zeros(cache_mla_ckv.shape[:2] + (MLA_QK - MLA_KV_LORA - MLA_ROPE,), F32)], axis=-1).astype(BF16)
    kt_s, kv_s = _kv_tiles(cache, k_all[o_s:o_m].reshape(bs, ss, MLA_QK))
    ol_sample = _mla(q_all[o_s:o_m], kt_s, kv_s, bs, 1, ss, 0, n_meta + past_len, ss)
    kt_p, kv_p = _kv_tiles(jnp.broadcast_to(k_meta[None], (bp, n_meta, MLA_QK)), k_all[:tp].reshape(bp, sp, MLA_QK))
    ol_prompt = _mla(q_all, kt_p, kv_p, bp, sp // MLA_Q_TILE, MLA_Q_TILE, 0, n_meta, sp)
    o_lat = (ol_prompt, _pad_rows(jnp.concatenate([ol_sample, ol_meta], axis=0), t_all - tp))

    ln = lambda l, j: (ln_g[l, j].reshape(1, d), ln_b[l, j].reshape(1, d))
    rw = lambda l: moe_router_w[l].T
    moe_w = lambda l: (l, moe_w_gate, moe_w_up, moe_w_down,
                       moe_sh_gate[l].astype(BF16), moe_sh_up[l].astype(BF16), moe_sh_down[l].astype(BF16))
    x1, x1b, logits = _outproj(o_gla, o_lat, x_parts, w_uvbd, w_out0.astype(BF16), *ln(0, 0), rw(0))
    x2, _ = _moe(x1, x1b, logits, moe_router_b[0], *moe_w(0), *ln(0, 1))

    pm = POOL_MAX
    zrow = jnp.zeros((1, d), F32)
    hist_p = jnp.broadcast_to(jnp.concatenate([zrow, x2[o_m + n_meta - (pm - 1):o_m + n_meta]], axis=0)[None], (bp, pm, d))
    hist_s = jnp.concatenate([jnp.zeros((bs, 1, d), F32), cache_pool], axis=1)
    pool_wb = pool_w.astype(BF16)
    ps = pool_scale.reshape(1, d)
    t1 = -(-(tp + tsm) // TOK_TILE) * TOK_TILE
    y1p = _pool(x2, hist_p, pool_wb, ps, *ln(1, 0), rw(1), bp, sp // POOL_TILE, POOL_TILE, 0)
    y1s = _pool(x2[o_s:o_m], hist_s, pool_wb, ps, *ln(1, 0), rw(1), bs, 1, ss, 0)
    x3, x3b = [(a, _pad_rows(b_, t1 - tp)) for a, b_ in zip(y1p[:2], y1s[:2])]
    assert POOL_TILE == TOK_TILE
    rest_logits = jnp.pad(y1s[2].transpose(1, 0, 2).reshape(N_EXPERTS, tsm), ((0, 0), (0, t1 - tp - tsm)))
    logits1 = jnp.concatenate([y1p[2], rest_logits.reshape(N_EXPERTS, -1, TOK_TILE).transpose(1, 0, 2)], axis=0)
    (x4_prompt, _), (x4_rest, _) = _moe(x3, x3b, logits1, moe_router_b[1], *moe_w(1), *ln(1, 1))

    y_prompt = x4_prompt.reshape(bp, sp, d)
    y_sample = x4_rest[:tsm].reshape(bs, ss, d)
    ckv_meta, kr_meta = ckv[o_m:o_m + n_meta], kr[o_m:o_m + n_meta, :MLA_ROPE]
    p_ckv = jnp.concatenate([jnp.broadcast_to(ckv_meta[None], (bp, n_meta, MLA_KV_LORA)),
                             ckv[:tp].reshape(bp, sp, MLA_KV_LORA)], axis=1)
    p_kr = jnp.concatenate([jnp.broadcast_to(kr_meta[None], (bp, n_meta, MLA_ROPE)),
                            kr[:tp, :MLA_ROPE].reshape(bp, sp, MLA_ROPE)], axis=1)
    p_gla = st_prompt.reshape(bp, GLA_HEADS, GLA_DK, GLA_DV)
    p_pool = jnp.stack([x2[(b_ + 1) * sp - (pm - 1):(b_ + 1) * sp] for b_ in range(bp)])
    s_ckv = ckv[o_s:o_m].reshape(bs, ss, MLA_KV_LORA)
    s_kr = kr[o_s:o_m, :MLA_ROPE].reshape(bs, ss, MLA_ROPE)
    s_gla = st_small[:bs].reshape(bs, GLA_HEADS, GLA_DK, GLA_DV)
    s_pool = x2[o_s:o_m].reshape(bs, ss, d)[:, ss - (pm - 1):]
    return (y_prompt, y_sample, p_ckv, p_kr, p_gla, p_pool, s_ckv, s_kr, s_gla, s_pool)
```

```python
import functools
import math

import jax
import jax.numpy as jnp
import numpy as np
from jax import lax
from jax.experimental import pallas as pl
from jax.experimental.pallas import tpu as pltpu

F32 = jnp.float32
BF16 = jnp.bfloat16

CHUNK = 64
DEPTH = 2
ALPHA = (2 * DEPTH) ** 0.25
LN_EPS = 1e-5
RMS_EPS = 1e-6
NEG_INF = -1e30
GLA_HEADS = 4
GLA_DK = 64
GLA_DV = 128
GLA_RANK = 16
GLA_TAU = 16.0
MLA_HEADS = 8
MLA_Q_LORA = 256
MLA_KV_LORA = 128
MLA_NOPE = 64
MLA_ROPE = 32
MLA_V = 64
MLA_SCALE = (MLA_NOPE + MLA_ROPE) ** -0.5
ROPE_THETA = 10000.0
POOL_WINDOWS = (2, 4, 8, 16)
POOL_MAX = 16
N_EXPERTS = 64
TOP_K = 8
N_GROUPS = 8
TOPK_GROUPS = 4
ROUTED_SCALE = 2.5

LANES = 128
MXU_DIM = 256

TOK_TILE = 512
GLA_BLOCK = 128
MLA_Q_TILE = 256
MLA_KV_TILE = 256
MLA_QK = 256
MLA_ROW_STRIP = 128
LOG2E = math.log2(math.e)
EXPERT_TILE = 1024
EXPERT_SUBTILE = 512
POOL_TILE = 512
SEG_ROWS = 16
BIG_PIECE = 4
DRAIN_UNITS = (64, 16, 4, 1)
ROUTE_TILE = 512
SORT_ROWS = 5120
K_BIG = (TOP_K * ROUTE_TILE + N_EXPERTS * (SEG_ROWS - 1)) // (SEG_ROWS * BIG_PIECE)
K_SMALL = N_EXPERTS * (BIG_PIECE - 1)
PACK_BITS = 9
NO_ROW = 8191.0
DISPATCH_STRIP = 512
COMBINE_STRIP = 2560
VMEM_LIMIT = 56 * 1024 * 1024


def _params(sem, vmem=VMEM_LIMIT):
    return pltpu.CompilerParams(dimension_semantics=sem, vmem_limit_bytes=vmem)


def _const_spec(shape):
    nd = len(shape)
    return pl.BlockSpec(shape, lambda *_: (0,) * nd)


def _split3(x):
    hi = x.astype(BF16)
    r = x - hi.astype(F32)
    mid = r.astype(BF16)
    lo = (r - mid.astype(F32)).astype(BF16)
    return hi, mid, lo


def _dot(a, b):
    return jnp.dot(a, b, preferred_element_type=F32)


def _dot_f32(a, b):
    a_hi = a.astype(BF16)
    a_lo = (a - a_hi.astype(F32)).astype(BF16)
    b_hi = b.astype(BF16)
    b_lo = (b - b_hi.astype(F32)).astype(BF16)
    return _dot(a_hi, b_hi) + (_dot(a_lo, b_hi) + _dot(a_hi, b_lo))


def _dot_f32_nt(a, b):
    nt = lambda u, v: lax.dot_general(u, v, (((1,), (1,)), ((), ())), preferred_element_type=F32)
    a_hi = a.astype(BF16)
    a_lo = (a - a_hi.astype(F32)).astype(BF16)
    b_hi = b.astype(BF16)
    b_lo = (b - b_hi.astype(F32)).astype(BF16)
    return nt(a_hi, b_hi) + (nt(a_lo, b_hi) + nt(a_hi, b_lo))


def _silu(x):
    return x * (1.0 / (1.0 + jnp.exp(-x)))


def _layer_norm(x, g, b):
    mu = jnp.mean(x, axis=-1, keepdims=True)
    xc = x - mu
    var = jnp.mean(xc * xc, axis=-1, keepdims=True)
    return xc * lax.rsqrt(var + LN_EPS) * g + b


def _rms(x, g):
    return x * lax.rsqrt(jnp.mean(x * x, axis=-1, keepdims=True) + RMS_EPS) * g


_C_Q, _C_K, _C_V, _C_R, _C_CQ, _C_CKV, _C_KR, _C_KRS, _C_A, _C_END = (
    0, 256, 512, 1024, 1536, 1792, 1920, 2048, 2176, 2304)


def _two_part_specs(tm, n, n_main):
    return [pl.BlockSpec((tm, n), lambda i, *_: (jnp.minimum(i, n_main - 1), 0)),
            pl.BlockSpec((tm, n), lambda i, *_: (jnp.maximum(i - n_main, 0), 0))]


def _pick(main_ref, rest_ref, n_main):
    return jnp.where(pl.program_id(0) < n_main, main_ref[...], rest_ref[...])


def _inproj_kernel(xm_ref, xr_ref, cos_ref, sin_ref, w_ref, wg2_ref, bg_ref, qn_ref, kvn_ref, wuq_ref, wabs_ref,
                   place_ref, q_ref, k_ref, v_ref, sr_ref, lg_ref, qall_ref, ckv_ref, kr_ref, kall_ref, *, n_main):
    h = _dot(_pick(xm_ref, xr_ref, n_main).astype(BF16), w_ref[...])
    q_ref[...] = (h[:, _C_Q:_C_K] * GLA_DK ** -0.5).astype(BF16)
    k_ref[...] = h[:, _C_K:_C_V].astype(BF16)
    v_ref[...] = h[:, _C_V:_C_R].astype(BF16)
    sr_ref[...] = _silu(h[:, _C_R:_C_CQ]).astype(BF16)
    z = _dot_f32(h[:, _C_A:_C_END], wg2_ref[...]) + bg_ref[...]
    lg_ref[...] = (jnp.minimum(z, 0.0) - jnp.log(1.0 + jnp.exp(-jnp.abs(z)))) * (1.0 / GLA_TAU)
    cqn = _rms(h[:, _C_CQ:_C_CKV], qn_ref[...]).astype(BF16)
    qh = _dot(cqn, wuq_ref[...])
    n_nope = MLA_HEADS * MLA_NOPE
    cos, sin = cos_ref[...], sin_ref[...]
    x1, x2 = qh[:, n_nope:n_nope + LANES], qh[:, n_nope + LANES:]
    qscale = MLA_SCALE * LOG2E
    rot = (jnp.concatenate([x1 * cos - x2 * sin, x1 * sin + x2 * cos], axis=1) * qscale).astype(BF16)
    qlat = (_dot(qh[:, :n_nope].astype(BF16), wabs_ref[...]) * qscale).astype(BF16)
    qrope = _dot(rot, place_ref[...]).astype(BF16)
    for hd in range(MLA_HEADS):
        qall_ref[:, MLA_QK * hd:MLA_QK * hd + LANES] = qlat[:, LANES * hd:LANES * (hd + 1)]
        qall_ref[:, MLA_QK * hd + LANES:MLA_QK * (hd + 1)] = qrope[:, LANES * hd:LANES * (hd + 1)]
    ckv = _rms(h[:, _C_CKV:_C_KR], kvn_ref[...])
    lane = lax.broadcasted_iota(jnp.int32, (1, LANES), 1)
    sgn = jnp.where(lane < MLA_ROPE // 2, -1.0, 1.0)
    kr = h[:, _C_KR:_C_KRS] * cos + h[:, _C_KRS:_C_A] * (sin * sgn)
    ckv_ref[...] = ckv
    kr_ref[...] = kr
    kall_ref[:, :LANES] = ckv.astype(BF16)
    kall_ref[:, LANES:] = kr.astype(BF16)


def _inproj(x_main, x_rest, cos_t, sin_t, w):
    t = x_main.shape[0] + x_rest.shape[0]
    d = x_main.shape[1]
    tm = TOK_TILE
    n_main = x_main.shape[0] // tm
    row = lambda n: pl.BlockSpec((tm, n), lambda i: (i, 0))
    consts = [w['w_in'], w['w_g2'], w['b_g'], w['q_norm'], w['kv_norm'], w['w_uq'], w['w_abs'], w['place']]
    outs = [(GLA_HEADS * GLA_DK, BF16), (GLA_HEADS * GLA_DK, BF16), (GLA_HEADS * GLA_DV, BF16),
            (GLA_HEADS * GLA_DV, BF16), (GLA_HEADS * GLA_DK, F32), (MLA_HEADS * MLA_QK, BF16),
            (LANES, F32), (LANES, F32), (MLA_QK, BF16)]
    return pl.pallas_call(
        functools.partial(_inproj_kernel, n_main=n_main),
        grid=(t // tm,),
        in_specs=_two_part_specs(tm, d, n_main) + [row(LANES), row(LANES)] + [_const_spec(c.shape) for c in consts],
        out_specs=[row(n) for n, _ in outs],
        out_shape=[jax.ShapeDtypeStruct((t, n), dt) for n, dt in outs],
        compiler_params=_params(("parallel",)),
        name="inproj",
    )(x_main, x_rest, cos_t, sin_t, *consts)


def _gla_tables(c):
    levels = int(math.log2(c))
    assert 1 << levels == c
    t = np.arange(c)[:, None]
    u = np.arange(c)[None, :]
    mats, masks = [], []
    for l in range(levels):
        half = 1 << l
        base = (t // half) * half
        upper = ((t >> l) & 1) == 1
        a_q = (u >= base) & (u <= t)
        a_k = (u > t) & (u <= base + half - 1)
        mats.append(np.where(upper, a_q, a_k))
        same = (t >> (l + 1)) == (u >> (l + 1))
        masks.append(same & upper & (((u >> l) & 1) == 0))
    mats.append(u <= t)
    mats.append(u > t)
    masks.append(t == u)
    amat = np.concatenate(mats, axis=0).astype(np.float32)
    mask = np.stack([np.tile(m, (GLA_HEADS, 1)) for m in masks]).astype(np.float32)
    return jnp.asarray(amat, BF16), jnp.asarray(mask, F32), levels


def _gla_kernel(q_ref, k_ref, v_ref, lg_ref, sr_ref, s0_ref, amat_ref, mask_ref, g_ref,
                o_ref, sout_ref, state, *, levels):
    c = q_ref.shape[0]
    hk = GLA_HEADS * GLA_DK
    step = pl.program_id(1)

    @pl.when(step == 0)
    def _():
        state[...] = s0_ref[0]

    lg = lg_ref[...]
    parts = _split3(lg)
    e3 = _dot(amat_ref[...], jnp.concatenate(parts, axis=1))
    ex = jnp.exp(e3[:, :hk] + e3[:, hk:2 * hk] + e3[:, 2 * hk:])
    q = q_ref[...].astype(F32)
    k = k_ref[...].astype(F32)
    v = v_ref[...]
    head = lax.broadcasted_iota(jnp.int32, (1, hk), 1) // GLA_DK

    def per_head(xf):
        return jnp.concatenate([jnp.where(head == h, xf, 0.0) for h in range(GLA_HEADS)], axis=0).astype(BF16)

    att = jnp.zeros((GLA_HEADS * c, c), F32)
    for l in range(levels + 1):
        if l < levels:
            el = ex[l * c:(l + 1) * c]
            ql, kl = q * el, k * el
        else:
            ql, kl = q, k
        a = lax.dot_general(per_head(ql), kl.astype(BF16), (((1,), (1,)), ((), ())), preferred_element_type=F32)
        att = att + a * mask_ref[l]
    s_prev = state[...]
    o_inter = _dot(per_head(q * ex[levels * c:(levels + 1) * c]), s_prev.astype(BF16))
    att = att.astype(BF16)
    g = g_ref[...]
    for h in range(GLA_HEADS):
        o = o_inter[h * c:(h + 1) * c] + _dot(att[h * c:(h + 1) * c], v[:, GLA_DV * h:GLA_DV * (h + 1)])
        o = _rms(o, g)
        o_ref[:, GLA_DV * h:GLA_DV * (h + 1)] = (o * sr_ref[:, GLA_DV * h:GLA_DV * (h + 1)].astype(F32)).astype(BF16)
    kr = (k * ex[(levels + 1) * c:]).astype(BF16)
    upd = lax.dot_general(kr, v, (((0,), (0,)), ((), ())), preferred_element_type=F32)
    ones = jnp.ones((c, GLA_DV), BF16)
    b_last = sum(lax.dot_general(p, ones, (((0,), (0,)), ((), ())), preferred_element_type=F32) for p in parts)
    new = jnp.exp(b_last) * s_prev + jnp.concatenate(
        [upd[GLA_DK * h:GLA_DK * (h + 1), GLA_DV * h:GLA_DV * (h + 1)] for h in range(GLA_HEADS)], axis=0)
    state[...] = new

    @pl.when(step == pl.num_programs(1) - 1)
    def _():
        sout_ref[0] = new


def _gla(q, k, v, lg, sr, s0, g, tables, batch, blocks_per_seq):
    amat, mask, levels = tables
    c = GLA_BLOCK
    hk, hv = GLA_HEADS * GLA_DK, GLA_HEADS * GLA_DV
    row = lambda n: pl.BlockSpec((c, n), lambda b, s: (b * blocks_per_seq + s, 0))
    st = pl.BlockSpec((1, hk, GLA_DV), lambda b, s: (b, 0, 0))
    return pl.pallas_call(
        functools.partial(_gla_kernel, levels=levels),
        grid=(batch, blocks_per_seq),
        in_specs=[row(hk), row(hk), row(hv), row(hk), row(hv), st,
                  _const_spec(amat.shape), _const_spec(mask.shape), _const_spec(g.shape)],
        out_specs=[row(hv), st],
        out_shape=[jax.ShapeDtypeStruct((batch * blocks_per_seq * c, hv), BF16),
                   jax.ShapeDtypeStruct((batch, hk, GLA_DV), F32)],
        scratch_shapes=[pltpu.VMEM((hk, GLA_DV), F32)],
        compiler_params=_params(("parallel", "arbitrary")),
        name="gla",
    )(q, k, v, lg, sr, s0, amat, mask, g)


def _mla_kernel(q_ref, kt_ref, kv_ref, o_ref, s_sc, p_sc, linv_sc, acc_sc, *, tq, n_past, past_valid, own_valid,
                diag_chunks):
    kvb = MLA_KV_TILE
    qi = pl.program_id(1)
    qs = jnp.concatenate([q_ref[:, MLA_QK * h:MLA_QK * (h + 1)] for h in range(MLA_HEADS)], axis=0)
    m_rows = MLA_HEADS * tq
    n_vis = n_past + qi + 1

    def score(j, mask):
        s = _dot(qs, kt_ref[0, j])
        s_sc[j] = s if mask is None else jnp.where(mask, s, NEG_INF)

    def scores(lo, hi):
        def body(j, carry):
            score(j, None)
            return carry
        lax.fori_loop(lo, hi, body, 0)

    col = lax.broadcasted_iota(jnp.int32, (m_rows, kvb), 1)
    n_past_full = past_valid // kvb
    if n_past_full:
        scores(0, n_past_full)
    if n_past_full < n_past:
        score(n_past_full, col < past_valid - n_past_full * kvb)
    scores(n_past, n_past + qi)
    mask = col < own_valid - qi * kvb
    if diag_chunks:
        qrow = lax.broadcasted_iota(jnp.int32, (m_rows, kvb), 0) % tq
        mask = mask & (col // CHUNK <= qrow // CHUNK)
    score(n_past + qi, mask)

    rc = MLA_ROW_STRIP

    def strip(r, carry):
        rows = pl.ds(pl.multiple_of(r * rc, rc), rc)
        mx = lax.fori_loop(0, n_vis, lambda j, m: jnp.maximum(m, s_sc[j, rows, :]),
                           jnp.full((rc, kvb), -jnp.inf, F32))
        m = jnp.max(mx, axis=1, keepdims=True)

        def body(j, l):
            p = jnp.exp2(s_sc[j, rows, :] - m)
            p_sc[j, rows, :] = p.astype(BF16)
            return l + p
        l = lax.fori_loop(0, n_vis, body, jnp.zeros((rc, kvb), F32))
        linv_sc[rows, :] = 1.0 / jnp.sum(l, axis=1, keepdims=True)
        return carry
    lax.fori_loop(0, m_rows // rc, strip, 0)

    acc_sc[...] = jnp.zeros(acc_sc.shape, F32)

    def pv(j, carry):
        vblk = kv_ref[0, pl.ds(pl.multiple_of(j * kvb, kvb), kvb), :MLA_KV_LORA]
        acc_sc[...] += _dot(p_sc[j], vblk)
        return carry
    lax.fori_loop(0, n_vis, pv, 0)
    o = acc_sc[...] * linv_sc[...]
    for h in range(MLA_HEADS):
        o_ref[:, MLA_KV_LORA * h:MLA_KV_LORA * (h + 1)] = o[h * tq:(h + 1) * tq].astype(BF16)


def _mla(q_all, kt, kv, batch, nq, tq, q_block0, past_valid, own_valid):
    n_kv = kt.shape[1]
    n_past = -(-past_valid // MLA_KV_TILE)
    assert nq == 1 or tq == MLA_KV_TILE
    assert n_kv == n_past + nq
    return pl.pallas_call(
        functools.partial(_mla_kernel, tq=tq, n_past=n_past, past_valid=past_valid, own_valid=own_valid,
                          diag_chunks=nq > 1),
        grid=(batch, nq),
        in_specs=[pl.BlockSpec((tq, MLA_HEADS * MLA_QK), lambda b, i: (q_block0 + b * nq + i, 0)),
                  pl.BlockSpec((1,) + kt.shape[1:], lambda b, i: (b, 0, 0, 0)),
                  pl.BlockSpec((1,) + kv.shape[1:], lambda b, i: (b, 0, 0))],
        out_specs=pl.BlockSpec((tq, MLA_HEADS * MLA_KV_LORA), lambda b, i: (b * nq + i, 0)),
        out_shape=jax.ShapeDtypeStruct((batch * nq * tq, MLA_HEADS * MLA_KV_LORA), BF16),
        scratch_shapes=[pltpu.VMEM((n_kv, MLA_HEADS * tq, MLA_KV_TILE), F32),
                        pltpu.VMEM((n_kv, MLA_HEADS * tq, MLA_KV_TILE), BF16),
                        pltpu.VMEM((MLA_HEADS * tq, 1), F32),
                        pltpu.VMEM((MLA_HEADS * tq, MLA_KV_LORA), F32)],
        compiler_params=_params(("parallel", "arbitrary")),
        name="mla",
    )(q_all, kt, kv)


def _post_mixer(x, out, g, b, rw_ref, x1_ref, x1b_ref, lgt_ref):
    y = _layer_norm(ALPHA * x + out, g, b)
    x1_ref[...] = y
    x1b_ref[...] = y.astype(BF16)
    lgt_ref[0] = _dot_f32_nt(rw_ref[...], y)


def _outproj_kernel(ogm_ref, ogr_ref, olm_ref, olr_ref, xm_ref, xr_ref, wuv_ref, wo_ref, g_ref, b_ref, rw_ref,
                    x1_ref, x1b_ref, lgt_ref, *, n_main):
    og = _pick(ogm_ref, ogr_ref, n_main)
    o_mla = _dot(_pick(olm_ref, olr_ref, n_main), wuv_ref[...]).astype(BF16)
    n_gla = og.shape[1]
    out = _dot(og, wo_ref[:n_gla, :]) + _dot(o_mla, wo_ref[n_gla:, :])
    _post_mixer(_pick(xm_ref, xr_ref, n_main), out, g_ref[...], b_ref[...], rw_ref, x1_ref, x1b_ref, lgt_ref)


def _outproj(o_gla, o_lat, x, w_uvbd, w_out, g, b, rw):
    t, d = x[0].shape[0] + x[1].shape[0], x[0].shape[1]
    tm = TOK_TILE
    n_main = x[0].shape[0] // tm
    row = lambda n: pl.BlockSpec((tm, n), lambda i: (i, 0))
    consts = [w_uvbd, w_out, g, b, rw]
    parts = [*o_gla, *o_lat, *x]
    return pl.pallas_call(
        functools.partial(_outproj_kernel, n_main=n_main),
        grid=(t // tm,),
        in_specs=[s for a in (o_gla, o_lat, x) for s in _two_part_specs(tm, a[0].shape[1], n_main)]
        + [_const_spec(c.shape) for c in consts],
        out_specs=[row(d), row(d), pl.BlockSpec((1, N_EXPERTS, tm), lambda i: (i, 0, 0))],
        out_shape=[jax.ShapeDtypeStruct((t, d), F32), jax.ShapeDtypeStruct((t, d), BF16),
                   jax.ShapeDtypeStruct((t // tm, N_EXPERTS, tm), F32)],
        compiler_params=_params(("parallel",)),
        name="outproj",
    )(*parts, *consts)


def _pool_kernel(x_ref, prev_ref, hist_ref, pw_ref, ps_ref, g_ref, b_ref, rw_ref, x1_ref, x1b_ref, lgt_ref):
    ts = x_ref.shape[0]
    pm = POOL_MAX
    x = x_ref[...]
    prev = jnp.where(pl.program_id(1) == 0, hist_ref[0], prev_ref[...])
    xc = jnp.concatenate([prev, x], axis=0)
    grp = x.shape[1] // len(POOL_WINDOWS)
    outs = []
    for gi, w in enumerate(POOL_WINDOWS):
        s = xc[:, gi * grp:(gi + 1) * grp]
        span = 1
        while span < w:
            s = s + pltpu.roll(s, span, 0)
            span *= 2
        win = s[pm:]
        mix = (win * (1.0 / w) - x[:, gi * grp:(gi + 1) * grp]).astype(BF16)
        outs.append(_dot(mix, pw_ref[gi]))
    out = jnp.concatenate(outs, axis=1) * ps_ref[...]
    _post_mixer(x, out, g_ref[...], b_ref[...], rw_ref, x1_ref, x1b_ref, lgt_ref)


def _pool(x_all, hist, pool_w, pool_scale, g, b, rw, batch, tiles_per_seq, ts, tile0):
    d = x_all.shape[1]
    pm = POOL_MAX
    per = ts // pm
    consts = [pool_w, pool_scale, g, b, rw]
    row = lambda n: pl.BlockSpec((ts, n), lambda bb, i: (bb * tiles_per_seq + i, 0))
    return pl.pallas_call(
        _pool_kernel,
        grid=(batch, tiles_per_seq),
        in_specs=[pl.BlockSpec((ts, d), lambda bb, i: (tile0 + bb * tiles_per_seq + i, 0)),
                  pl.BlockSpec((pm, d), lambda bb, i: (jnp.maximum((tile0 + bb * tiles_per_seq + i) * per - 1, 0), 0)),
                  pl.BlockSpec((1, pm, d), lambda bb, i: (bb, 0, 0))] + [_const_spec(c.shape) for c in consts],
        out_specs=[row(d), row(d), pl.BlockSpec((1, N_EXPERTS, ts), lambda bb, i: (bb * tiles_per_seq + i, 0, 0))],
        out_shape=[jax.ShapeDtypeStruct((batch * tiles_per_seq * ts, d), F32),
                   jax.ShapeDtypeStruct((batch * tiles_per_seq * ts, d), BF16),
                   jax.ShapeDtypeStruct((batch * tiles_per_seq, N_EXPERTS, ts), F32)],
        compiler_params=_params(("parallel", "arbitrary")),
        name="pool",
    )(x_all, x_all, hist, *consts)


def _experts_kernel(be_ref, nb_ref, x_ref, wg_ref, wu_ref, wd_ref, y_ref, wg_sc, wu_sc, wd_sc):
    i = pl.program_id(0)

    @pl.when((i == 0) | (be_ref[i] != be_ref[jnp.maximum(i - 1, 0)]))
    def _():
        wg_sc[...] = wg_ref[0, 0].astype(BF16)
        wu_sc[...] = wu_ref[0, 0].astype(BF16)
        wd_sc[...] = wd_ref[0, 0].astype(BF16)

    @pl.when(i < nb_ref[0])
    def _():
        sub = EXPERT_SUBTILE
        for r in range(x_ref.shape[0] // sub):
            x = x_ref[r * sub:(r + 1) * sub, :]
            hmid = (_silu(_dot(x, wg_sc[...])) * _dot(x, wu_sc[...])).astype(BF16)
            y_ref[r * sub:(r + 1) * sub, :] = _dot(hmid, wd_sc[...]).astype(BF16)

    @pl.when(pl.program_id(0) >= nb_ref[0])
    def _():
        y_ref[...] = jnp.zeros_like(y_ref)


def _experts(blk_e, n_used, x_sorted, wg, wu, wd, layer):
    p, d = x_sorted.shape
    f = wg.shape[3]
    bm = EXPERT_TILE
    blk = lambda i, be, nb: (i, 0)
    grid_spec = pltpu.PrefetchScalarGridSpec(
        num_scalar_prefetch=2,
        grid=(p // bm,),
        in_specs=[pl.BlockSpec((bm, d), lambda i, be, nb: (jnp.minimum(i, nb[0] - 1), 0)),
                  pl.BlockSpec((1, 1, d, f), lambda i, be, nb: (layer, be[i], 0, 0)),
                  pl.BlockSpec((1, 1, d, f), lambda i, be, nb: (layer, be[i], 0, 0)),
                  pl.BlockSpec((1, 1, f, d), lambda i, be, nb: (layer, be[i], 0, 0))],
        out_specs=pl.BlockSpec((bm, d), blk),
        scratch_shapes=[pltpu.VMEM((d, f), BF16), pltpu.VMEM((d, f), BF16), pltpu.VMEM((f, d), BF16)],
    )
    return pl.pallas_call(
        _experts_kernel,
        grid_spec=grid_spec,
        out_shape=jax.ShapeDtypeStruct((p, d), BF16),
        compiler_params=_params(("arbitrary",)),
        name="experts",
    )(blk_e, n_used, x_sorted, wg, wu, wd)


def _segment_copies(tile, slot, dst_ref, n16_ref, seg_ref, make, entries=N_EXPERTS):
    big = SEG_ROWS * BIG_PIECE

    def per_expert(e, carry):
        j = tile * N_EXPERTS + e
        d0, s0, n = dst_ref[j], seg_ref[j], n16_ref[j]
        n_big = n // BIG_PIECE

        def big_piece(c, carry2):
            make(pl.multiple_of(d0 + c * big, SEG_ROWS), pl.multiple_of(s0 + c * big, SEG_ROWS), big, slot).start()
            return carry2
        lax.fori_loop(0, n_big, big_piece, 0)
        d1, s1 = d0 + n_big * big, s0 + n_big * big

        def small_piece(c, carry2):
            make(pl.multiple_of(d1 + c * SEG_ROWS, SEG_ROWS), pl.multiple_of(s1 + c * SEG_ROWS, SEG_ROWS),
                 SEG_ROWS, slot).start()
            return carry2
        lax.fori_loop(0, n - n_big * BIG_PIECE, small_piece, 0)
        return carry
    lax.fori_loop(0, entries, per_expert, 0)


def _start_pieces(tile, slot, big_ref, small_ref, cnt_ref, make):
    def run(tab_ref, kmax, n, rows):
        def body(k, carry):
            v = tab_ref[tile * kmax + k]
            make(pl.multiple_of((v >> PACK_BITS) * SEG_ROWS, SEG_ROWS),
                 pl.multiple_of((v & ((1 << PACK_BITS) - 1)) * SEG_ROWS, SEG_ROWS), rows, slot).start()
            return carry
        lax.fori_loop(0, n, body, 0)
    run(big_ref, K_BIG, cnt_ref[2 * tile], SEG_ROWS * BIG_PIECE)
    run(small_ref, K_SMALL, cnt_ref[2 * tile + 1], SEG_ROWS)


def _tile_rows(tile, cnt_ref):
    return cnt_ref[2 * tile] * (SEG_ROWS * BIG_PIECE) + cnt_ref[2 * tile + 1] * SEG_ROWS


def _piece_tables(dst, n16, seg):
    nt, ne = n16.shape
    big_rows = SEG_ROWS * BIG_PIECE
    n_big, n_small = n16 // BIG_PIECE, n16 % BIG_PIECE

    def flat(counts, first_dst, first_src, step, kmax):
        cum = jnp.cumsum(counts, axis=1)[:, None, :]
        before = cum - counts[:, None, :]
        k = jnp.arange(kmax, dtype=jnp.int32)[None, :, None]
        mine = (before <= k) & (k < cum)
        off = step * (k - before)
        d = jnp.sum(jnp.where(mine, first_dst[:, None, :] + off, 0), axis=2)
        s = jnp.sum(jnp.where(mine, first_src[:, None, :] + off, 0), axis=2)
        return (((d // SEG_ROWS) << PACK_BITS) | (s // SEG_ROWS)).reshape(-1).astype(jnp.int32), cum[:, 0, -1]
    big, cnt_big = flat(n_big, dst, seg, big_rows, K_BIG)
    small, cnt_small = flat(n_small, dst + n_big * big_rows, seg + n_big * big_rows, SEG_ROWS, K_SMALL)
    return big, small, jnp.stack([cnt_big, cnt_small], axis=1).reshape(-1).astype(jnp.int32)


def _drain(rows, slot, make):
    left = rows // SEG_ROWS
    for unit in DRAIN_UNITS:
        n = left // unit

        def body(c, carry, unit=unit):
            make(0, 0, unit * SEG_ROWS, slot).wait()
            return carry
        lax.fori_loop(0, n, body, 0)
        left = left - n * unit


def _ffn_out_kernel(big_ref, small_ref, cnt_ref, x_ref, xb_ref, pw_ref, ys_ref,
                    sg_ref, su_ref, sd_ref, g_ref, b_ref, y_ref, yb_ref, ybuf, racc, sem, *, tile0):
    i = pl.program_id(0)
    nt = pl.num_programs(0)
    slot = i % 2
    tile = i + tile0

    def make(d0, s0, rows, sl):
        return pltpu.make_async_copy(ys_ref.at[pl.ds(d0, rows)], ybuf.at[sl, pl.ds(s0, rows)], sem.at[sl])

    @pl.when(i == 0)
    def _():
        ybuf[...] = jnp.zeros_like(ybuf)
        _start_pieces(tile, 0, big_ref, small_ref, cnt_ref, make)

    @pl.when(i + 1 < nt)
    def _():
        _start_pieces(tile + 1, 1 - slot, big_ref, small_ref, cnt_ref, make)

    xb = xb_ref[...]
    hmid = (_silu(_dot(xb, sg_ref[...])) * _dot(xb, su_ref[...])).astype(BF16)
    shared = _dot(hmid, sd_ref[...])
    used = _tile_rows(tile, cnt_ref)
    _drain(used, slot, make)

    rs = COMBINE_STRIP

    def strip(s):
        return lax.dot_general(pw_ref[0, s * rs:(s + 1) * rs, :], ybuf[slot, s * rs:(s + 1) * rs, :],
                               (((0,), (0,)), ((), ())), preferred_element_type=F32)
    racc[...] = strip(0)
    for s in range(1, ybuf.shape[1] // rs):
        @pl.when(s * rs < used)
        def _():
            racc[...] += strip(s)
    ffn = racc[...] + shared
    y = _layer_norm(ALPHA * x_ref[...] + ffn, g_ref[...], b_ref[...])
    y_ref[...] = y
    yb_ref[...] = y.astype(BF16)


def _ffn_out(big, small, cnt, x1, x1b, pw, y_sorted, sg, su, sd, g, b, tile0=0, n_tiles=None, row0=None):
    d = x1.shape[1]
    nt, _, tm = pw.shape
    n_tiles = nt - tile0 if n_tiles is None else n_tiles
    row0 = tile0 if row0 is None else row0
    row_in = pl.BlockSpec((tm, d), lambda i, *_: (i + row0, 0))
    row_out = pl.BlockSpec((tm, d), lambda i, *_: (i, 0))
    consts = [sg, su, sd, g, b]
    grid_spec = pltpu.PrefetchScalarGridSpec(
        num_scalar_prefetch=3,
        grid=(n_tiles,),
        in_specs=[row_in, row_in, pl.BlockSpec((1,) + pw.shape[1:], lambda i, *_: (i + tile0, 0, 0)),
                  pl.BlockSpec(memory_space=pl.ANY)] + [_const_spec(c.shape) for c in consts],
        out_specs=[row_out, row_out],
        scratch_shapes=[pltpu.VMEM((2, SORT_ROWS, d), BF16), pltpu.VMEM((tm, d), F32), pltpu.SemaphoreType.DMA((2,))],
    )
    return pl.pallas_call(
        functools.partial(_ffn_out_kernel, tile0=tile0),
        grid_spec=grid_spec,
        out_shape=[jax.ShapeDtypeStruct((n_tiles * tm, d), F32), jax.ShapeDtypeStruct((n_tiles * tm, d), BF16)],
        compiler_params=_params(("arbitrary",)),
        name="ffn_out",
    )(big, small, cnt, x1, x1b, pw, y_sorted, *consts)


def _dispatch_kernel(big_ref, small_ref, cnt_ref, tdst_ref, tn16_ref, xm_ref, xr_ref, rid_ref, w_ref, segr_ref, c16r_ref,
                     xs_ref, pw_ref, buf, zbuf, sem, *, n_main):
    i = pl.program_id(0)
    nt = pl.num_programs(0)
    slot = i % 2
    tm = xm_ref.shape[0]

    def make(d0, s0, rows, sl):
        return pltpu.make_async_copy(buf.at[sl, pl.ds(s0, rows)], xs_ref.at[pl.ds(d0, rows)], sem.at[sl])

    @pl.when(i >= 2)
    def _():
        _drain(_tile_rows(i - 2, cnt_ref), slot, make)

    x = _pick(xm_ref, xr_ref, n_main)
    rid = rid_ref[0]
    wb = w_ref[0].astype(BF16)
    used = _tile_rows(i, cnt_ref)
    rs = DISPATCH_STRIP
    hs = MXU_DIM
    for s in range(buf.shape[1] // rs):
        @pl.when(s * rs < used)
        def _():
            hits = []
            for h0 in range(s * rs, (s + 1) * rs, hs):
                r = lax.broadcasted_iota(jnp.int32, (hs, N_EXPERTS), 0) + h0
                owner = jnp.where((r >= segr_ref[0]) & (r < segr_ref[0] + c16r_ref[0]), 1.0, 0.0).astype(BF16)
                local = jnp.where((rid > h0) & (rid <= h0 + hs), rid - h0, 0.0).astype(BF16)
                at_row = _dot(owner, jnp.concatenate([local, wb], axis=1))
                hit = at_row[:, :tm] == (lax.broadcasted_iota(jnp.int32, (hs, tm), 0) + 1).astype(F32)
                hits.append(jnp.where(hit, 1.0, 0.0).astype(BF16))
                pw_ref[0, h0:h0 + hs, :] = jnp.where(hit, at_row[:, tm:], 0.0).astype(BF16)
            buf[slot, s * rs:(s + 1) * rs, :] = _dot(jnp.concatenate(hits, axis=0), x).astype(BF16)

        @pl.when(s * rs >= used)
        def _():
            pw_ref[0, s * rs:(s + 1) * rs, :] = jnp.zeros((rs, tm), BF16)
    _start_pieces(i, slot, big_ref, small_ref, cnt_ref, make)

    @pl.when(i == nt - 1)
    def _():
        zbuf[...] = jnp.zeros_like(zbuf)

        def zmake(d0, s0, rows, sl):
            return pltpu.make_async_copy(zbuf.at[pl.ds(0, rows)], xs_ref.at[pl.ds(d0, rows)], sem.at[sl])
        _segment_copies(0, 2, tdst_ref, tn16_ref, tdst_ref, zmake, entries=N_EXPERTS + 1)
        tail = lax.fori_loop(0, N_EXPERTS + 1, lambda e, acc: acc + tn16_ref[e], 0) * SEG_ROWS
        _drain(tail, 2, make)

        @pl.when(i >= 1)
        def _():
            _drain(_tile_rows(i - 1, cnt_ref), 1 - slot, make)
        _drain(used, slot, make)


def _dispatch(big, small, cnt, tail_dst, tail_n16, xb, rid, wdense, segr, c16r, n_rows):
    d = xb[0].shape[1]
    nt, ne, tm = rid.shape
    n_main = xb[0].shape[0] // tm
    per_tile = lambda a: pl.BlockSpec((1,) + a.shape[1:], lambda i, *_: (i, 0, 0))
    grid_spec = pltpu.PrefetchScalarGridSpec(
        num_scalar_prefetch=5,
        grid=(nt,),
        in_specs=_two_part_specs(tm, d, n_main) + [per_tile(rid), per_tile(wdense), per_tile(segr), per_tile(c16r)],
        out_specs=[pl.BlockSpec(memory_space=pl.ANY), pl.BlockSpec((1, SORT_ROWS, tm), lambda i, *_: (i, 0, 0))],
        scratch_shapes=[pltpu.VMEM((2, SORT_ROWS, d), BF16), pltpu.VMEM((SEG_ROWS * BIG_PIECE, d), BF16),
                        pltpu.SemaphoreType.DMA((3,))],
    )
    return pl.pallas_call(
        functools.partial(_dispatch_kernel, n_main=n_main),
        grid_spec=grid_spec,
        out_shape=[jax.ShapeDtypeStruct((n_rows, d), BF16), jax.ShapeDtypeStruct((nt, SORT_ROWS, tm), BF16)],
        compiler_params=_params(("arbitrary",)),
        name="dispatch",
    )(big, small, cnt, tail_dst, tail_n16, *xb, rid, wdense, segr, c16r)


def _top_rows(vals, n, k, ids):
    sel = jnp.zeros(vals.shape, jnp.bool_)
    for _ in range(k):
        m = jnp.max(vals, axis=0, keepdims=True)
        hit = ids == jnp.min(jnp.where(vals == m, ids, n), axis=0, keepdims=True)
        sel = sel | hit
        vals = jnp.where(hit, -jnp.inf, vals)
    return sel


def _router_kernel(lg_ref, b_ref, tri_ref, low_ref, rid_ref, w_ref, tab_ref, carry):
    @pl.when(pl.program_id(0) == 0)
    def _():
        carry[...] = jnp.zeros_like(carry)

    tm = lg_ref.shape[2]
    gs = N_EXPERTS // N_GROUPS
    scores = 1.0 / (1.0 + jnp.exp(-lg_ref[0]))
    biased = scores + b_ref[...]
    member = lax.broadcasted_iota(jnp.int32, (gs, tm), 0)
    group_score = []
    for g in range(N_GROUPS):
        blk = biased[g * gs:(g + 1) * gs]
        m1 = jnp.max(blk, axis=0, keepdims=True)
        first = jnp.min(jnp.where(blk == m1, member, gs), axis=0, keepdims=True)
        m2 = jnp.max(jnp.where(member == first, -jnp.inf, blk), axis=0, keepdims=True)
        group_score.append(m1 + m2)
    group_score = jnp.concatenate(group_score, axis=0)
    gid = lax.broadcasted_iota(jnp.int32, (N_GROUPS, tm), 0)
    group_ok = _top_rows(group_score, N_GROUPS, TOPK_GROUPS, gid)
    masked = jnp.concatenate([jnp.where(group_ok[g:g + 1], biased[g * gs:(g + 1) * gs], -jnp.inf)
                              for g in range(N_GROUPS)], axis=0)
    eid = lax.broadcasted_iota(jnp.int32, (N_EXPERTS, tm), 0)
    sel = _top_rows(masked, N_EXPERTS, TOP_K, eid)
    self = jnp.where(sel, 1.0, 0.0)
    wsel = self * scores
    wts = wsel / jnp.sum(wsel, axis=0, keepdims=True) * ROUTED_SCALE
    selb = self.astype(BF16)
    before = _dot(selb, tri_ref[...])
    pieces = jnp.floor((jnp.sum(self, axis=1, keepdims=True) + (SEG_ROWS - 1.0)) * (1.0 / SEG_ROWS))
    seg = SEG_ROWS * _dot(low_ref[...], jnp.broadcast_to(pieces, (N_EXPERTS, LANES)).astype(BF16))
    rid_ref[0] = jnp.where(sel, seg[:, :1] + before + 1.0, NO_ROW)
    w_ref[0] = wts
    lane = lax.broadcasted_iota(jnp.int32, (N_EXPERTS, LANES), 1)
    tab_ref[0] = jnp.where(lane == 0, seg, jnp.where(lane == 1, SEG_ROWS * pieces, jnp.where(lane == 2, carry[...], 0.0)))
    carry[...] = carry[...] + SEG_ROWS * pieces


def _router(logits, router_b):
    nt, ne, tm = logits.shape
    tri = jnp.asarray(np.triu(np.ones((tm, tm), np.float32), 1), BF16)
    low = jnp.asarray(np.tril(np.ones((ne, ne), np.float32), -1), BF16)
    per_tile = lambda n: pl.BlockSpec((1, ne, n), lambda i: (i, 0, 0))
    return pl.pallas_call(
        _router_kernel,
        grid=(nt,),
        in_specs=[per_tile(tm), _const_spec((ne, 1)), _const_spec(tri.shape), _const_spec(low.shape)],
        out_specs=[per_tile(tm), per_tile(tm), per_tile(LANES)],
        out_shape=[jax.ShapeDtypeStruct((nt, ne, tm), F32), jax.ShapeDtypeStruct((nt, ne, tm), F32),
                   jax.ShapeDtypeStruct((nt, ne, LANES), F32)],
        scratch_shapes=[pltpu.VMEM((ne, 1), F32)],
        compiler_params=_params(("arbitrary",)),
        name="router",
    )(logits, router_b.reshape(ne, 1).astype(F32), tri, low)


def _moe(x1, x1b, logits, router_b, layer, wg, wu, wd, sg, su, sd, g, b):
    two_parts = isinstance(x1, tuple)
    nt, ne, tm = logits.shape
    t = nt * tm
    if tm != ROUTE_TILE:
        split = tm // ROUTE_TILE
        logits = logits.reshape(nt, ne, split, ROUTE_TILE).transpose(0, 2, 1, 3).reshape(nt * split, ne, ROUTE_TILE)
        nt, tm = nt * split, ROUTE_TILE
    assert SORT_ROWS >= TOP_K * tm + ne * (SEG_ROWS - 1) and SORT_ROWS < NO_ROW
    rid, wdense, tab = _router(logits, router_b)
    tab = tab[:, :, :3].astype(jnp.int32)
    seg, rows, base = tab[:, :, 0], tab[:, :, 1], tab[:, :, 2]
    bm = EXPERT_TILE
    region = base[-1] + rows[-1]
    padded = (region + bm - 1) // bm * bm
    pad_end = jnp.cumsum(padded)
    dst = (pad_end - padded)[None, :] + base
    n_blocks = -(-(t * TOP_K + nt * ne * (SEG_ROWS - 1)) // bm) + ne
    blk_start = jnp.arange(n_blocks, dtype=jnp.int32) * bm
    blk_e = jnp.minimum(jnp.sum((pad_end[None, :] <= blk_start[:, None]).astype(jnp.int32), axis=1), ne - 1)
    n_used = (pad_end[-1] // bm).astype(jnp.int32).reshape(1)
    tail_dst = jnp.concatenate([pad_end - padded + region, pad_end[-1:]])
    tail_n16 = jnp.concatenate([padded - region, n_blocks * bm - pad_end[-1:]]) // SEG_ROWS
    big, small, cnt = _piece_tables(dst, rows // SEG_ROWS, seg)
    x_sorted, pw = _dispatch(big, small, cnt, tail_dst, tail_n16, x1b if two_parts else (x1b, x1b), rid, wdense,
                             seg[:, None, :], rows[:, None, :], n_blocks * bm)
    y_sorted = _experts(blk_e, n_used, x_sorted, wg, wu, wd, layer)
    finish = lambda xf, xb, tile0, n_tiles: _ffn_out(big, small, cnt, xf, xb, pw, y_sorted, sg, su, sd, g, b,
                                                     tile0, n_tiles, row0=0)
    if not two_parts:
        return finish(x1, x1b, 0, nt)
    n_main = x1[0].shape[0] // tm
    return finish(x1[0], x1b[0], 0, n_main), finish(x1[1], x1b[1], n_main, nt - n_main)


def _rope_tables(pos):
    half = MLA_ROPE // 2
    inv = ROPE_THETA ** (-jnp.arange(half, dtype=F32) / half)
    ang = pos.astype(F32)[:, None] * inv
    reps = LANES // half
    return jnp.tile(jnp.cos(ang), (1, reps)), jnp.tile(jnp.sin(ang), (1, reps))


def _pack_layer0_weights(w_in0, gla_w_g2, gla_b_g, mla_q_norm_g, mla_kv_norm_g, mla_w_uq, mla_w_uk, mla_w_uv):
    d = w_in0.shape[0]
    hk, hv = GLA_HEADS * GLA_DK, GLA_HEADS * GLA_DV
    o_q, o_k, o_v, o_r = 0, hk, 2 * hk, 2 * hk + hv
    o_a = o_r + hv
    o_cq = o_a + GLA_RANK
    o_ckv = o_cq + MLA_Q_LORA
    o_kr = o_ckv + MLA_KV_LORA
    half = MLA_ROPE // 2
    zeros = lambda n: jnp.zeros((d, n), w_in0.dtype)
    kr1, kr2 = w_in0[:, o_kr:o_kr + half], w_in0[:, o_kr + half:o_kr + MLA_ROPE]
    w_in = jnp.concatenate([
        w_in0[:, o_q:o_a], w_in0[:, o_cq:o_kr],
        kr1, kr2, zeros(LANES - MLA_ROPE),
        kr2, kr1, zeros(LANES - MLA_ROPE),
        w_in0[:, o_a:o_cq], zeros(LANES - GLA_RANK)], axis=1).astype(BF16)
    assert w_in.shape[1] == _C_END
    w_g2 = jnp.concatenate([gla_w_g2, jnp.zeros((LANES - GLA_RANK, hk), gla_w_g2.dtype)], axis=0)
    uq = mla_w_uq.reshape(MLA_Q_LORA, MLA_HEADS, MLA_NOPE + MLA_ROPE)
    w_uq = jnp.concatenate([uq[:, :, :MLA_NOPE].reshape(MLA_Q_LORA, -1),
                            uq[:, :, MLA_NOPE:MLA_NOPE + half].reshape(MLA_Q_LORA, -1),
                            uq[:, :, MLA_NOPE + half:].reshape(MLA_Q_LORA, -1)], axis=1).astype(BF16)
    eye = jnp.eye(MLA_HEADS, dtype=mla_w_uk.dtype)
    w_abs = jnp.einsum('chn,hg->hngc', mla_w_uk, eye).reshape(MLA_HEADS * MLA_NOPE, MLA_HEADS * MLA_KV_LORA).astype(BF16)
    w_uvbd = jnp.einsum('chv,hg->hcgv', mla_w_uv, eye).reshape(MLA_HEADS * MLA_KV_LORA, MLA_HEADS * MLA_V).astype(BF16)
    place = np.zeros((2 * LANES, MLA_HEADS * LANES), np.float32)
    for h in range(MLA_HEADS):
        for j in range(half):
            place[h * half + j, h * LANES + j] = 1.0
            place[LANES + h * half + j, h * LANES + half + j] = 1.0
    return dict(w_in=w_in, w_g2=w_g2, b_g=gla_b_g.reshape(1, hk), q_norm=mla_q_norm_g.reshape(1, -1),
                kv_norm=mla_kv_norm_g.reshape(1, -1), w_uq=w_uq, w_abs=w_abs,
                place=jnp.asarray(place, BF16)), w_uvbd


def _pad_rows(x, n):
    return jnp.pad(x, ((0, n - x.shape[0]),) + ((0, 0),) * (x.ndim - 1))


def _kv_tiles(past, own):
    kvb = MLA_KV_TILE
    pad = lambda a: jnp.pad(a, ((0, 0), (0, -a.shape[1] % kvb), (0, 0)))
    kv = jnp.concatenate([pad(past), pad(own)], axis=1) if past.shape[1] else pad(own)
    b, n, w = kv.shape
    kt = kv.reshape(b, n // kvb, kvb, w).transpose(0, 1, 3, 2)
    return kt, kv


def kernel(x_prompt, x_sample, cache_mla_ckv, cache_mla_krope, state_gla, cache_pool, meta_tokens, w_in0, gla_w_g2, gla_b_g, gla_norm_g, mla_q_norm_g, mla_kv_norm_g, mla_w_uq, mla_w_uk, mla_w_uv, w_out0, pool_w, pool_scale, ln_g, ln_b, moe_router_w, moe_router_b, moe_w_gate, moe_w_up, moe_w_down, moe_sh_gate, moe_sh_up, moe_sh_down):
    bp, sp, d = x_prompt.shape
    bs, ss, _ = x_sample.shape
    n_meta = meta_tokens.shape[0]
    past_len = cache_mla_ckv.shape[1] - n_meta
    tp, tsm = bp * sp, bs * ss
    assert sp % MLA_Q_TILE == 0 and sp % POOL_TILE == 0 and sp % GLA_BLOCK == 0 and sp % CHUNK == 0
    assert ss <= CHUNK and past_len % CHUNK == 0 and n_meta <= CHUNK and ss <= GLA_BLOCK and n_meta <= GLA_BLOCK
    assert n_meta == POOL_MAX and ss >= POOL_MAX and tp % TOK_TILE == 0
    t_all = -(-(tp + tsm + n_meta) // TOK_TILE) * TOK_TILE
    o_s, o_m = tp, tp + tsm

    x_parts = (x_prompt.reshape(tp, d),
               _pad_rows(jnp.concatenate([x_sample.reshape(tsm, d), meta_tokens], axis=0), t_all - tp))
    pos = jnp.concatenate([jnp.tile(n_meta + jnp.arange(sp), bp), jnp.tile(n_meta + past_len + jnp.arange(ss), bs),
                           jnp.arange(n_meta), jnp.zeros((t_all - o_m - n_meta,), jnp.int32)])
    cos_t, sin_t = _rope_tables(pos)
    w0, w_uvbd = _pack_layer0_weights(w_in0, gla_w_g2, gla_b_g, mla_q_norm_g, mla_kv_norm_g, mla_w_uq, mla_w_uk, mla_w_uv)

    q, k, v, sr, lg, q_all, ckv, kr, k_all = _inproj(*x_parts, cos_t, sin_t, w0)

    tables = _gla_tables(GLA_BLOCK)
    g_gla = gla_norm_g.reshape(1, GLA_DV)
    c = GLA_BLOCK
    hk = GLA_HEADS * GLA_DK

    def small(a):
        sm = jnp.pad(a[o_s:o_m].reshape(bs, ss, -1), ((0, 0), (0, c - ss), (0, 0)))
        me = jnp.pad(a[o_m:o_m + n_meta], ((0, c - n_meta), (0, 0)))[None]
        return jnp.concatenate([sm, me], axis=0).reshape((bs + 1) * c, -1)

    s0_small = jnp.concatenate([state_gla.reshape(bs, hk, GLA_DV), jnp.zeros((1, hk, GLA_DV), F32)], axis=0)
    og_small, st_small = _gla(small(q), small(k), small(v), small(lg), small(sr), s0_small, g_gla, tables, bs + 1, 1)
    s0_prompt = jnp.broadcast_to(st_small[bs:], (bp, hk, GLA_DV))
    og_prompt, st_prompt = _gla(q, k, v, lg, sr, s0_prompt, g_gla, tables, bp, sp // c)
    og_small = og_small.reshape(bs + 1, c, -1)
    o_gla = (og_prompt, _pad_rows(jnp.concatenate([og_small[:bs, :ss].reshape(tsm, -1), og_small[bs, :n_meta]], axis=0),
                                  t_all - tp))

    k_meta = k_all[o_m:o_m + n_meta]
    kt_m, kv_m = _kv_tiles(jnp.zeros((1, 0, MLA_QK), BF16), k_meta[None])
    ol_meta = _mla(q_all[o_m:o_m + n_meta], kt_m, kv_m, 1, 1, n_meta, 0, 0, n_meta)
    cache = jnp.concatenate([cache_mla_ckv, cache_mla_krope,
                             jnp.zeros(cache_mla_ckv.shape[:2] + (MLA_QK - MLA_KV_LORA - MLA_ROPE,), F32)], axis=-1).astype(BF16)
    kt_s, kv_s = _kv_tiles(cache, k_all[o_s:o_m].reshape(bs, ss, MLA_QK))
    ol_sample = _mla(q_all[o_s:o_m], kt_s, kv_s, bs, 1, ss, 0, n_meta + past_len, ss)
    kt_p, kv_p = _kv_tiles(jnp.broadcast_to(k_meta[None], (bp, n_meta, MLA_QK)), k_all[:tp].reshape(bp, sp, MLA_QK))
    ol_prompt = _mla(q_all, kt_p, kv_p, bp, sp // MLA_Q_TILE, MLA_Q_TILE, 0, n_meta, sp)
    o_lat = (ol_prompt, _pad_rows(jnp.concatenate([ol_sample, ol_meta], axis=0), t_all - tp))

    ln = lambda l, j: (ln_g[l, j].reshape(1, d), ln_b[l, j].reshape(1, d))
    rw = lambda l: moe_router_w[l].T
    moe_w = lambda l: (l, moe_w_gate, moe_w_up, moe_w_down,
                       moe_sh_gate[l].astype(BF16), moe_sh_up[l].astype(BF16), moe_sh_down[l].astype(BF16))
    x1, x1b, logits = _outproj(o_gla, o_lat, x_parts, w_uvbd, w_out0.astype(BF16), *ln(0, 0), rw(0))
    x2, _ = _moe(x1, x1b, logits, moe_router_b[0], *moe_w(0), *ln(0, 1))

    pm = POOL_MAX
    zrow = jnp.zeros((1, d), F32)
    hist_p = jnp.broadcast_to(jnp.concatenate([zrow, x2[o_m + n_meta - (pm - 1):o_m + n_meta]], axis=0)[None], (bp, pm, d))
    hist_s = jnp.concatenate([jnp.zeros((bs, 1, d), F32), cache_pool], axis=1)
    pool_wb = pool_w.astype(BF16)
    ps = pool_scale.reshape(1, d)
    t1 = -(-(tp + tsm) // TOK_TILE) * TOK_TILE
    y1p = _pool(x2, hist_p, pool_wb, ps, *ln(1, 0), rw(1), bp, sp // POOL_TILE, POOL_TILE, 0)
    y1s = _pool(x2[o_s:o_m], hist_s, pool_wb, ps, *ln(1, 0), rw(1), bs, 1, ss, 0)
    x3, x3b = [(a, _pad_rows(b_, t1 - tp)) for a, b_ in zip(y1p[:2], y1s[:2])]
    assert POOL_TILE == TOK_TILE
    rest_logits = jnp.pad(y1s[2].transpose(1, 0, 2).reshape(N_EXPERTS, tsm), ((0, 0), (0, t1 - tp - tsm)))
    logits1 = jnp.concatenate([y1p[2], rest_logits.reshape(N_EXPERTS, -1, TOK_TILE).transpose(1, 0, 2)], axis=0)
    (x4_prompt, _), (x4_rest, _) = _moe(x3, x3b, logits1, moe_router_b[1], *moe_w(1), *ln(1, 1))

    y_prompt = x4_prompt.reshape(bp, sp, d)
    y_sample = x4_rest[:tsm].reshape(bs, ss, d)
    ckv_meta, kr_meta = ckv[o_m:o_m + n_meta], kr[o_m:o_m + n_meta, :MLA_ROPE]
    p_ckv = jnp.concatenate([jnp.broadcast_to(ckv_meta[None], (bp, n_meta, MLA_KV_LORA)),
                             ckv[:tp].reshape(bp, sp, MLA_KV_LORA)], axis=1)
    p_kr = jnp.concatenate([jnp.broadcast_to(kr_meta[None], (bp, n_meta, MLA_ROPE)),
                            kr[:tp, :MLA_ROPE].reshape(bp, sp, MLA_ROPE)], axis=1)
    p_gla = st_prompt.reshape(bp, GLA_HEADS, GLA_DK, GLA_DV)
    p_pool = jnp.stack([x2[(b_ + 1) * sp - (pm - 1):(b_ + 1) * sp] for b_ in range(bp)])
    s_ckv = ckv[o_s:o_m].reshape(bs, ss, MLA_KV_LORA)
    s_kr = kr[o_s:o_m, :MLA_ROPE].reshape(bs, ss, MLA_ROPE)
    s_gla = st_small[:bs].reshape(bs, GLA_HEADS, GLA_DK, GLA_DV)
    s_pool = x2[o_s:o_m].reshape(bs, ss, d)[:, ss - (pm - 1):]
    return (y_prompt, y_sample, p_ckv, p_kr, p_gla, p_pool, s_ckv, s_kr, s_gla, s_pool)
```

```python
import functools
import math

import jax
import jax.numpy as jnp
import numpy as np
from jax import lax
from jax.experimental import pallas as pl
from jax.experimental.pallas import tpu as pltpu

F32 = jnp.float32
BF16 = jnp.bfloat16

CHUNK = 64
DEPTH = 2
ALPHA = (2 * DEPTH) ** 0.25
LN_EPS = 1e-5
RMS_EPS = 1e-6
NEG_INF = -1e30
GLA_HEADS = 4
GLA_DK = 64
GLA_DV = 128
GLA_RANK = 16
GLA_TAU = 16.0
MLA_HEADS = 8
MLA_Q_LORA = 256
MLA_KV_LORA = 128
MLA_NOPE = 64
MLA_ROPE = 32
MLA_V = 64
MLA_SCALE = (MLA_NOPE + MLA_ROPE) ** -0.5
ROPE_THETA = 10000.0
POOL_WINDOWS = (2, 4, 8, 16)
POOL_MAX = 16
N_EXPERTS = 64
TOP_K = 8
N_GROUPS = 8
TOPK_GROUPS = 4
ROUTED_SCALE = 2.5

LANES = 128
MXU_DIM = 256

TOK_TILE = 512
GLA_BLOCK = 128
MLA_Q_TILE = 256
MLA_KV_TILE = 256
MLA_QK = 256
MLA_ROW_STRIP = 128
LOG2E = math.log2(math.e)
EXPERT_TILE = 1024
EXPERT_SUBTILE = 512
POOL_TILE = 512
SEG_ROWS = 16
BIG_PIECE = 4
DRAIN_UNITS = (64, 16, 4, 1)
ROUTE_TILE = 512
SORT_ROWS = 5120
K_BIG = (TOP_K * ROUTE_TILE + N_EXPERTS * (SEG_ROWS - 1)) // (SEG_ROWS * BIG_PIECE)
K_SMALL = N_EXPERTS * (BIG_PIECE - 1)
PACK_BITS = 9
NO_ROW = 8191.0
DISPATCH_STRIP = 512
COMBINE_STRIP = 2560
VMEM_LIMIT = 56 * 1024 * 1024


def _params(sem, vmem=VMEM_LIMIT):
    return pltpu.CompilerParams(dimension_semantics=sem, vmem_limit_bytes=vmem)


def _const_spec(shape):
    nd = len(shape)
    return pl.BlockSpec(shape, lambda *_: (0,) * nd)


def _split3(x):
    hi = x.astype(BF16)
    r = x - hi.astype(F32)
    mid = r.astype(BF16)
    lo = (r - mid.astype(F32)).astype(BF16)
    return hi, mid, lo


def _dot(a, b):
    return jnp.dot(a, b, preferred_element_type=F32)


def _dot_f32(a, b):
    a_hi = a.astype(BF16)
    a_lo = (a - a_hi.astype(F32)).astype(BF16)
    b_hi = b.astype(BF16)
    b_lo = (b - b_hi.astype(F32)).astype(BF16)
    return _dot(a_hi, b_hi) + (_dot(a_lo, b_hi) + _dot(a_hi, b_lo))


def _dot_f32_nt(a, b):
    nt = lambda u, v: lax.dot_general(u, v, (((1,), (1,)), ((), ())), preferred_element_type=F32)
    a_hi = a.astype(BF16)
    a_lo = (a - a_hi.astype(F32)).astype(BF16)
    b_hi = b.astype(BF16)
    b_lo = (b - b_hi.astype(F32)).astype(BF16)
    return nt(a_hi, b_hi) + (nt(a_lo, b_hi) + nt(a_hi, b_lo))


def _silu(x):
    return x * (1.0 / (1.0 + jnp.exp(-x)))


def _layer_norm(x, g, b):
    mu = jnp.mean(x, axis=-1, keepdims=True)
    xc = x - mu
    var = jnp.mean(xc * xc, axis=-1, keepdims=True)
    return xc * lax.rsqrt(var + LN_EPS) * g + b


def _rms(x, g):
    return x * lax.rsqrt(jnp.mean(x * x, axis=-1, keepdims=True) + RMS_EPS) * g


_C_Q, _C_K, _C_V, _C_R, _C_CQ, _C_CKV, _C_KR, _C_KRS, _C_A, _C_END = (
    0, 256, 512, 1024, 1536, 1792, 1920, 2048, 2176, 2304)


def _two_part_specs(tm, n, n_main):
    return [pl.BlockSpec((tm, n), lambda i, *_: (jnp.minimum(i, n_main - 1), 0)),
            pl.BlockSpec((tm, n), lambda i, *_: (jnp.maximum(i - n_main, 0), 0))]


def _pick(main_ref, rest_ref, n_main):
    return jnp.where(pl.program_id(0) < n_main, main_ref[...], rest_ref[...])


def _inproj_kernel(xm_ref, xr_ref, cos_ref, sin_ref, w_ref, wg2_ref, bg_ref, qn_ref, kvn_ref, wuq_ref, wabs_ref,
                   place_ref, q_ref, k_ref, v_ref, sr_ref, lg_ref, qall_ref, ckv_ref, kr_ref, kall_ref, *, n_main):
    h = _dot(_pick(xm_ref, xr_ref, n_main).astype(BF16), w_ref[...])
    q_ref[...] = (h[:, _C_Q:_C_K] * GLA_DK ** -0.5).astype(BF16)
    k_ref[...] = h[:, _C_K:_C_V].astype(BF16)
    v_ref[...] = h[:, _C_V:_C_R].astype(BF16)
    sr_ref[...] = _silu(h[:, _C_R:_C_CQ]).astype(BF16)
    z = _dot_f32(h[:, _C_A:_C_END], wg2_ref[...]) + bg_ref[...]
    lg_ref[...] = (jnp.minimum(z, 0.0) - jnp.log(1.0 + jnp.exp(-jnp.abs(z)))) * (1.0 / GLA_TAU)
    cqn = _rms(h[:, _C_CQ:_C_CKV], qn_ref[...]).astype(BF16)
    qh = _dot(cqn, wuq_ref[...])
    n_nope = MLA_HEADS * MLA_NOPE
    cos, sin = cos_ref[...], sin_ref[...]
    x1, x2 = qh[:, n_nope:n_nope + LANES], qh[:, n_nope + LANES:]
    qscale = MLA_SCALE * LOG2E
    rot = (jnp.concatenate([x1 * cos - x2 * sin, x1 * sin + x2 * cos], axis=1) * qscale).astype(BF16)
    qlat = (_dot(qh[:, :n_nope].astype(BF16), wabs_ref[...]) * qscale).astype(BF16)
    qrope = _dot(rot, place_ref[...]).astype(BF16)
    for hd in range(MLA_HEADS):
        qall_ref[:, MLA_QK * hd:MLA_QK * hd + LANES] = qlat[:, LANES * hd:LANES * (hd + 1)]
        qall_ref[:, MLA_QK * hd + LANES:MLA_QK * (hd + 1)] = qrope[:, LANES * hd:LANES * (hd + 1)]
    ckv = _rms(h[:, _C_CKV:_C_KR], kvn_ref[...])
    lane = lax.broadcasted_iota(jnp.int32, (1, LANES), 1)
    sgn = jnp.where(lane < MLA_ROPE // 2, -1.0, 1.0)
    kr = h[:, _C_KR:_C_KRS] * cos + h[:, _C_KRS:_C_A] * (sin * sgn)
    ckv_ref[...] = ckv
    kr_ref[...] = kr
    kall_ref[:, :LANES] = ckv.astype(BF16)
    kall_ref[:, LANES:] = kr.astype(BF16)


def _inproj(x_main, x_rest, cos_t, sin_t, w):
    t = x_main.shape[0] + x_rest.shape[0]
    d = x_main.shape[1]
    tm = TOK_TILE
    n_main = x_main.shape[0] // tm
    row = lambda n: pl.BlockSpec((tm, n), lambda i: (i, 0))
    consts = [w['w_in'], w['w_g2'], w['b_g'], w['q_norm'], w['kv_norm'], w['w_uq'], w['w_abs'], w['place']]
    outs = [(GLA_HEADS * GLA_DK, BF16), (GLA_HEADS * GLA_DK, BF16), (GLA_HEADS * GLA_DV, BF16),
            (GLA_HEADS * GLA_DV, BF16), (GLA_HEADS * GLA_DK, F32), (MLA_HEADS * MLA_QK, BF16),
            (LANES, F32), (LANES, F32), (MLA_QK, BF16)]
    return pl.pallas_call(
        functools.partial(_inproj_kernel, n_main=n_main),
        grid=(t // tm,),
        in_specs=_two_part_specs(tm, d, n_main) + [row(LANES), row(LANES)] + [_const_spec(c.shape) for c in consts],
        out_specs=[row(n) for n, _ in outs],
        out_shape=[jax.ShapeDtypeStruct((t, n), dt) for n, dt in outs],
        compiler_params=_params(("parallel",)),
        name="inproj",
    )(x_main, x_rest, cos_t, sin_t, *consts)


def _gla_tables(c):
    levels = int(math.log2(c))
    assert 1 << levels == c
    t = np.arange(c)[:, None]
    u = np.arange(c)[None, :]
    mats, masks = [], []
    for l in range(levels):
        half = 1 << l
        base = (t // half) * half
        upper = ((t >> l) & 1) == 1
        a_q = (u >= base) & (u <= t)
        a_k = (u > t) & (u <= base + half - 1)
        mats.append(np.where(upper, a_q, a_k))
        same = (t >> (l + 1)) == (u >> (l + 1))
        masks.append(same & upper & (((u >> l) & 1) == 0))
    mats.append(u <= t)
    mats.append(u > t)
    masks.append(t == u)
    amat = np.concatenate(mats, axis=0).astype(np.float32)
    mask = np.stack([np.tile(m, (GLA_HEADS, 1)) for m in masks]).astype(np.float32)
    return jnp.asarray(amat, BF16), jnp.asarray(mask, F32), levels


def _gla_kernel(q_ref, k_ref, v_ref, lg_ref, sr_ref, s0_ref, amat_ref, mask_ref, g_ref,
                o_ref, sout_ref, state, *, levels):
    c = q_ref.shape[0]
    hk = GLA_HEADS * GLA_DK
    step = pl.program_id(1)

    @pl.when(step == 0)
    def _():
        state[...] = s0_ref[0]

    lg = lg_ref[...]
    parts = _split3(lg)
    e3 = _dot(amat_ref[...], jnp.concatenate(parts, axis=1))
    ex = jnp.exp(e3[:, :hk] + e3[:, hk:2 * hk] + e3[:, 2 * hk:])
    q = q_ref[...].astype(F32)
    k = k_ref[...].astype(F32)
    v = v_ref[...]
    head = lax.broadcasted_iota(jnp.int32, (1, hk), 1) // GLA_DK

    def per_head(xf):
        return jnp.concatenate([jnp.where(head == h, xf, 0.0) for h in range(GLA_HEADS)], axis=0).astype(BF16)

    att = jnp.zeros((GLA_HEADS * c, c), F32)
    for l in range(levels + 1):
        if l < levels:
            el = ex[l * c:(l + 1) * c]
            ql, kl = q * el, k * el
        else:
            ql, kl = q, k
        a = lax.dot_general(per_head(ql), kl.astype(BF16), (((1,), (1,)), ((), ())), preferred_element_type=F32)
        att = att + a * mask_ref[l]
    s_prev = state[...]
    o_inter = _dot(per_head(q * ex[levels * c:(levels + 1) * c]), s_prev.astype(BF16))
    att = att.astype(BF16)
    g = g_ref[...]
    for h in range(GLA_HEADS):
        o = o_inter[h * c:(h + 1) * c] + _dot(att[h * c:(h + 1) * c], v[:, GLA_DV * h:GLA_DV * (h + 1)])
        o = _rms(o, g)
        o_ref[:, GLA_DV * h:GLA_DV * (h + 1)] = (o * sr_ref[:, GLA_DV * h:GLA_DV * (h + 1)].astype(F32)).astype(BF16)
    kr = (k * ex[(levels + 1) * c:]).astype(BF16)
    upd = lax.dot_general(kr, v, (((0,), (0,)), ((), ())), preferred_element_type=F32)
    ones = jnp.ones((c, GLA_DV), BF16)
    b_last = sum(lax.dot_general(p, ones, (((0,), (0,)), ((), ())), preferred_element_type=F32) for p in parts)
    new = jnp.exp(b_last) * s_prev + jnp.concatenate(
        [upd[GLA_DK * h:GLA_DK * (h + 1), GLA_DV * h:GLA_DV * (h + 1)] for h in range(GLA_HEADS)], axis=0)
    state[...] = new

    @pl.when(step == pl.num_programs(1) - 1)
    def _():
        sout_ref[0] = new


def _gla(q, k, v, lg, sr, s0, g, tables, batch, blocks_per_seq):
    amat, mask, levels = tables
    c = GLA_BLOCK
    hk, hv = GLA_HEADS * GLA_DK, GLA_HEADS * GLA_DV
    row = lambda n: pl.BlockSpec((c, n), lambda b, s: (b * blocks_per_seq + s, 0))
    st = pl.BlockSpec((1, hk, GLA_DV), lambda b, s: (b, 0, 0))
    return pl.pallas_call(
        functools.partial(_gla_kernel, levels=levels),
        grid=(batch, blocks_per_seq),
        in_specs=[row(hk), row(hk), row(hv), row(hk), row(hv), st,
                  _const_spec(amat.shape), _const_spec(mask.shape), _const_spec(g.shape)],
        out_specs=[row(hv), st],
        out_shape=[jax.ShapeDtypeStruct((batch * blocks_per_seq * c, hv), BF16),
                   jax.ShapeDtypeStruct((batch, hk, GLA_DV), F32)],
        scratch_shapes=[pltpu.VMEM((hk, GLA_DV), F32)],
        compiler_params=_params(("parallel", "arbitrary")),
        name="gla",
    )(q, k, v, lg, sr, s0, amat, mask, g)


def _mla_kernel(q_ref, kt_ref, kv_ref, o_ref, s_sc, p_sc, linv_sc, acc_sc, *, tq, n_past, past_valid, own_valid,
                diag_chunks):
    kvb = MLA_KV_TILE
    qi = pl.program_id(1)
    qs = jnp.concatenate([q_ref[:, MLA_QK * h:MLA_QK * (h + 1)] for h in range(MLA_HEADS)], axis=0)
    m_rows = MLA_HEADS * tq
    n_vis = n_past + qi + 1

    def score(j, mask):
        s = _dot(qs, kt_ref[0, j])
        s_sc[j] = s if mask is None else jnp.where(mask, s, NEG_INF)

    def scores(lo, hi):
        def body(j, carry):
            score(j, None)
            return carry
        lax.fori_loop(lo, hi, body, 0)

    col = lax.broadcasted_iota(jnp.int32, (m_rows, kvb), 1)
    n_past_full = past_valid // kvb
    if n_past_full:
        scores(0, n_past_full)
    if n_past_full < n_past:
        score(n_past_full, col < past_valid - n_past_full * kvb)
    scores(n_past, n_past + qi)
    mask = col < own_valid - qi * kvb
    if diag_chunks:
        qrow = lax.broadcasted_iota(jnp.int32, (m_rows, kvb), 0) % tq
        mask = mask & (col // CHUNK <= qrow // CHUNK)
    score(n_past + qi, mask)

    rc = MLA_ROW_STRIP

    def strip(r, carry):
        rows = pl.ds(pl.multiple_of(r * rc, rc), rc)
        mx = lax.fori_loop(0, n_vis, lambda j, m: jnp.maximum(m, s_sc[j, rows, :]),
                           jnp.full((rc, kvb), -jnp.inf, F32))
        m = jnp.max(mx, axis=1, keepdims=True)

        def body(j, l):
            p = jnp.exp2(s_sc[j, rows, :] - m)
            p_sc[j, rows, :] = p.astype(BF16)
            return l + p
        l = lax.fori_loop(0, n_vis, body, jnp.zeros((rc, kvb), F32))
        linv_sc[rows, :] = 1.0 / jnp.sum(l, axis=1, keepdims=True)
        return carry
    lax.fori_loop(0, m_rows // rc, strip, 0)

    acc_sc[...] = jnp.zeros(acc_sc.shape, F32)

    def pv(j, carry):
        vblk = kv_ref[0, pl.ds(pl.multiple_of(j * kvb, kvb), kvb), :MLA_KV_LORA]
        acc_sc[...] += _dot(p_sc[j], vblk)
        return carry
    lax.fori_loop(0, n_vis, pv, 0)
    o = acc_sc[...] * linv_sc[...]
    for h in range(MLA_HEADS):
        o_ref[:, MLA_KV_LORA * h:MLA_KV_LORA * (h + 1)] = o[h * tq:(h + 1) * tq].astype(BF16)


def _mla(q_all, kt, kv, batch, nq, tq, q_block0, past_valid, own_valid):
    n_kv = kt.shape[1]
    n_past = -(-past_valid // MLA_KV_TILE)
    assert nq == 1 or tq == MLA_KV_TILE
    assert n_kv == n_past + nq
    return pl.pallas_call(
        functools.partial(_mla_kernel, tq=tq, n_past=n_past, past_valid=past_valid, own_valid=own_valid,
                          diag_chunks=nq > 1),
        grid=(batch, nq),
        in_specs=[pl.BlockSpec((tq, MLA_HEADS * MLA_QK), lambda b, i: (q_block0 + b * nq + i, 0)),
                  pl.BlockSpec((1,) + kt.shape[1:], lambda b, i: (b, 0, 0, 0)),
                  pl.BlockSpec((1,) + kv.shape[1:], lambda b, i: (b, 0, 0))],
        out_specs=pl.BlockSpec((tq, MLA_HEADS * MLA_KV_LORA), lambda b, i: (b * nq + i, 0)),
        out_shape=jax.ShapeDtypeStruct((batch * nq * tq, MLA_HEADS * MLA_KV_LORA), BF16),
        scratch_shapes=[pltpu.VMEM((n_kv, MLA_HEADS * tq, MLA_KV_TILE), F32),
                        pltpu.VMEM((n_kv, MLA_HEADS * tq, MLA_KV_TILE), BF16),
                        pltpu.VMEM((MLA_HEADS * tq, 1), F32),
                        pltpu.VMEM((MLA_HEADS * tq, MLA_KV_LORA), F32)],
        compiler_params=_params(("parallel", "arbitrary")),
        name="mla",
    )(q_all, kt, kv)


def _post_mixer(x, out, g, b, rw_ref, x1_ref, x1b_ref, lgt_ref):
    y = _layer_norm(ALPHA * x + out, g, b)
    x1_ref[...] = y
    x1b_ref[...] = y.astype(BF16)
    lgt_ref[0] = _dot_f32_nt(rw_ref[...], y)


def _outproj_kernel(ogm_ref, ogr_ref, olm_ref, olr_ref, xm_ref, xr_ref, wuv_ref, wo_ref, g_ref, b_ref, rw_ref,
                    x1_ref, x1b_ref, lgt_ref, *, n_main):
    og = _pick(ogm_ref, ogr_ref, n_main)
    o_mla = _dot(_pick(olm_ref, olr_ref, n_main), wuv_ref[...]).astype(BF16)
    n_gla = og.shape[1]
    out = _dot(og, wo_ref[:n_gla, :]) + _dot(o_mla, wo_ref[n_gla:, :])
    _post_mixer(_pick(xm_ref, xr_ref, n_main), out, g_ref[...], b_ref[...], rw_ref, x1_ref, x1b_ref, lgt_ref)


def _outproj(o_gla, o_lat, x, w_uvbd, w_out, g, b, rw):
    t, d = x[0].shape[0] + x[1].shape[0], x[0].shape[1]
    tm = TOK_TILE
    n_main = x[0].shape[0] // tm
    row = lambda n: pl.BlockSpec((tm, n), lambda i: (i, 0))
    consts = [w_uvbd, w_out, g, b, rw]
    parts = [*o_gla, *o_lat, *x]
    return pl.pallas_call(
        functools.partial(_outproj_kernel, n_main=n_main),
        grid=(t // tm,),
        in_specs=[s for a in (o_gla, o_lat, x) for s in _two_part_specs(tm, a[0].shape[1], n_main)]
        + [_const_spec(c.shape) for c in consts],
        out_specs=[row(d), row(d), pl.BlockSpec((1, N_EXPERTS, tm), lambda i: (i, 0, 0))],
        out_shape=[jax.ShapeDtypeStruct((t, d), F32), jax.ShapeDtypeStruct((t, d), BF16),
                   jax.ShapeDtypeStruct((t // tm, N_EXPERTS, tm), F32)],
        compiler_params=_params(("parallel",)),
        name="outproj",
    )(*parts, *consts)


def _pool_kernel(x_ref, prev_ref, hist_ref, pw_ref, ps_ref, g_ref, b_ref, rw_ref, x1_ref, x1b_ref, lgt_ref):
    ts = x_ref.shape[0]
    pm = POOL_MAX
    x = x_ref[...]
    prev = jnp.where(pl.program_id(1) == 0, hist_ref[0], prev_ref[...])
    xc = jnp.concatenate([prev, x], axis=0)
    grp = x.shape[1] // len(POOL_WINDOWS)
    outs = []
    for gi, w in enumerate(POOL_WINDOWS):
        s = xc[:, gi * grp:(gi + 1) * grp]
        span = 1
        while span < w:
            s = s + pltpu.roll(s, span, 0)
            span *= 2
        win = s[pm:]
        mix = (win * (1.0 / w) - x[:, gi * grp:(gi + 1) * grp]).astype(BF16)
        outs.append(_dot(mix, pw_ref[gi]))
    out = jnp.concatenate(outs, axis=1) * ps_ref[...]
    _post_mixer(x, out, g_ref[...], b_ref[...], rw_ref, x1_ref, x1b_ref, lgt_ref)


def _pool(x_all, hist, pool_w, pool_scale, g, b, rw, batch, tiles_per_seq, ts, tile0):
    d = x_all.shape[1]
    pm = POOL_MAX
    per = ts // pm
    consts = [pool_w, pool_scale, g, b, rw]
    row = lambda n: pl.BlockSpec((ts, n), lambda bb, i: (bb * tiles_per_seq + i, 0))
    return pl.pallas_call(
        _pool_kernel,
        grid=(batch, tiles_per_seq),
        in_specs=[pl.BlockSpec((ts, d), lambda bb, i: (tile0 + bb * tiles_per_seq + i, 0)),
                  pl.BlockSpec((pm, d), lambda bb, i: (jnp.maximum((tile0 + bb * tiles_per_seq + i) * per - 1, 0), 0)),
                  pl.BlockSpec((1, pm, d), lambda bb, i: (bb, 0, 0))] + [_const_spec(c.shape) for c in consts],
        out_specs=[row(d), row(d), pl.BlockSpec((1, N_EXPERTS, ts), lambda bb, i: (bb * tiles_per_seq + i, 0, 0))],
        out_shape=[jax.ShapeDtypeStruct((batch * tiles_per_seq * ts, d), F32),
                   jax.ShapeDtypeStruct((batch * tiles_per_seq * ts, d), BF16),
                   jax.ShapeDtypeStruct((batch * tiles_per_seq, N_EXPERTS, ts), F32)],
        compiler_params=_params(("parallel", "arbitrary")),
        name="pool",
    )(x_all, x_all, hist, *consts)


def _experts_kernel(be_ref, nb_ref, x_ref, wg_ref, wu_ref, wd_ref, y_ref, wg_sc, wu_sc, wd_sc):
    i = pl.program_id(0)

    @pl.when((i == 0) | (be_ref[i] != be_ref[jnp.maximum(i - 1, 0)]))
    def _():
        wg_sc[...] = wg_ref[0, 0].astype(BF16)
        wu_sc[...] = wu_ref[0, 0].astype(BF16)
        wd_sc[...] = wd_ref[0, 0].astype(BF16)

    @pl.when(i < nb_ref[0])
    def _():
        sub = EXPERT_SUBTILE
        for r in range(x_ref.shape[0] // sub):
            x = x_ref[r * sub:(r + 1) * sub, :]
            hmid = (_silu(_dot(x, wg_sc[...])) * _dot(x, wu_sc[...])).astype(BF16)
            y_ref[r * sub:(r + 1) * sub, :] = _dot(hmid, wd_sc[...]).astype(BF16)

    @pl.when(pl.program_id(0) >= nb_ref[0])
    def _():
        y_ref[...] = jnp.zeros_like(y_ref)


def _experts(blk_e, n_used, x_sorted, wg, wu, wd, layer):
    p, d = x_sorted.shape
    f = wg.shape[3]
    bm = EXPERT_TILE
    blk = lambda i, be, nb: (i, 0)
    grid_spec = pltpu.PrefetchScalarGridSpec(
        num_scalar_prefetch=2,
        grid=(p // bm,),
        in_specs=[pl.BlockSpec((bm, d), lambda i, be, nb: (jnp.minimum(i, nb[0] - 1), 0)),
                  pl.BlockSpec((1, 1, d, f), lambda i, be, nb: (layer, be[i], 0, 0)),
                  pl.BlockSpec((1, 1, d, f), lambda i, be, nb: (layer, be[i], 0, 0)),
                  pl.BlockSpec((1, 1, f, d), lambda i, be, nb: (layer, be[i], 0, 0))],
        out_specs=pl.BlockSpec((bm, d), blk),
        scratch_shapes=[pltpu.VMEM((d, f), BF16), pltpu.VMEM((d, f), BF16), pltpu.VMEM((f, d), BF16)],
    )
    return pl.pallas_call(
        _experts_kernel,
        grid_spec=grid_spec,
        out_shape=jax.ShapeDtypeStruct((p, d), BF16),
        compiler_params=_params(("arbitrary",)),
        name="experts",
    )(blk_e, n_used, x_sorted, wg, wu, wd)


def _segment_copies(tile, slot, dst_ref, n16_ref, seg_ref, make, entries=N_EXPERTS):
    big = SEG_ROWS * BIG_PIECE

    def per_expert(e, carry):
        j = tile * N_EXPERTS + e
        d0, s0, n = dst_ref[j], seg_ref[j], n16_ref[j]
        n_big = n // BIG_PIECE

        def big_piece(c, carry2):
            make(pl.multiple_of(d0 + c * big, SEG_ROWS), pl.multiple_of(s0 + c * big, SEG_ROWS), big, slot).start()
            return carry2
        lax.fori_loop(0, n_big, big_piece, 0)
        d1, s1 = d0 + n_big * big, s0 + n_big * big

        def small_piece(c, carry2):
            make(pl.multiple_of(d1 + c * SEG_ROWS, SEG_ROWS), pl.multiple_of(s1 + c * SEG_ROWS, SEG_ROWS),
                 SEG_ROWS, slot).start()
            return carry2
        lax.fori_loop(0, n - n_big * BIG_PIECE, small_piece, 0)
        return carry
    lax.fori_loop(0, entries, per_expert, 0)


def _start_pieces(tile, slot, big_ref, small_ref, cnt_ref, make):
    def run(tab_ref, kmax, n, rows):
        def body(k, carry):
            v = tab_ref[tile * kmax + k]
            make(pl.multiple_of((v >> PACK_BITS) * SEG_ROWS, SEG_ROWS),
                 pl.multiple_of((v & ((1 << PACK_BITS) - 1)) * SEG_ROWS, SEG_ROWS), rows, slot).start()
            return carry
        lax.fori_loop(0, n, body, 0)
    run(big_ref, K_BIG, cnt_ref[2 * tile], SEG_ROWS * BIG_PIECE)
    run(small_ref, K_SMALL, cnt_ref[2 * tile + 1], SEG_ROWS)


def _tile_rows(tile, cnt_ref):
    return cnt_ref[2 * tile] * (SEG_ROWS * BIG_PIECE) + cnt_ref[2 * tile + 1] * SEG_ROWS


def _piece_tables(dst, n16, seg):
    nt, ne = n16.shape
    big_rows = SEG_ROWS * BIG_PIECE
    n_big, n_small = n16 // BIG_PIECE, n16 % BIG_PIECE

    def flat(counts, first_dst, first_src, step, kmax):
        cum = jnp.cumsum(counts, axis=1)[:, None, :]
        before = cum - counts[:, None, :]
        k = jnp.arange(kmax, dtype=jnp.int32)[None, :, None]
        mine = (before <= k) & (k < cum)
        off = step * (k - before)
        d = jnp.sum(jnp.where(mine, first_dst[:, None, :] + off, 0), axis=2)
        s = jnp.sum(jnp.where(mine, first_src[:, None, :] + off, 0), axis=2)
        return (((d // SEG_ROWS) << PACK_BITS) | (s // SEG_ROWS)).reshape(-1).astype(jnp.int32), cum[:, 0, -1]
    big, cnt_big = flat(n_big, dst, seg, big_rows, K_BIG)
    small, cnt_small = flat(n_small, dst + n_big * big_rows, seg + n_big * big_rows, SEG_ROWS, K_SMALL)
    return big, small, jnp.stack([cnt_big, cnt_small], axis=1).reshape(-1).astype(jnp.int32)


def _drain(rows, slot, make):
    left = rows // SEG_ROWS
    for unit in DRAIN_UNITS:
        n = left // unit

        def body(c, carry, unit=unit):
            make(0, 0, unit * SEG_ROWS, slot).wait()
            return carry
        lax.fori_loop(0, n, body, 0)
        left = left - n * unit


def _ffn_out_kernel(big_ref, small_ref, cnt_ref, x_ref, xb_ref, pw_ref, ys_ref,
                    sg_ref, su_ref, sd_ref, g_ref, b_ref, y_ref, ybuf, racc, sem, *, tile0):
    i = pl.program_id(0)
    nt = pl.num_programs(0)
    slot = i % 2
    tile = i + tile0

    def make(d0, s0, rows, sl):
        return pltpu.make_async_copy(ys_ref.at[pl.ds(d0, rows)], ybuf.at[sl, pl.ds(s0, rows)], sem.at[sl])

    @pl.when(i == 0)
    def _():
        ybuf[...] = jnp.zeros_like(ybuf)
        _start_pieces(tile, 0, big_ref, small_ref, cnt_ref, make)

    @pl.when(i + 1 < nt)
    def _():
        _start_pieces(tile + 1, 1 - slot, big_ref, small_ref, cnt_ref, make)

    xb = xb_ref[...]
    hmid = (_silu(_dot(xb, sg_ref[...])) * _dot(xb, su_ref[...])).astype(BF16)
    shared = _dot(hmid, sd_ref[...])
    used = _tile_rows(tile, cnt_ref)
    _drain(used, slot, make)

    rs = COMBINE_STRIP

    def strip(s):
        return lax.dot_general(pw_ref[0, s * rs:(s + 1) * rs, :], ybuf[slot, s * rs:(s + 1) * rs, :],
                               (((0,), (0,)), ((), ())), preferred_element_type=F32)
    racc[...] = strip(0)
    for s in range(1, ybuf.shape[1] // rs):
        @pl.when(s * rs < used)
        def _():
            racc[...] += strip(s)
    ffn = racc[...] + shared
    y_ref[...] = _layer_norm(ALPHA * x_ref[...] + ffn, g_ref[...], b_ref[...])


def _ffn_out(big, small, cnt, x1, x1b, pw, y_sorted, sg, su, sd, g, b, tile0=0, n_tiles=None, row0=None):
    d = x1.shape[1]
    nt, _, tm = pw.shape
    n_tiles = nt - tile0 if n_tiles is None else n_tiles
    row0 = tile0 if row0 is None else row0
    row_in = pl.BlockSpec((tm, d), lambda i, *_: (i + row0, 0))
    row_out = pl.BlockSpec((tm, d), lambda i, *_: (i, 0))
    consts = [sg, su, sd, g, b]
    grid_spec = pltpu.PrefetchScalarGridSpec(
        num_scalar_prefetch=3,
        grid=(n_tiles,),
        in_specs=[row_in, row_in, pl.BlockSpec((1,) + pw.shape[1:], lambda i, *_: (i + tile0, 0, 0)),
                  pl.BlockSpec(memory_space=pl.ANY)] + [_const_spec(c.shape) for c in consts],
        out_specs=row_out,
        scratch_shapes=[pltpu.VMEM((2, SORT_ROWS, d), BF16), pltpu.VMEM((tm, d), F32), pltpu.SemaphoreType.DMA((2,))],
    )
    return pl.pallas_call(
        functools.partial(_ffn_out_kernel, tile0=tile0),
        grid_spec=grid_spec,
        out_shape=jax.ShapeDtypeStruct((n_tiles * tm, d), F32),
        compiler_params=_params(("arbitrary",)),
        name="ffn_out",
    )(big, small, cnt, x1, x1b, pw, y_sorted, *consts)


def _dispatch_kernel(big_ref, small_ref, cnt_ref, tdst_ref, tn16_ref, xm_ref, xr_ref, rid_ref, w_ref, segr_ref, c16r_ref,
                     xs_ref, pw_ref, buf, zbuf, sem, *, n_main):
    i = pl.program_id(0)
    nt = pl.num_programs(0)
    slot = i % 2
    tm = xm_ref.shape[0]

    def make(d0, s0, rows, sl):
        return pltpu.make_async_copy(buf.at[sl, pl.ds(s0, rows)], xs_ref.at[pl.ds(d0, rows)], sem.at[sl])

    @pl.when(i >= 2)
    def _():
        _drain(_tile_rows(i - 2, cnt_ref), slot, make)

    x = _pick(xm_ref, xr_ref, n_main)
    rid = rid_ref[0]
    wb = w_ref[0].astype(BF16)
    used = _tile_rows(i, cnt_ref)
    rs = DISPATCH_STRIP
    hs = MXU_DIM
    for s in range(buf.shape[1] // rs):
        @pl.when(s * rs < used)
        def _():
            hits = []
            for h0 in range(s * rs, (s + 1) * rs, hs):
                r = lax.broadcasted_iota(jnp.int32, (hs, N_EXPERTS), 0) + h0
                owner = jnp.where((r >= segr_ref[0]) & (r < segr_ref[0] + c16r_ref[0]), 1.0, 0.0).astype(BF16)
                local = jnp.where((rid > h0) & (rid <= h0 + hs), rid - h0, 0.0).astype(BF16)
                at_row = _dot(owner, jnp.concatenate([local, wb], axis=1))
                hit = at_row[:, :tm] == (lax.broadcasted_iota(jnp.int32, (hs, tm), 0) + 1).astype(F32)
                hits.append(jnp.where(hit, 1.0, 0.0).astype(BF16))
                pw_ref[0, h0:h0 + hs, :] = jnp.where(hit, at_row[:, tm:], 0.0).astype(BF16)
            buf[slot, s * rs:(s + 1) * rs, :] = _dot(jnp.concatenate(hits, axis=0), x).astype(BF16)

        @pl.when(s * rs >= used)
        def _():
            pw_ref[0, s * rs:(s + 1) * rs, :] = jnp.zeros((rs, tm), BF16)
    _start_pieces(i, slot, big_ref, small_ref, cnt_ref, make)

    @pl.when(i == nt - 1)
    def _():
        zbuf[...] = jnp.zeros_like(zbuf)

        def zmake(d0, s0, rows, sl):
            return pltpu.make_async_copy(zbuf.at[pl.ds(0, rows)], xs_ref.at[pl.ds(d0, rows)], sem.at[sl])
        _segment_copies(0, 2, tdst_ref, tn16_ref, tdst_ref, zmake, entries=N_EXPERTS + 1)
        tail = lax.fori_loop(0, N_EXPERTS + 1, lambda e, acc: acc + tn16_ref[e], 0) * SEG_ROWS
        _drain(tail, 2, make)

        @pl.when(i >= 1)
        def _():
            _drain(_tile_rows(i - 1, cnt_ref), 1 - slot, make)
        _drain(used, slot, make)


def _dispatch(big, small, cnt, tail_dst, tail_n16, xb, rid, wdense, segr, c16r, n_rows):
    d = xb[0].shape[1]
    nt, ne, tm = rid.shape
    n_main = xb[0].shape[0] // tm
    per_tile = lambda a: pl.BlockSpec((1,) + a.shape[1:], lambda i, *_: (i, 0, 0))
    grid_spec = pltpu.PrefetchScalarGridSpec(
        num_scalar_prefetch=5,
        grid=(nt,),
        in_specs=_two_part_specs(tm, d, n_main) + [per_tile(rid), per_tile(wdense), per_tile(segr), per_tile(c16r)],
        out_specs=[pl.BlockSpec(memory_space=pl.ANY), pl.BlockSpec((1, SORT_ROWS, tm), lambda i, *_: (i, 0, 0))],
        scratch_shapes=[pltpu.VMEM((2, SORT_ROWS, d), BF16), pltpu.VMEM((SEG_ROWS * BIG_PIECE, d), BF16),
                        pltpu.SemaphoreType.DMA((3,))],
    )
    return pl.pallas_call(
        functools.partial(_dispatch_kernel, n_main=n_main),
        grid_spec=grid_spec,
        out_shape=[jax.ShapeDtypeStruct((n_rows, d), BF16), jax.ShapeDtypeStruct((nt, SORT_ROWS, tm), BF16)],
        compiler_params=_params(("arbitrary",)),
        name="dispatch",
    )(big, small, cnt, tail_dst, tail_n16, *xb, rid, wdense, segr, c16r)


def _top_rows(vals, n, k, ids):
    sel = jnp.zeros(vals.shape, jnp.bool_)
    for _ in range(k):
        m = jnp.max(vals, axis=0, keepdims=True)
        hit = ids == jnp.min(jnp.where(vals == m, ids, n), axis=0, keepdims=True)
        sel = sel | hit
        vals = jnp.where(hit, -jnp.inf, vals)
    return sel


def _router_kernel(lg_ref, b_ref, tri_ref, low_ref, rid_ref, w_ref, tab_ref, carry):
    @pl.when(pl.program_id(0) == 0)
    def _():
        carry[...] = jnp.zeros_like(carry)

    tm = lg_ref.shape[2]
    gs = N_EXPERTS // N_GROUPS
    scores = 1.0 / (1.0 + jnp.exp(-lg_ref[0]))
    biased = scores + b_ref[...]
    member = lax.broadcasted_iota(jnp.int32, (gs, tm), 0)
    group_score = []
    for g in range(N_GROUPS):
        blk = biased[g * gs:(g + 1) * gs]
        m1 = jnp.max(blk, axis=0, keepdims=True)
        first = jnp.min(jnp.where(blk == m1, member, gs), axis=0, keepdims=True)
        m2 = jnp.max(jnp.where(member == first, -jnp.inf, blk), axis=0, keepdims=True)
        group_score.append(m1 + m2)
    group_score = jnp.concatenate(group_score, axis=0)
    gid = lax.broadcasted_iota(jnp.int32, (N_GROUPS, tm), 0)
    group_ok = _top_rows(group_score, N_GROUPS, TOPK_GROUPS, gid)
    masked = jnp.concatenate([jnp.where(group_ok[g:g + 1], biased[g * gs:(g + 1) * gs], -jnp.inf)
                              for g in range(N_GROUPS)], axis=0)
    eid = lax.broadcasted_iota(jnp.int32, (N_EXPERTS, tm), 0)
    sel = _top_rows(masked, N_EXPERTS, TOP_K, eid)
    self = jnp.where(sel, 1.0, 0.0)
    wsel = self * scores
    wts = wsel / jnp.sum(wsel, axis=0, keepdims=True) * ROUTED_SCALE
    selb = self.astype(BF16)
    before = _dot(selb, tri_ref[...])
    pieces = jnp.floor((jnp.sum(self, axis=1, keepdims=True) + (SEG_ROWS - 1.0)) * (1.0 / SEG_ROWS))
    seg = SEG_ROWS * _dot(low_ref[...], jnp.broadcast_to(pieces, (N_EXPERTS, LANES)).astype(BF16))
    rid_ref[0] = jnp.where(sel, seg[:, :1] + before + 1.0, NO_ROW)
    w_ref[0] = wts
    lane = lax.broadcasted_iota(jnp.int32, (N_EXPERTS, LANES), 1)
    tab_ref[0] = jnp.where(lane == 0, seg, jnp.where(lane == 1, SEG_ROWS * pieces, jnp.where(lane == 2, carry[...], 0.0)))
    carry[...] = carry[...] + SEG_ROWS * pieces


def _router(logits, router_b):
    nt, ne, tm = logits.shape
    tri = jnp.asarray(np.triu(np.ones((tm, tm), np.float32), 1), BF16)
    low = jnp.asarray(np.tril(np.ones((ne, ne), np.float32), -1), BF16)
    per_tile = lambda n: pl.BlockSpec((1, ne, n), lambda i: (i, 0, 0))
    return pl.pallas_call(
        _router_kernel,
        grid=(nt,),
        in_specs=[per_tile(tm), _const_spec((ne, 1)), _const_spec(tri.shape), _const_spec(low.shape)],
        out_specs=[per_tile(tm), per_tile(tm), per_tile(LANES)],
        out_shape=[jax.ShapeDtypeStruct((nt, ne, tm), F32), jax.ShapeDtypeStruct((nt, ne, tm), F32),
                   jax.ShapeDtypeStruct((nt, ne, LANES), F32)],
        scratch_shapes=[pltpu.VMEM((ne, 1), F32)],
        compiler_params=_params(("arbitrary",)),
        name="router",
    )(logits, router_b.reshape(ne, 1).astype(F32), tri, low)


def _moe(x1, x1b, logits, router_b, layer, wg, wu, wd, sg, su, sd, g, b):
    two_parts = isinstance(x1, tuple)
    nt, ne, tm = logits.shape
    t = nt * tm
    if tm != ROUTE_TILE:
        split = tm // ROUTE_TILE
        logits = logits.reshape(nt, ne, split, ROUTE_TILE).transpose(0, 2, 1, 3).reshape(nt * split, ne, ROUTE_TILE)
        nt, tm = nt * split, ROUTE_TILE
    assert SORT_ROWS >= TOP_K * tm + ne * (SEG_ROWS - 1) and SORT_ROWS < NO_ROW
    rid, wdense, tab = _router(logits, router_b)
    tab = tab[:, :, :3].astype(jnp.int32)
    seg, rows, base = tab[:, :, 0], tab[:, :, 1], tab[:, :, 2]
    bm = EXPERT_TILE
    region = base[-1] + rows[-1]
    padded = (region + bm - 1) // bm * bm
    pad_end = jnp.cumsum(padded)
    dst = (pad_end - padded)[None, :] + base
    n_blocks = -(-(t * TOP_K + nt * ne * (SEG_ROWS - 1)) // bm) + ne
    blk_start = jnp.arange(n_blocks, dtype=jnp.int32) * bm
    blk_e = jnp.minimum(jnp.sum((pad_end[None, :] <= blk_start[:, None]).astype(jnp.int32), axis=1), ne - 1)
    n_used = (pad_end[-1] // bm).astype(jnp.int32).reshape(1)
    tail_dst = jnp.concatenate([pad_end - padded + region, pad_end[-1:]])
    tail_n16 = jnp.concatenate([padded - region, n_blocks * bm - pad_end[-1:]]) // SEG_ROWS
    big, small, cnt = _piece_tables(dst, rows // SEG_ROWS, seg)
    x_sorted, pw = _dispatch(big, small, cnt, tail_dst, tail_n16, x1b if two_parts else (x1b, x1b), rid, wdense,
                             seg[:, None, :], rows[:, None, :], n_blocks * bm)
    y_sorted = _experts(blk_e, n_used, x_sorted, wg, wu, wd, layer)
    finish = lambda xf, xb, tile0, n_tiles: _ffn_out(big, small, cnt, xf, xb, pw, y_sorted, sg, su, sd, g, b,
                                                     tile0, n_tiles, row0=0)
    if not two_parts:
        return finish(x1, x1b, 0, nt)
    n_main = x1[0].shape[0] // tm
    return finish(x1[0], x1b[0], 0, n_main), finish(x1[1], x1b[1], n_main, nt - n_main)


def _rope_tables(pos):
    half = MLA_ROPE // 2
    inv = ROPE_THETA ** (-jnp.arange(half, dtype=F32) / half)
    ang = pos.astype(F32)[:, None] * inv
    reps = LANES // half
    return jnp.tile(jnp.cos(ang), (1, reps)), jnp.tile(jnp.sin(ang), (1, reps))


def _pack_layer0_weights(w_in0, gla_w_g2, gla_b_g, mla_q_norm_g, mla_kv_norm_g, mla_w_uq, mla_w_uk, mla_w_uv):
    d = w_in0.shape[0]
    hk, hv = GLA_HEADS * GLA_DK, GLA_HEADS * GLA_DV
    o_q, o_k, o_v, o_r = 0, hk, 2 * hk, 2 * hk + hv
    o_a = o_r + hv
    o_cq = o_a + GLA_RANK
    o_ckv = o_cq + MLA_Q_LORA
    o_kr = o_ckv + MLA_KV_LORA
    half = MLA_ROPE // 2
    zeros = lambda n: jnp.zeros((d, n), w_in0.dtype)
    kr1, kr2 = w_in0[:, o_kr:o_kr + half], w_in0[:, o_kr + half:o_kr + MLA_ROPE]
    w_in = jnp.concatenate([
        w_in0[:, o_q:o_a], w_in0[:, o_cq:o_kr],
        kr1, kr2, zeros(LANES - MLA_ROPE),
        kr2, kr1, zeros(LANES - MLA_ROPE),
        w_in0[:, o_a:o_cq], zeros(LANES - GLA_RANK)], axis=1).astype(BF16)
    assert w_in.shape[1] == _C_END
    w_g2 = jnp.concatenate([gla_w_g2, jnp.zeros((LANES - GLA_RANK, hk), gla_w_g2.dtype)], axis=0)
    uq = mla_w_uq.reshape(MLA_Q_LORA, MLA_HEADS, MLA_NOPE + MLA_ROPE)
    w_uq = jnp.concatenate([uq[:, :, :MLA_NOPE].reshape(MLA_Q_LORA, -1),
                            uq[:, :, MLA_NOPE:MLA_NOPE + half].reshape(MLA_Q_LORA, -1),
                            uq[:, :, MLA_NOPE + half:].reshape(MLA_Q_LORA, -1)], axis=1).astype(BF16)
    eye = jnp.eye(MLA_HEADS, dtype=mla_w_uk.dtype)
    w_abs = jnp.einsum('chn,hg->hngc', mla_w_uk, eye).reshape(MLA_HEADS * MLA_NOPE, MLA_HEADS * MLA_KV_LORA).astype(BF16)
    w_uvbd = jnp.einsum('chv,hg->hcgv', mla_w_uv, eye).reshape(MLA_HEADS * MLA_KV_LORA, MLA_HEADS * MLA_V).astype(BF16)
    place = np.zeros((2 * LANES, MLA_HEADS * LANES), np.float32)
    for h in range(MLA_HEADS):
        for j in range(half):
            place[h * half + j, h * LANES + j] = 1.0
            place[LANES + h * half + j, h * LANES + half + j] = 1.0
    return dict(w_in=w_in, w_g2=w_g2, b_g=gla_b_g.reshape(1, hk), q_norm=mla_q_norm_g.reshape(1, -1),
                kv_norm=mla_kv_norm_g.reshape(1, -1), w_uq=w_uq, w_abs=w_abs,
                place=jnp.asarray(place, BF16)), w_uvbd


def _pad_rows(x, n):
    return jnp.pad(x, ((0, n - x.shape[0]),) + ((0, 0),) * (x.ndim - 1))


def _kv_tiles(past, own):
    kvb = MLA_KV_TILE
    pad = lambda a: jnp.pad(a, ((0, 0), (0, -a.shape[1] % kvb), (0, 0)))
    kv = jnp.concatenate([pad(past), pad(own)], axis=1) if past.shape[1] else pad(own)
    b, n, w = kv.shape
    kt = kv.reshape(b, n // kvb, kvb, w).transpose(0, 1, 3, 2)
    return kt, kv


def kernel(x_prompt, x_sample, cache_mla_ckv, cache_mla_krope, state_gla, cache_pool, meta_tokens, w_in0, gla_w_g2, gla_b_g, gla_norm_g, mla_q_norm_g, mla_kv_norm_g, mla_w_uq, mla_w_uk, mla_w_uv, w_out0, pool_w, pool_scale, ln_g, ln_b, moe_router_w, moe_router_b, moe_w_gate, moe_w_up, moe_w_down, moe_sh_gate, moe_sh_up, moe_sh_down):
    bp, sp, d = x_prompt.shape
    bs, ss, _ = x_sample.shape
    n_meta = meta_tokens.shape[0]
    past_len = cache_mla_ckv.shape[1] - n_meta
    tp, tsm = bp * sp, bs * ss
    assert sp % MLA_Q_TILE == 0 and sp % POOL_TILE == 0 and sp % GLA_BLOCK == 0 and sp % CHUNK == 0
    assert ss <= CHUNK and past_len % CHUNK == 0 and n_meta <= CHUNK and ss <= GLA_BLOCK and n_meta <= GLA_BLOCK
    assert n_meta == POOL_MAX and ss >= POOL_MAX and tp % TOK_TILE == 0
    t_all = -(-(tp + tsm + n_meta) // TOK_TILE) * TOK_TILE
    o_s, o_m = tp, tp + tsm

    x_parts = (x_prompt.reshape(tp, d),
               _pad_rows(jnp.concatenate([x_sample.reshape(tsm, d), meta_tokens], axis=0), t_all - tp))
    pos = jnp.concatenate([jnp.tile(n_meta + jnp.arange(sp), bp), jnp.tile(n_meta + past_len + jnp.arange(ss), bs),
                           jnp.arange(n_meta), jnp.zeros((t_all - o_m - n_meta,), jnp.int32)])
    cos_t, sin_t = _rope_tables(pos)
    w0, w_uvbd = _pack_layer0_weights(w_in0, gla_w_g2, gla_b_g, mla_q_norm_g, mla_kv_norm_g, mla_w_uq, mla_w_uk, mla_w_uv)

    q, k, v, sr, lg, q_all, ckv, kr, k_all = _inproj(*x_parts, cos_t, sin_t, w0)

    tables = _gla_tables(GLA_BLOCK)
    g_gla = gla_norm_g.reshape(1, GLA_DV)
    c = GLA_BLOCK
    hk = GLA_HEADS * GLA_DK

    def small(a):
        sm = jnp.pad(a[o_s:o_m].reshape(bs, ss, -1), ((0, 0), (0, c - ss), (0, 0)))
        me = jnp.pad(a[o_m:o_m + n_meta], ((0, c - n_meta), (0, 0)))[None]
        return jnp.concatenate([sm, me], axis=0).reshape((bs + 1) * c, -1)

    s0_small = jnp.concatenate([state_gla.reshape(bs, hk, GLA_DV), jnp.zeros((1, hk, GLA_DV), F32)], axis=0)
    og_small, st_small = _gla(small(q), small(k), small(v), small(lg), small(sr), s0_small, g_gla, tables, bs + 1, 1)
    s0_prompt = jnp.broadcast_to(st_small[bs:], (bp, hk, GLA_DV))
    og_prompt, st_prompt = _gla(q, k, v, lg, sr, s0_prompt, g_gla, tables, bp, sp // c)
    og_small = og_small.reshape(bs + 1, c, -1)
    o_gla = (og_prompt, _pad_rows(jnp.concatenate([og_small[:bs, :ss].reshape(tsm, -1), og_small[bs, :n_meta]], axis=0),
                                  t_all - tp))

    k_meta = k_all[o_m:o_m + n_meta]
    kt_m, kv_m = _kv_tiles(jnp.zeros((1, 0, MLA_QK), BF16), k_meta[None])
    ol_meta = _mla(q_all[o_m:o_m + n_meta], kt_m, kv_m, 1, 1, n_meta, 0, 0, n_meta)
    cache = jnp.concatenate([cache_mla_ckv, cache_mla_krope,
                             jnp.zeros(cache_mla_ckv.shape[:2] + (MLA_QK - MLA_KV_LORA - MLA_ROPE,), F32)], axis=-1).astype(BF16)
    kt_s, kv_s = _kv_tiles(cache, k_all[o_s:o_m].reshape(bs, ss, MLA_QK))
    ol_sample = _mla(q_all[o_s:o_m], kt_s, kv_s, bs, 1, ss, 0, n_meta + past_len, ss)
    kt_p, kv_p = _kv_tiles(jnp.broadcast_to(k_meta[None], (bp, n_meta, MLA_QK)), k_all[:tp].reshape(bp, sp, MLA_QK))
    ol_prompt = _mla(q_all, kt_p, kv_p, bp, sp // MLA_Q_TILE, MLA_Q_TILE, 0, n_meta, sp)
    o_lat = (ol_prompt, _pad_rows(jnp.concatenate([ol_sample, ol_meta], axis=0), t_all - tp))

    ln = lambda l, j: (ln_g[l, j].reshape(1, d), ln_b[l, j].reshape(1, d))
    rw = lambda l: moe_router_w[l].T
    moe_w = lambda l: (l, moe_w_gate, moe_w_up, moe_w_down,
                       moe_sh_gate[l].astype(BF16), moe_sh_up[l].astype(BF16), moe_sh_down[l].astype(BF16))
    x1, x1b, logits = _outproj(o_gla, o_lat, x_parts, w_uvbd, w_out0.astype(BF16), *ln(0, 0), rw(0))
    x2 = _moe(x1, x1b, logits, moe_router_b[0], *moe_w(0), *ln(0, 1))

    pm = POOL_MAX
    zrow = jnp.zeros((1, d), F32)
    hist_p = jnp.broadcast_to(jnp.concatenate([zrow, x2[o_m + n_meta - (pm - 1):o_m + n_meta]], axis=0)[None], (bp, pm, d))
    hist_s = jnp.concatenate([jnp.zeros((bs, 1, d), F32), cache_pool], axis=1)
    pool_wb = pool_w.astype(BF16)
    ps = pool_scale.reshape(1, d)
    t1 = -(-(tp + tsm) // TOK_TILE) * TOK_TILE
    y1p = _pool(x2, hist_p, pool_wb, ps, *ln(1, 0), rw(1), bp, sp // POOL_TILE, POOL_TILE, 0)
    y1s = _pool(x2[o_s:o_m], hist_s, pool_wb, ps, *ln(1, 0), rw(1), bs, 1, ss, 0)
    x3, x3b = [(a, _pad_rows(b_, t1 - tp)) for a, b_ in zip(y1p[:2], y1s[:2])]
    assert POOL_TILE == TOK_TILE
    rest_logits = jnp.pad(y1s[2].transpose(1, 0, 2).reshape(N_EXPERTS, tsm), ((0, 0), (0, t1 - tp - tsm)))
    logits1 = jnp.concatenate([y1p[2], rest_logits.reshape(N_EXPERTS, -1, TOK_TILE).transpose(1, 0, 2)], axis=0)
    x4_prompt, x4_rest = _moe(x3, x3b, logits1, moe_router_b[1], *moe_w(1), *ln(1, 1))

    y_prompt = x4_prompt.reshape(bp, sp, d)
    y_sample = x4_rest[:tsm].reshape(bs, ss, d)
    ckv_meta, kr_meta = ckv[o_m:o_m + n_meta], kr[o_m:o_m + n_meta, :MLA_ROPE]
    p_ckv = jnp.concatenate([jnp.broadcast_to(ckv_meta[None], (bp, n_meta, MLA_KV_LORA)),
                             ckv[:tp].reshape(bp, sp, MLA_KV_LORA)], axis=1)
    p_kr = jnp.concatenate([jnp.broadcast_to(kr_meta[None], (bp, n_meta, MLA_ROPE)),
                            kr[:tp, :MLA_ROPE].reshape(bp, sp, MLA_ROPE)], axis=1)
    p_gla = st_prompt.reshape(bp, GLA_HEADS, GLA_DK, GLA_DV)
    p_pool = jnp.stack([x2[(b_ + 1) * sp - (pm - 1):(b_ + 1) * sp] for b_ in range(bp)])
    s_ckv = ckv[o_s:o_m].reshape(bs, ss, MLA_KV_LORA)
    s_kr = kr[o_s:o_m, :MLA_ROPE].reshape(bs, ss, MLA_ROPE)
    s_gla = st_small[:bs].reshape(bs, GLA_HEADS, GLA_DK, GLA_DV)
    s_pool = x2[o_s:o_m].reshape(bs, ss, d)[:, ss - (pm - 1):]
    return (y_prompt, y_sample, p_ckv, p_kr, p_gla, p_pool, s_ckv, s_kr, s_gla, s_pool)
```
